```python
import jax
import jax.numpy as jnp
from jax import lax
import numpy as np

D_MODEL = 1024
BATCH = 4
SEQ = 4096
DEPTH = 1

GRID_W = 64
CTX_LEN = 256
RET_HEADS = 4
RET_QK_DIM = 256
RET_V_DIM = 512
RET_QK_W = RET_HEADS * RET_QK_DIM
RET_V_W = RET_HEADS * RET_V_DIM
RET_CHUNK = 128
CONV_W = D_MODEL
CONV_K = 3
N_BRANCH = 2
N_GROUPS = 4
EXPERTS_PER_GROUP = 8
N_EXPERTS = N_GROUPS * EXPERTS_PER_GROUP
EXPERT_HIDDEN = 512
TOP_K_IN_GROUP = 2
ROPE_BASE = 10000.0
NORM_EPS = 1e-6
GN_EPS = 1e-5

OFF_Q = 0
OFF_K = OFF_Q + RET_QK_W
OFF_V = OFF_K + RET_QK_W
OFF_G = OFF_V + RET_V_W
OFF_CB = OFF_G + RET_V_W
OFF_CC = OFF_CB + CONV_W
OFF_CH = OFF_CC + CONV_W
OFF_GATE = OFF_CH + CONV_W
PROJ_W = OFF_GATE + N_BRANCH * D_MODEL

kernel_name = "hybrid_retention_shortconv_hmoe_dit"


def _rmsnorm(x, w):
    xf = x.astype(jnp.float32)
    y = xf * lax.rsqrt(jnp.mean(xf * xf, axis=-1, keepdims=True) + NORM_EPS)
    return (y * w.astype(jnp.float32)).astype(x.dtype)


def _modulate(h, shift, scale):
    return h * (1 + scale) + shift


def _split_heads(t, n_heads):
    b, l, _ = t.shape
    return t.reshape(b, l, n_heads, -1).transpose(0, 2, 1, 3)


def _rope_axis(u, pos):
    nf = u.shape[-1] // 2
    inv = ROPE_BASE ** (-jnp.arange(nf, dtype=jnp.float32) / nf)
    ang = pos[:, None] * inv[None, :]
    cos = jnp.cos(ang).astype(u.dtype)
    sin = jnp.sin(ang).astype(u.dtype)
    u1, u2 = u[..., :nf], u[..., nf:]
    return jnp.concatenate([u1 * cos - u2 * sin, u1 * sin + u2 * cos], axis=-1)


def _rope_2d(t):
    pos = jnp.arange(t.shape[2])
    row = (pos // GRID_W).astype(jnp.float32)
    col = (pos % GRID_W).astype(jnp.float32)
    half = t.shape[-1] // 2
    return jnp.concatenate([_rope_axis(t[..., :half], row), _rope_axis(t[..., half:], col)], axis=-1)


def _decay_logs(log_decay):
    return jnp.log1p(-jnp.exp(log_decay.astype(jnp.float32)))


def _retention_chunked(q, k, v, log_gamma, s0, strict):
    b, h, l, _ = q.shape
    dv = v.shape[-1]
    n_chunks = l // RET_CHUNK
    idx = jnp.arange(RET_CHUNK, dtype=jnp.float32)
    diff = idx[:, None] - idx[None, :]
    mask = (diff > 0) if strict else (diff >= 0)
    lg = log_gamma[:, None, None]
    intra = jnp.where(mask, jnp.exp(jnp.where(mask, diff, 0.0) * lg), 0.0)
    xi = jnp.exp((idx + 1.0)[None, :] * log_gamma[:, None])[:, :, None]
    zeta = jnp.exp((RET_CHUNK - 1.0 - idx)[None, :] * log_gamma[:, None])[:, :, None]
    chunk_decay = jnp.exp(RET_CHUNK * log_gamma)[:, None, None]

    def to_chunks(t):
        t = t.astype(jnp.float32).reshape(b, h, n_chunks, RET_CHUNK, t.shape[-1])
        return jnp.moveaxis(t, 2, 0)

    def step(state, qkv):
        qc, kc, vc = qkv
        scores = jnp.einsum('bhid,bhjd->bhij', qc, kc) * intra
        out = (jnp.einsum('bhij,bhje->bhie', scores, vc)
               + jnp.einsum('bhid,bhde->bhie', qc * xi, state))
        state = chunk_decay * state + jnp.einsum('bhjd,bhje->bhde', kc * zeta, vc)
        return state, out

    s_final, out = lax.scan(step, s0, (to_chunks(q), to_chunks(k), to_chunks(v)))
    return jnp.moveaxis(out, 0, 2).reshape(b, h, l, dv), s_final


def _bidir_retention(q, k, v, log_gamma, s_f, s_b):
    out_f, _ = _retention_chunked(q, k, v, log_gamma[0], s_f, strict=False)
    flip = lambda t: jnp.flip(t, axis=2)
    out_b, _ = _retention_chunked(flip(q), flip(k), flip(v), log_gamma[1], s_b, strict=True)
    return out_f + flip(out_b)


def _context_states(hc, w_in, log_decay):
    kv = hc @ w_in[:, OFF_K:OFF_G]
    k = _split_heads(kv[..., :RET_QK_W], RET_HEADS).astype(jnp.float32) * RET_QK_DIM ** -0.5
    v = _split_heads(kv[..., RET_QK_W:], RET_HEADS).astype(jnp.float32)
    lg = _decay_logs(log_decay)
    ctx_len = k.shape[2]
    pos = jnp.arange(ctx_len, dtype=jnp.float32)
    w_f = jnp.exp((ctx_len - 1.0 - pos)[None, :] * lg[0][:, None])[:, :, None]
    w_b = jnp.exp(pos[None, :] * lg[1][:, None])[:, :, None]
    s_f = jnp.einsum('bhld,bhle->bhde', k * w_f, v)
    s_b = jnp.einsum('bhld,bhle->bhde', k * w_b, v)
    return s_f, s_b


def _short_conv(u, conv_w, n_rows):
    b, l, ch = u.shape
    seqs = u.reshape(b * n_rows, GRID_W, ch) if n_rows is not None else u
    y = lax.conv_general_dilated(
        seqs, conv_w[:, None, :].astype(u.dtype), window_strides=(1,),
        padding=((CONV_K // 2, CONV_K // 2),),
        dimension_numbers=('NWC', 'WIO', 'NWC'), feature_group_count=ch)
    return y.reshape(b, l, ch)


def _token_mixers(h, w_in, log_decay, gn_w, w_ret_o, conv_w, w_conv_o, w_out, s_f, s_b, n_rows):
    b, l, _ = h.shape
    proj = h @ w_in
    q = _split_heads(proj[..., OFF_Q:OFF_K], RET_HEADS)
    k = _split_heads(proj[..., OFF_K:OFF_V], RET_HEADS) * RET_QK_DIM ** -0.5
    v = _split_heads(proj[..., OFF_V:OFF_G], RET_HEADS)
    if n_rows is not None:
        q, k = _rope_2d(q), _rope_2d(k)
    y = _bidir_retention(q, k, v, _decay_logs(log_decay), s_f, s_b)
    mu = jnp.mean(y, axis=-1, keepdims=True)
    var = jnp.mean(jnp.square(y - mu), axis=-1, keepdims=True)
    y = (y - mu) * lax.rsqrt(var + GN_EPS)
    y = y.transpose(0, 2, 1, 3).reshape(b, l, RET_V_W) * gn_w.astype(jnp.float32)
    y = jax.nn.silu(proj[..., OFF_G:OFF_CB]) * y.astype(h.dtype)
    ret_branch = y @ w_ret_o
    u = _short_conv(proj[..., OFF_CC:OFF_CH] * proj[..., OFF_CH:OFF_GATE], conv_w, n_rows)
    conv_branch = (proj[..., OFF_CB:OFF_CC] * u) @ w_conv_o
    gates = jax.nn.sigmoid(proj[..., OFF_GATE:])
    merged = gates[..., :D_MODEL] * ret_branch + gates[..., D_MODEL:] * conv_branch
    return merged @ w_out


def _hier_moe(h, rg_w, rg_b, re_w, re_b, w13, w2):
    b, l, d = h.shape
    t = h.reshape(-1, d)
    g_logits = (t @ rg_w + rg_b).astype(jnp.float32)
    g_prob = jax.nn.softmax(g_logits, axis=-1)
    g_idx = jnp.argmax(g_logits, axis=-1)
    g_w = jnp.take_along_axis(g_prob, g_idx[:, None], axis=1)
    e_logits = (t @ re_w + re_b).astype(jnp.float32).reshape(-1, N_GROUPS, EXPERTS_PER_GROUP)
    e_logits = jnp.take_along_axis(e_logits, g_idx[:, None, None], axis=1)[:, 0]
    e_prob = jax.nn.softmax(e_logits, axis=-1)
    top_w, top_i = lax.top_k(e_prob, TOP_K_IN_GROUP)
    top_w = top_w / jnp.sum(top_w, axis=-1, keepdims=True)
    expert_id = g_idx[:, None] * EXPERTS_PER_GROUP + top_i
    combine = jnp.sum(jax.nn.one_hot(expert_id, N_EXPERTS, dtype=jnp.float32)
                      * (g_w * top_w)[..., None], axis=1).astype(t.dtype)
    out = jnp.zeros_like(t)
    for e in range(N_EXPERTS):
        hid = t @ w13[e]
        act = jax.nn.silu(hid[:, :EXPERT_HIDDEN]) * hid[:, EXPERT_HIDDEN:]
        out = out + combine[:, e:e + 1] * (act @ w2[e])
    return out.reshape(b, l, d)


def setup_inputs(seed: int = 0) -> dict:
    key = jax.random.key(seed)
    ks = jax.random.split(key, 24)
    nrm = jax.random.normal
    f32 = jnp.float32
    base_decay = -(5.0 + jnp.arange(RET_HEADS, dtype=f32)) * np.float32(np.log(2.0))
    return {
        'x': nrm(ks[0], (BATCH, SEQ, D_MODEL), f32),
        'c': nrm(ks[1], (BATCH, D_MODEL), f32),
        'ctx': nrm(ks[2], (BATCH, CTX_LEN, D_MODEL), f32),
        'c_ctx': nrm(ks[3], (D_MODEL,), f32),
        'w_mod': nrm(ks[4], (DEPTH, D_MODEL, 6 * D_MODEL), f32) * (0.5 * D_MODEL ** -0.5),
        'b_mod': nrm(ks[5], (DEPTH, 6 * D_MODEL), f32) * 0.02,
        'norm1_w': 1.0 + 0.02 * nrm(ks[6], (DEPTH, D_MODEL), f32),
        'norm2_w': 1.0 + 0.02 * nrm(ks[7], (DEPTH, D_MODEL), f32),
        'w_in': nrm(ks[8], (DEPTH, D_MODEL, PROJ_W), f32) * D_MODEL ** -0.5,
        'ret_log_decay': base_decay[None, None, :] + 0.1 * nrm(ks[9], (DEPTH, 2, RET_HEADS), f32),
        'ret_gn_w': 1.0 + 0.02 * nrm(ks[10], (DEPTH, RET_V_W), f32),
        'w_ret_o': nrm(ks[11], (DEPTH, RET_V_W, D_MODEL), f32) * RET_V_W ** -0.5,
        'conv_w': nrm(ks[12], (DEPTH, CONV_K, CONV_W), f32) * CONV_K ** -0.5,
        'w_conv_o': nrm(ks[13], (DEPTH, CONV_W, D_MODEL), f32) * CONV_W ** -0.5,
        'w_out': nrm(ks[14], (DEPTH, D_MODEL, D_MODEL), f32) * D_MODEL ** -0.5,
        'router_group_w': nrm(ks[15], (DEPTH, D_MODEL, N_GROUPS), f32) * D_MODEL ** -0.5,
        'router_group_b': 0.01 * nrm(ks[16], (DEPTH, N_GROUPS), f32),
        'router_expert_w': nrm(ks[17], (DEPTH, D_MODEL, N_EXPERTS), f32) * D_MODEL ** -0.5,
        'router_expert_b': 0.01 * nrm(ks[18], (DEPTH, N_EXPERTS), f32),
        'expert_w13': nrm(ks[19], (DEPTH, N_EXPERTS, D_MODEL, 2 * EXPERT_HIDDEN), f32) * D_MODEL ** -0.5,
        'expert_w2': nrm(ks[20], (DEPTH, N_EXPERTS, EXPERT_HIDDEN, D_MODEL), f32) * EXPERT_HIDDEN ** -0.5,
        'final_norm_w': 1.0 + 0.02 * nrm(ks[21], (D_MODEL,), f32),
    }


def reference(x, c, ctx, c_ctx, w_mod, b_mod, norm1_w, norm2_w, w_in, ret_log_decay, ret_gn_w,
              w_ret_o, conv_w, w_conv_o, w_out, router_group_w, router_group_b, router_expert_w,
              router_expert_b, expert_w13, expert_w2, final_norm_w):
    n_rows = x.shape[1] // GRID_W
    for layer in range(DEPTH):
        mod_x = jnp.split((jax.nn.silu(c) @ w_mod[layer] + b_mod[layer])[:, None, :], 6, axis=-1)
        mod_c = jnp.split(jax.nn.silu(c_ctx) @ w_mod[layer] + b_mod[layer], 6, axis=-1)
        mixer_w = (w_in[layer], ret_log_decay[layer], ret_gn_w[layer], w_ret_o[layer],
                   conv_w[layer], w_conv_o[layer], w_out[layer])
        moe_w = (router_group_w[layer], router_group_b[layer], router_expert_w[layer],
                 router_expert_b[layer], expert_w13[layer], expert_w2[layer])
        hx = _modulate(_rmsnorm(x, norm1_w[layer]), mod_x[0], mod_x[1])
        hc = _modulate(_rmsnorm(ctx, norm1_w[layer]), mod_c[0], mod_c[1])
        s_f, s_b = _context_states(hc, w_in[layer], ret_log_decay[layer])
        x = x + mod_x[2] * _token_mixers(hx, *mixer_w, s_f, s_b, n_rows)
        if layer < DEPTH - 1:
            zero_state = jnp.zeros_like(s_f)
            ctx = ctx + mod_c[2] * _token_mixers(hc, *mixer_w, zero_state, zero_state, None)
            hc2 = _modulate(_rmsnorm(ctx, norm2_w[layer]), mod_c[3], mod_c[4])
            ctx = ctx + mod_c[5] * _hier_moe(hc2, *moe_w)
        hx2 = _modulate(_rmsnorm(x, norm2_w[layer]), mod_x[3], mod_x[4])
        x = x + mod_x[5] * _hier_moe(hx2, *moe_w)
    return _rmsnorm(x, final_norm_w)
```

```python
import functools

import jax
import jax.numpy as jnp
from jax import lax
from jax.experimental import pallas as pl
from jax.experimental.pallas import tpu as pltpu

GRID_W = 64
RET_HEADS = 4
RET_QK_DIM = 256
RET_V_DIM = 512
RET_QK_W = RET_HEADS * RET_QK_DIM
RET_V_W = RET_HEADS * RET_V_DIM
N_GROUPS = 4
EXPERTS_PER_GROUP = 8
N_EXPERTS = N_GROUPS * EXPERTS_PER_GROUP
EXPERT_HIDDEN = 512
ROPE_BASE = 10000.0
NORM_EPS = 1e-6
GN_EPS = 1e-5

RET_CHUNK = 256
LANES = 128
EXPERT_TILE = 256
VMEM_LIMIT = 56 * 1024 * 1024

BF16 = jnp.bfloat16
F32 = jnp.float32


def _params(sem, vmem=VMEM_LIMIT):
    return pltpu.CompilerParams(dimension_semantics=sem, vmem_limit_bytes=vmem)


def _mod_kernel(c_ref, w_ref, b_ref, o_ref):
    c = c_ref[...]
    s = c * jax.nn.sigmoid(c)
    o_ref[...] = jnp.dot(s.astype(BF16), w_ref[...].astype(BF16),
                         preferred_element_type=F32) + b_ref[...]


def _modulation(c_all, w_mod, b_mod):
    rows, d = c_all.shape
    n_out = w_mod.shape[1]
    tn = 1536
    return pl.pallas_call(
        _mod_kernel,
        out_shape=jax.ShapeDtypeStruct((rows, n_out), F32),
        grid=(n_out // tn,),
        in_specs=[pl.BlockSpec((rows, d), lambda j: (0, 0)),
                  pl.BlockSpec((d, tn), lambda j: (0, j)),
                  pl.BlockSpec((1, tn), lambda j: (0, j))],
        out_specs=pl.BlockSpec((rows, tn), lambda j: (0, j)),
        compiler_params=_params(("parallel",)),
        name="mod",
    )(c_all, w_mod, b_mod.reshape(1, n_out))


def _rope_store(acc, cos, sin, scale, o_ref):
    for g in range(acc.shape[1] // LANES):
        u = acc[:, g * LANES:(g + 1) * LANES]
        t = (g % 2) * LANES
        r = u * cos[:, t:t + LANES] + pltpu.roll(u, LANES // 2, axis=1) * sin[:, t:t + LANES]
        if scale != 1.0:
            r = r * scale
        o_ref[:, g * LANES:(g + 1) * LANES] = r.astype(o_ref.dtype)


def _proj_kernel(x_ref, nw_ref, sh_ref, sc_ref, cos_ref, sin_ref, w_ref, o_ref, h_ref, *, modes):
    j = pl.program_id(1)

    @pl.when(j == 0)
    def _():
        x = x_ref[...]
        y = x * lax.rsqrt(jnp.mean(x * x, axis=-1, keepdims=True) + NORM_EPS) * nw_ref[...]
        h_ref[...] = (y * (1.0 + sc_ref[0]) + sh_ref[0]).astype(BF16)

    acc = jnp.dot(h_ref[...], w_ref[...], preferred_element_type=F32)
    n_special = len(modes)
    for t, (rope, scale) in enumerate(modes):
        @pl.when(j == t)
        def _(rope=rope, scale=scale):
            if rope:
                _rope_store(acc, cos_ref[...], sin_ref[...], scale, o_ref)
            else:
                o_ref[...] = (acc * scale).astype(o_ref.dtype)

    @pl.when(j >= n_special)
    def _():
        o_ref[...] = acc.astype(o_ref.dtype)


def _projection(x2d, norm_w, shift, scale, cos_t, sin_t, w_bf16, seq, modes, tm):
    n, d = x2d.shape
    width = w_bf16.shape[1]
    tn = 1024
    per_b = seq // tm
    nb = shift.shape[0]
    bidx = (lambda i, j: (i // per_b, 0, 0)) if nb > 1 else (lambda i, j: (0, 0, 0))
    return pl.pallas_call(
        functools.partial(_proj_kernel, modes=modes),
        out_shape=jax.ShapeDtypeStruct((n, width), BF16),
        grid=(n // tm, width // tn),
        in_specs=[pl.BlockSpec((tm, d), lambda i, j: (i, 0)),
                  pl.BlockSpec((1, d), lambda i, j: (0, 0)),
                  pl.BlockSpec((1, 1, d), bidx),
                  pl.BlockSpec((1, 1, d), bidx),
                  pl.BlockSpec((tm, RET_QK_DIM), lambda i, j: (i % per_b, 0)),
                  pl.BlockSpec((tm, RET_QK_DIM), lambda i, j: (i % per_b, 0)),
                  pl.BlockSpec((d, tn), lambda i, j: (0, j))],
        out_specs=pl.BlockSpec((tm, tn), lambda i, j: (i, j)),
        scratch_shapes=[pltpu.VMEM((tm, d), BF16)],
        compiler_params=_params(("parallel", "arbitrary")),
        name="proj",
    )(x2d, norm_w.reshape(1, d), shift, scale, cos_t, sin_t, w_bf16)


def _dot_t0(a, b):
    return lax.dot_general(a, b, (((0,), (0,)), ((), ())), preferred_element_type=F32)


def _ret_kernel(lg_ref, q_ref, k_ref, v_ref, g_ref, kc_ref, vc_ref, gn_ref, o_ref,
                acc_ref, sf_ref, sb_ref, dm_ref, *, seq, ctx_len):
    h = pl.program_id(1)
    lgf = lg_ref[0, h]
    lgb = lg_ref[1, h]
    c = RET_CHUNK
    n_chunks = seq // c

    cpos = lax.broadcasted_iota(jnp.int32, (ctx_len, 1), 0).astype(F32)
    kc = kc_ref[...].astype(F32)
    vc = vc_ref[...]
    sf_ref[...] = _dot_t0((kc * jnp.exp((ctx_len - 1.0 - cpos) * lgf)).astype(BF16), vc)
    sb_ref[...] = _dot_t0((kc * jnp.exp(cpos * lgb)).astype(BF16), vc)

    ri = lax.broadcasted_iota(jnp.int32, (c, c), 0)
    ci = lax.broadcasted_iota(jnp.int32, (c, c), 1)
    dist = (ri - ci).astype(F32)
    dm_ref[...] = jnp.exp(jnp.abs(dist) * jnp.where(dist >= 0, lgf, lgb))

    idx = lax.broadcasted_iota(jnp.int32, (c, 1), 0).astype(F32)
    xi_f = jnp.exp((idx + 1.0) * lgf)
    zeta_f = jnp.exp((c - 1.0 - idx) * lgf)
    xi_b = jnp.exp((c - idx) * lgb)
    zeta_b = jnp.exp(idx * lgb)
    cd_f = jnp.exp(jnp.full((1, 1), float(c), F32) * lgf)
    cd_b = jnp.exp(jnp.full((1, 1), float(c), F32) * lgb)

    def fwd(ic, carry):
        rows = pl.ds(pl.multiple_of(ic * c, c), c)
        q = q_ref[rows, :]
        k = k_ref[rows, :]
        v = v_ref[rows, :]
        s = lax.dot_general(q, k, (((1,), (1,)), ((), ())), preferred_element_type=F32)
        intra = jnp.dot((s * dm_ref[...]).astype(BF16), v, preferred_element_type=F32)
        inter = jnp.dot(q, sf_ref[...].astype(BF16), preferred_element_type=F32)
        acc_ref[rows, :] = intra + xi_f * inter
        kz = (k.astype(F32) * zeta_f).astype(BF16)
        sf_ref[...] = cd_f * sf_ref[...] + _dot_t0(kz, v)
        return carry

    lax.fori_loop(0, n_chunks, fwd, 0)

    gn_w = gn_ref[...]

    def bwd(t, carry):
        ic = n_chunks - 1 - t
        rows = pl.ds(pl.multiple_of(ic * c, c), c)
        q = q_ref[rows, :]
        k = k_ref[rows, :]
        v = v_ref[rows, :]
        inter = jnp.dot(q, sb_ref[...].astype(BF16), preferred_element_type=F32)
        y = acc_ref[rows, :] + xi_b * inter
        mu = jnp.mean(y, axis=-1, keepdims=True)
        yc = y - mu
        var = jnp.mean(yc * yc, axis=-1, keepdims=True)
        yn = yc * lax.rsqrt(var + GN_EPS) * gn_w
        g = g_ref[rows, :].astype(F32)
        o_ref[rows, :] = (g * jax.nn.sigmoid(g) * yn).astype(o_ref.dtype)
        kz = (k.astype(F32) * zeta_b).astype(BF16)
        sb_ref[...] = cd_b * sb_ref[...] + _dot_t0(kz, v)
        return carry

    lax.fori_loop(0, n_chunks, bwd, 0)


def _retention(proj, kv_ctx, log_gamma, gn_w, batch, seq, ctx_len, off_q, off_k, off_v, off_g):
    n = proj.shape[0]
    dk, dv = RET_QK_DIM, RET_V_DIM
    grid_spec = pltpu.PrefetchScalarGridSpec(
        num_scalar_prefetch=1,
        grid=(batch, RET_HEADS),
        in_specs=[pl.BlockSpec((seq, dk), lambda b, h, lg: (b, off_q // dk + h)),
                  pl.BlockSpec((seq, dk), lambda b, h, lg: (b, off_k // dk + h)),
                  pl.BlockSpec((seq, dv), lambda b, h, lg: (b, off_v // dv + h)),
                  pl.BlockSpec((seq, dv), lambda b, h, lg: (b, off_g // dv + h)),
                  pl.BlockSpec((ctx_len, dk), lambda b, h, lg: (b, h)),
                  pl.BlockSpec((ctx_len, dv), lambda b, h, lg: (b, RET_QK_W // dv + h)),
                  pl.BlockSpec((1, dv), lambda b, h, lg: (0, h))],
        out_specs=pl.BlockSpec((seq, dv), lambda b, h, lg: (b, h)),
        scratch_shapes=[pltpu.VMEM((seq, dv), F32),
                        pltpu.VMEM((dk, dv), F32),
                        pltpu.VMEM((dk, dv), F32),
                        pltpu.VMEM((RET_CHUNK, RET_CHUNK), F32)],
    )
    return pl.pallas_call(
        functools.partial(_ret_kernel, seq=seq, ctx_len=ctx_len),
        out_shape=jax.ShapeDtypeStruct((n, RET_V_W), BF16),
        grid_spec=grid_spec,
        compiler_params=_params(("parallel", "arbitrary")),
        name="ret",
    )(log_gamma, proj, proj, proj, proj, kv_ctx, kv_ctx, gn_w.reshape(1, RET_V_W))


def _mix_kernel(yg_ref, cb_ref, cc_ref, ch_ref, gr_ref, gc_ref, x_ref, wro_ref, wco_ref, wo_ref,
                cw_ref, g1_ref, sh2_ref, sc2_ref, n2_ref, wr_ref, br_ref,
                x1_ref, h2_ref, ri_ref, rw_ref, cnt_ref, carry_ref):
    i = pl.program_id(0)
    tm = x_ref.shape[0]

    @pl.when(i == 0)
    def _():
        carry_ref[...] = jnp.zeros_like(carry_ref)

    ret_branch = jnp.dot(yg_ref[...], wro_ref[...], preferred_element_type=F32)

    p = cc_ref[...].astype(F32) * ch_ref[...].astype(F32)
    tpos = lax.broadcasted_iota(jnp.int32, (tm, 1), 0) % GRID_W
    prev = jnp.where(tpos != 0, pltpu.roll(p, 1, axis=0), 0.0)
    nxt = jnp.where(tpos != GRID_W - 1, pltpu.roll(p, tm - 1, axis=0), 0.0)
    cw = cw_ref[...]
    u = cw[0:1, :] * prev + cw[1:2, :] * p + cw[2:3, :] * nxt
    conv_in = (cb_ref[...].astype(F32) * u).astype(BF16)
    conv_branch = jnp.dot(conv_in, wco_ref[...], preferred_element_type=F32)

    merged = (jax.nn.sigmoid(gr_ref[...].astype(F32)) * ret_branch
              + jax.nn.sigmoid(gc_ref[...].astype(F32)) * conv_branch)
    mixed = jnp.dot(merged.astype(BF16), wo_ref[...], preferred_element_type=F32)
    x1 = x_ref[...] + g1_ref[0] * mixed
    x1_ref[...] = x1

    y = x1 * lax.rsqrt(jnp.mean(x1 * x1, axis=-1, keepdims=True) + NORM_EPS) * n2_ref[...]
    h2 = y * (1.0 + sc2_ref[0]) + sh2_ref[0]
    h2_ref[...] = h2

    logits = jnp.dot(h2, wr_ref[...], precision=lax.Precision.HIGHEST,
                     preferred_element_type=F32) + br_ref[...]
    lane = lax.broadcasted_iota(jnp.int32, (tm, LANES), 1)
    neg = jnp.float32(-jnp.inf)
    big = jnp.int32(1 << 20)
    is_g = (lane >= N_EXPERTS) & (lane < N_EXPERTS + N_GROUPS)
    gl = jnp.where(is_g, logits, neg)
    gmax = jnp.max(gl, axis=1, keepdims=True)
    gidx = jnp.min(jnp.where(gl == gmax, lane, big), axis=1, keepdims=True) - N_EXPERTS
    gsum = jnp.sum(jnp.where(is_g, jnp.exp(logits - gmax), 0.0), axis=1, keepdims=True)
    g_w = 1.0 / gsum

    in_grp = (lane // EXPERTS_PER_GROUP) == gidx
    el = jnp.where(in_grp, logits, neg)
    emax = jnp.max(el, axis=1, keepdims=True)
    ex = jnp.where(in_grp, jnp.exp(logits - emax), 0.0)
    prob = ex / jnp.sum(ex, axis=1, keepdims=True)
    pm = jnp.where(in_grp, prob, -1.0)
    p1 = jnp.max(pm, axis=1, keepdims=True)
    i1 = jnp.min(jnp.where(pm == p1, lane, big), axis=1, keepdims=True)
    pm2 = jnp.where(lane == i1, -1.0, pm)
    p2 = jnp.max(pm2, axis=1, keepdims=True)
    i2 = jnp.min(jnp.where(pm2 == p2, lane, big), axis=1, keepdims=True)
    den = p1 + p2
    c1 = g_w * (p1 / den)
    c2 = g_w * (p2 / den)

    hot1 = lane == i1
    hot2 = lane == i2
    onehot = jnp.where(hot1, 1.0, jnp.where(hot2, 1.0, 0.0))
    tri = (lax.broadcasted_iota(jnp.int32, (tm, tm), 0)
           > lax.broadcasted_iota(jnp.int32, (tm, tm), 1))
    ranks = jnp.dot(jnp.where(tri, 1.0, 0.0).astype(BF16), onehot.astype(BF16),
                    preferred_element_type=F32) + carry_ref[...]
    r1 = jnp.sum(jnp.where(hot1, ranks, 0.0), axis=1, keepdims=True).astype(jnp.int32)
    r2 = jnp.sum(jnp.where(hot2, ranks, 0.0), axis=1, keepdims=True).astype(jnp.int32)
    carry_ref[...] = carry_ref[...] + jnp.sum(onehot, axis=0, keepdims=True)
    cnt_ref[...] = carry_ref[...]

    ri_ref[...] = jnp.where(lane == 0, i1, jnp.where(lane == 1, i2,
                            jnp.where(lane == 2, r1, jnp.where(lane == 3, r2, 0))))
    rw_ref[...] = jnp.where(lane == 0, c1, jnp.where(lane == 1, c2, 0.0))


def _mix(yg, proj, x2d, w_ret_o, w_conv_o, w_out, conv_w, gate1, shift2, scale2, norm2_w,
         w_router, b_router, seq, off_cb, tm):
    n, d = x2d.shape
    per_b = seq // tm
    cblk = off_cb // d
    row = lambda i: (i, 0)
    const2 = lambda i: (0, 0)
    bidx = lambda i: (i // per_b, 0, 0)
    col = lambda k: (lambda i: (i, cblk + k))
    return pl.pallas_call(
        _mix_kernel,
        out_shape=(jax.ShapeDtypeStruct((n, d), F32),
                   jax.ShapeDtypeStruct((n, d), F32),
                   jax.ShapeDtypeStruct((n, LANES), jnp.int32),
                   jax.ShapeDtypeStruct((n, LANES), F32),
                   jax.ShapeDtypeStruct((1, LANES), F32)),
        grid=(n // tm,),
        in_specs=[pl.BlockSpec((tm, RET_V_W), row),
                  pl.BlockSpec((tm, d), col(0)),
                  pl.BlockSpec((tm, d), col(1)),
                  pl.BlockSpec((tm, d), col(2)),
                  pl.BlockSpec((tm, d), col(3)),
                  pl.BlockSpec((tm, d), col(4)),
                  pl.BlockSpec((tm, d), row),
                  pl.BlockSpec((RET_V_W, d), const2),
                  pl.BlockSpec((d, d), const2),
                  pl.BlockSpec((d, d), const2),
                  pl.BlockSpec((3, d), const2),
                  pl.BlockSpec((1, 1, d), bidx),
                  pl.BlockSpec((1, 1, d), bidx),
                  pl.BlockSpec((1, 1, d), bidx),
                  pl.BlockSpec((1, d), const2),
                  pl.BlockSpec((d, LANES), const2),
                  pl.BlockSpec((1, LANES), const2)],
        out_specs=(pl.BlockSpec((tm, d), row),
                   pl.BlockSpec((tm, d), row),
                   pl.BlockSpec((tm, LANES), row),
                   pl.BlockSpec((tm, LANES), row),
                   pl.BlockSpec((1, LANES), const2)),
        scratch_shapes=[pltpu.VMEM((1, LANES), F32)],
        compiler_params=_params(("arbitrary",)),
        name="mix",
    )(yg, proj, proj, proj, proj, proj, x2d, w_ret_o, w_conv_o, w_out, conv_w,
      gate1, shift2, scale2, norm2_w.reshape(1, d), w_router, b_router)


def _row_copy(src_ref, src_row, dst_ref, dst_row, sem):
    return pltpu.make_async_copy(src_ref.at[pl.ds(src_row, 1), :],
                                 dst_ref.at[pl.ds(dst_row, 1), :], sem)


def _scatter_kernel(pos_ref, zt_ref, h_ref, xs_ref, zero_ref, sem, zsem):
    i = pl.program_id(0)
    tm = h_ref.shape[0]
    te = zero_ref.shape[0]
    base = i * tm

    @pl.when(i == 0)
    def _():
        zero_ref[...] = jnp.zeros_like(zero_ref)

        def tile_copy(z):
            row = pl.multiple_of(zt_ref[z] * te, te)
            return pltpu.make_async_copy(zero_ref, xs_ref.at[pl.ds(row, te), :], zsem)

        def zissue(z, carry):
            @pl.when(zt_ref[z] >= 0)
            def _():
                tile_copy(z).start()
            return carry

        def zdrain(z, carry):
            @pl.when(zt_ref[z] >= 0)
            def _():
                tile_copy(z).wait()
            return carry

        lax.fori_loop(0, zt_ref.shape[0], zissue, 0)
        lax.fori_loop(0, zt_ref.shape[0], zdrain, 0)

    def issue(r, carry):
        a = 2 * (base + r)
        _row_copy(h_ref, r, xs_ref, pos_ref[a], sem).start()
        _row_copy(h_ref, r, xs_ref, pos_ref[a + 1], sem).start()
        return carry

    lax.fori_loop(0, tm, issue, 0)

    def drain(r, carry):
        _row_copy(h_ref, 0, xs_ref, 0, sem).wait()
        _row_copy(h_ref, 0, xs_ref, 0, sem).wait()
        return carry

    lax.fori_loop(0, tm, drain, 0)


def _scatter_rows(pos_flat, zero_tiles, h2, n_sorted, tm):
    n, d = h2.shape
    grid_spec = pltpu.PrefetchScalarGridSpec(
        num_scalar_prefetch=2,
        grid=(n // tm,),
        in_specs=[pl.BlockSpec((tm, d), lambda i, pos, zt: (i, 0))],
        out_specs=pl.BlockSpec(memory_space=pl.ANY),
        scratch_shapes=[pltpu.VMEM((EXPERT_TILE, d), F32),
                        pltpu.SemaphoreType.DMA,
                        pltpu.SemaphoreType.DMA],
    )
    return pl.pallas_call(
        _scatter_kernel,
        out_shape=jax.ShapeDtypeStruct((n_sorted, d), F32),
        grid_spec=grid_spec,
        compiler_params=_params(("arbitrary",)),
        name="scatter",
    )(pos_flat, zero_tiles, h2)


def _expert_kernel(te_ref, tf_ref, xs_ref, w13_ref, w2_ref, ys_ref, w13b_ref, w2b_ref):
    t = pl.program_id(0)
    flag = tf_ref[t]

    @pl.when(flag == 2)
    def _():
        w13b_ref[...] = w13_ref[0].astype(BF16)
        w2b_ref[...] = w2_ref[0].astype(BF16)

    @pl.when(flag > 0)
    def _():
        x = xs_ref[...].astype(BF16)
        hid = jnp.dot(x, w13b_ref[...], preferred_element_type=F32)
        a = hid[:, :EXPERT_HIDDEN]
        act = a * jax.nn.sigmoid(a) * hid[:, EXPERT_HIDDEN:]
        ys_ref[...] = jnp.dot(act.astype(BF16), w2b_ref[...], preferred_element_type=F32)

    @pl.when(flag == 0)
    def _():
        ys_ref[...] = jnp.zeros_like(ys_ref)


def _experts(tile_expert, tile_flag, xs, w13, w2):
    n_sorted, d = xs.shape
    te = EXPERT_TILE
    n_tiles = n_sorted // te
    hid2 = w13.shape[2]
    grid_spec = pltpu.PrefetchScalarGridSpec(
        num_scalar_prefetch=2,
        grid=(n_tiles,),
        in_specs=[pl.BlockSpec((te, d), lambda t, e, f: (t, 0)),
                  pl.BlockSpec((1, d, hid2), lambda t, e, f: (e[t], 0, 0)),
                  pl.BlockSpec((1, hid2 // 2, d), lambda t, e, f: (e[t], 0, 0))],
        out_specs=pl.BlockSpec((te, d), lambda t, e, f: (t, 0)),
        scratch_shapes=[pltpu.VMEM((d, hid2), BF16),
                        pltpu.VMEM((hid2 // 2, d), BF16)],
    )
    return pl.pallas_call(
        _expert_kernel,
        out_shape=jax.ShapeDtypeStruct((n_sorted, d), F32),
        grid_spec=grid_spec,
        compiler_params=_params(("arbitrary",)),
        name="experts",
    )(tile_expert, tile_flag, xs, w13, w2)


def _combine_kernel(pos_ref, ys_ref, x1_ref, rw_ref, g2_ref, fw_ref, o_ref, y0_ref, y1_ref, sem):
    i = pl.program_id(0)
    tm = x1_ref.shape[0]
    base = i * tm

    def issue(r, carry):
        a = 2 * (base + r)
        _row_copy(ys_ref, pos_ref[a], y0_ref, r, sem).start()
        _row_copy(ys_ref, pos_ref[a + 1], y1_ref, r, sem).start()
        return carry

    lax.fori_loop(0, tm, issue, 0)

    def drain(r, carry):
        _row_copy(ys_ref, 0, y0_ref, 0, sem).wait()
        _row_copy(ys_ref, 0, y1_ref, 0, sem).wait()
        return carry

    lax.fori_loop(0, tm, drain, 0)

    rw = rw_ref[...]
    moe = rw[:, 0:1] * y0_ref[...] + rw[:, 1:2] * y1_ref[...]
    x2 = x1_ref[...] + g2_ref[0] * moe
    o_ref[...] = x2 * lax.rsqrt(jnp.mean(x2 * x2, axis=-1, keepdims=True) + NORM_EPS) * fw_ref[...]


def _combine(pos_flat, ys, x1, route_w, gate2, final_w, seq, tm):
    n, d = x1.shape
    per_b = seq // tm
    grid_spec = pltpu.PrefetchScalarGridSpec(
        num_scalar_prefetch=1,
        grid=(n // tm,),
        in_specs=[pl.BlockSpec(memory_space=pl.ANY),
                  pl.BlockSpec((tm, d), lambda i, pos: (i, 0)),
                  pl.BlockSpec((tm, LANES), lambda i, pos: (i, 0)),
                  pl.BlockSpec((1, 1, d), lambda i, pos: (i // per_b, 0, 0)),
                  pl.BlockSpec((1, d), lambda i, pos: (0, 0))],
        out_specs=pl.BlockSpec((tm, d), lambda i, pos: (i, 0)),
        scratch_shapes=[pltpu.VMEM((tm, d), F32),
                        pltpu.VMEM((tm, d), F32),
                        pltpu.SemaphoreType.DMA],
    )
    return pl.pallas_call(
        _combine_kernel,
        out_shape=jax.ShapeDtypeStruct((n, d), F32),
        grid_spec=grid_spec,
        compiler_params=_params(("arbitrary",)),
        name="combine",
    )(pos_flat, ys, x1, route_w, gate2, final_w.reshape(1, d))


def _rope_tables(seq):
    pos = jnp.arange(seq)
    nf = RET_QK_DIM // 4
    inv = ROPE_BASE ** (-jnp.arange(nf, dtype=F32) / nf)
    ang_r = (pos // GRID_W).astype(F32)[:, None] * inv[None, :]
    ang_c = (pos % GRID_W).astype(F32)[:, None] * inv[None, :]
    cr, sr, cc, sc = jnp.cos(ang_r), jnp.sin(ang_r), jnp.cos(ang_c), jnp.sin(ang_c)
    return (jnp.concatenate([cr, cr, cc, cc], axis=1),
            jnp.concatenate([-sr, sr, -sc, sc], axis=1))


def _layer(x, c, ctx, c_ctx, w_mod, b_mod, norm1_w, norm2_w, w_in, log_decay, gn_w, w_ret_o,
           conv_w, w_conv_o, w_out, rg_w, rg_b, re_w, re_b, w13, w2, final_w):
    batch, seq, d = x.shape
    ctx_len = ctx.shape[1]
    n = batch * seq
    off_q = 0
    off_k = off_q + RET_QK_W
    off_v = off_k + RET_QK_W
    off_g = off_v + RET_V_W
    off_cb = off_g + RET_V_W

    pad = (-(batch + 1)) % 8
    c_all = jnp.concatenate([c, c_ctx[None, :], jnp.zeros((pad, d), F32)], axis=0)
    mod = _modulation(c_all, w_mod, b_mod)
    mx = mod[:batch].reshape(batch, 1, 6, d)
    shift1, scale1, gate1, shift2, scale2, gate2 = (mx[:, :, k, :] for k in range(6))
    mc = mod[batch].reshape(1, 1, 6, d)
    shift1c, scale1c = mc[:, :, 0, :], mc[:, :, 1, :]

    w_in_b = w_in.astype(BF16)
    cos_t, sin_t = _rope_tables(seq)
    k_scale = RET_QK_DIM ** -0.5

    tm_proj = min(1024, seq)
    proj = _projection(x.reshape(n, d), norm1_w, shift1, scale1, cos_t, sin_t, w_in_b, seq,
                       modes=((True, 1.0), (True, k_scale)), tm=tm_proj)
    tm_ctx = ctx_len
    kv_ctx = _projection(ctx.reshape(batch * ctx_len, d), norm1_w, shift1c, scale1c,
                         cos_t[:tm_ctx], sin_t[:tm_ctx], w_in_b[:, off_k:off_g], ctx_len,
                         modes=((False, k_scale),), tm=tm_ctx)

    log_gamma = jnp.log1p(-jnp.exp(log_decay.astype(F32)))
    yg = _retention(proj, kv_ctx, log_gamma, gn_w, batch, seq, ctx_len, off_q, off_k, off_v, off_g)

    w_router = jnp.concatenate(
        [re_w, rg_w, jnp.zeros((d, LANES - N_EXPERTS - N_GROUPS), F32)], axis=1)
    b_router = jnp.concatenate(
        [re_b, rg_b, jnp.zeros((LANES - N_EXPERTS - N_GROUPS,), F32)]).reshape(1, LANES)
    tm_mix = min(256, seq)
    x1, h2, route_i, route_w, counts = _mix(
        yg, proj, x.reshape(n, d), w_ret_o.astype(BF16), w_conv_o.astype(BF16), w_out.astype(BF16),
        conv_w, gate1, shift2, scale2, norm2_w, w_router, b_router, seq, off_cb, tm_mix)

    te = EXPERT_TILE
    cnt = counts[0, :N_EXPERTS].astype(jnp.int32)
    padded = ((cnt + te - 1) // te) * te
    ends = jnp.cumsum(padded)
    starts = ends - padded
    pos = starts[route_i[:, 0:2]] + route_i[:, 2:4]
    pos_flat = pos.reshape(-1).astype(jnp.int32)
    n_tiles = (2 * n) // te + N_EXPERTS
    tile_row = jnp.arange(n_tiles, dtype=jnp.int32) * te
    tile_expert = jnp.minimum(jnp.searchsorted(ends, tile_row, side='right'),
                              N_EXPERTS - 1).astype(jnp.int32)
    new_expert = jnp.concatenate([jnp.ones((1,), bool), tile_expert[1:] != tile_expert[:-1]])
    tile_flag = jnp.where(tile_row < ends[-1], jnp.where(new_expert, 2, 1), 0).astype(jnp.int32)

    valid_tiles = ends[-1] // te
    last_tile = jnp.where(padded > 0, ends // te - 1, -1)
    tail_tile = valid_tiles + jnp.arange(N_EXPERTS, dtype=jnp.int32)
    tail_tile = jnp.where(tail_tile < n_tiles, tail_tile, -1)
    zero_tiles = jnp.concatenate([last_tile, tail_tile]).astype(jnp.int32)

    xs = _scatter_rows(pos_flat, zero_tiles, h2, n_tiles * te, min(512, seq))
    ys = _experts(tile_expert, tile_flag, xs, w13, w2)
    out = _combine(pos_flat, ys, x1, route_w, gate2, final_w, seq, min(256, seq))
    return out.reshape(batch, seq, d)


def kernel(x, c, ctx, c_ctx, w_mod, b_mod, norm1_w, norm2_w, w_in, ret_log_decay, ret_gn_w, w_ret_o,
           conv_w, w_conv_o, w_out, router_group_w, router_group_b, router_expert_w, router_expert_b,
           expert_w13, expert_w2, final_norm_w):
    assert w_mod.shape[0] == 1, "single-layer problem"
    return _layer(x, c, ctx, c_ctx, w_mod[0], b_mod[0], norm1_w[0], norm2_w[0], w_in[0],
                  ret_log_decay[0], ret_gn_w[0], w_ret_o[0], conv_w[0], w_conv_o[0], w_out[0],
                  router_group_w[0], router_group_b[0], router_expert_w[0], router_expert_b[0],
                  expert_w13[0], expert_w2[0], final_norm_w)
```

```python
import functools

import jax
import jax.numpy as jnp
from jax import lax
from jax.experimental import pallas as pl
from jax.experimental.pallas import tpu as pltpu

GRID_W = 64
RET_HEADS = 4
RET_QK_DIM = 256
RET_V_DIM = 512
RET_QK_W = RET_HEADS * RET_QK_DIM
RET_V_W = RET_HEADS * RET_V_DIM
N_GROUPS = 4
EXPERTS_PER_GROUP = 8
N_EXPERTS = N_GROUPS * EXPERTS_PER_GROUP
EXPERT_HIDDEN = 512
ROPE_BASE = 10000.0
NORM_EPS = 1e-6
GN_EPS = 1e-5

RET_CHUNK = 256
LANES = 128
EXPERT_TILE = 256
VMEM_LIMIT = 56 * 1024 * 1024

BF16 = jnp.bfloat16
F32 = jnp.float32


def _params(sem, vmem=VMEM_LIMIT):
    return pltpu.CompilerParams(dimension_semantics=sem, vmem_limit_bytes=vmem)


def _mod_kernel(c_ref, w_ref, b_ref, o_ref):
    c = c_ref[...]
    s = c * jax.nn.sigmoid(c)
    o_ref[...] = jnp.dot(s.astype(BF16), w_ref[...].astype(BF16),
                         preferred_element_type=F32) + b_ref[...]


def _modulation(c_all, w_mod, b_mod):
    rows, d = c_all.shape
    n_out = w_mod.shape[1]
    tn = 1536
    return pl.pallas_call(
        _mod_kernel,
        out_shape=jax.ShapeDtypeStruct((rows, n_out), F32),
        grid=(n_out // tn,),
        in_specs=[pl.BlockSpec((rows, d), lambda j: (0, 0)),
                  pl.BlockSpec((d, tn), lambda j: (0, j)),
                  pl.BlockSpec((1, tn), lambda j: (0, j))],
        out_specs=pl.BlockSpec((rows, tn), lambda j: (0, j)),
        compiler_params=_params(("parallel",)),
        name="mod",
    )(c_all, w_mod, b_mod.reshape(1, n_out))


def _rope_store(acc, cos, sin, scale, o_ref):
    for g in range(acc.shape[1] // LANES):
        u = acc[:, g * LANES:(g + 1) * LANES]
        t = (g % 2) * LANES
        r = u * cos[:, t:t + LANES] + pltpu.roll(u, LANES // 2, axis=1) * sin[:, t:t + LANES]
        if scale != 1.0:
            r = r * scale
        o_ref[:, g * LANES:(g + 1) * LANES] = r.astype(o_ref.dtype)


def _proj_kernel(x_ref, nw_ref, sh_ref, sc_ref, cos_ref, sin_ref, w_ref, o_ref, h_ref, *, modes):
    j = pl.program_id(1)

    @pl.when(j == 0)
    def _():
        x = x_ref[...]
        y = x * lax.rsqrt(jnp.mean(x * x, axis=-1, keepdims=True) + NORM_EPS) * nw_ref[...]
        h_ref[...] = (y * (1.0 + sc_ref[0]) + sh_ref[0]).astype(BF16)

    acc = jnp.dot(h_ref[...], w_ref[...], preferred_element_type=F32)
    n_special = len(modes)
    for t, (rope, scale) in enumerate(modes):
        @pl.when(j == t)
        def _(rope=rope, scale=scale):
            if rope:
                _rope_store(acc, cos_ref[...], sin_ref[...], scale, o_ref)
            else:
                o_ref[...] = (acc * scale).astype(o_ref.dtype)

    @pl.when(j >= n_special)
    def _():
        o_ref[...] = acc.astype(o_ref.dtype)


def _projection(x2d, norm_w, shift, scale, cos_t, sin_t, w_bf16, seq, modes, tm):
    n, d = x2d.shape
    width = w_bf16.shape[1]
    tn = 1024
    per_b = seq // tm
    nb = shift.shape[0]
    bidx = (lambda i, j: (i // per_b, 0, 0)) if nb > 1 else (lambda i, j: (0, 0, 0))
    return pl.pallas_call(
        functools.partial(_proj_kernel, modes=modes),
        out_shape=jax.ShapeDtypeStruct((n, width), BF16),
        grid=(n // tm, width // tn),
        in_specs=[pl.BlockSpec((tm, d), lambda i, j: (i, 0)),
                  pl.BlockSpec((1, d), lambda i, j: (0, 0)),
                  pl.BlockSpec((1, 1, d), bidx),
                  pl.BlockSpec((1, 1, d), bidx),
                  pl.BlockSpec((tm, RET_QK_DIM), lambda i, j: (i % per_b, 0)),
                  pl.BlockSpec((tm, RET_QK_DIM), lambda i, j: (i % per_b, 0)),
                  pl.BlockSpec((d, tn), lambda i, j: (0, j))],
        out_specs=pl.BlockSpec((tm, tn), lambda i, j: (i, j)),
        scratch_shapes=[pltpu.VMEM((tm, d), BF16)],
        compiler_params=_params(("parallel", "arbitrary")),
        name="proj",
    )(x2d, norm_w.reshape(1, d), shift, scale, cos_t, sin_t, w_bf16)


def _dot_t0(a, b):
    return lax.dot_general(a, b, (((0,), (0,)), ((), ())), preferred_element_type=F32)


def _ret_kernel(lg_ref, q_ref, k_ref, v_ref, g_ref, kc_ref, vc_ref, gn_ref, o_ref,
                acc_ref, sf_ref, sb_ref, dm_ref, *, seq, ctx_len):
    h = pl.program_id(1)
    lgf = lg_ref[0, h]
    lgb = lg_ref[1, h]
    c = RET_CHUNK
    n_chunks = seq // c

    cpos = lax.broadcasted_iota(jnp.int32, (ctx_len, 1), 0).astype(F32)
    kc = kc_ref[...].astype(F32)
    vc = vc_ref[...]
    sf_ref[...] = _dot_t0((kc * jnp.exp((ctx_len - 1.0 - cpos) * lgf)).astype(BF16), vc)
    sb_ref[...] = _dot_t0((kc * jnp.exp(cpos * lgb)).astype(BF16), vc)

    ri = lax.broadcasted_iota(jnp.int32, (c, c), 0)
    ci = lax.broadcasted_iota(jnp.int32, (c, c), 1)
    dist = (ri - ci).astype(F32)
    dm_ref[...] = jnp.exp(jnp.abs(dist) * jnp.where(dist >= 0, lgf, lgb))

    idx = lax.broadcasted_iota(jnp.int32, (c, 1), 0).astype(F32)
    xi_f = jnp.exp((idx + 1.0) * lgf)
    zeta_f = jnp.exp((c - 1.0 - idx) * lgf)
    xi_b = jnp.exp((c - idx) * lgb)
    zeta_b = jnp.exp(idx * lgb)
    cd_f = jnp.exp(jnp.full((1, 1), float(c), F32) * lgf)
    cd_b = jnp.exp(jnp.full((1, 1), float(c), F32) * lgb)

    def fwd(ic, carry):
        rows = pl.ds(pl.multiple_of(ic * c, c), c)
        q = q_ref[rows, :]
        k = k_ref[rows, :]
        v = v_ref[rows, :]
        s = lax.dot_general(q, k, (((1,), (1,)), ((), ())), preferred_element_type=F32)
        intra = jnp.dot((s * dm_ref[...]).astype(BF16), v, preferred_element_type=F32)
        inter = jnp.dot(q, sf_ref[...].astype(BF16), preferred_element_type=F32)
        acc_ref[rows, :] = intra + xi_f * inter
        kz = (k.astype(F32) * zeta_f).astype(BF16)
        sf_ref[...] = cd_f * sf_ref[...] + _dot_t0(kz, v)
        return carry

    lax.fori_loop(0, n_chunks, fwd, 0)

    gn_w = gn_ref[...]

    def bwd(t, carry):
        ic = n_chunks - 1 - t
        rows = pl.ds(pl.multiple_of(ic * c, c), c)
        q = q_ref[rows, :]
        k = k_ref[rows, :]
        v = v_ref[rows, :]
        inter = jnp.dot(q, sb_ref[...].astype(BF16), preferred_element_type=F32)
        y = acc_ref[rows, :] + xi_b * inter
        mu = jnp.mean(y, axis=-1, keepdims=True)
        yc = y - mu
        var = jnp.mean(yc * yc, axis=-1, keepdims=True)
        yn = yc * lax.rsqrt(var + GN_EPS) * gn_w
        g = g_ref[rows, :].astype(F32)
        o_ref[rows, :] = (g * jax.nn.sigmoid(g) * yn).astype(o_ref.dtype)
        kz = (k.astype(F32) * zeta_b).astype(BF16)
        sb_ref[...] = cd_b * sb_ref[...] + _dot_t0(kz, v)
        return carry

    lax.fori_loop(0, n_chunks, bwd, 0)


def _retention(proj, kv_ctx, log_gamma, gn_w, batch, seq, ctx_len, off_q, off_k, off_v, off_g):
    n = proj.shape[0]
    dk, dv = RET_QK_DIM, RET_V_DIM
    grid_spec = pltpu.PrefetchScalarGridSpec(
        num_scalar_prefetch=1,
        grid=(batch, RET_HEADS),
        in_specs=[pl.BlockSpec((seq, dk), lambda b, h, lg: (b, off_q // dk + h)),
                  pl.BlockSpec((seq, dk), lambda b, h, lg: (b, off_k // dk + h)),
                  pl.BlockSpec((seq, dv), lambda b, h, lg: (b, off_v // dv + h)),
                  pl.BlockSpec((seq, dv), lambda b, h, lg: (b, off_g // dv + h)),
                  pl.BlockSpec((ctx_len, dk), lambda b, h, lg: (b, h)),
                  pl.BlockSpec((ctx_len, dv), lambda b, h, lg: (b, RET_QK_W // dv + h)),
                  pl.BlockSpec((1, dv), lambda b, h, lg: (0, h))],
        out_specs=pl.BlockSpec((seq, dv), lambda b, h, lg: (b, h)),
        scratch_shapes=[pltpu.VMEM((seq, dv), F32),
                        pltpu.VMEM((dk, dv), F32),
                        pltpu.VMEM((dk, dv), F32),
                        pltpu.VMEM((RET_CHUNK, RET_CHUNK), F32)],
    )
    return pl.pallas_call(
        functools.partial(_ret_kernel, seq=seq, ctx_len=ctx_len),
        out_shape=jax.ShapeDtypeStruct((n, RET_V_W), BF16),
        grid_spec=grid_spec,
        compiler_params=_params(("parallel", "arbitrary")),
        name="ret",
    )(log_gamma, proj, proj, proj, proj, kv_ctx, kv_ctx, gn_w.reshape(1, RET_V_W))


def _mix_kernel(yg_ref, cb_ref, cc_ref, ch_ref, gr_ref, gc_ref, x_ref, wro_ref, wco_ref, wo_ref,
                cw_ref, g1_ref, sh2_ref, sc2_ref, n2_ref, wr_ref, br_ref,
                x1_ref, h2_ref, ri_ref, rw_ref, cnt_ref, carry_ref):
    i = pl.program_id(0)
    tm = x_ref.shape[0]

    @pl.when(i == 0)
    def _():
        carry_ref[...] = jnp.zeros_like(carry_ref)

    ret_branch = jnp.dot(yg_ref[...], wro_ref[...], preferred_element_type=F32)

    p = cc_ref[...].astype(F32) * ch_ref[...].astype(F32)
    tpos = lax.broadcasted_iota(jnp.int32, (tm, 1), 0) % GRID_W
    prev = jnp.where(tpos != 0, pltpu.roll(p, 1, axis=0), 0.0)
    nxt = jnp.where(tpos != GRID_W - 1, pltpu.roll(p, tm - 1, axis=0), 0.0)
    cw = cw_ref[...]
    u = cw[0:1, :] * prev + cw[1:2, :] * p + cw[2:3, :] * nxt
    conv_in = (cb_ref[...].astype(F32) * u).astype(BF16)
    conv_branch = jnp.dot(conv_in, wco_ref[...], preferred_element_type=F32)

    merged = (jax.nn.sigmoid(gr_ref[...].astype(F32)) * ret_branch
              + jax.nn.sigmoid(gc_ref[...].astype(F32)) * conv_branch)
    mixed = jnp.dot(merged.astype(BF16), wo_ref[...], preferred_element_type=F32)
    x1 = x_ref[...] + g1_ref[0] * mixed
    x1_ref[...] = x1

    y = x1 * lax.rsqrt(jnp.mean(x1 * x1, axis=-1, keepdims=True) + NORM_EPS) * n2_ref[...]
    h2 = y * (1.0 + sc2_ref[0]) + sh2_ref[0]
    h2_ref[...] = h2

    logits = jnp.dot(h2, wr_ref[...], precision=lax.Precision.HIGHEST,
                     preferred_element_type=F32) + br_ref[...]
    lane = lax.broadcasted_iota(jnp.int32, (tm, LANES), 1)
    neg = jnp.float32(-jnp.inf)
    big = jnp.int32(1 << 20)
    is_g = (lane >= N_EXPERTS) & (lane < N_EXPERTS + N_GROUPS)
    gl = jnp.where(is_g, logits, neg)
    gmax = jnp.max(gl, axis=1, keepdims=True)
    gidx = jnp.min(jnp.where(gl == gmax, lane, big), axis=1, keepdims=True) - N_EXPERTS
    gsum = jnp.sum(jnp.where(is_g, jnp.exp(logits - gmax), 0.0), axis=1, keepdims=True)
    g_w = 1.0 / gsum

    in_grp = (lane // EXPERTS_PER_GROUP) == gidx
    el = jnp.where(in_grp, logits, neg)
    emax = jnp.max(el, axis=1, keepdims=True)
    ex = jnp.where(in_grp, jnp.exp(logits - emax), 0.0)
    prob = ex / jnp.sum(ex, axis=1, keepdims=True)
    pm = jnp.where(in_grp, prob, -1.0)
    p1 = jnp.max(pm, axis=1, keepdims=True)
    i1 = jnp.min(jnp.where(pm == p1, lane, big), axis=1, keepdims=True)
    pm2 = jnp.where(lane == i1, -1.0, pm)
    p2 = jnp.max(pm2, axis=1, keepdims=True)
    i2 = jnp.min(jnp.where(pm2 == p2, lane, big), axis=1, keepdims=True)
    den = p1 + p2
    c1 = g_w * (p1 / den)
    c2 = g_w * (p2 / den)

    hot1 = lane == i1
    hot2 = lane == i2
    onehot = jnp.where(hot1, 1.0, jnp.where(hot2, 1.0, 0.0))
    tri = (lax.broadcasted_iota(jnp.int32, (tm, tm), 0)
           > lax.broadcasted_iota(jnp.int32, (tm, tm), 1))
    ranks = jnp.dot(jnp.where(tri, 1.0, 0.0).astype(BF16), onehot.astype(BF16),
                    preferred_element_type=F32) + carry_ref[...]
    r1 = jnp.sum(jnp.where(hot1, ranks, 0.0), axis=1, keepdims=True).astype(jnp.int32)
    r2 = jnp.sum(jnp.where(hot2, ranks, 0.0), axis=1, keepdims=True).astype(jnp.int32)
    carry_ref[...] = carry_ref[...] + jnp.sum(onehot, axis=0, keepdims=True)
    cnt_ref[...] = carry_ref[...]

    ri_ref[...] = jnp.where(lane == 0, i1, jnp.where(lane == 1, i2,
                            jnp.where(lane == 2, r1, jnp.where(lane == 3, r2, 0))))
    rw_ref[...] = jnp.where(lane == 0, c1, jnp.where(lane == 1, c2, 0.0))


def _mix(yg, proj, x2d, w_ret_o, w_conv_o, w_out, conv_w, gate1, shift2, scale2, norm2_w,
         w_router, b_router, seq, off_cb, tm):
    n, d = x2d.shape
    per_b = seq // tm
    cblk = off_cb // d
    row = lambda i: (i, 0)
    const2 = lambda i: (0, 0)
    bidx = lambda i: (i // per_b, 0, 0)
    col = lambda k: (lambda i: (i, cblk + k))
    return pl.pallas_call(
        _mix_kernel,
        out_shape=(jax.ShapeDtypeStruct((n, d), F32),
                   jax.ShapeDtypeStruct((n, d), F32),
                   jax.ShapeDtypeStruct((n, LANES), jnp.int32),
                   jax.ShapeDtypeStruct((n, LANES), F32),
                   jax.ShapeDtypeStruct((1, LANES), F32)),
        grid=(n // tm,),
        in_specs=[pl.BlockSpec((tm, RET_V_W), row),
                  pl.BlockSpec((tm, d), col(0)),
                  pl.BlockSpec((tm, d), col(1)),
                  pl.BlockSpec((tm, d), col(2)),
                  pl.BlockSpec((tm, d), col(3)),
                  pl.BlockSpec((tm, d), col(4)),
                  pl.BlockSpec((tm, d), row),
                  pl.BlockSpec((RET_V_W, d), const2),
                  pl.BlockSpec((d, d), const2),
                  pl.BlockSpec((d, d), const2),
                  pl.BlockSpec((3, d), const2),
                  pl.BlockSpec((1, 1, d), bidx),
                  pl.BlockSpec((1, 1, d), bidx),
                  pl.BlockSpec((1, 1, d), bidx),
                  pl.BlockSpec((1, d), const2),
                  pl.BlockSpec((d, LANES), const2),
                  pl.BlockSpec((1, LANES), const2)],
        out_specs=(pl.BlockSpec((tm, d), row),
                   pl.BlockSpec((tm, d), row),
                   pl.BlockSpec((tm, LANES), row),
                   pl.BlockSpec((tm, LANES), row),
                   pl.BlockSpec((1, LANES), const2)),
        scratch_shapes=[pltpu.VMEM((1, LANES), F32)],
        compiler_params=_params(("arbitrary",)),
        name="mix",
    )(yg, proj, proj, proj, proj, proj, x2d, w_ret_o, w_conv_o, w_out, conv_w,
      gate1, shift2, scale2, norm2_w.reshape(1, d), w_router, b_router)


def _row_copy(src_ref, src_row, dst_ref, dst_row, sem):
    return pltpu.make_async_copy(src_ref.at[pl.ds(src_row, 1), :],
                                 dst_ref.at[pl.ds(dst_row, 1), :], sem)


def _scatter_kernel(pos_ref, zt_ref, h_ref, xs_ref, zero_ref, sem, zsem):
    i = pl.program_id(0)
    tm = h_ref.shape[0]
    te = zero_ref.shape[0]
    base = i * tm

    @pl.when(i == 0)
    def _():
        zero_ref[...] = jnp.zeros_like(zero_ref)

        def tile_copy(z):
            row = pl.multiple_of(zt_ref[z] * te, te)
            return pltpu.make_async_copy(zero_ref, xs_ref.at[pl.ds(row, te), :], zsem)

        def zissue(z, carry):
            @pl.when(zt_ref[z] >= 0)
            def _():
                tile_copy(z).start()
            return carry

        def zdrain(z, carry):
            @pl.when(zt_ref[z] >= 0)
            def _():
                tile_copy(z).wait()
            return carry

        lax.fori_loop(0, zt_ref.shape[0], zissue, 0)
        lax.fori_loop(0, zt_ref.shape[0], zdrain, 0)

    def issue(r, carry):
        a = 2 * (base + r)
        _row_copy(h_ref, r, xs_ref, pos_ref[a], sem).start()
        _row_copy(h_ref, r, xs_ref, pos_ref[a + 1], sem).start()
        return carry

    lax.fori_loop(0, tm, issue, 0, unroll=8)

    for _ in range(2):
        pltpu.make_async_copy(h_ref, xs_ref.at[pl.ds(0, tm), :], sem).wait()


def _scatter_rows(pos_flat, zero_tiles, h2, n_sorted, tm):
    n, d = h2.shape
    grid_spec = pltpu.PrefetchScalarGridSpec(
        num_scalar_prefetch=2,
        grid=(n // tm,),
        in_specs=[pl.BlockSpec((tm, d), lambda i, pos, zt: (i, 0))],
        out_specs=pl.BlockSpec(memory_space=pl.ANY),
        scratch_shapes=[pltpu.VMEM((EXPERT_TILE, d), F32),
                        pltpu.SemaphoreType.DMA,
                        pltpu.SemaphoreType.DMA],
    )
    return pl.pallas_call(
        _scatter_kernel,
        out_shape=jax.ShapeDtypeStruct((n_sorted, d), F32),
        grid_spec=grid_spec,
        compiler_params=_params(("arbitrary",)),
        name="scatter",
    )(pos_flat, zero_tiles, h2)


def _expert_kernel(te_ref, tf_ref, xs_ref, w13_ref, w2_ref, ys_ref, w13b_ref, w2b_ref):
    t = pl.program_id(0)
    flag = tf_ref[t]

    @pl.when(flag == 2)
    def _():
        w13b_ref[...] = w13_ref[0].astype(BF16)
        w2b_ref[...] = w2_ref[0].astype(BF16)

    @pl.when(flag > 0)
    def _():
        x = xs_ref[...].astype(BF16)
        hid = jnp.dot(x, w13b_ref[...], preferred_element_type=F32)
        a = hid[:, :EXPERT_HIDDEN]
        act = a * jax.nn.sigmoid(a) * hid[:, EXPERT_HIDDEN:]
        ys_ref[...] = jnp.dot(act.astype(BF16), w2b_ref[...], preferred_element_type=F32)

    @pl.when(flag == 0)
    def _():
        ys_ref[...] = jnp.zeros_like(ys_ref)


def _experts(tile_expert, tile_flag, xs, w13, w2):
    n_sorted, d = xs.shape
    te = EXPERT_TILE
    n_tiles = n_sorted // te
    hid2 = w13.shape[2]
    grid_spec = pltpu.PrefetchScalarGridSpec(
        num_scalar_prefetch=2,
        grid=(n_tiles,),
        in_specs=[pl.BlockSpec((te, d), lambda t, e, f: (t, 0)),
                  pl.BlockSpec((1, d, hid2), lambda t, e, f: (e[t], 0, 0)),
                  pl.BlockSpec((1, hid2 // 2, d), lambda t, e, f: (e[t], 0, 0))],
        out_specs=pl.BlockSpec((te, d), lambda t, e, f: (t, 0)),
        scratch_shapes=[pltpu.VMEM((d, hid2), BF16),
                        pltpu.VMEM((hid2 // 2, d), BF16)],
    )
    return pl.pallas_call(
        _expert_kernel,
        out_shape=jax.ShapeDtypeStruct((n_sorted, d), F32),
        grid_spec=grid_spec,
        compiler_params=_params(("arbitrary",)),
        name="experts",
    )(tile_expert, tile_flag, xs, w13, w2)


def _combine_kernel(pos_ref, ys_ref, x1_ref, rw_ref, g2_ref, fw_ref, o_ref, y0_ref, y1_ref, sem):
    i = pl.program_id(0)
    tm = x1_ref.shape[0]
    base = i * tm

    def issue(r, carry):
        a = 2 * (base + r)
        _row_copy(ys_ref, pos_ref[a], y0_ref, r, sem).start()
        _row_copy(ys_ref, pos_ref[a + 1], y1_ref, r, sem).start()
        return carry

    lax.fori_loop(0, tm, issue, 0, unroll=8)

    pltpu.make_async_copy(ys_ref.at[pl.ds(0, tm), :], y0_ref, sem).wait()
    pltpu.make_async_copy(ys_ref.at[pl.ds(0, tm), :], y1_ref, sem).wait()

    rw = rw_ref[...]
    moe = rw[:, 0:1] * y0_ref[...] + rw[:, 1:2] * y1_ref[...]
    x2 = x1_ref[...] + g2_ref[0] * moe
    o_ref[...] = x2 * lax.rsqrt(jnp.mean(x2 * x2, axis=-1, keepdims=True) + NORM_EPS) * fw_ref[...]


def _combine(pos_flat, ys, x1, route_w, gate2, final_w, seq, tm):
    n, d = x1.shape
    per_b = seq // tm
    grid_spec = pltpu.PrefetchScalarGridSpec(
        num_scalar_prefetch=1,
        grid=(n // tm,),
        in_specs=[pl.BlockSpec(memory_space=pl.ANY),
                  pl.BlockSpec((tm, d), lambda i, pos: (i, 0)),
                  pl.BlockSpec((tm, LANES), lambda i, pos: (i, 0)),
                  pl.BlockSpec((1, 1, d), lambda i, pos: (i // per_b, 0, 0)),
                  pl.BlockSpec((1, d), lambda i, pos: (0, 0))],
        out_specs=pl.BlockSpec((tm, d), lambda i, pos: (i, 0)),
        scratch_shapes=[pltpu.VMEM((tm, d), F32),
                        pltpu.VMEM((tm, d), F32),
                        pltpu.SemaphoreType.DMA],
    )
    return pl.pallas_call(
        _combine_kernel,
        out_shape=jax.ShapeDtypeStruct((n, d), F32),
        grid_spec=grid_spec,
        compiler_params=_params(("arbitrary",)),
        name="combine",
    )(pos_flat, ys, x1, route_w, gate2, final_w.reshape(1, d))


def _rope_tables(seq):
    pos = jnp.arange(seq)
    nf = RET_QK_DIM // 4
    inv = ROPE_BASE ** (-jnp.arange(nf, dtype=F32) / nf)
    ang_r = (pos // GRID_W).astype(F32)[:, None] * inv[None, :]
    ang_c = (pos % GRID_W).astype(F32)[:, None] * inv[None, :]
    cr, sr, cc, sc = jnp.cos(ang_r), jnp.sin(ang_r), jnp.cos(ang_c), jnp.sin(ang_c)
    return (jnp.concatenate([cr, cr, cc, cc], axis=1),
            jnp.concatenate([-sr, sr, -sc, sc], axis=1))


def _layer(x, c, ctx, c_ctx, w_mod, b_mod, norm1_w, norm2_w, w_in, log_decay, gn_w, w_ret_o,
           conv_w, w_conv_o, w_out, rg_w, rg_b, re_w, re_b, w13, w2, final_w):
    batch, seq, d = x.shape
    ctx_len = ctx.shape[1]
    n = batch * seq
    off_q = 0
    off_k = off_q + RET_QK_W
    off_v = off_k + RET_QK_W
    off_g = off_v + RET_V_W
    off_cb = off_g + RET_V_W

    pad = (-(batch + 1)) % 8
    c_all = jnp.concatenate([c, c_ctx[None, :], jnp.zeros((pad, d), F32)], axis=0)
    mod = _modulation(c_all, w_mod, b_mod)
    mx = mod[:batch].reshape(batch, 1, 6, d)
    shift1, scale1, gate1, shift2, scale2, gate2 = (mx[:, :, k, :] for k in range(6))
    mc = mod[batch].reshape(1, 1, 6, d)
    shift1c, scale1c = mc[:, :, 0, :], mc[:, :, 1, :]

    w_in_b = w_in.astype(BF16)
    cos_t, sin_t = _rope_tables(seq)
    k_scale = RET_QK_DIM ** -0.5

    tm_proj = min(1024, seq)
    proj = _projection(x.reshape(n, d), norm1_w, shift1, scale1, cos_t, sin_t, w_in_b, seq,
                       modes=((True, 1.0), (True, k_scale)), tm=tm_proj)
    tm_ctx = ctx_len
    kv_ctx = _projection(ctx.reshape(batch * ctx_len, d), norm1_w, shift1c, scale1c,
                         cos_t[:tm_ctx], sin_t[:tm_ctx], w_in_b[:, off_k:off_g], ctx_len,
                         modes=((False, k_scale),), tm=tm_ctx)

    log_gamma = jnp.log1p(-jnp.exp(log_decay.astype(F32)))
    yg = _retention(proj, kv_ctx, log_gamma, gn_w, batch, seq, ctx_len, off_q, off_k, off_v, off_g)

    w_router = jnp.concatenate(
        [re_w, rg_w, jnp.zeros((d, LANES - N_EXPERTS - N_GROUPS), F32)], axis=1)
    b_router = jnp.concatenate(
        [re_b, rg_b, jnp.zeros((LANES - N_EXPERTS - N_GROUPS,), F32)]).reshape(1, LANES)
    tm_mix = min(256, seq)
    x1, h2, route_i, route_w, counts = _mix(
        yg, proj, x.reshape(n, d), w_ret_o.astype(BF16), w_conv_o.astype(BF16), w_out.astype(BF16),
        conv_w, gate1, shift2, scale2, norm2_w, w_router, b_router, seq, off_cb, tm_mix)

    te = EXPERT_TILE
    cnt = counts[0, :N_EXPERTS].astype(jnp.int32)
    padded = ((cnt + te - 1) // te) * te
    ends = jnp.cumsum(padded)
    starts = ends - padded
    expert_ids = jnp.arange(N_EXPERTS, dtype=jnp.int32)
    seg_start = jnp.sum(jnp.where(route_i[:, 0:2, None] == expert_ids, starts, 0), axis=-1)
    pos = seg_start + route_i[:, 2:4]
    pos_flat = pos.reshape(-1).astype(jnp.int32)
    n_tiles = (2 * n) // te + N_EXPERTS
    tile_row = jnp.arange(n_tiles, dtype=jnp.int32) * te
    tile_expert = jnp.minimum(jnp.sum((tile_row[:, None] >= ends[None, :]).astype(jnp.int32), axis=1),
                              N_EXPERTS - 1).astype(jnp.int32)
    new_expert = jnp.concatenate([jnp.ones((1,), bool), tile_expert[1:] != tile_expert[:-1]])
    tile_flag = jnp.where(tile_row < ends[-1], jnp.where(new_expert, 2, 1), 0).astype(jnp.int32)

    valid_tiles = ends[-1] // te
    last_tile = jnp.where(padded > 0, ends // te - 1, -1)
    tail_tile = valid_tiles + jnp.arange(N_EXPERTS, dtype=jnp.int32)
    tail_tile = jnp.where(tail_tile < n_tiles, tail_tile, -1)
    zero_tiles = jnp.concatenate([last_tile, tail_tile]).astype(jnp.int32)

    xs = _scatter_rows(pos_flat, zero_tiles, h2, n_tiles * te, min(512, seq))
    ys = _experts(tile_expert, tile_flag, xs, w13, w2)
    out = _combine(pos_flat, ys, x1, route_w, gate2, final_w, seq, min(256, seq))
    return out.reshape(batch, seq, d)


def kernel(x, c, ctx, c_ctx, w_mod, b_mod, norm1_w, norm2_w, w_in, ret_log_decay, ret_gn_w, w_ret_o,
           conv_w, w_conv_o, w_out, router_group_w, router_group_b, router_expert_w, router_expert_b,
           expert_w13, expert_w2, final_norm_w):
    assert w_mod.shape[0] == 1, "single-layer problem"
    return _layer(x, c, ctx, c_ctx, w_mod[0], b_mod[0], norm1_w[0], norm2_w[0], w_in[0],
                  ret_log_decay[0], ret_gn_w[0], w_ret_o[0], conv_w[0], w_conv_o[0], w_out[0],
                  router_group_w[0], router_group_b[0], router_expert_w[0], router_expert_b[0],
                  expert_w13[0], expert_w2[0], final_norm_w)
```

```python
import functools

import jax
import jax.numpy as jnp
from jax import lax
from jax.experimental import pallas as pl
from jax.experimental.pallas import tpu as pltpu

GRID_W = 64
RET_HEADS = 4
RET_QK_DIM = 256
RET_V_DIM = 512
RET_QK_W = RET_HEADS * RET_QK_DIM
RET_V_W = RET_HEADS * RET_V_DIM
N_GROUPS = 4
EXPERTS_PER_GROUP = 8
N_EXPERTS = N_GROUPS * EXPERTS_PER_GROUP
EXPERT_HIDDEN = 512
ROPE_BASE = 10000.0
NORM_EPS = 1e-6
GN_EPS = 1e-5

RET_CHUNK = 256
LANES = 128
EXPERT_TILE = 256
VMEM_LIMIT = 56 * 1024 * 1024

BF16 = jnp.bfloat16
F32 = jnp.float32


def _params(sem, vmem=VMEM_LIMIT):
    return pltpu.CompilerParams(dimension_semantics=sem, vmem_limit_bytes=vmem)


def _mod_kernel(c_ref, w_ref, b_ref, o_ref):
    c = c_ref[...]
    s = c * jax.nn.sigmoid(c)
    o_ref[...] = jnp.dot(s.astype(BF16), w_ref[...].astype(BF16),
                         preferred_element_type=F32) + b_ref[...]


def _modulation(c_all, w_mod, b_mod):
    rows, d = c_all.shape
    n_out = w_mod.shape[1]
    tn = 1536
    return pl.pallas_call(
        _mod_kernel,
        out_shape=jax.ShapeDtypeStruct((rows, n_out), F32),
        grid=(n_out // tn,),
        in_specs=[pl.BlockSpec((rows, d), lambda j: (0, 0)),
                  pl.BlockSpec((d, tn), lambda j: (0, j)),
                  pl.BlockSpec((1, tn), lambda j: (0, j))],
        out_specs=pl.BlockSpec((rows, tn), lambda j: (0, j)),
        compiler_params=_params(("parallel",)),
        name="mod",
    )(c_all, w_mod, b_mod.reshape(1, n_out))


def _rope_store(acc, cos, sin, scale, o_ref):
    for g in range(acc.shape[1] // LANES):
        u = acc[:, g * LANES:(g + 1) * LANES]
        t = (g % 2) * LANES
        r = u * cos[:, t:t + LANES] + pltpu.roll(u, LANES // 2, axis=1) * sin[:, t:t + LANES]
        if scale != 1.0:
            r = r * scale
        o_ref[:, g * LANES:(g + 1) * LANES] = r.astype(o_ref.dtype)


def _proj_kernel(x_ref, nw_ref, sh_ref, sc_ref, cos_ref, sin_ref, w_ref, o_ref, h_ref, *, modes):
    j = pl.program_id(1)

    @pl.when(j == 0)
    def _():
        x = x_ref[...]
        y = x * lax.rsqrt(jnp.mean(x * x, axis=-1, keepdims=True) + NORM_EPS) * nw_ref[...]
        h_ref[...] = (y * (1.0 + sc_ref[0]) + sh_ref[0]).astype(BF16)

    acc = jnp.dot(h_ref[...], w_ref[...], preferred_element_type=F32)
    n_special = len(modes)
    for t, (rope, scale) in enumerate(modes):
        @pl.when(j == t)
        def _(rope=rope, scale=scale):
            if rope:
                _rope_store(acc, cos_ref[...], sin_ref[...], scale, o_ref)
            else:
                o_ref[...] = (acc * scale).astype(o_ref.dtype)

    @pl.when(j >= n_special)
    def _():
        o_ref[...] = acc.astype(o_ref.dtype)


def _projection(x2d, norm_w, shift, scale, cos_t, sin_t, w_bf16, seq, modes, tm):
    n, d = x2d.shape
    width = w_bf16.shape[1]
    tn = 1024
    per_b = seq // tm
    nb = shift.shape[0]
    bidx = (lambda i, j: (i // per_b, 0, 0)) if nb > 1 else (lambda i, j: (0, 0, 0))
    return pl.pallas_call(
        functools.partial(_proj_kernel, modes=modes),
        out_shape=jax.ShapeDtypeStruct((n, width), BF16),
        grid=(n // tm, width // tn),
        in_specs=[pl.BlockSpec((tm, d), lambda i, j: (i, 0)),
                  pl.BlockSpec((1, d), lambda i, j: (0, 0)),
                  pl.BlockSpec((1, 1, d), bidx),
                  pl.BlockSpec((1, 1, d), bidx),
                  pl.BlockSpec((tm, RET_QK_DIM), lambda i, j: (i % per_b, 0)),
                  pl.BlockSpec((tm, RET_QK_DIM), lambda i, j: (i % per_b, 0)),
                  pl.BlockSpec((d, tn), lambda i, j: (0, j))],
        out_specs=pl.BlockSpec((tm, tn), lambda i, j: (i, j)),
        scratch_shapes=[pltpu.VMEM((tm, d), BF16)],
        compiler_params=_params(("parallel", "arbitrary")),
        name="proj",
    )(x2d, norm_w.reshape(1, d), shift, scale, cos_t, sin_t, w_bf16)


def _dot_t0(a, b):
    return lax.dot_general(a, b, (((0,), (0,)), ((), ())), preferred_element_type=F32)


def _ret_kernel(lg_ref, q_ref, k_ref, v_ref, g_ref, kc_ref, vc_ref, gn_ref, o_ref,
                acc_ref, sf_ref, sb_ref, dm_ref, *, seq, ctx_len):
    h = pl.program_id(1)
    lgf = lg_ref[0, h]
    lgb = lg_ref[1, h]
    c = RET_CHUNK
    n_chunks = seq // c

    cpos = lax.broadcasted_iota(jnp.int32, (ctx_len, 1), 0).astype(F32)
    kc = kc_ref[...].astype(F32)
    vc = vc_ref[...]
    sf_ref[...] = _dot_t0((kc * jnp.exp((ctx_len - 1.0 - cpos) * lgf)).astype(BF16), vc)
    sb_ref[...] = _dot_t0((kc * jnp.exp(cpos * lgb)).astype(BF16), vc)

    ri = lax.broadcasted_iota(jnp.int32, (c, c), 0)
    ci = lax.broadcasted_iota(jnp.int32, (c, c), 1)
    dist = (ri - ci).astype(F32)
    dm_ref[...] = jnp.exp(jnp.abs(dist) * jnp.where(dist >= 0, lgf, lgb))

    idx = lax.broadcasted_iota(jnp.int32, (c, 1), 0).astype(F32)
    xi_f = jnp.exp((idx + 1.0) * lgf)
    zeta_f = jnp.exp((c - 1.0 - idx) * lgf)
    xi_b = jnp.exp((c - idx) * lgb)
    zeta_b = jnp.exp(idx * lgb)
    cd_f = jnp.exp(jnp.full((1, 1), float(c), F32) * lgf)
    cd_b = jnp.exp(jnp.full((1, 1), float(c), F32) * lgb)

    gn_w = gn_ref[...]

    def chunk_rows(ic):
        return pl.ds(pl.multiple_of(ic * c, c), c)

    def fwd_part(ic):
        rows = chunk_rows(ic)
        q = q_ref[rows, :]
        k = k_ref[rows, :]
        v = v_ref[rows, :]
        s = lax.dot_general(q, k, (((1,), (1,)), ((), ())), preferred_element_type=F32)
        intra = jnp.dot((s * dm_ref[...]).astype(BF16), v, preferred_element_type=F32)
        inter = jnp.dot(q, sf_ref[...].astype(BF16), preferred_element_type=F32)
        kz = (k.astype(F32) * zeta_f).astype(BF16)
        sf_ref[...] = cd_f * sf_ref[...] + _dot_t0(kz, v)
        return intra + xi_f * inter

    def bwd_part(ic):
        rows = chunk_rows(ic)
        q = q_ref[rows, :]
        k = k_ref[rows, :]
        v = v_ref[rows, :]
        inter = jnp.dot(q, sb_ref[...].astype(BF16), preferred_element_type=F32)
        kz = (k.astype(F32) * zeta_b).astype(BF16)
        sb_ref[...] = cd_b * sb_ref[...] + _dot_t0(kz, v)
        return xi_b * inter

    def finalize(ic, y):
        rows = chunk_rows(ic)
        mu = jnp.mean(y, axis=-1, keepdims=True)
        yc = y - mu
        var = jnp.mean(yc * yc, axis=-1, keepdims=True)
        yn = yc * lax.rsqrt(var + GN_EPS) * gn_w
        g = g_ref[rows, :].astype(F32)
        o_ref[rows, :] = (g * jax.nn.sigmoid(g) * yn).astype(o_ref.dtype)

    half = n_chunks // 2

    def first_half(t, carry):
        acc_ref[chunk_rows(t), :] = fwd_part(t)
        acc_ref[chunk_rows(n_chunks - 1 - t), :] = bwd_part(n_chunks - 1 - t)
        return carry

    def second_half(t, carry):
        finalize(t, acc_ref[chunk_rows(t), :] + fwd_part(t))
        u = n_chunks - 1 - t
        finalize(u, acc_ref[chunk_rows(u), :] + bwd_part(u))
        return carry

    lax.fori_loop(0, half, first_half, 0)
    lax.fori_loop(half, n_chunks, second_half, 0)


def _retention(proj, kv_ctx, log_gamma, gn_w, batch, seq, ctx_len, off_q, off_k, off_v, off_g):
    n = proj.shape[0]
    dk, dv = RET_QK_DIM, RET_V_DIM
    assert seq % (2 * RET_CHUNK) == 0, "the paired scan needs an even number of chunks"
    grid_spec = pltpu.PrefetchScalarGridSpec(
        num_scalar_prefetch=1,
        grid=(batch, RET_HEADS),
        in_specs=[pl.BlockSpec((seq, dk), lambda b, h, lg: (b, off_q // dk + h)),
                  pl.BlockSpec((seq, dk), lambda b, h, lg: (b, off_k // dk + h)),
                  pl.BlockSpec((seq, dv), lambda b, h, lg: (b, off_v // dv + h)),
                  pl.BlockSpec((seq, dv), lambda b, h, lg: (b, off_g // dv + h)),
                  pl.BlockSpec((ctx_len, dk), lambda b, h, lg: (b, h)),
                  pl.BlockSpec((ctx_len, dv), lambda b, h, lg: (b, RET_QK_W // dv + h)),
                  pl.BlockSpec((1, dv), lambda b, h, lg: (0, h))],
        out_specs=pl.BlockSpec((seq, dv), lambda b, h, lg: (b, h)),
        scratch_shapes=[pltpu.VMEM((seq, dv), F32),
                        pltpu.VMEM((dk, dv), F32),
                        pltpu.VMEM((dk, dv), F32),
                        pltpu.VMEM((RET_CHUNK, RET_CHUNK), F32)],
    )
    return pl.pallas_call(
        functools.partial(_ret_kernel, seq=seq, ctx_len=ctx_len),
        out_shape=jax.ShapeDtypeStruct((n, RET_V_W), BF16),
        grid_spec=grid_spec,
        compiler_params=_params(("parallel", "arbitrary")),
        name="ret",
    )(log_gamma, proj, proj, proj, proj, kv_ctx, kv_ctx, gn_w.reshape(1, RET_V_W))


def _mix_kernel(yg_ref, cb_ref, cc_ref, ch_ref, gr_ref, gc_ref, x_ref, wro_ref, wco_ref, wo_ref,
                cw_ref, g1_ref, sh2_ref, sc2_ref, n2_ref, wr_ref, br_ref,
                x1_ref, h2_ref, ri_ref, rw_ref, cnt_ref, carry_ref):
    i = pl.program_id(0)
    tm = x_ref.shape[0]

    @pl.when(i == 0)
    def _():
        carry_ref[...] = jnp.zeros_like(carry_ref)

    ret_branch = jnp.dot(yg_ref[...], wro_ref[...], preferred_element_type=F32)

    p = cc_ref[...].astype(F32) * ch_ref[...].astype(F32)
    tpos = lax.broadcasted_iota(jnp.int32, (tm, 1), 0) % GRID_W
    prev = jnp.where(tpos != 0, pltpu.roll(p, 1, axis=0), 0.0)
    nxt = jnp.where(tpos != GRID_W - 1, pltpu.roll(p, tm - 1, axis=0), 0.0)
    cw = cw_ref[...]
    u = cw[0:1, :] * prev + cw[1:2, :] * p + cw[2:3, :] * nxt
    conv_in = (cb_ref[...].astype(F32) * u).astype(BF16)
    conv_branch = jnp.dot(conv_in, wco_ref[...], preferred_element_type=F32)

    merged = (jax.nn.sigmoid(gr_ref[...].astype(F32)) * ret_branch
              + jax.nn.sigmoid(gc_ref[...].astype(F32)) * conv_branch)
    mixed = jnp.dot(merged.astype(BF16), wo_ref[...], preferred_element_type=F32)
    x1 = x_ref[...] + g1_ref[0] * mixed
    x1_ref[...] = x1

    y = x1 * lax.rsqrt(jnp.mean(x1 * x1, axis=-1, keepdims=True) + NORM_EPS) * n2_ref[...]
    h2 = y * (1.0 + sc2_ref[0]) + sh2_ref[0]
    h2_ref[...] = h2

    h_hi = h2.astype(BF16)
    h_lo = (h2 - h_hi.astype(F32)).astype(BF16)
    hw = jnp.dot(h_hi, wr_ref[...], preferred_element_type=F32)
    lw = jnp.dot(h_lo, wr_ref[:, :LANES], preferred_element_type=F32)
    logits = hw[:, :LANES] + (hw[:, LANES:] + lw) + br_ref[...]
    lane = lax.broadcasted_iota(jnp.int32, (tm, LANES), 1)
    neg = jnp.float32(-jnp.inf)
    big = jnp.int32(1 << 20)
    is_g = (lane >= N_EXPERTS) & (lane < N_EXPERTS + N_GROUPS)
    gl = jnp.where(is_g, logits, neg)
    gmax = jnp.max(gl, axis=1, keepdims=True)
    gidx = jnp.min(jnp.where(gl == gmax, lane, big), axis=1, keepdims=True) - N_EXPERTS
    gsum = jnp.sum(jnp.where(is_g, jnp.exp(logits - gmax), 0.0), axis=1, keepdims=True)
    g_w = 1.0 / gsum

    in_grp = (lane // EXPERTS_PER_GROUP) == gidx
    el = jnp.where(in_grp, logits, neg)
    emax = jnp.max(el, axis=1, keepdims=True)
    ex = jnp.where(in_grp, jnp.exp(logits - emax), 0.0)
    prob = ex / jnp.sum(ex, axis=1, keepdims=True)
    pm = jnp.where(in_grp, prob, -1.0)
    p1 = jnp.max(pm, axis=1, keepdims=True)
    i1 = jnp.min(jnp.where(pm == p1, lane, big), axis=1, keepdims=True)
    pm2 = jnp.where(lane == i1, -1.0, pm)
    p2 = jnp.max(pm2, axis=1, keepdims=True)
    i2 = jnp.min(jnp.where(pm2 == p2, lane, big), axis=1, keepdims=True)
    den = p1 + p2
    c1 = g_w * (p1 / den)
    c2 = g_w * (p2 / den)

    hot1 = lane == i1
    hot2 = lane == i2
    onehot = jnp.where(hot1, 1.0, jnp.where(hot2, 1.0, 0.0))
    tri = (lax.broadcasted_iota(jnp.int32, (tm, tm), 0)
           > lax.broadcasted_iota(jnp.int32, (tm, tm), 1))
    ranks = jnp.dot(jnp.where(tri, 1.0, 0.0).astype(BF16), onehot.astype(BF16),
                    preferred_element_type=F32) + carry_ref[...]
    r1 = jnp.sum(jnp.where(hot1, ranks, 0.0), axis=1, keepdims=True).astype(jnp.int32)
    r2 = jnp.sum(jnp.where(hot2, ranks, 0.0), axis=1, keepdims=True).astype(jnp.int32)
    carry_ref[...] = carry_ref[...] + jnp.sum(onehot, axis=0, keepdims=True)
    cnt_ref[...] = carry_ref[...]

    ri_ref[...] = jnp.where(lane == 0, i1, jnp.where(lane == 1, i2,
                            jnp.where(lane == 2, r1, jnp.where(lane == 3, r2, 0))))
    rw_ref[...] = jnp.where(lane == 0, c1, jnp.where(lane == 1, c2, 0.0))


def _mix(yg, proj, x2d, w_ret_o, w_conv_o, w_out, conv_w, gate1, shift2, scale2, norm2_w,
         w_router, b_router, seq, off_cb, tm):
    n, d = x2d.shape
    per_b = seq // tm
    cblk = off_cb // d
    row = lambda i: (i, 0)
    const2 = lambda i: (0, 0)
    bidx = lambda i: (i // per_b, 0, 0)
    col = lambda k: (lambda i: (i, cblk + k))
    return pl.pallas_call(
        _mix_kernel,
        out_shape=(jax.ShapeDtypeStruct((n, d), F32),
                   jax.ShapeDtypeStruct((n, d), F32),
                   jax.ShapeDtypeStruct((n, LANES), jnp.int32),
                   jax.ShapeDtypeStruct((n, LANES), F32),
                   jax.ShapeDtypeStruct((1, LANES), F32)),
        grid=(n // tm,),
        in_specs=[pl.BlockSpec((tm, RET_V_W), row),
                  pl.BlockSpec((tm, d), col(0)),
                  pl.BlockSpec((tm, d), col(1)),
                  pl.BlockSpec((tm, d), col(2)),
                  pl.BlockSpec((tm, d), col(3)),
                  pl.BlockSpec((tm, d), col(4)),
                  pl.BlockSpec((tm, d), row),
                  pl.BlockSpec((RET_V_W, d), const2, pipeline_mode=pl.Buffered(1)),
                  pl.BlockSpec((d, d), const2, pipeline_mode=pl.Buffered(1)),
                  pl.BlockSpec((d, d), const2, pipeline_mode=pl.Buffered(1)),
                  pl.BlockSpec((3, d), const2),
                  pl.BlockSpec((1, 1, d), bidx),
                  pl.BlockSpec((1, 1, d), bidx),
                  pl.BlockSpec((1, 1, d), bidx),
                  pl.BlockSpec((1, d), const2),
                  pl.BlockSpec((d, 2 * LANES), const2),
                  pl.BlockSpec((1, LANES), const2)],
        out_specs=(pl.BlockSpec((tm, d), row),
                   pl.BlockSpec((tm, d), row),
                   pl.BlockSpec((tm, LANES), row),
                   pl.BlockSpec((tm, LANES), row),
                   pl.BlockSpec((1, LANES), const2)),
        scratch_shapes=[pltpu.VMEM((1, LANES), F32)],
        compiler_params=_params(("arbitrary",)),
        name="mix",
    )(yg, proj, proj, proj, proj, proj, x2d, w_ret_o, w_conv_o, w_out, conv_w,
      gate1, shift2, scale2, norm2_w.reshape(1, d), w_router, b_router)


def _row_copy(src_ref, src_row, dst_ref, dst_row, sem):
    return pltpu.make_async_copy(src_ref.at[pl.ds(src_row, 1), :],
                                 dst_ref.at[pl.ds(dst_row, 1), :], sem)


def _scatter_kernel(pos_ref, zt_ref, h_ref, xs_ref, zero_ref, sem, zsem):
    i = pl.program_id(0)
    tm = h_ref.shape[0]
    te = zero_ref.shape[0]
    base = i * tm

    @pl.when(i == 0)
    def _():
        zero_ref[...] = jnp.zeros_like(zero_ref)

        def tile_copy(z):
            row = pl.multiple_of(zt_ref[z] * te, te)
            return pltpu.make_async_copy(zero_ref, xs_ref.at[pl.ds(row, te), :], zsem)

        def zissue(z, carry):
            @pl.when(zt_ref[z] >= 0)
            def _():
                tile_copy(z).start()
            return carry

        def zdrain(z, carry):
            @pl.when(zt_ref[z] >= 0)
            def _():
                tile_copy(z).wait()
            return carry

        lax.fori_loop(0, zt_ref.shape[0], zissue, 0)
        lax.fori_loop(0, zt_ref.shape[0], zdrain, 0)

    def issue(r, carry):
        a = 2 * (base + r)
        _row_copy(h_ref, r, xs_ref, pos_ref[a], sem).start()
        _row_copy(h_ref, r, xs_ref, pos_ref[a + 1], sem).start()
        return carry

    lax.fori_loop(0, tm, issue, 0, unroll=8)

    for _ in range(2):
        pltpu.make_async_copy(h_ref, xs_ref.at[pl.ds(0, tm), :], sem).wait()


def _scatter_rows(pos_flat, zero_tiles, h2, n_sorted, tm):
    n, d = h2.shape
    grid_spec = pltpu.PrefetchScalarGridSpec(
        num_scalar_prefetch=2,
        grid=(n // tm,),
        in_specs=[pl.BlockSpec((tm, d), lambda i, pos, zt: (i, 0))],
        out_specs=pl.BlockSpec(memory_space=pl.ANY),
        scratch_shapes=[pltpu.VMEM((EXPERT_TILE, d), F32),
                        pltpu.SemaphoreType.DMA,
                        pltpu.SemaphoreType.DMA],
    )
    return pl.pallas_call(
        _scatter_kernel,
        out_shape=jax.ShapeDtypeStruct((n_sorted, d), F32),
        grid_spec=grid_spec,
        compiler_params=_params(("arbitrary",)),
        name="scatter",
    )(pos_flat, zero_tiles, h2)


def _expert_kernel(te_ref, tf_ref, xs_ref, w13_ref, w2_ref, ys_ref, w13b_ref, w2b_ref):
    t = pl.program_id(0)
    flag = tf_ref[t]

    @pl.when(flag == 2)
    def _():
        w13b_ref[...] = w13_ref[0].astype(BF16)
        w2b_ref[...] = w2_ref[0].astype(BF16)

    @pl.when(flag > 0)
    def _():
        x = xs_ref[...].astype(BF16)
        hid = jnp.dot(x, w13b_ref[...], preferred_element_type=F32)
        a = hid[:, :EXPERT_HIDDEN]
        act = a * jax.nn.sigmoid(a) * hid[:, EXPERT_HIDDEN:]
        ys_ref[...] = jnp.dot(act.astype(BF16), w2b_ref[...], preferred_element_type=F32)

    @pl.when(flag == 0)
    def _():
        ys_ref[...] = jnp.zeros_like(ys_ref)


def _experts(tile_expert, tile_flag, xs, w13, w2):
    n_sorted, d = xs.shape
    te = EXPERT_TILE
    n_tiles = n_sorted // te
    hid2 = w13.shape[2]
    grid_spec = pltpu.PrefetchScalarGridSpec(
        num_scalar_prefetch=2,
        grid=(n_tiles,),
        in_specs=[pl.BlockSpec((te, d), lambda t, e, f: (t, 0)),
                  pl.BlockSpec((1, d, hid2), lambda t, e, f: (e[t], 0, 0)),
                  pl.BlockSpec((1, hid2 // 2, d), lambda t, e, f: (e[t], 0, 0))],
        out_specs=pl.BlockSpec((te, d), lambda t, e, f: (t, 0)),
        scratch_shapes=[pltpu.VMEM((d, hid2), BF16),
                        pltpu.VMEM((hid2 // 2, d), BF16)],
    )
    return pl.pallas_call(
        _expert_kernel,
        out_shape=jax.ShapeDtypeStruct((n_sorted, d), F32),
        grid_spec=grid_spec,
        compiler_params=_params(("arbitrary",)),
        name="experts",
    )(tile_expert, tile_flag, xs, w13, w2)


def _combine_kernel(pos_ref, ys_ref, x1_ref, rw_ref, g2_ref, fw_ref, o_ref, y0_ref, y1_ref, sem):
    i = pl.program_id(0)
    tm = x1_ref.shape[0]
    base = i * tm

    def issue(r, carry):
        a = 2 * (base + r)
        _row_copy(ys_ref, pos_ref[a], y0_ref, r, sem).start()
        _row_copy(ys_ref, pos_ref[a + 1], y1_ref, r, sem).start()
        return carry

    lax.fori_loop(0, tm, issue, 0, unroll=8)

    pltpu.make_async_copy(ys_ref.at[pl.ds(0, tm), :], y0_ref, sem).wait()
    pltpu.make_async_copy(ys_ref.at[pl.ds(0, tm), :], y1_ref, sem).wait()

    rw = rw_ref[...]
    moe = rw[:, 0:1] * y0_ref[...] + rw[:, 1:2] * y1_ref[...]
    x2 = x1_ref[...] + g2_ref[0] * moe
    o_ref[...] = x2 * lax.rsqrt(jnp.mean(x2 * x2, axis=-1, keepdims=True) + NORM_EPS) * fw_ref[...]


def _combine(pos_flat, ys, x1, route_w, gate2, final_w, seq, tm):
    n, d = x1.shape
    per_b = seq // tm
    grid_spec = pltpu.PrefetchScalarGridSpec(
        num_scalar_prefetch=1,
        grid=(n // tm,),
        in_specs=[pl.BlockSpec(memory_space=pl.ANY),
                  pl.BlockSpec((tm, d), lambda i, pos: (i, 0)),
                  pl.BlockSpec((tm, LANES), lambda i, pos: (i, 0)),
                  pl.BlockSpec((1, 1, d), lambda i, pos: (i // per_b, 0, 0)),
                  pl.BlockSpec((1, d), lambda i, pos: (0, 0))],
        out_specs=pl.BlockSpec((tm, d), lambda i, pos: (i, 0)),
        scratch_shapes=[pltpu.VMEM((tm, d), F32),
                        pltpu.VMEM((tm, d), F32),
                        pltpu.SemaphoreType.DMA],
    )
    return pl.pallas_call(
        _combine_kernel,
        out_shape=jax.ShapeDtypeStruct((n, d), F32),
        grid_spec=grid_spec,
        compiler_params=_params(("arbitrary",)),
        name="combine",
    )(pos_flat, ys, x1, route_w, gate2, final_w.reshape(1, d))


def _rope_tables(seq):
    pos = jnp.arange(seq)
    nf = RET_QK_DIM // 4
    inv = ROPE_BASE ** (-jnp.arange(nf, dtype=F32) / nf)
    ang_r = (pos // GRID_W).astype(F32)[:, None] * inv[None, :]
    ang_c = (pos % GRID_W).astype(F32)[:, None] * inv[None, :]
    cr, sr, cc, sc = jnp.cos(ang_r), jnp.sin(ang_r), jnp.cos(ang_c), jnp.sin(ang_c)
    return (jnp.concatenate([cr, cr, cc, cc], axis=1),
            jnp.concatenate([-sr, sr, -sc, sc], axis=1))


def _layer(x, c, ctx, c_ctx, w_mod, b_mod, norm1_w, norm2_w, w_in, log_decay, gn_w, w_ret_o,
           conv_w, w_conv_o, w_out, rg_w, rg_b, re_w, re_b, w13, w2, final_w):
    batch, seq, d = x.shape
    ctx_len = ctx.shape[1]
    n = batch * seq
    off_q = 0
    off_k = off_q + RET_QK_W
    off_v = off_k + RET_QK_W
    off_g = off_v + RET_V_W
    off_cb = off_g + RET_V_W

    pad = (-(batch + 1)) % 8
    c_all = jnp.concatenate([c, c_ctx[None, :], jnp.zeros((pad, d), F32)], axis=0)
    mod = _modulation(c_all, w_mod, b_mod)
    mx = mod[:batch].reshape(batch, 1, 6, d)
    shift1, scale1, gate1, shift2, scale2, gate2 = (mx[:, :, k, :] for k in range(6))
    mc = mod[batch].reshape(1, 1, 6, d)
    shift1c, scale1c = mc[:, :, 0, :], mc[:, :, 1, :]

    w_in_b = w_in.astype(BF16)
    cos_t, sin_t = _rope_tables(seq)
    k_scale = RET_QK_DIM ** -0.5

    tm_proj = min(2048, seq)
    proj = _projection(x.reshape(n, d), norm1_w, shift1, scale1, cos_t, sin_t, w_in_b, seq,
                       modes=((True, 1.0), (True, k_scale)), tm=tm_proj)
    tm_ctx = ctx_len
    kv_ctx = _projection(ctx.reshape(batch * ctx_len, d), norm1_w, shift1c, scale1c,
                         cos_t[:tm_ctx], sin_t[:tm_ctx], w_in_b[:, off_k:off_g], ctx_len,
                         modes=((False, k_scale),), tm=tm_ctx)

    log_gamma = jnp.log1p(-jnp.exp(log_decay.astype(F32)))
    yg = _retention(proj, kv_ctx, log_gamma, gn_w, batch, seq, ctx_len, off_q, off_k, off_v, off_g)

    w_router = jnp.concatenate(
        [re_w, rg_w, jnp.zeros((d, LANES - N_EXPERTS - N_GROUPS), F32)], axis=1)
    b_router = jnp.concatenate(
        [re_b, rg_b, jnp.zeros((LANES - N_EXPERTS - N_GROUPS,), F32)]).reshape(1, LANES)
    w_router_hi = w_router.astype(BF16)
    w_router_lo = (w_router - w_router_hi.astype(F32)).astype(BF16)
    w_router = jnp.concatenate([w_router_hi, w_router_lo], axis=1)
    tm_mix = min(512, seq)
    x1, h2, route_i, route_w, counts = _mix(
        yg, proj, x.reshape(n, d), w_ret_o.astype(BF16), w_conv_o.astype(BF16), w_out.astype(BF16),
        conv_w, gate1, shift2, scale2, norm2_w, w_router, b_router, seq, off_cb, tm_mix)

    te = EXPERT_TILE
    cnt = counts[0, :N_EXPERTS].astype(jnp.int32)
    padded = ((cnt + te - 1) // te) * te
    ends = jnp.cumsum(padded)
    starts = ends - padded
    expert_ids = jnp.arange(N_EXPERTS, dtype=jnp.int32)
    seg_start = jnp.sum(jnp.where(route_i[:, 0:2, None] == expert_ids, starts, 0), axis=-1)
    pos = seg_start + route_i[:, 2:4]
    pos_flat = pos.reshape(-1).astype(jnp.int32)
    n_tiles = (2 * n) // te + N_EXPERTS
    tile_row = jnp.arange(n_tiles, dtype=jnp.int32) * te
    tile_expert = jnp.minimum(jnp.sum((tile_row[:, None] >= ends[None, :]).astype(jnp.int32), axis=1),
                              N_EXPERTS - 1).astype(jnp.int32)
    new_expert = jnp.concatenate([jnp.ones((1,), bool), tile_expert[1:] != tile_expert[:-1]])
    tile_flag = jnp.where(tile_row < ends[-1], jnp.where(new_expert, 2, 1), 0).astype(jnp.int32)

    valid_tiles = ends[-1] // te
    last_tile = jnp.where(padded > 0, ends // te - 1, -1)
    tail_tile = valid_tiles + jnp.arange(N_EXPERTS, dtype=jnp.int32)
    tail_tile = jnp.where(tail_tile < n_tiles, tail_tile, -1)
    zero_tiles = jnp.concatenate([last_tile, tail_tile]).astype(jnp.int32)

    xs = _scatter_rows(pos_flat, zero_tiles, h2, n_tiles * te, min(512, seq))
    ys = _experts(tile_expert, tile_flag, xs, w13, w2)
    out = _combine(pos_flat, ys, x1, route_w, gate2, final_w, seq, min(256, seq))
    return out.reshape(batch, seq, d)


def kernel(x, c, ctx, c_ctx, w_mod, b_mod, norm1_w, norm2_w, w_in, ret_log_decay, ret_gn_w, w_ret_o,
           conv_w, w_conv_o, w_out, router_group_w, router_group_b, router_expert_w, router_expert_b,
           expert_w13, expert_w2, final_norm_w):
    assert w_mod.shape[0] == 1, "single-layer problem"
    return _layer(x, c, ctx, c_ctx, w_mod[0], b_mod[0], norm1_w[0], norm2_w[0], w_in[0],
                  ret_log_decay[0], ret_gn_w[0], w_ret_o[0], conv_w[0], w_conv_o[0], w_out[0],
                  router_group_w[0], router_group_b[0], router_expert_w[0], router_expert_b[0],
                  expert_w13[0], expert_w2[0], final_norm_w)
```

```python
import functools

import jax
import jax.numpy as jnp
from jax import lax
from jax.experimental import pallas as pl
from jax.experimental.pallas import tpu as pltpu

GRID_W = 64
RET_HEADS = 4
RET_QK_DIM = 256
RET_V_DIM = 512
RET_QK_W = RET_HEADS * RET_QK_DIM
RET_V_W = RET_HEADS * RET_V_DIM
N_GROUPS = 4
EXPERTS_PER_GROUP = 8
N_EXPERTS = N_GROUPS * EXPERTS_PER_GROUP
EXPERT_HIDDEN = 512
ROPE_BASE = 10000.0
NORM_EPS = 1e-6
GN_EPS = 1e-5

RET_CHUNK = 256
LANES = 128
EXPERT_TILE = 256
VMEM_LIMIT = 56 * 1024 * 1024

BF16 = jnp.bfloat16
F32 = jnp.float32


def _params(sem, vmem=VMEM_LIMIT):
    return pltpu.CompilerParams(dimension_semantics=sem, vmem_limit_bytes=vmem)


def _mod_kernel(c_ref, w_ref, b_ref, o_ref):
    c = c_ref[...]
    s = c * jax.nn.sigmoid(c)
    o_ref[...] = jnp.dot(s.astype(BF16), w_ref[...].astype(BF16),
                         preferred_element_type=F32) + b_ref[...]


def _modulation(c_all, w_mod, b_mod):
    rows, d = c_all.shape
    n_out = w_mod.shape[1]
    tn = 1536
    return pl.pallas_call(
        _mod_kernel,
        out_shape=jax.ShapeDtypeStruct((rows, n_out), F32),
        grid=(n_out // tn,),
        in_specs=[pl.BlockSpec((rows, d), lambda j: (0, 0)),
                  pl.BlockSpec((d, tn), lambda j: (0, j)),
                  pl.BlockSpec((1, tn), lambda j: (0, j))],
        out_specs=pl.BlockSpec((rows, tn), lambda j: (0, j)),
        compiler_params=_params(("parallel",)),
        name="mod",
    )(c_all, w_mod, b_mod.reshape(1, n_out))


def _rope_store(acc, cos, sin, scale, o_ref):
    for g in range(acc.shape[1] // LANES):
        u = acc[:, g * LANES:(g + 1) * LANES]
        t = (g % 2) * LANES
        r = u * cos[:, t:t + LANES] + pltpu.roll(u, LANES // 2, axis=1) * sin[:, t:t + LANES]
        if scale != 1.0:
            r = r * scale
        o_ref[:, g * LANES:(g + 1) * LANES] = r.astype(o_ref.dtype)


def _proj_kernel(x_ref, nw_ref, sh_ref, sc_ref, cos_ref, sin_ref, w_ref, o_ref, h_ref, *, modes):
    j = pl.program_id(1)

    @pl.when(j == 0)
    def _():
        x = x_ref[...]
        y = x * lax.rsqrt(jnp.mean(x * x, axis=-1, keepdims=True) + NORM_EPS) * nw_ref[...]
        h_ref[...] = (y * (1.0 + sc_ref[0]) + sh_ref[0]).astype(BF16)

    acc = jnp.dot(h_ref[...], w_ref[...], preferred_element_type=F32)
    n_special = len(modes)
    for t, (rope, scale) in enumerate(modes):
        @pl.when(j == t)
        def _(rope=rope, scale=scale):
            if rope:
                _rope_store(acc, cos_ref[...], sin_ref[...], scale, o_ref)
            else:
                o_ref[...] = (acc * scale).astype(o_ref.dtype)

    @pl.when(j >= n_special)
    def _():
        o_ref[...] = acc.astype(o_ref.dtype)


def _projection(x2d, norm_w, shift, scale, cos_t, sin_t, w_bf16, seq, modes, tm):
    n, d = x2d.shape
    width = w_bf16.shape[1]
    tn = 1024
    per_b = seq // tm
    nb = shift.shape[0]
    bidx = (lambda i, j: (i // per_b, 0, 0)) if nb > 1 else (lambda i, j: (0, 0, 0))
    return pl.pallas_call(
        functools.partial(_proj_kernel, modes=modes),
        out_shape=jax.ShapeDtypeStruct((n, width), BF16),
        grid=(n // tm, width // tn),
        in_specs=[pl.BlockSpec((tm, d), lambda i, j: (i, 0)),
                  pl.BlockSpec((1, d), lambda i, j: (0, 0)),
                  pl.BlockSpec((1, 1, d), bidx),
                  pl.BlockSpec((1, 1, d), bidx),
                  pl.BlockSpec((tm, RET_QK_DIM), lambda i, j: (i % per_b, 0)),
                  pl.BlockSpec((tm, RET_QK_DIM), lambda i, j: (i % per_b, 0)),
                  pl.BlockSpec((d, tn), lambda i, j: (0, j))],
        out_specs=pl.BlockSpec((tm, tn), lambda i, j: (i, j)),
        scratch_shapes=[pltpu.VMEM((tm, d), BF16)],
        compiler_params=_params(("parallel", "arbitrary")),
        name="proj",
    )(x2d, norm_w.reshape(1, d), shift, scale, cos_t, sin_t, w_bf16)


def _dot_t0(a, b):
    return lax.dot_general(a, b, (((0,), (0,)), ((), ())), preferred_element_type=F32)


def _ret_kernel(lg_ref, q_ref, k_ref, v_ref, g_ref, kc_ref, vc_ref, gn_ref, o_ref,
                acc_ref, sf_ref, sb_ref, dm_ref, *, seq, ctx_len):
    h = pl.program_id(1)
    lgf = lg_ref[0, h]
    lgb = lg_ref[1, h]
    c = RET_CHUNK
    n_chunks = seq // c

    cpos = lax.broadcasted_iota(jnp.int32, (ctx_len, 1), 0).astype(F32)
    kc = kc_ref[...].astype(F32)
    vc = vc_ref[...]
    sf_ref[...] = _dot_t0((kc * jnp.exp((ctx_len - 1.0 - cpos) * lgf)).astype(BF16), vc)
    sb_ref[...] = _dot_t0((kc * jnp.exp(cpos * lgb)).astype(BF16), vc)

    ri = lax.broadcasted_iota(jnp.int32, (c, c), 0)
    ci = lax.broadcasted_iota(jnp.int32, (c, c), 1)
    dist = (ri - ci).astype(F32)
    dm_ref[...] = jnp.exp(jnp.abs(dist) * jnp.where(dist >= 0, lgf, lgb))

    idx = lax.broadcasted_iota(jnp.int32, (c, 1), 0).astype(F32)
    xi_f = jnp.exp((idx + 1.0) * lgf)
    zeta_f = jnp.exp((c - 1.0 - idx) * lgf)
    xi_b = jnp.exp((c - idx) * lgb)
    zeta_b = jnp.exp(idx * lgb)
    cd_f = jnp.exp(jnp.full((1, 1), float(c), F32) * lgf)
    cd_b = jnp.exp(jnp.full((1, 1), float(c), F32) * lgb)

    gn_w = gn_ref[...]

    def chunk_rows(ic):
        return pl.ds(pl.multiple_of(ic * c, c), c)

    def fwd_part(ic):
        rows = chunk_rows(ic)
        q = q_ref[rows, :]
        k = k_ref[rows, :]
        v = v_ref[rows, :]
        s = lax.dot_general(q, k, (((1,), (1,)), ((), ())), preferred_element_type=F32)
        intra = jnp.dot((s * dm_ref[...]).astype(BF16), v, preferred_element_type=F32)
        inter = jnp.dot(q, sf_ref[...].astype(BF16), preferred_element_type=F32)
        kz = (k.astype(F32) * zeta_f).astype(BF16)
        sf_ref[...] = cd_f * sf_ref[...] + _dot_t0(kz, v)
        return intra + xi_f * inter

    def bwd_part(ic):
        rows = chunk_rows(ic)
        q = q_ref[rows, :]
        k = k_ref[rows, :]
        v = v_ref[rows, :]
        inter = jnp.dot(q, sb_ref[...].astype(BF16), preferred_element_type=F32)
        kz = (k.astype(F32) * zeta_b).astype(BF16)
        sb_ref[...] = cd_b * sb_ref[...] + _dot_t0(kz, v)
        return xi_b * inter

    def finalize(ic, y):
        rows = chunk_rows(ic)
        mu = jnp.mean(y, axis=-1, keepdims=True)
        yc = y - mu
        var = jnp.mean(yc * yc, axis=-1, keepdims=True)
        yn = yc * lax.rsqrt(var + GN_EPS) * gn_w
        g = g_ref[rows, :].astype(F32)
        o_ref[rows, :] = (g * jax.nn.sigmoid(g) * yn).astype(o_ref.dtype)

    half = n_chunks // 2

    def first_half(t, carry):
        acc_ref[chunk_rows(t), :] = fwd_part(t)
        acc_ref[chunk_rows(n_chunks - 1 - t), :] = bwd_part(n_chunks - 1 - t)
        return carry

    def second_half(t, carry):
        finalize(t, acc_ref[chunk_rows(t), :] + fwd_part(t))
        u = n_chunks - 1 - t
        finalize(u, acc_ref[chunk_rows(u), :] + bwd_part(u))
        return carry

    lax.fori_loop(0, half, first_half, 0)
    lax.fori_loop(half, n_chunks, second_half, 0)


def _retention(proj, kv_ctx, log_gamma, gn_w, batch, seq, ctx_len, off_q, off_k, off_v, off_g):
    n = proj.shape[0]
    dk, dv = RET_QK_DIM, RET_V_DIM
    assert seq % (2 * RET_CHUNK) == 0, "the paired scan needs an even number of chunks"
    grid_spec = pltpu.PrefetchScalarGridSpec(
        num_scalar_prefetch=1,
        grid=(batch, RET_HEADS),
        in_specs=[pl.BlockSpec((seq, dk), lambda b, h, lg: (b, off_q // dk + h)),
                  pl.BlockSpec((seq, dk), lambda b, h, lg: (b, off_k // dk + h)),
                  pl.BlockSpec((seq, dv), lambda b, h, lg: (b, off_v // dv + h)),
                  pl.BlockSpec((seq, dv), lambda b, h, lg: (b, off_g // dv + h)),
                  pl.BlockSpec((ctx_len, dk), lambda b, h, lg: (b, h)),
                  pl.BlockSpec((ctx_len, dv), lambda b, h, lg: (b, RET_QK_W // dv + h)),
                  pl.BlockSpec((1, dv), lambda b, h, lg: (0, h))],
        out_specs=pl.BlockSpec((seq, dv), lambda b, h, lg: (b, h)),
        scratch_shapes=[pltpu.VMEM((seq, dv), F32),
                        pltpu.VMEM((dk, dv), F32),
                        pltpu.VMEM((dk, dv), F32),
                        pltpu.VMEM((RET_CHUNK, RET_CHUNK), F32)],
    )
    return pl.pallas_call(
        functools.partial(_ret_kernel, seq=seq, ctx_len=ctx_len),
        out_shape=jax.ShapeDtypeStruct((n, RET_V_W), BF16),
        grid_spec=grid_spec,
        compiler_params=_params(("parallel", "arbitrary")),
        name="ret",
    )(log_gamma, proj, proj, proj, proj, kv_ctx, kv_ctx, gn_w.reshape(1, RET_V_W))


def _mix_kernel(yg_ref, cb_ref, cc_ref, ch_ref, gr_ref, gc_ref, x_ref, wro_ref, wco_ref, wo_ref,
                cw_ref, g1_ref, sh2_ref, sc2_ref, n2_ref, wr_ref, br_ref,
                x1_ref, h2_ref, ri_ref, rw_ref, cnt_ref, carry_ref):
    i = pl.program_id(0)
    tm = x_ref.shape[0]

    @pl.when(i == 0)
    def _():
        carry_ref[...] = jnp.zeros_like(carry_ref)

    ret_branch = jnp.dot(yg_ref[...], wro_ref[...], preferred_element_type=F32)

    p = cc_ref[...].astype(F32) * ch_ref[...].astype(F32)
    tpos = lax.broadcasted_iota(jnp.int32, (tm, 1), 0) % GRID_W
    prev = jnp.where(tpos != 0, pltpu.roll(p, 1, axis=0), 0.0)
    nxt = jnp.where(tpos != GRID_W - 1, pltpu.roll(p, tm - 1, axis=0), 0.0)
    cw = cw_ref[...]
    u = cw[0:1, :] * prev + cw[1:2, :] * p + cw[2:3, :] * nxt
    conv_in = (cb_ref[...].astype(F32) * u).astype(BF16)
    conv_branch = jnp.dot(conv_in, wco_ref[...], preferred_element_type=F32)

    merged = (jax.nn.sigmoid(gr_ref[...].astype(F32)) * ret_branch
              + jax.nn.sigmoid(gc_ref[...].astype(F32)) * conv_branch)
    mixed = jnp.dot(merged.astype(BF16), wo_ref[...], preferred_element_type=F32)
    x1 = x_ref[...] + g1_ref[0] * mixed
    x1_ref[...] = x1

    y = x1 * lax.rsqrt(jnp.mean(x1 * x1, axis=-1, keepdims=True) + NORM_EPS) * n2_ref[...]
    h2 = y * (1.0 + sc2_ref[0]) + sh2_ref[0]
    h2_ref[...] = h2

    h_hi = h2.astype(BF16)
    h_lo = (h2 - h_hi.astype(F32)).astype(BF16)
    hw = jnp.dot(h_hi, wr_ref[...], preferred_element_type=F32)
    lw = jnp.dot(h_lo, wr_ref[:, :LANES], preferred_element_type=F32)
    logits = hw[:, :LANES] + (hw[:, LANES:] + lw) + br_ref[...]
    lane = lax.broadcasted_iota(jnp.int32, (tm, LANES), 1)
    neg = jnp.float32(-jnp.inf)
    big = jnp.int32(1 << 20)
    is_g = (lane >= N_EXPERTS) & (lane < N_EXPERTS + N_GROUPS)
    gl = jnp.where(is_g, logits, neg)
    gmax = jnp.max(gl, axis=1, keepdims=True)
    gidx = jnp.min(jnp.where(gl == gmax, lane, big), axis=1, keepdims=True) - N_EXPERTS
    gsum = jnp.sum(jnp.where(is_g, jnp.exp(logits - gmax), 0.0), axis=1, keepdims=True)
    g_w = 1.0 / gsum

    in_grp = (lane // EXPERTS_PER_GROUP) == gidx
    el = jnp.where(in_grp, logits, neg)
    emax = jnp.max(el, axis=1, keepdims=True)
    ex = jnp.where(in_grp, jnp.exp(logits - emax), 0.0)
    prob = ex / jnp.sum(ex, axis=1, keepdims=True)
    pm = jnp.where(in_grp, prob, -1.0)
    p1 = jnp.max(pm, axis=1, keepdims=True)
    i1 = jnp.min(jnp.where(pm == p1, lane, big), axis=1, keepdims=True)
    pm2 = jnp.where(lane == i1, -1.0, pm)
    p2 = jnp.max(pm2, axis=1, keepdims=True)
    i2 = jnp.min(jnp.where(pm2 == p2, lane, big), axis=1, keepdims=True)
    den = p1 + p2
    c1 = g_w * (p1 / den)
    c2 = g_w * (p2 / den)

    hot1 = lane == i1
    hot2 = lane == i2
    onehot = jnp.where(hot1, 1.0, jnp.where(hot2, 1.0, 0.0))
    tri = (lax.broadcasted_iota(jnp.int32, (tm, tm), 0)
           > lax.broadcasted_iota(jnp.int32, (tm, tm), 1))
    ranks = jnp.dot(jnp.where(tri, 1.0, 0.0).astype(BF16), onehot.astype(BF16),
                    preferred_element_type=F32) + carry_ref[...]
    r1 = jnp.sum(jnp.where(hot1, ranks, 0.0), axis=1, keepdims=True).astype(jnp.int32)
    r2 = jnp.sum(jnp.where(hot2, ranks, 0.0), axis=1, keepdims=True).astype(jnp.int32)
    carry_ref[...] = carry_ref[...] + jnp.sum(onehot, axis=0, keepdims=True)
    cnt_ref[...] = carry_ref[...]

    ri_ref[...] = jnp.where(lane == 0, i1, jnp.where(lane == 1, i2,
                            jnp.where(lane == 2, r1, jnp.where(lane == 3, r2, 0))))
    rw_ref[...] = jnp.where(lane == 0, c1, jnp.where(lane == 1, c2, 0.0))


def _mix(yg, proj, x2d, w_ret_o, w_conv_o, w_out, conv_w, gate1, shift2, scale2, norm2_w,
         w_router, b_router, seq, off_cb, tm):
    n, d = x2d.shape
    per_b = seq // tm
    cblk = off_cb // d
    row = lambda i: (i, 0)
    const2 = lambda i: (0, 0)
    bidx = lambda i: (i // per_b, 0, 0)
    col = lambda k: (lambda i: (i, cblk + k))
    return pl.pallas_call(
        _mix_kernel,
        out_shape=(jax.ShapeDtypeStruct((n, d), F32),
                   jax.ShapeDtypeStruct((n, d), F32),
                   jax.ShapeDtypeStruct((n, LANES), jnp.int32),
                   jax.ShapeDtypeStruct((n, LANES), F32),
                   jax.ShapeDtypeStruct((1, LANES), F32)),
        grid=(n // tm,),
        in_specs=[pl.BlockSpec((tm, RET_V_W), row),
                  pl.BlockSpec((tm, d), col(0)),
                  pl.BlockSpec((tm, d), col(1)),
                  pl.BlockSpec((tm, d), col(2)),
                  pl.BlockSpec((tm, d), col(3)),
                  pl.BlockSpec((tm, d), col(4)),
                  pl.BlockSpec((tm, d), row),
                  pl.BlockSpec((RET_V_W, d), const2, pipeline_mode=pl.Buffered(1)),
                  pl.BlockSpec((d, d), const2, pipeline_mode=pl.Buffered(1)),
                  pl.BlockSpec((d, d), const2, pipeline_mode=pl.Buffered(1)),
                  pl.BlockSpec((3, d), const2),
                  pl.BlockSpec((1, 1, d), bidx),
                  pl.BlockSpec((1, 1, d), bidx),
                  pl.BlockSpec((1, 1, d), bidx),
                  pl.BlockSpec((1, d), const2),
                  pl.BlockSpec((d, 2 * LANES), const2),
                  pl.BlockSpec((1, LANES), const2)],
        out_specs=(pl.BlockSpec((tm, d), row),
                   pl.BlockSpec((tm, d), row),
                   pl.BlockSpec((tm, LANES), row),
                   pl.BlockSpec((tm, LANES), row),
                   pl.BlockSpec((1, LANES), const2)),
        scratch_shapes=[pltpu.VMEM((1, LANES), F32)],
        compiler_params=_params(("arbitrary",)),
        name="mix",
    )(yg, proj, proj, proj, proj, proj, x2d, w_ret_o, w_conv_o, w_out, conv_w,
      gate1, shift2, scale2, norm2_w.reshape(1, d), w_router, b_router)


def _row_copy(src_ref, src_row, dst_ref, dst_row, sem):
    return pltpu.make_async_copy(src_ref.at[pl.ds(src_row, 1), :],
                                 dst_ref.at[pl.ds(dst_row, 1), :], sem)


def _scatter_kernel(pos_ref, zt_ref, h_ref, xs_ref, zero_ref, sem, zsem):
    i = pl.program_id(0)
    tm = h_ref.shape[0]
    te = zero_ref.shape[0]
    base = i * tm

    @pl.when(i == 0)
    def _():
        zero_ref[...] = jnp.zeros_like(zero_ref)

        def tile_copy(z):
            row = pl.multiple_of(zt_ref[z] * te, te)
            return pltpu.make_async_copy(zero_ref, xs_ref.at[pl.ds(row, te), :], zsem)

        def zissue(z, carry):
            @pl.when(zt_ref[z] >= 0)
            def _():
                tile_copy(z).start()
            return carry

        def zdrain(z, carry):
            @pl.when(zt_ref[z] >= 0)
            def _():
                tile_copy(z).wait()
            return carry

        lax.fori_loop(0, zt_ref.shape[0], zissue, 0)
        lax.fori_loop(0, zt_ref.shape[0], zdrain, 0)

    def issue(r, carry):
        a = 2 * (base + r)
        _row_copy(h_ref, r, xs_ref, pos_ref[a], sem).start(priority=0)
        _row_copy(h_ref, r, xs_ref, pos_ref[a + 1], sem).start(priority=1)
        return carry

    lax.fori_loop(0, tm, issue, 0, unroll=8)

    for _ in range(2):
        pltpu.make_async_copy(h_ref, xs_ref.at[pl.ds(0, tm), :], sem).wait()


def _scatter_rows(pos_flat, zero_tiles, h2, n_sorted, tm):
    n, d = h2.shape
    grid_spec = pltpu.PrefetchScalarGridSpec(
        num_scalar_prefetch=2,
        grid=(n // tm,),
        in_specs=[pl.BlockSpec((tm, d), lambda i, pos, zt: (i, 0))],
        out_specs=pl.BlockSpec(memory_space=pl.ANY),
        scratch_shapes=[pltpu.VMEM((EXPERT_TILE, d), F32),
                        pltpu.SemaphoreType.DMA,
                        pltpu.SemaphoreType.DMA],
    )
    return pl.pallas_call(
        _scatter_kernel,
        out_shape=jax.ShapeDtypeStruct((n_sorted, d), F32),
        grid_spec=grid_spec,
        compiler_params=_params(("arbitrary",)),
        name="scatter",
    )(pos_flat, zero_tiles, h2)


def _expert_kernel(tf_ref, te_ref, tn_ref, ts_ref, xs_ref, w13_ref, w2_ref, ys_ref,
                   w13f_ref, w2f_ref, w13b_ref, w2b_ref, sem13, sem2):
    t = pl.program_id(0)
    flag = tf_ref[t]

    def weight_copies(e, slot):
        return (pltpu.make_async_copy(w13_ref.at[e], w13f_ref.at[slot], sem13.at[slot]),
                pltpu.make_async_copy(w2_ref.at[e], w2f_ref.at[slot], sem2.at[slot]))

    @pl.when(t == 0)
    def _():
        for cp in weight_copies(te_ref[0], ts_ref[0]):
            cp.start()

    @pl.when(flag == 2)
    def _():
        slot = ts_ref[t]
        for cp in weight_copies(te_ref[t], slot):
            cp.wait()
        w13b_ref[...] = w13f_ref[slot].astype(BF16)
        w2b_ref[...] = w2f_ref[slot].astype(BF16)

        @pl.when(tn_ref[t] >= 0)
        def _():
            for cp in weight_copies(tn_ref[t], 1 - slot):
                cp.start()

    @pl.when(flag > 0)
    def _():
        x = xs_ref[...].astype(BF16)
        hid = jnp.dot(x, w13b_ref[...], preferred_element_type=F32)
        a = hid[:, :EXPERT_HIDDEN]
        act = a * jax.nn.sigmoid(a) * hid[:, EXPERT_HIDDEN:]
        ys_ref[...] = jnp.dot(act.astype(BF16), w2b_ref[...], preferred_element_type=F32)

    @pl.when(flag == 0)
    def _():
        ys_ref[...] = jnp.zeros_like(ys_ref)


def _experts(tile_flag, tile_expert, tile_next, tile_slot, xs, w13, w2):
    n_sorted, d = xs.shape
    te = EXPERT_TILE
    n_tiles = n_sorted // te
    hid2 = w13.shape[2]
    grid_spec = pltpu.PrefetchScalarGridSpec(
        num_scalar_prefetch=4,
        grid=(n_tiles,),
        in_specs=[pl.BlockSpec((te, d), lambda t, *_: (t, 0)),
                  pl.BlockSpec(memory_space=pl.ANY),
                  pl.BlockSpec(memory_space=pl.ANY)],
        out_specs=pl.BlockSpec((te, d), lambda t, *_: (t, 0)),
        scratch_shapes=[pltpu.VMEM((2, d, hid2), F32),
                        pltpu.VMEM((2, hid2 // 2, d), F32),
                        pltpu.VMEM((d, hid2), BF16),
                        pltpu.VMEM((hid2 // 2, d), BF16),
                        pltpu.SemaphoreType.DMA((2,)),
                        pltpu.SemaphoreType.DMA((2,))],
    )
    return pl.pallas_call(
        _expert_kernel,
        out_shape=jax.ShapeDtypeStruct((n_sorted, d), F32),
        grid_spec=grid_spec,
        compiler_params=_params(("arbitrary",)),
        name="experts",
    )(tile_flag, tile_expert, tile_next, tile_slot, xs, w13, w2)


def _combine_kernel(pos_ref, ys_ref, x1_ref, rw_ref, g2_ref, fw_ref, o_ref, y0_ref, y1_ref, sem):
    i = pl.program_id(0)
    tm = x1_ref.shape[0]
    base = i * tm

    def issue(r, carry):
        a = 2 * (base + r)
        _row_copy(ys_ref, pos_ref[a], y0_ref, r, sem).start(priority=0)
        _row_copy(ys_ref, pos_ref[a + 1], y1_ref, r, sem).start(priority=1)
        return carry

    lax.fori_loop(0, tm, issue, 0, unroll=8)

    pltpu.make_async_copy(ys_ref.at[pl.ds(0, tm), :], y0_ref, sem).wait()
    pltpu.make_async_copy(ys_ref.at[pl.ds(0, tm), :], y1_ref, sem).wait()

    rw = rw_ref[...]
    moe = rw[:, 0:1] * y0_ref[...] + rw[:, 1:2] * y1_ref[...]
    x2 = x1_ref[...] + g2_ref[0] * moe
    o_ref[...] = x2 * lax.rsqrt(jnp.mean(x2 * x2, axis=-1, keepdims=True) + NORM_EPS) * fw_ref[...]


def _combine(pos_flat, ys, x1, route_w, gate2, final_w, seq, tm):
    n, d = x1.shape
    per_b = seq // tm
    grid_spec = pltpu.PrefetchScalarGridSpec(
        num_scalar_prefetch=1,
        grid=(n // tm,),
        in_specs=[pl.BlockSpec(memory_space=pl.ANY),
                  pl.BlockSpec((tm, d), lambda i, pos: (i, 0)),
                  pl.BlockSpec((tm, LANES), lambda i, pos: (i, 0)),
                  pl.BlockSpec((1, 1, d), lambda i, pos: (i // per_b, 0, 0)),
                  pl.BlockSpec((1, d), lambda i, pos: (0, 0))],
        out_specs=pl.BlockSpec((tm, d), lambda i, pos: (i, 0)),
        scratch_shapes=[pltpu.VMEM((tm, d), F32),
                        pltpu.VMEM((tm, d), F32),
                        pltpu.SemaphoreType.DMA],
    )
    return pl.pallas_call(
        _combine_kernel,
        out_shape=jax.ShapeDtypeStruct((n, d), F32),
        grid_spec=grid_spec,
        compiler_params=_params(("arbitrary",)),
        name="combine",
    )(pos_flat, ys, x1, route_w, gate2, final_w.reshape(1, d))


def _rope_tables(seq):
    pos = jnp.arange(seq)
    nf = RET_QK_DIM // 4
    inv = ROPE_BASE ** (-jnp.arange(nf, dtype=F32) / nf)
    ang_r = (pos // GRID_W).astype(F32)[:, None] * inv[None, :]
    ang_c = (pos % GRID_W).astype(F32)[:, None] * inv[None, :]
    cr, sr, cc, sc = jnp.cos(ang_r), jnp.sin(ang_r), jnp.cos(ang_c), jnp.sin(ang_c)
    return (jnp.concatenate([cr, cr, cc, cc], axis=1),
            jnp.concatenate([-sr, sr, -sc, sc], axis=1))


def _layer(x, c, ctx, c_ctx, w_mod, b_mod, norm1_w, norm2_w, w_in, log_decay, gn_w, w_ret_o,
           conv_w, w_conv_o, w_out, rg_w, rg_b, re_w, re_b, w13, w2, final_w):
    batch, seq, d = x.shape
    ctx_len = ctx.shape[1]
    n = batch * seq
    off_q = 0
    off_k = off_q + RET_QK_W
    off_v = off_k + RET_QK_W
    off_g = off_v + RET_V_W
    off_cb = off_g + RET_V_W

    pad = (-(batch + 1)) % 8
    c_all = jnp.concatenate([c, c_ctx[None, :], jnp.zeros((pad, d), F32)], axis=0)
    mod = _modulation(c_all, w_mod, b_mod)
    mx = mod[:batch].reshape(batch, 1, 6, d)
    shift1, scale1, gate1, shift2, scale2, gate2 = (mx[:, :, k, :] for k in range(6))
    mc = mod[batch].reshape(1, 1, 6, d)
    shift1c, scale1c = mc[:, :, 0, :], mc[:, :, 1, :]

    w_in_b = w_in.astype(BF16)
    cos_t, sin_t = _rope_tables(seq)
    k_scale = RET_QK_DIM ** -0.5

    tm_proj = min(2048, seq)
    proj = _projection(x.reshape(n, d), norm1_w, shift1, scale1, cos_t, sin_t, w_in_b, seq,
                       modes=((True, 1.0), (True, k_scale)), tm=tm_proj)
    tm_ctx = ctx_len
    kv_ctx = _projection(ctx.reshape(batch * ctx_len, d), norm1_w, shift1c, scale1c,
                         cos_t[:tm_ctx], sin_t[:tm_ctx], w_in_b[:, off_k:off_g], ctx_len,
                         modes=((False, k_scale),), tm=tm_ctx)

    log_gamma = jnp.log1p(-jnp.exp(log_decay.astype(F32)))
    yg = _retention(proj, kv_ctx, log_gamma, gn_w, batch, seq, ctx_len, off_q, off_k, off_v, off_g)

    w_router = jnp.concatenate(
        [re_w, rg_w, jnp.zeros((d, LANES - N_EXPERTS - N_GROUPS), F32)], axis=1)
    b_router = jnp.concatenate(
        [re_b, rg_b, jnp.zeros((LANES - N_EXPERTS - N_GROUPS,), F32)]).reshape(1, LANES)
    w_router_hi = w_router.astype(BF16)
    w_router_lo = (w_router - w_router_hi.astype(F32)).astype(BF16)
    w_router = jnp.concatenate([w_router_hi, w_router_lo], axis=1)
    tm_mix = min(512, seq)
    x1, h2, route_i, route_w, counts = _mix(
        yg, proj, x.reshape(n, d), w_ret_o.astype(BF16), w_conv_o.astype(BF16), w_out.astype(BF16),
        conv_w, gate1, shift2, scale2, norm2_w, w_router, b_router, seq, off_cb, tm_mix)

    te = EXPERT_TILE
    cnt = counts[0, :N_EXPERTS].astype(jnp.int32)
    padded = ((cnt + te - 1) // te) * te
    ends = jnp.cumsum(padded)
    starts = ends - padded
    expert_ids = jnp.arange(N_EXPERTS, dtype=jnp.int32)
    seg_start = jnp.sum(jnp.where(route_i[:, 0:2, None] == expert_ids, starts, 0), axis=-1)
    pos = seg_start + route_i[:, 2:4]
    pos_flat = pos.reshape(-1).astype(jnp.int32)
    n_tiles = (2 * n) // te + N_EXPERTS
    tile_row = jnp.arange(n_tiles, dtype=jnp.int32) * te
    tile_expert = jnp.minimum(jnp.sum((tile_row[:, None] >= ends[None, :]).astype(jnp.int32), axis=1),
                              N_EXPERTS - 1).astype(jnp.int32)
    new_expert = jnp.concatenate([jnp.ones((1,), bool), tile_expert[1:] != tile_expert[:-1]])
    tile_flag = jnp.where(tile_row < ends[-1], jnp.where(new_expert, 2, 1), 0).astype(jnp.int32)

    valid_tiles = ends[-1] // te
    last_tile = jnp.where(padded > 0, ends // te - 1, -1)
    tail_tile = valid_tiles + jnp.arange(N_EXPERTS, dtype=jnp.int32)
    tail_tile = jnp.where(tail_tile < n_tiles, tail_tile, -1)
    zero_tiles = jnp.concatenate([last_tile, tail_tile]).astype(jnp.int32)

    xs = _scatter_rows(pos_flat, zero_tiles, h2, n_tiles * te, min(512, seq))
    nonempty = padded > 0
    later = nonempty[None, :] & (expert_ids[None, :] > expert_ids[:, None])
    next_expert = jnp.min(jnp.where(later, expert_ids[None, :], N_EXPERTS), axis=1)
    next_expert = jnp.where(next_expert == N_EXPERTS, -1, next_expert)
    slot = (jnp.cumsum(nonempty.astype(jnp.int32)) - 1) % 2
    tile_hot = tile_expert[:, None] == expert_ids[None, :]
    tile_next = jnp.sum(jnp.where(tile_hot, next_expert, 0), axis=1).astype(jnp.int32)
    tile_slot = jnp.sum(jnp.where(tile_hot, slot, 0), axis=1).astype(jnp.int32)

    ys = _experts(tile_flag, tile_expert, tile_next, tile_slot, xs, w13, w2)
    out = _combine(pos_flat, ys, x1, route_w, gate2, final_w, seq, min(256, seq))
    return out.reshape(batch, seq, d)


def kernel(x, c, ctx, c_ctx, w_mod, b_mod, norm1_w, norm2_w, w_in, ret_log_decay, ret_gn_w, w_ret_o,
           conv_w, w_conv_o, w_out, router_group_w, router_group_b, router_expert_w, router_expert_b,
           expert_w13, expert_w2, final_norm_w):
    assert w_mod.shape[0] == 1, "single-layer problem"
    return _layer(x, c, ctx, c_ctx, w_mod[0], b_mod[0], norm1_w[0], norm2_w[0], w_in[0],
                  ret_log_decay[0], ret_gn_w[0], w_ret_o[0], conv_w[0], w_conv_o[0], w_out[0],
                  router_group_w[0], router_group_b[0], router_expert_w[0], router_expert_b[0],
                  expert_w13[0], expert_w2[0], final_norm_w)
```

```python
import functools

import jax
import jax.numpy as jnp
from jax import lax
from jax.experimental import pallas as pl
from jax.experimental.pallas import tpu as pltpu

GRID_W = 64
RET_HEADS = 4
RET_QK_DIM = 256
RET_V_DIM = 512
RET_QK_W = RET_HEADS * RET_QK_DIM
RET_V_W = RET_HEADS * RET_V_DIM
N_GROUPS = 4
EXPERTS_PER_GROUP = 8
N_EXPERTS = N_GROUPS * EXPERTS_PER_GROUP
EXPERT_HIDDEN = 512
ROPE_BASE = 10000.0
NORM_EPS = 1e-6
GN_EPS = 1e-5

RET_CHUNK = 256
LANES = 128
EXPERT_TILE = 256
VMEM_LIMIT = 56 * 1024 * 1024

BF16 = jnp.bfloat16
F32 = jnp.float32


def _params(sem, vmem=VMEM_LIMIT):
    return pltpu.CompilerParams(dimension_semantics=sem, vmem_limit_bytes=vmem)


def _mod_kernel(c_ref, w_ref, b_ref, o_ref):
    c = c_ref[...]
    s = c * jax.nn.sigmoid(c)
    o_ref[...] = jnp.dot(s.astype(BF16), w_ref[...].astype(BF16),
                         preferred_element_type=F32) + b_ref[...]


def _modulation(c_all, w_mod, b_mod):
    rows, d = c_all.shape
    n_out = w_mod.shape[1]
    tn = 1536
    return pl.pallas_call(
        _mod_kernel,
        out_shape=jax.ShapeDtypeStruct((rows, n_out), F32),
        grid=(n_out // tn,),
        in_specs=[pl.BlockSpec((rows, d), lambda j: (0, 0)),
                  pl.BlockSpec((d, tn), lambda j: (0, j)),
                  pl.BlockSpec((1, tn), lambda j: (0, j))],
        out_specs=pl.BlockSpec((rows, tn), lambda j: (0, j)),
        compiler_params=_params(("parallel",)),
        name="mod",
    )(c_all, w_mod, b_mod.reshape(1, n_out))


def _rope_store(acc, cos, sin, scale, o_ref):
    for hd in range(acc.shape[1] // RET_QK_DIM):
        lo = hd * RET_QK_DIM
        u1 = acc[:, lo:lo + LANES]
        u2 = acc[:, lo + LANES:lo + 2 * LANES]
        r1 = u1 * cos - u2 * sin
        r2 = u1 * sin + u2 * cos
        if scale != 1.0:
            r1 = r1 * scale
            r2 = r2 * scale
        o_ref[:, lo:lo + LANES] = r1.astype(o_ref.dtype)
        o_ref[:, lo + LANES:lo + 2 * LANES] = r2.astype(o_ref.dtype)


def _proj_kernel(x_ref, nw_ref, sh_ref, sc_ref, cos_ref, sin_ref, w_ref, o_ref, h_ref, *, modes):
    j = pl.program_id(1)

    @pl.when(j == 0)
    def _():
        x = x_ref[...]
        y = x * lax.rsqrt(jnp.mean(x * x, axis=-1, keepdims=True) + NORM_EPS) * nw_ref[...]
        h_ref[...] = (y * (1.0 + sc_ref[0]) + sh_ref[0]).astype(BF16)

    acc = jnp.dot(h_ref[...], w_ref[...], preferred_element_type=F32)
    n_special = len(modes)
    for t, (rope, scale) in enumerate(modes):
        @pl.when(j == t)
        def _(rope=rope, scale=scale):
            if rope:
                _rope_store(acc, cos_ref[...], sin_ref[...], scale, o_ref)
            else:
                o_ref[...] = (acc * scale).astype(o_ref.dtype)

    @pl.when(j >= n_special)
    def _():
        o_ref[...] = acc.astype(o_ref.dtype)


def _projection(x2d, norm_w, shift, scale, cos_t, sin_t, w_bf16, seq, modes, tm):
    n, d = x2d.shape
    width = w_bf16.shape[1]
    tn = 1024
    per_b = seq // tm
    nb = shift.shape[0]
    bidx = (lambda i, j: (i // per_b, 0, 0)) if nb > 1 else (lambda i, j: (0, 0, 0))
    return pl.pallas_call(
        functools.partial(_proj_kernel, modes=modes),
        out_shape=jax.ShapeDtypeStruct((n, width), BF16),
        grid=(n // tm, width // tn),
        in_specs=[pl.BlockSpec((tm, d), lambda i, j: (i, 0)),
                  pl.BlockSpec((1, d), lambda i, j: (0, 0)),
                  pl.BlockSpec((1, 1, d), bidx),
                  pl.BlockSpec((1, 1, d), bidx),
                  pl.BlockSpec((tm, LANES), lambda i, j: (i % per_b, 0)),
                  pl.BlockSpec((tm, LANES), lambda i, j: (i % per_b, 0)),
                  pl.BlockSpec((d, tn), lambda i, j: (0, j))],
        out_specs=pl.BlockSpec((tm, tn), lambda i, j: (i, j)),
        scratch_shapes=[pltpu.VMEM((tm, d), BF16)],
        compiler_params=_params(("parallel", "arbitrary")),
        name="proj",
    )(x2d, norm_w.reshape(1, d), shift, scale, cos_t, sin_t, w_bf16)


def _dot_t0(a, b):
    return lax.dot_general(a, b, (((0,), (0,)), ((), ())), preferred_element_type=F32)


def _ret_kernel(lg_ref, q_ref, k_ref, v_ref, g_ref, kc_ref, vc_ref, gn_ref, o_ref,
                acc_ref, sf_ref, sb_ref, dm_ref, *, seq, ctx_len):
    h = pl.program_id(1)
    lgf = lg_ref[0, h]
    lgb = lg_ref[1, h]
    c = RET_CHUNK
    n_chunks = seq // c

    cpos = lax.broadcasted_iota(jnp.int32, (ctx_len, 1), 0).astype(F32)
    kc = kc_ref[...].astype(F32)
    vc = vc_ref[...]
    sf_ref[...] = _dot_t0((kc * jnp.exp((ctx_len - 1.0 - cpos) * lgf)).astype(BF16), vc)
    sb_ref[...] = _dot_t0((kc * jnp.exp(cpos * lgb)).astype(BF16), vc)

    ri = lax.broadcasted_iota(jnp.int32, (c, c), 0)
    ci = lax.broadcasted_iota(jnp.int32, (c, c), 1)
    dist = (ri - ci).astype(F32)
    dm_ref[...] = jnp.exp(jnp.abs(dist) * jnp.where(dist >= 0, lgf, lgb))

    idx = lax.broadcasted_iota(jnp.int32, (c, 1), 0).astype(F32)
    xi_f = jnp.exp((idx + 1.0) * lgf)
    zeta_f = jnp.exp((c - 1.0 - idx) * lgf)
    xi_b = jnp.exp((c - idx) * lgb)
    zeta_b = jnp.exp(idx * lgb)
    cd_f = jnp.exp(jnp.full((1, 1), float(c), F32) * lgf)
    cd_b = jnp.exp(jnp.full((1, 1), float(c), F32) * lgb)

    gn_w = gn_ref[...]

    def chunk_rows(ic):
        return pl.ds(pl.multiple_of(ic * c, c), c)

    def fwd_part(ic):
        rows = chunk_rows(ic)
        q = q_ref[rows, :]
        k = k_ref[rows, :]
        v = v_ref[rows, :]
        s = lax.dot_general(q, k, (((1,), (1,)), ((), ())), preferred_element_type=F32)
        intra = jnp.dot((s * dm_ref[...]).astype(BF16), v, preferred_element_type=F32)
        inter = jnp.dot(q, sf_ref[...].astype(BF16), preferred_element_type=F32)
        kz = (k.astype(F32) * zeta_f).astype(BF16)
        sf_ref[...] = cd_f * sf_ref[...] + _dot_t0(kz, v)
        return intra + xi_f * inter

    def bwd_part(ic):
        rows = chunk_rows(ic)
        q = q_ref[rows, :]
        k = k_ref[rows, :]
        v = v_ref[rows, :]
        inter = jnp.dot(q, sb_ref[...].astype(BF16), preferred_element_type=F32)
        kz = (k.astype(F32) * zeta_b).astype(BF16)
        sb_ref[...] = cd_b * sb_ref[...] + _dot_t0(kz, v)
        return xi_b * inter

    def finalize(ic, y):
        rows = chunk_rows(ic)
        mu = jnp.mean(y, axis=-1, keepdims=True)
        yc = y - mu
        var = jnp.mean(yc * yc, axis=-1, keepdims=True)
        yn = yc * lax.rsqrt(var + GN_EPS) * gn_w
        g = g_ref[rows, :].astype(F32)
        o_ref[rows, :] = (g * jax.nn.sigmoid(g) * yn).astype(o_ref.dtype)

    half = n_chunks // 2

    def first_half(t, carry):
        acc_ref[chunk_rows(t), :] = fwd_part(t)
        acc_ref[chunk_rows(n_chunks - 1 - t), :] = bwd_part(n_chunks - 1 - t)
        return carry

    def second_half(t, carry):
        finalize(t, acc_ref[chunk_rows(t), :] + fwd_part(t))
        u = n_chunks - 1 - t
        finalize(u, acc_ref[chunk_rows(u), :] + bwd_part(u))
        return carry

    lax.fori_loop(0, half, first_half, 0)
    lax.fori_loop(half, n_chunks, second_half, 0)


def _retention(proj, kv_ctx, log_gamma, gn_w, batch, seq, ctx_len, off_q, off_k, off_v, off_g):
    n = proj.shape[0]
    dk, dv = RET_QK_DIM, RET_V_DIM
    assert seq % (2 * RET_CHUNK) == 0, "the paired scan needs an even number of chunks"
    grid_spec = pltpu.PrefetchScalarGridSpec(
        num_scalar_prefetch=1,
        grid=(batch, RET_HEADS),
        in_specs=[pl.BlockSpec((seq, dk), lambda b, h, lg: (b, off_q // dk + h)),
                  pl.BlockSpec((seq, dk), lambda b, h, lg: (b, off_k // dk + h)),
                  pl.BlockSpec((seq, dv), lambda b, h, lg: (b, off_v // dv + h)),
                  pl.BlockSpec((seq, dv), lambda b, h, lg: (b, off_g // dv + h)),
                  pl.BlockSpec((ctx_len, dk), lambda b, h, lg: (b, h)),
                  pl.BlockSpec((ctx_len, dv), lambda b, h, lg: (b, RET_QK_W // dv + h)),
                  pl.BlockSpec((1, dv), lambda b, h, lg: (0, h))],
        out_specs=pl.BlockSpec((seq, dv), lambda b, h, lg: (b, h)),
        scratch_shapes=[pltpu.VMEM((seq, dv), F32),
                        pltpu.VMEM((dk, dv), F32),
                        pltpu.VMEM((dk, dv), F32),
                        pltpu.VMEM((RET_CHUNK, RET_CHUNK), F32)],
    )
    return pl.pallas_call(
        functools.partial(_ret_kernel, seq=seq, ctx_len=ctx_len),
        out_shape=jax.ShapeDtypeStruct((n, RET_V_W), BF16),
        grid_spec=grid_spec,
        compiler_params=_params(("parallel", "arbitrary")),
        name="ret",
    )(log_gamma, proj, proj, proj, proj, kv_ctx, kv_ctx, gn_w.reshape(1, RET_V_W))


def _mix_kernel(yg_ref, cb_ref, cc_ref, ch_ref, gr_ref, gc_ref, x_ref, wro_ref, wco_ref, wo_ref,
                cw_ref, g1_ref, sh2_ref, sc2_ref, n2_ref, wr_ref, br_ref,
                x1_ref, h2_ref, ri_ref, rw_ref, cnt_ref, carry_ref):
    i = pl.program_id(0)
    tm = x_ref.shape[0]

    @pl.when(i == 0)
    def _():
        carry_ref[...] = jnp.zeros_like(carry_ref)

    ret_branch = jnp.dot(yg_ref[...], wro_ref[...], preferred_element_type=F32)

    p = cc_ref[...].astype(F32) * ch_ref[...].astype(F32)
    tpos = lax.broadcasted_iota(jnp.int32, (tm, 1), 0) % GRID_W
    prev = jnp.where(tpos != 0, pltpu.roll(p, 1, axis=0), 0.0)
    nxt = jnp.where(tpos != GRID_W - 1, pltpu.roll(p, tm - 1, axis=0), 0.0)
    cw = cw_ref[...]
    u = cw[0:1, :] * prev + cw[1:2, :] * p + cw[2:3, :] * nxt
    conv_in = (cb_ref[...].astype(F32) * u).astype(BF16)
    conv_branch = jnp.dot(conv_in, wco_ref[...], preferred_element_type=F32)

    merged = (jax.nn.sigmoid(gr_ref[...].astype(F32)) * ret_branch
              + jax.nn.sigmoid(gc_ref[...].astype(F32)) * conv_branch)
    mixed = jnp.dot(merged.astype(BF16), wo_ref[...], preferred_element_type=F32)
    x1 = x_ref[...] + g1_ref[0] * mixed
    x1_ref[...] = x1

    y = x1 * lax.rsqrt(jnp.mean(x1 * x1, axis=-1, keepdims=True) + NORM_EPS) * n2_ref[...]
    h2 = y * (1.0 + sc2_ref[0]) + sh2_ref[0]
    h2_ref[...] = h2

    h_hi = h2.astype(BF16)
    h_lo = (h2 - h_hi.astype(F32)).astype(BF16)
    hw = jnp.dot(h_hi, wr_ref[...], preferred_element_type=F32)
    lw = jnp.dot(h_lo, wr_ref[:, :LANES], preferred_element_type=F32)
    logits = hw[:, :LANES] + (hw[:, LANES:] + lw) + br_ref[...]
    lane = lax.broadcasted_iota(jnp.int32, (tm, LANES), 1)
    neg = jnp.float32(-jnp.inf)
    big = jnp.int32(1 << 20)
    is_g = (lane >= N_EXPERTS) & (lane < N_EXPERTS + N_GROUPS)
    gl = jnp.where(is_g, logits, neg)
    gmax = jnp.max(gl, axis=1, keepdims=True)
    gidx = jnp.min(jnp.where(gl == gmax, lane, big), axis=1, keepdims=True) - N_EXPERTS
    gsum = jnp.sum(jnp.where(is_g, jnp.exp(logits - gmax), 0.0), axis=1, keepdims=True)
    g_w = 1.0 / gsum

    in_grp = (lane // EXPERTS_PER_GROUP) == gidx
    el = jnp.where(in_grp, logits, neg)
    emax = jnp.max(el, axis=1, keepdims=True)
    ex = jnp.where(in_grp, jnp.exp(logits - emax), 0.0)
    prob = ex / jnp.sum(ex, axis=1, keepdims=True)
    pm = jnp.where(in_grp, prob, -1.0)
    p1 = jnp.max(pm, axis=1, keepdims=True)
    i1 = jnp.min(jnp.where(pm == p1, lane, big), axis=1, keepdims=True)
    pm2 = jnp.where(lane == i1, -1.0, pm)
    p2 = jnp.max(pm2, axis=1, keepdims=True)
    i2 = jnp.min(jnp.where(pm2 == p2, lane, big), axis=1, keepdims=True)
    den = p1 + p2
    c1 = g_w * (p1 / den)
    c2 = g_w * (p2 / den)

    hot1 = lane == i1
    hot2 = lane == i2
    onehot = jnp.where(hot1, 1.0, jnp.where(hot2, 1.0, 0.0))
    tri = (lax.broadcasted_iota(jnp.int32, (tm, tm), 0)
           > lax.broadcasted_iota(jnp.int32, (tm, tm), 1))
    ranks = jnp.dot(jnp.where(tri, 1.0, 0.0).astype(BF16), onehot.astype(BF16),
                    preferred_element_type=F32) + carry_ref[...]
    r1 = jnp.sum(jnp.where(hot1, ranks, 0.0), axis=1, keepdims=True).astype(jnp.int32)
    r2 = jnp.sum(jnp.where(hot2, ranks, 0.0), axis=1, keepdims=True).astype(jnp.int32)
    carry_ref[...] = carry_ref[...] + jnp.sum(onehot, axis=0, keepdims=True)
    cnt_ref[...] = carry_ref[...]

    ri_ref[...] = jnp.where(lane == 0, i1, jnp.where(lane == 1, i2,
                            jnp.where(lane == 2, r1, jnp.where(lane == 3, r2, 0))))
    rw_ref[...] = jnp.where(lane == 0, c1, jnp.where(lane == 1, c2, 0.0))


def _mix(yg, proj, x2d, w_ret_o, w_conv_o, w_out, conv_w, gate1, shift2, scale2, norm2_w,
         w_router, b_router, seq, off_cb, tm):
    n, d = x2d.shape
    per_b = seq // tm
    cblk = off_cb // d
    row = lambda i: (i, 0)
    const2 = lambda i: (0, 0)
    bidx = lambda i: (i // per_b, 0, 0)
    col = lambda k: (lambda i: (i, cblk + k))
    return pl.pallas_call(
        _mix_kernel,
        out_shape=(jax.ShapeDtypeStruct((n, d), F32),
                   jax.ShapeDtypeStruct((n, d), F32),
                   jax.ShapeDtypeStruct((n, LANES), jnp.int32),
                   jax.ShapeDtypeStruct((n, LANES), F32),
                   jax.ShapeDtypeStruct((1, LANES), F32)),
        grid=(n // tm,),
        in_specs=[pl.BlockSpec((tm, RET_V_W), row),
                  pl.BlockSpec((tm, d), col(0)),
                  pl.BlockSpec((tm, d), col(1)),
                  pl.BlockSpec((tm, d), col(2)),
                  pl.BlockSpec((tm, d), col(3)),
                  pl.BlockSpec((tm, d), col(4)),
                  pl.BlockSpec((tm, d), row),
                  pl.BlockSpec((RET_V_W, d), const2, pipeline_mode=pl.Buffered(1)),
                  pl.BlockSpec((d, d), const2, pipeline_mode=pl.Buffered(1)),
                  pl.BlockSpec((d, d), const2, pipeline_mode=pl.Buffered(1)),
                  pl.BlockSpec((3, d), const2),
                  pl.BlockSpec((1, 1, d), bidx),
                  pl.BlockSpec((1, 1, d), bidx),
                  pl.BlockSpec((1, 1, d), bidx),
                  pl.BlockSpec((1, d), const2),
                  pl.BlockSpec((d, 2 * LANES), const2),
                  pl.BlockSpec((1, LANES), const2)],
        out_specs=(pl.BlockSpec((tm, d), row),
                   pl.BlockSpec((tm, d), row),
                   pl.BlockSpec((tm, LANES), row),
                   pl.BlockSpec((tm, LANES), row),
                   pl.BlockSpec((1, LANES), const2)),
        scratch_shapes=[pltpu.VMEM((1, LANES), F32)],
        compiler_params=_params(("arbitrary",)),
        name="mix",
    )(yg, proj, proj, proj, proj, proj, x2d, w_ret_o, w_conv_o, w_out, conv_w,
      gate1, shift2, scale2, norm2_w.reshape(1, d), w_router, b_router)


def _row_copy(src_ref, src_row, dst_ref, dst_row, sem):
    return pltpu.make_async_copy(src_ref.at[pl.ds(src_row, 1), :],
                                 dst_ref.at[pl.ds(dst_row, 1), :], sem)


def _scatter_kernel(pos_ref, zt_ref, h_ref, xs_ref, zero_ref, sem, zsem):
    i = pl.program_id(0)
    tm = h_ref.shape[0]
    te = zero_ref.shape[0]
    base = i * tm

    @pl.when(i == 0)
    def _():
        zero_ref[...] = jnp.zeros_like(zero_ref)

        def tile_copy(z):
            row = pl.multiple_of(zt_ref[z] * te, te)
            return pltpu.make_async_copy(zero_ref, xs_ref.at[pl.ds(row, te), :], zsem)

        def zissue(z, carry):
            @pl.when(zt_ref[z] >= 0)
            def _():
                tile_copy(z).start()
            return carry

        def zdrain(z, carry):
            @pl.when(zt_ref[z] >= 0)
            def _():
                tile_copy(z).wait()
            return carry

        lax.fori_loop(0, zt_ref.shape[0], zissue, 0)
        lax.fori_loop(0, zt_ref.shape[0], zdrain, 0)

    for r in range(tm):
        a = 2 * base + 2 * r
        _row_copy(h_ref, r, xs_ref, pos_ref[a], sem).start(priority=0)
        _row_copy(h_ref, r, xs_ref, pos_ref[a + 1], sem).start(priority=1)

    for _ in range(2):
        pltpu.make_async_copy(h_ref, xs_ref.at[pl.ds(0, tm), :], sem).wait()


def _scatter_rows(pos_flat, zero_tiles, h2, n_sorted, tm):
    n, d = h2.shape
    grid_spec = pltpu.PrefetchScalarGridSpec(
        num_scalar_prefetch=2,
        grid=(n // tm,),
        in_specs=[pl.BlockSpec((tm, d), lambda i, pos, zt: (i, 0))],
        out_specs=pl.BlockSpec(memory_space=pl.ANY),
        scratch_shapes=[pltpu.VMEM((EXPERT_TILE, d), F32),
                        pltpu.SemaphoreType.DMA,
                        pltpu.SemaphoreType.DMA],
    )
    return pl.pallas_call(
        _scatter_kernel,
        out_shape=jax.ShapeDtypeStruct((n_sorted, d), F32),
        grid_spec=grid_spec,
        compiler_params=_params(("arbitrary",)),
        name="scatter",
    )(pos_flat, zero_tiles, h2)


def _expert_kernel(tf_ref, te_ref, tn_ref, ts_ref, xs_ref, w13_ref, w2_ref, ys_ref,
                   w13f_ref, w2f_ref, w13b_ref, w2b_ref, sem13, sem2):
    t = pl.program_id(0)
    flag = tf_ref[t]

    def weight_copies(e, slot):
        return (pltpu.make_async_copy(w13_ref.at[e], w13f_ref.at[slot], sem13.at[slot]),
                pltpu.make_async_copy(w2_ref.at[e], w2f_ref.at[slot], sem2.at[slot]))

    @pl.when(t == 0)
    def _():
        for cp in weight_copies(te_ref[0], ts_ref[0]):
            cp.start()

    @pl.when(flag == 2)
    def _():
        slot = ts_ref[t]
        for cp in weight_copies(te_ref[t], slot):
            cp.wait()
        w13b_ref[...] = w13f_ref[slot].astype(BF16)
        w2b_ref[...] = w2f_ref[slot].astype(BF16)

        @pl.when(tn_ref[t] >= 0)
        def _():
            for cp in weight_copies(tn_ref[t], 1 - slot):
                cp.start()

    @pl.when(flag > 0)
    def _():
        x = xs_ref[...].astype(BF16)
        hid = jnp.dot(x, w13b_ref[...], preferred_element_type=F32)
        a = hid[:, :EXPERT_HIDDEN]
        act = a * jax.nn.sigmoid(a) * hid[:, EXPERT_HIDDEN:]
        ys_ref[...] = jnp.dot(act.astype(BF16), w2b_ref[...], preferred_element_type=F32)

    @pl.when(flag == 0)
    def _():
        ys_ref[...] = jnp.zeros_like(ys_ref)


def _experts(tile_flag, tile_expert, tile_next, tile_slot, xs, w13, w2):
    n_sorted, d = xs.shape
    te = EXPERT_TILE
    n_tiles = n_sorted // te
    hid2 = w13.shape[2]
    grid_spec = pltpu.PrefetchScalarGridSpec(
        num_scalar_prefetch=4,
        grid=(n_tiles,),
        in_specs=[pl.BlockSpec((te, d), lambda t, *_: (t, 0)),
                  pl.BlockSpec(memory_space=pl.ANY),
                  pl.BlockSpec(memory_space=pl.ANY)],
        out_specs=pl.BlockSpec((te, d), lambda t, *_: (t, 0)),
        scratch_shapes=[pltpu.VMEM((2, d, hid2), F32),
                        pltpu.VMEM((2, hid2 // 2, d), F32),
                        pltpu.VMEM((d, hid2), BF16),
                        pltpu.VMEM((hid2 // 2, d), BF16),
                        pltpu.SemaphoreType.DMA((2,)),
                        pltpu.SemaphoreType.DMA((2,))],
    )
    return pl.pallas_call(
        _expert_kernel,
        out_shape=jax.ShapeDtypeStruct((n_sorted, d), F32),
        grid_spec=grid_spec,
        compiler_params=_params(("arbitrary",)),
        name="experts",
    )(tile_flag, tile_expert, tile_next, tile_slot, xs, w13, w2)


def _combine_kernel(pos_ref, ys_ref, x1_ref, rw_ref, g2_ref, fw_ref, o_ref, y0_ref, y1_ref, sem):
    i = pl.program_id(0)
    tm = x1_ref.shape[0]
    base = i * tm

    for r in range(tm):
        a = 2 * base + 2 * r
        _row_copy(ys_ref, pos_ref[a], y0_ref, r, sem).start(priority=0)
        _row_copy(ys_ref, pos_ref[a + 1], y1_ref, r, sem).start(priority=1)

    pltpu.make_async_copy(ys_ref.at[pl.ds(0, tm), :], y0_ref, sem).wait()
    pltpu.make_async_copy(ys_ref.at[pl.ds(0, tm), :], y1_ref, sem).wait()

    rw = rw_ref[...]
    moe = rw[:, 0:1] * y0_ref[...] + rw[:, 1:2] * y1_ref[...]
    x2 = x1_ref[...] + g2_ref[0] * moe
    o_ref[...] = x2 * lax.rsqrt(jnp.mean(x2 * x2, axis=-1, keepdims=True) + NORM_EPS) * fw_ref[...]


def _combine(pos_flat, ys, x1, route_w, gate2, final_w, seq, tm):
    n, d = x1.shape
    per_b = seq // tm
    grid_spec = pltpu.PrefetchScalarGridSpec(
        num_scalar_prefetch=1,
        grid=(n // tm,),
        in_specs=[pl.BlockSpec(memory_space=pl.ANY),
                  pl.BlockSpec((tm, d), lambda i, pos: (i, 0)),
                  pl.BlockSpec((tm, LANES), lambda i, pos: (i, 0)),
                  pl.BlockSpec((1, 1, d), lambda i, pos: (i // per_b, 0, 0)),
                  pl.BlockSpec((1, d), lambda i, pos: (0, 0))],
        out_specs=pl.BlockSpec((tm, d), lambda i, pos: (i, 0)),
        scratch_shapes=[pltpu.VMEM((tm, d), F32),
                        pltpu.VMEM((tm, d), F32),
                        pltpu.SemaphoreType.DMA],
    )
    return pl.pallas_call(
        _combine_kernel,
        out_shape=jax.ShapeDtypeStruct((n, d), F32),
        grid_spec=grid_spec,
        compiler_params=_params(("arbitrary",)),
        name="combine",
    )(pos_flat, ys, x1, route_w, gate2, final_w.reshape(1, d))


def _rope_tables(seq):
    pos = jnp.arange(seq)
    nf = RET_QK_DIM // 4
    inv = ROPE_BASE ** (-jnp.arange(nf, dtype=F32) / nf)
    ang_r = (pos // GRID_W).astype(F32)[:, None] * inv[None, :]
    ang_c = (pos % GRID_W).astype(F32)[:, None] * inv[None, :]
    return (jnp.concatenate([jnp.cos(ang_r), jnp.cos(ang_c)], axis=1),
            jnp.concatenate([jnp.sin(ang_r), jnp.sin(ang_c)], axis=1))


def _pair_split_heads(w_qk):
    d, width = w_qk.shape
    quarter = RET_QK_DIM // 4
    w4 = w_qk.reshape(d, width // RET_QK_DIM, 4, quarter)
    return jnp.stack([w4[:, :, 0], w4[:, :, 2], w4[:, :, 1], w4[:, :, 3]], axis=2).reshape(d, width)


def _layer(x, c, ctx, c_ctx, w_mod, b_mod, norm1_w, norm2_w, w_in, log_decay, gn_w, w_ret_o,
           conv_w, w_conv_o, w_out, rg_w, rg_b, re_w, re_b, w13, w2, final_w):
    batch, seq, d = x.shape
    ctx_len = ctx.shape[1]
    n = batch * seq
    off_q = 0
    off_k = off_q + RET_QK_W
    off_v = off_k + RET_QK_W
    off_g = off_v + RET_V_W
    off_cb = off_g + RET_V_W

    pad = (-(batch + 1)) % 8
    c_all = jnp.concatenate([c, c_ctx[None, :], jnp.zeros((pad, d), F32)], axis=0)
    mod = _modulation(c_all, w_mod, b_mod)
    mx = mod[:batch].reshape(batch, 1, 6, d)
    shift1, scale1, gate1, shift2, scale2, gate2 = (mx[:, :, k, :] for k in range(6))
    mc = mod[batch].reshape(1, 1, 6, d)
    shift1c, scale1c = mc[:, :, 0, :], mc[:, :, 1, :]

    w_in_b = w_in.astype(BF16)
    w_in_b = jnp.concatenate([_pair_split_heads(w_in_b[:, :off_v]), w_in_b[:, off_v:]], axis=1)
    cos_t, sin_t = _rope_tables(seq)
    k_scale = RET_QK_DIM ** -0.5

    tm_proj = min(2048, seq)
    proj = _projection(x.reshape(n, d), norm1_w, shift1, scale1, cos_t, sin_t, w_in_b, seq,
                       modes=((True, 1.0), (True, k_scale)), tm=tm_proj)
    tm_ctx = ctx_len
    kv_ctx = _projection(ctx.reshape(batch * ctx_len, d), norm1_w, shift1c, scale1c,
                         cos_t[:tm_ctx], sin_t[:tm_ctx], w_in_b[:, off_k:off_g], ctx_len,
                         modes=((False, k_scale),), tm=tm_ctx)

    log_gamma = jnp.log1p(-jnp.exp(log_decay.astype(F32)))
    yg = _retention(proj, kv_ctx, log_gamma, gn_w, batch, seq, ctx_len, off_q, off_k, off_v, off_g)

    w_router = jnp.concatenate(
        [re_w, rg_w, jnp.zeros((d, LANES - N_EXPERTS - N_GROUPS), F32)], axis=1)
    b_router = jnp.concatenate(
        [re_b, rg_b, jnp.zeros((LANES - N_EXPERTS - N_GROUPS,), F32)]).reshape(1, LANES)
    w_router_hi = w_router.astype(BF16)
    w_router_lo = (w_router - w_router_hi.astype(F32)).astype(BF16)
    w_router = jnp.concatenate([w_router_hi, w_router_lo], axis=1)
    tm_mix = min(512, seq)
    x1, h2, route_i, route_w, counts = _mix(
        yg, proj, x.reshape(n, d), w_ret_o.astype(BF16), w_conv_o.astype(BF16), w_out.astype(BF16),
        conv_w, gate1, shift2, scale2, norm2_w, w_router, b_router, seq, off_cb, tm_mix)

    te = EXPERT_TILE
    cnt = counts[0, :N_EXPERTS].astype(jnp.int32)
    padded = ((cnt + te - 1) // te) * te
    ends = jnp.cumsum(padded)
    starts = ends - padded
    expert_ids = jnp.arange(N_EXPERTS, dtype=jnp.int32)
    seg_start = jnp.sum(jnp.where(route_i[:, 0:2, None] == expert_ids, starts, 0), axis=-1)
    pos = seg_start + route_i[:, 2:4]
    pos_flat = pos.reshape(-1).astype(jnp.int32)
    n_tiles = (2 * n) // te + N_EXPERTS
    tile_row = jnp.arange(n_tiles, dtype=jnp.int32) * te
    tile_expert = jnp.minimum(jnp.sum((tile_row[:, None] >= ends[None, :]).astype(jnp.int32), axis=1),
                              N_EXPERTS - 1).astype(jnp.int32)
    new_expert = jnp.concatenate([jnp.ones((1,), bool), tile_expert[1:] != tile_expert[:-1]])
    tile_flag = jnp.where(tile_row < ends[-1], jnp.where(new_expert, 2, 1), 0).astype(jnp.int32)

    valid_tiles = ends[-1] // te
    last_tile = jnp.where(padded > 0, ends // te - 1, -1)
    tail_tile = valid_tiles + jnp.arange(N_EXPERTS, dtype=jnp.int32)
    tail_tile = jnp.where(tail_tile < n_tiles, tail_tile, -1)
    zero_tiles = jnp.concatenate([last_tile, tail_tile]).astype(jnp.int32)

    xs = _scatter_rows(pos_flat, zero_tiles, h2, n_tiles * te, min(512, seq))
    nonempty = padded > 0
    later = nonempty[None, :] & (expert_ids[None, :] > expert_ids[:, None])
    next_expert = jnp.min(jnp.where(later, expert_ids[None, :], N_EXPERTS), axis=1)
    next_expert = jnp.where(next_expert == N_EXPERTS, -1, next_expert)
    slot = (jnp.cumsum(nonempty.astype(jnp.int32)) - 1) % 2
    tile_hot = tile_expert[:, None] == expert_ids[None, :]
    tile_next = jnp.sum(jnp.where(tile_hot, next_expert, 0), axis=1).astype(jnp.int32)
    tile_slot = jnp.sum(jnp.where(tile_hot, slot, 0), axis=1).astype(jnp.int32)

    ys = _experts(tile_flag, tile_expert, tile_next, tile_slot, xs, w13, w2)
    out = _combine(pos_flat, ys, x1, route_w, gate2, final_w, seq, min(256, seq))
    return out.reshape(batch, seq, d)


def kernel(x, c, ctx, c_ctx, w_mod, b_mod, norm1_w, norm2_w, w_in, ret_log_decay, ret_gn_w, w_ret_o,
           conv_w, w_conv_o, w_out, router_group_w, router_group_b, router_expert_w, router_expert_b,
           expert_w13, expert_w2, final_norm_w):
    assert w_mod.shape[0] == 1, "single-layer problem"
    return _layer(x, c, ctx, c_ctx, w_mod[0], b_mod[0], norm1_w[0], norm2_w[0], w_in[0],
                  ret_log_decay[0], ret_gn_w[0], w_ret_o[0], conv_w[0], w_conv_o[0], w_out[0],
                  router_group_w[0], router_group_b[0], router_expert_w[0], router_expert_b[0],
                  expert_w13[0], expert_w2[0], final_norm_w)
```

```python
import functools

import jax
import jax.numpy as jnp
import numpy as np
from jax import lax
from jax.experimental import pallas as pl
from jax.experimental.pallas import tpu as pltpu

GRID_W = 64
RET_HEADS = 4
RET_QK_DIM = 256
RET_V_DIM = 512
RET_QK_W = RET_HEADS * RET_QK_DIM
RET_V_W = RET_HEADS * RET_V_DIM
N_GROUPS = 4
EXPERTS_PER_GROUP = 8
N_EXPERTS = N_GROUPS * EXPERTS_PER_GROUP
EXPERT_HIDDEN = 512
ROPE_BASE = 10000.0
NORM_EPS = 1e-6
GN_EPS = 1e-5

RET_CHUNK = 256
LANES = 128
EXPERT_TILE = 256
VMEM_LIMIT = 56 * 1024 * 1024

BF16 = jnp.bfloat16
F32 = jnp.float32


def _params(sem, vmem=VMEM_LIMIT):
    return pltpu.CompilerParams(dimension_semantics=sem, vmem_limit_bytes=vmem)


def _mod_kernel(c_ref, w_ref, b_ref, o_ref):
    c = c_ref[...]
    s = c * jax.nn.sigmoid(c)
    o_ref[...] = jnp.dot(s.astype(BF16), w_ref[...].astype(BF16),
                         preferred_element_type=F32) + b_ref[...]


def _modulation(c_all, w_mod, b_mod):
    rows, d = c_all.shape
    n_out = w_mod.shape[1]
    tn = 1536
    return pl.pallas_call(
        _mod_kernel,
        out_shape=jax.ShapeDtypeStruct((rows, n_out), F32),
        grid=(n_out // tn,),
        in_specs=[pl.BlockSpec((rows, d), lambda j: (0, 0)),
                  pl.BlockSpec((d, tn), lambda j: (0, j)),
                  pl.BlockSpec((1, tn), lambda j: (0, j))],
        out_specs=pl.BlockSpec((rows, tn), lambda j: (0, j)),
        compiler_params=_params(("parallel",)),
        name="mod",
    )(c_all, w_mod, b_mod.reshape(1, n_out))


def _rope_store(acc, cos, sin, scale, o_ref):
    for hd in range(acc.shape[1] // RET_QK_DIM):
        lo = hd * RET_QK_DIM
        u1 = acc[:, lo:lo + LANES]
        u2 = acc[:, lo + LANES:lo + 2 * LANES]
        r1 = u1 * cos - u2 * sin
        r2 = u1 * sin + u2 * cos
        if scale != 1.0:
            r1 = r1 * scale
            r2 = r2 * scale
        o_ref[:, lo:lo + LANES] = r1.astype(o_ref.dtype)
        o_ref[:, lo + LANES:lo + 2 * LANES] = r2.astype(o_ref.dtype)


def _proj_kernel(x_ref, nw_ref, sh_ref, sc_ref, cos_ref, sin_ref, w_ref, o_ref, h_ref, *, modes):
    j = pl.program_id(1)

    @pl.when(j == 0)
    def _():
        x = x_ref[...]
        y = x * lax.rsqrt(jnp.mean(x * x, axis=-1, keepdims=True) + NORM_EPS) * nw_ref[...]
        h_ref[...] = (y * (1.0 + sc_ref[0]) + sh_ref[0]).astype(BF16)

    acc = jnp.dot(h_ref[...], w_ref[...], preferred_element_type=F32)
    n_special = len(modes)
    for t, (rope, scale) in enumerate(modes):
        @pl.when(j == t)
        def _(rope=rope, scale=scale):
            if rope:
                _rope_store(acc, cos_ref[...], sin_ref[...], scale, o_ref)
            else:
                o_ref[...] = (acc * scale).astype(o_ref.dtype)

    @pl.when(j >= n_special)
    def _():
        o_ref[...] = acc.astype(o_ref.dtype)


def _projection(x2d, norm_w, shift, scale, cos_t, sin_t, w_bf16, seq, modes, tm):
    n, d = x2d.shape
    width = w_bf16.shape[1]
    tn = 1024
    per_b = seq // tm
    nb = shift.shape[0]
    bidx = (lambda i, j: (i // per_b, 0, 0)) if nb > 1 else (lambda i, j: (0, 0, 0))
    return pl.pallas_call(
        functools.partial(_proj_kernel, modes=modes),
        out_shape=jax.ShapeDtypeStruct((n, width), BF16),
        grid=(n // tm, width // tn),
        in_specs=[pl.BlockSpec((tm, d), lambda i, j: (i, 0)),
                  pl.BlockSpec((1, d), lambda i, j: (0, 0)),
                  pl.BlockSpec((1, 1, d), bidx),
                  pl.BlockSpec((1, 1, d), bidx),
                  pl.BlockSpec((tm, LANES), lambda i, j: (i % per_b, 0)),
                  pl.BlockSpec((tm, LANES), lambda i, j: (i % per_b, 0)),
                  pl.BlockSpec((d, tn), lambda i, j: (0, j))],
        out_specs=pl.BlockSpec((tm, tn), lambda i, j: (i, j)),
        scratch_shapes=[pltpu.VMEM((tm, d), BF16)],
        compiler_params=_params(("parallel", "arbitrary")),
        name="proj",
    )(x2d, norm_w.reshape(1, d), shift, scale, cos_t, sin_t, w_bf16)


def _dot_t0(a, b):
    return lax.dot_general(a, b, (((0,), (0,)), ((), ())), preferred_element_type=F32)


def _ret_kernel(lg_ref, q_ref, k_ref, v_ref, g_ref, kc_ref, vc_ref, gn_ref, o_ref,
                acc_ref, sf_ref, sb_ref, dm_ref, *, seq, ctx_len):
    h = pl.program_id(1)
    lgf = lg_ref[0, h]
    lgb = lg_ref[1, h]
    c = RET_CHUNK
    n_chunks = seq // c

    cpos = lax.broadcasted_iota(jnp.int32, (ctx_len, 1), 0).astype(F32)
    kc = kc_ref[...].astype(F32)
    vc = vc_ref[...]
    sf_ref[...] = _dot_t0((kc * jnp.exp((ctx_len - 1.0 - cpos) * lgf)).astype(BF16), vc)
    sb_ref[...] = _dot_t0((kc * jnp.exp(cpos * lgb)).astype(BF16), vc)

    ri = lax.broadcasted_iota(jnp.int32, (c, c), 0)
    ci = lax.broadcasted_iota(jnp.int32, (c, c), 1)
    dist = (ri - ci).astype(F32)
    dm_ref[...] = jnp.exp(jnp.abs(dist) * jnp.where(dist >= 0, lgf, lgb))

    idx = lax.broadcasted_iota(jnp.int32, (c, 1), 0).astype(F32)
    xi_f = jnp.exp((idx + 1.0) * lgf)
    zeta_f = jnp.exp((c - 1.0 - idx) * lgf)
    xi_b = jnp.exp((c - idx) * lgb)
    zeta_b = jnp.exp(idx * lgb)
    cd_f = jnp.exp(jnp.full((1, 1), float(c), F32) * lgf)
    cd_b = jnp.exp(jnp.full((1, 1), float(c), F32) * lgb)

    gn_w = gn_ref[...]

    def chunk_rows(ic):
        return pl.ds(pl.multiple_of(ic * c, c), c)

    def fwd_part(ic):
        rows = chunk_rows(ic)
        q = q_ref[rows, :]
        k = k_ref[rows, :]
        v = v_ref[rows, :]
        s = lax.dot_general(q, k, (((1,), (1,)), ((), ())), preferred_element_type=F32)
        intra = jnp.dot((s * dm_ref[...]).astype(BF16), v, preferred_element_type=F32)
        inter = jnp.dot(q, sf_ref[...].astype(BF16), preferred_element_type=F32)
        kz = (k.astype(F32) * zeta_f).astype(BF16)
        sf_ref[...] = cd_f * sf_ref[...] + _dot_t0(kz, v)
        return intra + xi_f * inter

    def bwd_part(ic):
        rows = chunk_rows(ic)
        q = q_ref[rows, :]
        k = k_ref[rows, :]
        v = v_ref[rows, :]
        inter = jnp.dot(q, sb_ref[...].astype(BF16), preferred_element_type=F32)
        kz = (k.astype(F32) * zeta_b).astype(BF16)
        sb_ref[...] = cd_b * sb_ref[...] + _dot_t0(kz, v)
        return xi_b * inter

    def finalize(ic, y):
        rows = chunk_rows(ic)
        mu = jnp.mean(y, axis=-1, keepdims=True)
        yc = y - mu
        var = jnp.mean(yc * yc, axis=-1, keepdims=True)
        yn = yc * lax.rsqrt(var + GN_EPS) * gn_w
        g = g_ref[rows, :].astype(F32)
        o_ref[rows, :] = (g * jax.nn.sigmoid(g) * yn).astype(o_ref.dtype)

    half = n_chunks // 2

    def first_half(t, carry):
        acc_ref[chunk_rows(t), :] = fwd_part(t)
        acc_ref[chunk_rows(n_chunks - 1 - t), :] = bwd_part(n_chunks - 1 - t)
        return carry

    def second_half(t, carry):
        finalize(t, acc_ref[chunk_rows(t), :] + fwd_part(t))
        u = n_chunks - 1 - t
        finalize(u, acc_ref[chunk_rows(u), :] + bwd_part(u))
        return carry

    lax.fori_loop(0, half, first_half, 0)
    lax.fori_loop(half, n_chunks, second_half, 0)


def _retention(proj, kv_ctx, log_gamma, gn_w, batch, seq, ctx_len, off_q, off_k, off_v, off_g):
    n = proj.shape[0]
    dk, dv = RET_QK_DIM, RET_V_DIM
    assert seq % (2 * RET_CHUNK) == 0, "the paired scan needs an even number of chunks"
    grid_spec = pltpu.PrefetchScalarGridSpec(
        num_scalar_prefetch=1,
        grid=(batch, RET_HEADS),
        in_specs=[pl.BlockSpec((seq, dk), lambda b, h, lg: (b, off_q // dk + h)),
                  pl.BlockSpec((seq, dk), lambda b, h, lg: (b, off_k // dk + h)),
                  pl.BlockSpec((seq, dv), lambda b, h, lg: (b, off_v // dv + h)),
                  pl.BlockSpec((seq, dv), lambda b, h, lg: (b, off_g // dv + h)),
                  pl.BlockSpec((ctx_len, dk), lambda b, h, lg: (b, h)),
                  pl.BlockSpec((ctx_len, dv), lambda b, h, lg: (b, RET_QK_W // dv + h)),
                  pl.BlockSpec((1, dv), lambda b, h, lg: (0, h))],
        out_specs=pl.BlockSpec((seq, dv), lambda b, h, lg: (b, h)),
        scratch_shapes=[pltpu.VMEM((seq, dv), F32),
                        pltpu.VMEM((dk, dv), F32),
                        pltpu.VMEM((dk, dv), F32),
                        pltpu.VMEM((RET_CHUNK, RET_CHUNK), F32)],
    )
    return pl.pallas_call(
        functools.partial(_ret_kernel, seq=seq, ctx_len=ctx_len),
        out_shape=jax.ShapeDtypeStruct((n, RET_V_W), BF16),
        grid_spec=grid_spec,
        compiler_params=_params(("parallel", "arbitrary")),
        name="ret",
    )(log_gamma, proj, proj, proj, proj, kv_ctx, kv_ctx, gn_w.reshape(1, RET_V_W))


def _mix_kernel(yg_ref, cb_ref, cc_ref, ch_ref, gr_ref, gc_ref, x_ref, wro_ref, wco_ref, wo_ref,
                cw_ref, g1_ref, sh2_ref, sc2_ref, n2_ref, wr_ref, br_ref,
                x1_ref, h2_ref, ri_ref, rw_ref, cnt_ref, carry_ref):
    i = pl.program_id(0)
    tm = x_ref.shape[0]

    @pl.when(i == 0)
    def _():
        carry_ref[...] = jnp.zeros_like(carry_ref)

    ret_branch = jnp.dot(yg_ref[...], wro_ref[...], preferred_element_type=F32)

    p = cc_ref[...].astype(F32) * ch_ref[...].astype(F32)
    tpos = lax.broadcasted_iota(jnp.int32, (tm, 1), 0) % GRID_W
    prev = jnp.where(tpos != 0, pltpu.roll(p, 1, axis=0), 0.0)
    nxt = jnp.where(tpos != GRID_W - 1, pltpu.roll(p, tm - 1, axis=0), 0.0)
    cw = cw_ref[...]
    u = cw[0:1, :] * prev + cw[1:2, :] * p + cw[2:3, :] * nxt
    conv_in = (cb_ref[...].astype(F32) * u).astype(BF16)
    conv_branch = jnp.dot(conv_in, wco_ref[...], preferred_element_type=F32)

    merged = (jax.nn.sigmoid(gr_ref[...].astype(F32)) * ret_branch
              + jax.nn.sigmoid(gc_ref[...].astype(F32)) * conv_branch)
    mixed = jnp.dot(merged.astype(BF16), wo_ref[...], preferred_element_type=F32)
    x1 = x_ref[...] + g1_ref[0] * mixed
    x1_ref[...] = x1

    y = x1 * lax.rsqrt(jnp.mean(x1 * x1, axis=-1, keepdims=True) + NORM_EPS) * n2_ref[...]
    h2 = y * (1.0 + sc2_ref[0]) + sh2_ref[0]
    h2_ref[...] = h2

    h_hi = h2.astype(BF16)
    h_lo = (h2 - h_hi.astype(F32)).astype(BF16)
    hw = jnp.dot(h_hi, wr_ref[...], preferred_element_type=F32)
    lw = jnp.dot(h_lo, wr_ref[:, :LANES], preferred_element_type=F32)
    logits = hw[:, :LANES] + (hw[:, LANES:] + lw) + br_ref[...]
    lane = lax.broadcasted_iota(jnp.int32, (tm, LANES), 1)
    neg = jnp.float32(-jnp.inf)
    big = jnp.int32(1 << 20)
    is_g = (lane >= N_EXPERTS) & (lane < N_EXPERTS + N_GROUPS)
    gl = jnp.where(is_g, logits, neg)
    gmax = jnp.max(gl, axis=1, keepdims=True)
    gidx = jnp.min(jnp.where(gl == gmax, lane, big), axis=1, keepdims=True) - N_EXPERTS
    gsum = jnp.sum(jnp.where(is_g, jnp.exp(logits - gmax), 0.0), axis=1, keepdims=True)
    g_w = 1.0 / gsum

    in_grp = (lane // EXPERTS_PER_GROUP) == gidx
    el = jnp.where(in_grp, logits, neg)
    emax = jnp.max(el, axis=1, keepdims=True)
    ex = jnp.where(in_grp, jnp.exp(logits - emax), 0.0)
    prob = ex / jnp.sum(ex, axis=1, keepdims=True)
    pm = jnp.where(in_grp, prob, -1.0)
    p1 = jnp.max(pm, axis=1, keepdims=True)
    i1 = jnp.min(jnp.where(pm == p1, lane, big), axis=1, keepdims=True)
    pm2 = jnp.where(lane == i1, -1.0, pm)
    p2 = jnp.max(pm2, axis=1, keepdims=True)
    i2 = jnp.min(jnp.where(pm2 == p2, lane, big), axis=1, keepdims=True)
    den = p1 + p2
    c1 = g_w * (p1 / den)
    c2 = g_w * (p2 / den)

    hot1 = lane == i1
    hot2 = lane == i2
    onehot = jnp.where(hot1, 1.0, jnp.where(hot2, 1.0, 0.0))
    tri = (lax.broadcasted_iota(jnp.int32, (tm, tm), 0)
           > lax.broadcasted_iota(jnp.int32, (tm, tm), 1))
    ranks = jnp.dot(jnp.where(tri, 1.0, 0.0).astype(BF16), onehot.astype(BF16),
                    preferred_element_type=F32) + carry_ref[...]
    r1 = jnp.sum(jnp.where(hot1, ranks, 0.0), axis=1, keepdims=True).astype(jnp.int32)
    r2 = jnp.sum(jnp.where(hot2, ranks, 0.0), axis=1, keepdims=True).astype(jnp.int32)
    carry_ref[...] = carry_ref[...] + jnp.sum(onehot, axis=0, keepdims=True)
    cnt_ref[...] = carry_ref[...]

    ri_ref[...] = jnp.where(lane == 0, i1, jnp.where(lane == 1, i2,
                            jnp.where(lane == 2, r1, jnp.where(lane == 3, r2, 0))))
    rw_ref[...] = jnp.where(lane == 0, c1, jnp.where(lane == 1, c2, 0.0))


def _mix(yg, proj, x2d, w_ret_o, w_conv_o, w_out, conv_w, gate1, shift2, scale2, norm2_w,
         w_router, b_router, seq, off_cb, tm):
    n, d = x2d.shape
    per_b = seq // tm
    cblk = off_cb // d
    row = lambda i: (i, 0)
    const2 = lambda i: (0, 0)
    bidx = lambda i: (i // per_b, 0, 0)
    col = lambda k: (lambda i: (i, cblk + k))
    return pl.pallas_call(
        _mix_kernel,
        out_shape=(jax.ShapeDtypeStruct((n, d), F32),
                   jax.ShapeDtypeStruct((n, d), F32),
                   jax.ShapeDtypeStruct((n, LANES), jnp.int32),
                   jax.ShapeDtypeStruct((n, LANES), F32),
                   jax.ShapeDtypeStruct((1, LANES), F32)),
        grid=(n // tm,),
        in_specs=[pl.BlockSpec((tm, RET_V_W), row),
                  pl.BlockSpec((tm, d), col(0)),
                  pl.BlockSpec((tm, d), col(1)),
                  pl.BlockSpec((tm, d), col(2)),
                  pl.BlockSpec((tm, d), col(3)),
                  pl.BlockSpec((tm, d), col(4)),
                  pl.BlockSpec((tm, d), row),
                  pl.BlockSpec((RET_V_W, d), const2, pipeline_mode=pl.Buffered(1)),
                  pl.BlockSpec((d, d), const2, pipeline_mode=pl.Buffered(1)),
                  pl.BlockSpec((d, d), const2, pipeline_mode=pl.Buffered(1)),
                  pl.BlockSpec((3, d), const2),
                  pl.BlockSpec((1, 1, d), bidx),
                  pl.BlockSpec((1, 1, d), bidx),
                  pl.BlockSpec((1, 1, d), bidx),
                  pl.BlockSpec((1, d), const2),
                  pl.BlockSpec((d, 2 * LANES), const2),
                  pl.BlockSpec((1, LANES), const2)],
        out_specs=(pl.BlockSpec((tm, d), row),
                   pl.BlockSpec((tm, d), row),
                   pl.BlockSpec((tm, LANES), row),
                   pl.BlockSpec((tm, LANES), row),
                   pl.BlockSpec((1, LANES), const2)),
        scratch_shapes=[pltpu.VMEM((1, LANES), F32)],
        compiler_params=_params(("arbitrary",)),
        name="mix",
    )(yg, proj, proj, proj, proj, proj, x2d, w_ret_o, w_conv_o, w_out, conv_w,
      gate1, shift2, scale2, norm2_w.reshape(1, d), w_router, b_router)


def _row_copy(src_ref, src_row, dst_ref, dst_row, sem):
    return pltpu.make_async_copy(src_ref.at[pl.ds(src_row, 1), :],
                                 dst_ref.at[pl.ds(dst_row, 1), :], sem)


def _scatter_kernel(pos_ref, zt_ref, h_ref, xs_ref, zero_ref, sem, zsem):
    i = pl.program_id(0)
    tm = h_ref.shape[0]
    te = zero_ref.shape[0]
    base = i * tm

    @pl.when(i == 0)
    def _():
        zero_ref[...] = jnp.zeros_like(zero_ref)

        def tile_copy(z):
            row = pl.multiple_of(zt_ref[z] * te, te)
            return pltpu.make_async_copy(zero_ref, xs_ref.at[pl.ds(row, te), :], zsem)

        def zissue(z, carry):
            @pl.when(zt_ref[z] >= 0)
            def _():
                tile_copy(z).start()
            return carry

        def zdrain(z, carry):
            @pl.when(zt_ref[z] >= 0)
            def _():
                tile_copy(z).wait()
            return carry

        lax.fori_loop(0, zt_ref.shape[0], zissue, 0)
        lax.fori_loop(0, zt_ref.shape[0], zdrain, 0)

    for r in range(tm):
        a = 2 * base + 2 * r
        _row_copy(h_ref, r, xs_ref, pos_ref[a], sem).start(priority=0)
        _row_copy(h_ref, r, xs_ref, pos_ref[a + 1], sem).start(priority=1)

    for _ in range(2):
        pltpu.make_async_copy(h_ref, xs_ref.at[pl.ds(0, tm), :], sem).wait()


def _scatter_rows(pos_flat, zero_tiles, h2, n_sorted, tm):
    n, d = h2.shape
    grid_spec = pltpu.PrefetchScalarGridSpec(
        num_scalar_prefetch=2,
        grid=(n // tm,),
        in_specs=[pl.BlockSpec((tm, d), lambda i, pos, zt: (i, 0))],
        out_specs=pl.BlockSpec(memory_space=pl.ANY),
        scratch_shapes=[pltpu.VMEM((EXPERT_TILE, d), F32),
                        pltpu.SemaphoreType.DMA,
                        pltpu.SemaphoreType.DMA],
    )
    return pl.pallas_call(
        _scatter_kernel,
        out_shape=jax.ShapeDtypeStruct((n_sorted, d), F32),
        grid_spec=grid_spec,
        compiler_params=_params(("arbitrary",)),
        name="scatter",
    )(pos_flat, zero_tiles, h2)


def _expert_kernel(tf_ref, te_ref, tn_ref, ts_ref, xs_ref, w13_ref, w2_ref, ys_ref,
                   w13f_ref, w2f_ref, w13b_ref, w2b_ref, sem13, sem2):
    t = pl.program_id(0)
    flag = tf_ref[t]

    def weight_copies(e, slot):
        return (pltpu.make_async_copy(w13_ref.at[e], w13f_ref.at[slot], sem13.at[slot]),
                pltpu.make_async_copy(w2_ref.at[e], w2f_ref.at[slot], sem2.at[slot]))

    @pl.when(t == 0)
    def _():
        for cp in weight_copies(te_ref[0], ts_ref[0]):
            cp.start()

    @pl.when(flag == 2)
    def _():
        slot = ts_ref[t]
        for cp in weight_copies(te_ref[t], slot):
            cp.wait()
        w13b_ref[...] = w13f_ref[slot].astype(BF16)
        w2b_ref[...] = w2f_ref[slot].astype(BF16)

        @pl.when(tn_ref[t] >= 0)
        def _():
            for cp in weight_copies(tn_ref[t], 1 - slot):
                cp.start()

    @pl.when(flag > 0)
    def _():
        x = xs_ref[...].astype(BF16)
        hid = jnp.dot(x, w13b_ref[...], preferred_element_type=F32)
        a = hid[:, :EXPERT_HIDDEN]
        act = a * jax.nn.sigmoid(a) * hid[:, EXPERT_HIDDEN:]
        ys_ref[...] = jnp.dot(act.astype(BF16), w2b_ref[...], preferred_element_type=F32)

    @pl.when(flag == 0)
    def _():
        ys_ref[...] = jnp.zeros_like(ys_ref)


def _experts(tile_flag, tile_expert, tile_next, tile_slot, xs, w13, w2):
    n_sorted, d = xs.shape
    te = EXPERT_TILE
    n_tiles = n_sorted // te
    hid2 = w13.shape[2]
    grid_spec = pltpu.PrefetchScalarGridSpec(
        num_scalar_prefetch=4,
        grid=(n_tiles,),
        in_specs=[pl.BlockSpec((te, d), lambda t, *_: (t, 0)),
                  pl.BlockSpec(memory_space=pl.ANY),
                  pl.BlockSpec(memory_space=pl.ANY)],
        out_specs=pl.BlockSpec((te, d), lambda t, *_: (t, 0)),
        scratch_shapes=[pltpu.VMEM((2, d, hid2), F32),
                        pltpu.VMEM((2, hid2 // 2, d), F32),
                        pltpu.VMEM((d, hid2), BF16),
                        pltpu.VMEM((hid2 // 2, d), BF16),
                        pltpu.SemaphoreType.DMA((2,)),
                        pltpu.SemaphoreType.DMA((2,))],
    )
    return pl.pallas_call(
        _expert_kernel,
        out_shape=jax.ShapeDtypeStruct((n_sorted, d), F32),
        grid_spec=grid_spec,
        compiler_params=_params(("arbitrary",)),
        name="experts",
    )(tile_flag, tile_expert, tile_next, tile_slot, xs, w13, w2)


def _combine_kernel(pos_ref, ys_ref, x1_ref, rw_ref, g2_ref, fw_ref, o_ref, ya_ref, yb_ref, sem):
    k = pl.program_id(0)
    nk = pl.num_programs(0)
    tm = ya_ref.shape[1]

    def issue(tile, buf):
        base = 2 * tile * tm
        for r in range(tm):
            _row_copy(ys_ref, pos_ref[base + 2 * r], ya_ref.at[buf], r, sem.at[buf]).start(priority=0)
            _row_copy(ys_ref, pos_ref[base + 2 * r + 1], yb_ref.at[buf], r, sem.at[buf]).start(priority=1)

    def wait(buf):
        pltpu.make_async_copy(ys_ref.at[pl.ds(0, tm), :], ya_ref.at[buf], sem.at[buf]).wait()
        pltpu.make_async_copy(ys_ref.at[pl.ds(0, tm), :], yb_ref.at[buf], sem.at[buf]).wait()

    def finish(buf, half):
        rows = pl.ds(half * tm, tm)
        rw = rw_ref[rows, :]
        moe = rw[:, 0:1] * ya_ref[buf] + rw[:, 1:2] * yb_ref[buf]
        x2 = x1_ref[rows, :] + g2_ref[0] * moe
        o_ref[rows, :] = (x2 * lax.rsqrt(jnp.mean(x2 * x2, axis=-1, keepdims=True) + NORM_EPS)
                          * fw_ref[...])

    @pl.when(k == 0)
    def _():
        issue(0, 0)

    wait(0)
    issue(2 * k + 1, 1)
    finish(0, 0)
    wait(1)
    issue(jnp.minimum(2 * k + 2, 2 * nk - 1), 0)
    finish(1, 1)

    @pl.when(k == nk - 1)
    def _():
        wait(0)


def _combine(pos_flat, ys, x1, route_w, gate2, final_w, seq, tm):
    n, d = x1.shape
    blk = 2 * tm
    per_b = seq // blk
    grid_spec = pltpu.PrefetchScalarGridSpec(
        num_scalar_prefetch=1,
        grid=(n // blk,),
        in_specs=[pl.BlockSpec(memory_space=pl.ANY),
                  pl.BlockSpec((blk, d), lambda i, pos: (i, 0)),
                  pl.BlockSpec((blk, LANES), lambda i, pos: (i, 0)),
                  pl.BlockSpec((1, 1, d), lambda i, pos: (i // per_b, 0, 0)),
                  pl.BlockSpec((1, d), lambda i, pos: (0, 0))],
        out_specs=pl.BlockSpec((blk, d), lambda i, pos: (i, 0)),
        scratch_shapes=[pltpu.VMEM((2, tm, d), F32),
                        pltpu.VMEM((2, tm, d), F32),
                        pltpu.SemaphoreType.DMA((2,))],
    )
    return pl.pallas_call(
        _combine_kernel,
        out_shape=jax.ShapeDtypeStruct((n, d), F32),
        grid_spec=grid_spec,
        compiler_params=_params(("arbitrary",)),
        name="combine",
    )(pos_flat, ys, x1, route_w, gate2, final_w.reshape(1, d))


def _rope_tables(seq):
    pos = np.arange(seq)
    nf = RET_QK_DIM // 4
    inv = (np.float32(ROPE_BASE) ** (-np.arange(nf, dtype=np.float32) / np.float32(nf))).astype(np.float32)
    ang_r = ((pos // GRID_W).astype(np.float32)[:, None] * inv[None, :]).astype(np.float64)
    ang_c = ((pos % GRID_W).astype(np.float32)[:, None] * inv[None, :]).astype(np.float64)
    return (jnp.asarray(np.concatenate([np.cos(ang_r), np.cos(ang_c)], axis=1), F32),
            jnp.asarray(np.concatenate([np.sin(ang_r), np.sin(ang_c)], axis=1), F32))


def _pair_split_heads(w_qk):
    d, width = w_qk.shape
    quarter = RET_QK_DIM // 4
    w4 = w_qk.reshape(d, width // RET_QK_DIM, 4, quarter)
    return jnp.stack([w4[:, :, 0], w4[:, :, 2], w4[:, :, 1], w4[:, :, 3]], axis=2).reshape(d, width)


def _layer(x, c, ctx, c_ctx, w_mod, b_mod, norm1_w, norm2_w, w_in, log_decay, gn_w, w_ret_o,
           conv_w, w_conv_o, w_out, rg_w, rg_b, re_w, re_b, w13, w2, final_w):
    batch, seq, d = x.shape
    ctx_len = ctx.shape[1]
    n = batch * seq
    off_q = 0
    off_k = off_q + RET_QK_W
    off_v = off_k + RET_QK_W
    off_g = off_v + RET_V_W
    off_cb = off_g + RET_V_W

    pad = (-(batch + 1)) % 8
    c_all = jnp.concatenate([c, c_ctx[None, :], jnp.zeros((pad, d), F32)], axis=0)
    mod = _modulation(c_all, w_mod, b_mod)
    mx = mod[:batch].reshape(batch, 1, 6, d)
    shift1, scale1, gate1, shift2, scale2, gate2 = (mx[:, :, k, :] for k in range(6))
    mc = mod[batch].reshape(1, 1, 6, d)
    shift1c, scale1c = mc[:, :, 0, :], mc[:, :, 1, :]

    w_in_b = jnp.concatenate([_pair_split_heads(w_in[:, :off_v]), w_in[:, off_v:]], axis=1).astype(BF16)
    cos_t, sin_t = _rope_tables(seq)
    k_scale = RET_QK_DIM ** -0.5

    tm_proj = min(2048, seq)
    proj = _projection(x.reshape(n, d), norm1_w, shift1, scale1, cos_t, sin_t, w_in_b, seq,
                       modes=((True, 1.0), (True, k_scale)), tm=tm_proj)
    tm_ctx = batch * ctx_len
    no_rope = jnp.zeros((tm_ctx, LANES), F32)
    kv_ctx = _projection(ctx.reshape(tm_ctx, d), norm1_w, shift1c, scale1c, no_rope, no_rope,
                         w_in_b[:, off_k:off_g], tm_ctx, modes=((False, k_scale),), tm=tm_ctx)

    log_gamma = jnp.log1p(-jnp.exp(log_decay.astype(F32)))
    yg = _retention(proj, kv_ctx, log_gamma, gn_w, batch, seq, ctx_len, off_q, off_k, off_v, off_g)

    w_router = jnp.concatenate(
        [re_w, rg_w, jnp.zeros((d, LANES - N_EXPERTS - N_GROUPS), F32)], axis=1)
    b_router = jnp.concatenate(
        [re_b, rg_b, jnp.zeros((LANES - N_EXPERTS - N_GROUPS,), F32)]).reshape(1, LANES)
    w_router_hi = w_router.astype(BF16)
    w_router_lo = (w_router - w_router_hi.astype(F32)).astype(BF16)
    w_router = jnp.concatenate([w_router_hi, w_router_lo], axis=1)
    tm_mix = min(512, seq)
    x1, h2, route_i, route_w, counts = _mix(
        yg, proj, x.reshape(n, d), w_ret_o.astype(BF16), w_conv_o.astype(BF16), w_out.astype(BF16),
        conv_w, gate1, shift2, scale2, norm2_w, w_router, b_router, seq, off_cb, tm_mix)

    te = EXPERT_TILE
    cnt = counts[0, :N_EXPERTS].astype(jnp.int32)
    padded = ((cnt + te - 1) // te) * te
    ends = jnp.cumsum(padded)
    starts = ends - padded
    expert_ids = jnp.arange(N_EXPERTS, dtype=jnp.int32)
    seg_start = jnp.sum(jnp.where(route_i[:, 0:2, None] == expert_ids, starts, 0), axis=-1)
    pos = seg_start + route_i[:, 2:4]
    pos_flat = pos.reshape(-1).astype(jnp.int32)
    n_tiles = (2 * n) // te + N_EXPERTS
    tile_row = jnp.arange(n_tiles, dtype=jnp.int32) * te
    tile_expert = jnp.minimum(jnp.sum((tile_row[:, None] >= ends[None, :]).astype(jnp.int32), axis=1),
                              N_EXPERTS - 1).astype(jnp.int32)
    new_expert = jnp.concatenate([jnp.ones((1,), bool), tile_expert[1:] != tile_expert[:-1]])
    tile_flag = jnp.where(tile_row < ends[-1], jnp.where(new_expert, 2, 1), 0).astype(jnp.int32)

    valid_tiles = ends[-1] // te
    last_tile = jnp.where(padded > 0, ends // te - 1, -1)
    tail_tile = valid_tiles + jnp.arange(N_EXPERTS, dtype=jnp.int32)
    tail_tile = jnp.where(tail_tile < n_tiles, tail_tile, -1)
    zero_tiles = jnp.concatenate([last_tile, tail_tile]).astype(jnp.int32)

    xs = _scatter_rows(pos_flat, zero_tiles, h2, n_tiles * te, min(512, seq))
    nonempty = padded > 0
    later = nonempty[None, :] & (expert_ids[None, :] > expert_ids[:, None])
    next_expert = jnp.min(jnp.where(later, expert_ids[None, :], N_EXPERTS), axis=1)
    next_expert = jnp.where(next_expert == N_EXPERTS, -1, next_expert)
    slot = (jnp.cumsum(nonempty.astype(jnp.int32)) - 1) % 2
    tile_hot = tile_expert[:, None] == expert_ids[None, :]
    tile_next = jnp.sum(jnp.where(tile_hot, next_expert, 0), axis=1).astype(jnp.int32)
    tile_slot = jnp.sum(jnp.where(tile_hot, slot, 0), axis=1).astype(jnp.int32)

    ys = _experts(tile_flag, tile_expert, tile_next, tile_slot, xs, w13, w2)
    out = _combine(pos_flat, ys, x1, route_w, gate2, final_w, seq, min(256, seq // 2))
    return out.reshape(batch, seq, d)


def kernel(x, c, ctx, c_ctx, w_mod, b_mod, norm1_w, norm2_w, w_in, ret_log_decay, ret_gn_w, w_ret_o,
           conv_w, w_conv_o, w_out, router_group_w, router_group_b, router_expert_w, router_expert_b,
           expert_w13, expert_w2, final_norm_w):
    assert w_mod.shape[0] == 1, "single-layer problem"
    return _layer(x, c, ctx, c_ctx, w_mod[0], b_mod[0], norm1_w[0], norm2_w[0], w_in[0],
                  ret_log_decay[0], ret_gn_w[0], w_ret_o[0], conv_w[0], w_conv_o[0], w_out[0],
                  router_group_w[0], router_group_b[0], router_expert_w[0], router_expert_b[0],
                  expert_w13[0], expert_w2[0], final_norm_w)
```

```python
import functools

import jax
import jax.numpy as jnp
import numpy as np
from jax import lax
from jax.experimental import pallas as pl
from jax.experimental.pallas import tpu as pltpu

GRID_W = 64
RET_HEADS = 4
RET_QK_DIM = 256
RET_V_DIM = 512
RET_QK_W = RET_HEADS * RET_QK_DIM
RET_V_W = RET_HEADS * RET_V_DIM
N_GROUPS = 4
EXPERTS_PER_GROUP = 8
N_EXPERTS = N_GROUPS * EXPERTS_PER_GROUP
EXPERT_HIDDEN = 512
ROPE_BASE = 10000.0
NORM_EPS = 1e-6
GN_EPS = 1e-5

RET_CHUNK = 256
LANES = 128
EXPERT_TILE = 256
VMEM_LIMIT = 56 * 1024 * 1024

BF16 = jnp.bfloat16
F32 = jnp.float32


def _params(sem, vmem=VMEM_LIMIT):
    return pltpu.CompilerParams(dimension_semantics=sem, vmem_limit_bytes=vmem)


def _mod_kernel(c_ref, w_ref, b_ref, o_ref):
    c = c_ref[...]
    s = c * jax.nn.sigmoid(c)
    o_ref[...] = jnp.dot(s.astype(BF16), w_ref[...].astype(BF16),
                         preferred_element_type=F32) + b_ref[...]


def _modulation(c_all, w_mod, b_mod):
    rows, d = c_all.shape
    n_out = w_mod.shape[1]
    tn = 1536
    return pl.pallas_call(
        _mod_kernel,
        out_shape=jax.ShapeDtypeStruct((rows, n_out), F32),
        grid=(n_out // tn,),
        in_specs=[pl.BlockSpec((rows, d), lambda j: (0, 0)),
                  pl.BlockSpec((d, tn), lambda j: (0, j)),
                  pl.BlockSpec((1, tn), lambda j: (0, j))],
        out_specs=pl.BlockSpec((rows, tn), lambda j: (0, j)),
        compiler_params=_params(("parallel",)),
        name="mod",
    )(c_all, w_mod, b_mod.reshape(1, n_out))


def _rope_store(acc, cos, sin, scale, o_ref):
    for hd in range(acc.shape[1] // RET_QK_DIM):
        lo = hd * RET_QK_DIM
        u1 = acc[:, lo:lo + LANES]
        u2 = acc[:, lo + LANES:lo + 2 * LANES]
        r1 = u1 * cos - u2 * sin
        r2 = u1 * sin + u2 * cos
        if scale != 1.0:
            r1 = r1 * scale
            r2 = r2 * scale
        o_ref[:, lo:lo + LANES] = r1.astype(o_ref.dtype)
        o_ref[:, lo + LANES:lo + 2 * LANES] = r2.astype(o_ref.dtype)


def _proj_kernel(x_ref, nw_ref, sh_ref, sc_ref, cos_ref, sin_ref, w_ref, o_ref, h_ref, *, modes):
    j = pl.program_id(1)

    @pl.when(j == 0)
    def _():
        x = x_ref[...]
        y = x * lax.rsqrt(jnp.mean(x * x, axis=-1, keepdims=True) + NORM_EPS) * nw_ref[...]
        h_ref[...] = (y * (1.0 + sc_ref[0]) + sh_ref[0]).astype(BF16)

    acc = jnp.dot(h_ref[...], w_ref[...], preferred_element_type=F32)
    n_special = len(modes)
    for t, (rope, scale) in enumerate(modes):
        @pl.when(j == t)
        def _(rope=rope, scale=scale):
            if rope:
                _rope_store(acc, cos_ref[...], sin_ref[...], scale, o_ref)
            else:
                o_ref[...] = (acc * scale).astype(o_ref.dtype)

    @pl.when(j >= n_special)
    def _():
        o_ref[...] = acc.astype(o_ref.dtype)


def _projection(x2d, norm_w, shift, scale, cos_t, sin_t, w_bf16, seq, modes, tm, col0=0, width=None):
    n, d = x2d.shape
    tn = 1024
    width = w_bf16.shape[1] if width is None else width
    jcol = col0 // tn
    per_b = seq // tm
    nb = shift.shape[0]
    bidx = (lambda i, j: (i // per_b, 0, 0)) if nb > 1 else (lambda i, j: (0, 0, 0))
    return pl.pallas_call(
        functools.partial(_proj_kernel, modes=modes),
        out_shape=jax.ShapeDtypeStruct((n, width), BF16),
        grid=(n // tm, width // tn),
        in_specs=[pl.BlockSpec((tm, d), lambda i, j: (i, 0)),
                  pl.BlockSpec((1, d), lambda i, j: (0, 0)),
                  pl.BlockSpec((1, 1, d), bidx),
                  pl.BlockSpec((1, 1, d), bidx),
                  pl.BlockSpec((tm, LANES), lambda i, j: (i % per_b, 0)),
                  pl.BlockSpec((tm, LANES), lambda i, j: (i % per_b, 0)),
                  pl.BlockSpec((d, tn), lambda i, j: (0, jcol + j))],
        out_specs=pl.BlockSpec((tm, tn), lambda i, j: (i, j)),
        scratch_shapes=[pltpu.VMEM((tm, d), BF16)],
        compiler_params=_params(("parallel", "arbitrary")),
        name="proj",
    )(x2d, norm_w.reshape(1, d), shift, scale, cos_t, sin_t, w_bf16)


def _dot_t0(a, b):
    return lax.dot_general(a, b, (((0,), (0,)), ((), ())), preferred_element_type=F32)


def _ret_kernel(lg_ref, q_ref, k_ref, v_ref, g_ref, kc_ref, vc_ref, gn_ref, o_ref,
                acc_ref, sf_ref, sb_ref, dm_ref, *, seq, ctx_len):
    h = pl.program_id(1)
    lgf = lg_ref[0, h]
    lgb = lg_ref[1, h]
    c = RET_CHUNK
    n_chunks = seq // c

    cpos = lax.broadcasted_iota(jnp.int32, (ctx_len, 1), 0).astype(F32)
    kc = kc_ref[...].astype(F32)
    vc = vc_ref[...]
    sf_ref[...] = _dot_t0((kc * jnp.exp((ctx_len - 1.0 - cpos) * lgf)).astype(BF16), vc)
    sb_ref[...] = _dot_t0((kc * jnp.exp(cpos * lgb)).astype(BF16), vc)

    ri = lax.broadcasted_iota(jnp.int32, (c, c), 0)
    ci = lax.broadcasted_iota(jnp.int32, (c, c), 1)
    dist = (ri - ci).astype(F32)
    dm_ref[...] = jnp.exp(jnp.abs(dist) * jnp.where(dist >= 0, lgf, lgb))

    idx = lax.broadcasted_iota(jnp.int32, (c, 1), 0).astype(F32)
    xi_f = jnp.exp((idx + 1.0) * lgf)
    zeta_f = jnp.exp((c - 1.0 - idx) * lgf)
    xi_b = jnp.exp((c - idx) * lgb)
    zeta_b = jnp.exp(idx * lgb)
    cd_f = jnp.exp(jnp.full((1, 1), float(c), F32) * lgf)
    cd_b = jnp.exp(jnp.full((1, 1), float(c), F32) * lgb)

    gn_w = gn_ref[...]

    def chunk_rows(ic):
        return pl.ds(pl.multiple_of(ic * c, c), c)

    def fwd_part(ic):
        rows = chunk_rows(ic)
        q = q_ref[rows, :]
        k = k_ref[rows, :]
        v = v_ref[rows, :]
        s = lax.dot_general(q, k, (((1,), (1,)), ((), ())), preferred_element_type=F32)
        intra = jnp.dot((s * dm_ref[...]).astype(BF16), v, preferred_element_type=F32)
        inter = jnp.dot(q, sf_ref[...].astype(BF16), preferred_element_type=F32)
        kz = (k.astype(F32) * zeta_f).astype(BF16)
        sf_ref[...] = cd_f * sf_ref[...] + _dot_t0(kz, v)
        return intra + xi_f * inter

    def bwd_part(ic):
        rows = chunk_rows(ic)
        q = q_ref[rows, :]
        k = k_ref[rows, :]
        v = v_ref[rows, :]
        inter = jnp.dot(q, sb_ref[...].astype(BF16), preferred_element_type=F32)
        kz = (k.astype(F32) * zeta_b).astype(BF16)
        sb_ref[...] = cd_b * sb_ref[...] + _dot_t0(kz, v)
        return xi_b * inter

    def finalize(ic, y):
        rows = chunk_rows(ic)
        mu = jnp.mean(y, axis=-1, keepdims=True)
        yc = y - mu
        var = jnp.mean(yc * yc, axis=-1, keepdims=True)
        yn = yc * lax.rsqrt(var + GN_EPS) * gn_w
        g = g_ref[rows, :].astype(F32)
        o_ref[rows, :] = (g * jax.nn.sigmoid(g) * yn).astype(o_ref.dtype)

    half = n_chunks // 2

    def first_half(t, carry):
        acc_ref[chunk_rows(t), :] = fwd_part(t)
        acc_ref[chunk_rows(n_chunks - 1 - t), :] = bwd_part(n_chunks - 1 - t)
        return carry

    def second_half(t, carry):
        finalize(t, acc_ref[chunk_rows(t), :] + fwd_part(t))
        u = n_chunks - 1 - t
        finalize(u, acc_ref[chunk_rows(u), :] + bwd_part(u))
        return carry

    lax.fori_loop(0, half, first_half, 0)
    lax.fori_loop(half, n_chunks, second_half, 0)


def _retention(proj, kv_ctx, log_gamma, gn_w, batch, seq, ctx_len, off_q, off_k, off_v, off_g):
    n = proj.shape[0]
    dk, dv = RET_QK_DIM, RET_V_DIM
    assert seq % (2 * RET_CHUNK) == 0, "the paired scan needs an even number of chunks"
    grid_spec = pltpu.PrefetchScalarGridSpec(
        num_scalar_prefetch=1,
        grid=(batch, RET_HEADS),
        in_specs=[pl.BlockSpec((seq, dk), lambda b, h, lg: (b, off_q // dk + h)),
                  pl.BlockSpec((seq, dk), lambda b, h, lg: (b, off_k // dk + h)),
                  pl.BlockSpec((seq, dv), lambda b, h, lg: (b, off_v // dv + h)),
                  pl.BlockSpec((seq, dv), lambda b, h, lg: (b, off_g // dv + h)),
                  pl.BlockSpec((ctx_len, dk), lambda b, h, lg: (b, h)),
                  pl.BlockSpec((ctx_len, dv), lambda b, h, lg: (b, RET_QK_W // dv + h)),
                  pl.BlockSpec((1, dv), lambda b, h, lg: (0, h))],
        out_specs=pl.BlockSpec((seq, dv), lambda b, h, lg: (b, h)),
        scratch_shapes=[pltpu.VMEM((seq, dv), F32),
                        pltpu.VMEM((dk, dv), F32),
                        pltpu.VMEM((dk, dv), F32),
                        pltpu.VMEM((RET_CHUNK, RET_CHUNK), F32)],
    )
    return pl.pallas_call(
        functools.partial(_ret_kernel, seq=seq, ctx_len=ctx_len),
        out_shape=jax.ShapeDtypeStruct((n, RET_V_W), BF16),
        grid_spec=grid_spec,
        compiler_params=_params(("parallel", "arbitrary")),
        name="ret",
    )(log_gamma, proj, proj, proj, proj, kv_ctx, kv_ctx, gn_w.reshape(1, RET_V_W))


def _mix_kernel(yg_ref, cb_ref, cc_ref, ch_ref, gr_ref, gc_ref, x_ref, wro_ref, wco_ref, wo_ref,
                cw_ref, g1_ref, sh2_ref, sc2_ref, n2_ref, wr_ref, br_ref,
                x1_ref, h2_ref, ri_ref, rw_ref, cnt_ref, carry_ref):
    i = pl.program_id(0)
    tm = x_ref.shape[0]

    @pl.when(i == 0)
    def _():
        carry_ref[...] = jnp.zeros_like(carry_ref)

    ret_branch = jnp.dot(yg_ref[...], wro_ref[...], preferred_element_type=F32)

    p = cc_ref[...].astype(F32) * ch_ref[...].astype(F32)
    tpos = lax.broadcasted_iota(jnp.int32, (tm, 1), 0) % GRID_W
    prev = jnp.where(tpos != 0, pltpu.roll(p, 1, axis=0), 0.0)
    nxt = jnp.where(tpos != GRID_W - 1, pltpu.roll(p, tm - 1, axis=0), 0.0)
    cw = cw_ref[...]
    u = cw[0:1, :] * prev + cw[1:2, :] * p + cw[2:3, :] * nxt
    conv_in = (cb_ref[...].astype(F32) * u).astype(BF16)
    conv_branch = jnp.dot(conv_in, wco_ref[...], preferred_element_type=F32)

    merged = (jax.nn.sigmoid(gr_ref[...].astype(F32)) * ret_branch
              + jax.nn.sigmoid(gc_ref[...].astype(F32)) * conv_branch)
    mixed = jnp.dot(merged.astype(BF16), wo_ref[...], preferred_element_type=F32)
    x1 = x_ref[...] + g1_ref[0] * mixed
    x1_ref[...] = x1

    y = x1 * lax.rsqrt(jnp.mean(x1 * x1, axis=-1, keepdims=True) + NORM_EPS) * n2_ref[...]
    h2 = y * (1.0 + sc2_ref[0]) + sh2_ref[0]
    h2_ref[...] = h2

    h_hi = h2.astype(BF16)
    h_lo = (h2 - h_hi.astype(F32)).astype(BF16)
    hw = jnp.dot(h_hi, wr_ref[...], preferred_element_type=F32)
    lw = jnp.dot(h_lo, wr_ref[:, :LANES], preferred_element_type=F32)
    logits = hw[:, :LANES] + (hw[:, LANES:] + lw) + br_ref[...]
    lane = lax.broadcasted_iota(jnp.int32, (tm, LANES), 1)
    neg = jnp.float32(-jnp.inf)
    big = jnp.int32(1 << 20)
    is_g = (lane >= N_EXPERTS) & (lane < N_EXPERTS + N_GROUPS)
    gl = jnp.where(is_g, logits, neg)
    gmax = jnp.max(gl, axis=1, keepdims=True)
    gidx = jnp.min(jnp.where(gl == gmax, lane, big), axis=1, keepdims=True) - N_EXPERTS
    gsum = jnp.sum(jnp.where(is_g, jnp.exp(logits - gmax), 0.0), axis=1, keepdims=True)
    g_w = 1.0 / gsum

    in_grp = (lane // EXPERTS_PER_GROUP) == gidx
    el = jnp.where(in_grp, logits, neg)
    emax = jnp.max(el, axis=1, keepdims=True)
    ex = jnp.where(in_grp, jnp.exp(logits - emax), 0.0)
    prob = ex / jnp.sum(ex, axis=1, keepdims=True)
    pm = jnp.where(in_grp, prob, -1.0)
    p1 = jnp.max(pm, axis=1, keepdims=True)
    i1 = jnp.min(jnp.where(pm == p1, lane, big), axis=1, keepdims=True)
    pm2 = jnp.where(lane == i1, -1.0, pm)
    p2 = jnp.max(pm2, axis=1, keepdims=True)
    i2 = jnp.min(jnp.where(pm2 == p2, lane, big), axis=1, keepdims=True)
    den = p1 + p2
    c1 = g_w * (p1 / den)
    c2 = g_w * (p2 / den)

    hot1 = lane == i1
    hot2 = lane == i2
    onehot = jnp.where(hot1, 1.0, jnp.where(hot2, 1.0, 0.0))
    tri = (lax.broadcasted_iota(jnp.int32, (tm, tm), 0)
           > lax.broadcasted_iota(jnp.int32, (tm, tm), 1))
    ranks = jnp.dot(jnp.where(tri, 1.0, 0.0).astype(BF16), onehot.astype(BF16),
                    preferred_element_type=F32) + carry_ref[...]
    r1 = jnp.sum(jnp.where(hot1, ranks, 0.0), axis=1, keepdims=True).astype(jnp.int32)
    r2 = jnp.sum(jnp.where(hot2, ranks, 0.0), axis=1, keepdims=True).astype(jnp.int32)
    carry_ref[...] = carry_ref[...] + jnp.sum(onehot, axis=0, keepdims=True)
    cnt_ref[...] = carry_ref[...]

    ri_ref[...] = jnp.where(lane == 0, i1, jnp.where(lane == 1, i2,
                            jnp.where(lane == 2, r1, jnp.where(lane == 3, r2, 0))))
    rw_ref[...] = jnp.where(lane == 0, c1, jnp.where(lane == 1, c2, 0.0))


def _mix(yg, proj, x2d, w_ret_o, w_conv_o, w_out, conv_w, gate1, shift2, scale2, norm2_w,
         w_router, b_router, seq, off_cb, tm):
    n, d = x2d.shape
    per_b = seq // tm
    cblk = off_cb // d
    row = lambda i: (i, 0)
    const2 = lambda i: (0, 0)
    bidx = lambda i: (i // per_b, 0, 0)
    col = lambda k: (lambda i: (i, cblk + k))
    return pl.pallas_call(
        _mix_kernel,
        out_shape=(jax.ShapeDtypeStruct((n, d), F32),
                   jax.ShapeDtypeStruct((n, d), F32),
                   jax.ShapeDtypeStruct((n, LANES), jnp.int32),
                   jax.ShapeDtypeStruct((n, LANES), F32),
                   jax.ShapeDtypeStruct((1, LANES), F32)),
        grid=(n // tm,),
        in_specs=[pl.BlockSpec((tm, RET_V_W), row),
                  pl.BlockSpec((tm, d), col(0)),
                  pl.BlockSpec((tm, d), col(1)),
                  pl.BlockSpec((tm, d), col(2)),
                  pl.BlockSpec((tm, d), col(3)),
                  pl.BlockSpec((tm, d), col(4)),
                  pl.BlockSpec((tm, d), row),
                  pl.BlockSpec((RET_V_W, d), const2, pipeline_mode=pl.Buffered(1)),
                  pl.BlockSpec((d, d), const2, pipeline_mode=pl.Buffered(1)),
                  pl.BlockSpec((d, d), const2, pipeline_mode=pl.Buffered(1)),
                  pl.BlockSpec((3, d), const2),
                  pl.BlockSpec((1, 1, d), bidx),
                  pl.BlockSpec((1, 1, d), bidx),
                  pl.BlockSpec((1, 1, d), bidx),
                  pl.BlockSpec((1, d), const2),
                  pl.BlockSpec((d, 2 * LANES), const2),
                  pl.BlockSpec((1, LANES), const2)],
        out_specs=(pl.BlockSpec((tm, d), row),
                   pl.BlockSpec((tm, d), row),
                   pl.BlockSpec((tm, LANES), row),
                   pl.BlockSpec((tm, LANES), row),
                   pl.BlockSpec((1, LANES), const2)),
        scratch_shapes=[pltpu.VMEM((1, LANES), F32)],
        compiler_params=_params(("arbitrary",)),
        name="mix",
    )(yg, proj, proj, proj, proj, proj, x2d, w_ret_o, w_conv_o, w_out, conv_w,
      gate1, shift2, scale2, norm2_w.reshape(1, d), w_router, b_router)


def _row_copy(src_ref, src_row, dst_ref, dst_row, sem):
    return pltpu.make_async_copy(src_ref.at[pl.ds(src_row, 1), :],
                                 dst_ref.at[pl.ds(dst_row, 1), :], sem)


def _scatter_kernel(pos_ref, zt_ref, h_ref, xs_ref, zero_ref, sem, zsem):
    i = pl.program_id(0)
    tm = h_ref.shape[0]
    te = zero_ref.shape[0]
    base = i * tm

    @pl.when(i == 0)
    def _():
        zero_ref[...] = jnp.zeros_like(zero_ref)

        def tile_copy(z):
            row = pl.multiple_of(zt_ref[z] * te, te)
            return pltpu.make_async_copy(zero_ref, xs_ref.at[pl.ds(row, te), :], zsem)

        def zissue(z, carry):
            @pl.when(zt_ref[z] >= 0)
            def _():
                tile_copy(z).start()
            return carry

        def zdrain(z, carry):
            @pl.when(zt_ref[z] >= 0)
            def _():
                tile_copy(z).wait()
            return carry

        lax.fori_loop(0, zt_ref.shape[0], zissue, 0)
        lax.fori_loop(0, zt_ref.shape[0], zdrain, 0)

    for r in range(tm):
        a = 2 * base + 2 * r
        _row_copy(h_ref, r, xs_ref, pos_ref[a], sem).start(priority=0)
        _row_copy(h_ref, r, xs_ref, pos_ref[a + 1], sem).start(priority=1)

    for _ in range(2):
        pltpu.make_async_copy(h_ref, xs_ref.at[pl.ds(0, tm), :], sem).wait()


def _scatter_rows(pos_flat, zero_tiles, h2, n_sorted, tm):
    n, d = h2.shape
    grid_spec = pltpu.PrefetchScalarGridSpec(
        num_scalar_prefetch=2,
        grid=(n // tm,),
        in_specs=[pl.BlockSpec((tm, d), lambda i, pos, zt: (i, 0))],
        out_specs=pl.BlockSpec(memory_space=pl.ANY),
        scratch_shapes=[pltpu.VMEM((EXPERT_TILE, d), F32),
                        pltpu.SemaphoreType.DMA,
                        pltpu.SemaphoreType.DMA],
    )
    return pl.pallas_call(
        _scatter_kernel,
        out_shape=jax.ShapeDtypeStruct((n_sorted, d), F32),
        grid_spec=grid_spec,
        compiler_params=_params(("arbitrary",)),
        name="scatter",
    )(pos_flat, zero_tiles, h2)


def _expert_kernel(tf_ref, te_ref, tn_ref, ts_ref, xs_ref, w13_ref, w2_ref, ys_ref,
                   w13f_ref, w2f_ref, w13b_ref, w2b_ref, sem13, sem2):
    t = pl.program_id(0)
    flag = tf_ref[t]

    def weight_copies(e, slot):
        return (pltpu.make_async_copy(w13_ref.at[e], w13f_ref.at[slot], sem13.at[slot]),
                pltpu.make_async_copy(w2_ref.at[e], w2f_ref.at[slot], sem2.at[slot]))

    @pl.when(t == 0)
    def _():
        for cp in weight_copies(te_ref[0], ts_ref[0]):
            cp.start()

    @pl.when(flag == 2)
    def _():
        slot = ts_ref[t]
        for cp in weight_copies(te_ref[t], slot):
            cp.wait()
        w13b_ref[...] = w13f_ref[slot].astype(BF16)
        w2b_ref[...] = w2f_ref[slot].astype(BF16)

        @pl.when(tn_ref[t] >= 0)
        def _():
            for cp in weight_copies(tn_ref[t], 1 - slot):
                cp.start()

    @pl.when(flag > 0)
    def _():
        x = xs_ref[...].astype(BF16)
        hid = jnp.dot(x, w13b_ref[...], preferred_element_type=F32)
        a = hid[:, :EXPERT_HIDDEN]
        act = a * jax.nn.sigmoid(a) * hid[:, EXPERT_HIDDEN:]
        ys_ref[...] = jnp.dot(act.astype(BF16), w2b_ref[...], preferred_element_type=F32)

    @pl.when(flag == 0)
    def _():
        ys_ref[...] = jnp.zeros_like(ys_ref)


def _experts(tile_flag, tile_expert, tile_next, tile_slot, xs, w13, w2):
    n_sorted, d = xs.shape
    te = EXPERT_TILE
    n_tiles = n_sorted // te
    hid2 = w13.shape[2]
    grid_spec = pltpu.PrefetchScalarGridSpec(
        num_scalar_prefetch=4,
        grid=(n_tiles,),
        in_specs=[pl.BlockSpec((te, d), lambda t, *_: (t, 0)),
                  pl.BlockSpec(memory_space=pl.ANY),
                  pl.BlockSpec(memory_space=pl.ANY)],
        out_specs=pl.BlockSpec((te, d), lambda t, *_: (t, 0)),
        scratch_shapes=[pltpu.VMEM((2, d, hid2), F32),
                        pltpu.VMEM((2, hid2 // 2, d), F32),
                        pltpu.VMEM((d, hid2), BF16),
                        pltpu.VMEM((hid2 // 2, d), BF16),
                        pltpu.SemaphoreType.DMA((2,)),
                        pltpu.SemaphoreType.DMA((2,))],
    )
    return pl.pallas_call(
        _expert_kernel,
        out_shape=jax.ShapeDtypeStruct((n_sorted, d), F32),
        grid_spec=grid_spec,
        compiler_params=_params(("arbitrary",)),
        name="experts",
    )(tile_flag, tile_expert, tile_next, tile_slot, xs, w13, w2)


def _combine_kernel(pos_ref, ys_ref, x1_ref, rw_ref, g2_ref, fw_ref, o_ref, ya_ref, yb_ref, sem):
    k = pl.program_id(0)
    nk = pl.num_programs(0)
    tm = ya_ref.shape[1]

    def issue(tile, buf):
        base = 2 * tile * tm
        for r in range(tm):
            _row_copy(ys_ref, pos_ref[base + 2 * r], ya_ref.at[buf], r, sem.at[buf]).start(priority=0)
            _row_copy(ys_ref, pos_ref[base + 2 * r + 1], yb_ref.at[buf], r, sem.at[buf]).start(priority=1)

    def wait(buf):
        pltpu.make_async_copy(ys_ref.at[pl.ds(0, tm), :], ya_ref.at[buf], sem.at[buf]).wait()
        pltpu.make_async_copy(ys_ref.at[pl.ds(0, tm), :], yb_ref.at[buf], sem.at[buf]).wait()

    def finish(buf, half):
        rows = pl.ds(half * tm, tm)
        rw = rw_ref[rows, :]
        moe = rw[:, 0:1] * ya_ref[buf] + rw[:, 1:2] * yb_ref[buf]
        x2 = x1_ref[rows, :] + g2_ref[0] * moe
        o_ref[rows, :] = (x2 * lax.rsqrt(jnp.mean(x2 * x2, axis=-1, keepdims=True) + NORM_EPS)
                          * fw_ref[...])

    @pl.when(k == 0)
    def _():
        issue(0, 0)

    wait(0)
    issue(2 * k + 1, 1)
    finish(0, 0)
    wait(1)
    issue(jnp.minimum(2 * k + 2, 2 * nk - 1), 0)
    finish(1, 1)

    @pl.when(k == nk - 1)
    def _():
        wait(0)


def _combine(pos_flat, ys, x1, route_w, gate2, final_w, seq, tm):
    n, d = x1.shape
    blk = 2 * tm
    per_b = seq // blk
    grid_spec = pltpu.PrefetchScalarGridSpec(
        num_scalar_prefetch=1,
        grid=(n // blk,),
        in_specs=[pl.BlockSpec(memory_space=pl.ANY),
                  pl.BlockSpec((blk, d), lambda i, pos: (i, 0)),
                  pl.BlockSpec((blk, LANES), lambda i, pos: (i, 0)),
                  pl.BlockSpec((1, 1, d), lambda i, pos: (i // per_b, 0, 0)),
                  pl.BlockSpec((1, d), lambda i, pos: (0, 0))],
        out_specs=pl.BlockSpec((blk, d), lambda i, pos: (i, 0)),
        scratch_shapes=[pltpu.VMEM((2, tm, d), F32),
                        pltpu.VMEM((2, tm, d), F32),
                        pltpu.SemaphoreType.DMA((2,))],
    )
    return pl.pallas_call(
        _combine_kernel,
        out_shape=jax.ShapeDtypeStruct((n, d), F32),
        grid_spec=grid_spec,
        compiler_params=_params(("arbitrary",)),
        name="combine",
    )(pos_flat, ys, x1, route_w, gate2, final_w.reshape(1, d))


def _rope_tables(seq):
    pos = np.arange(seq)
    nf = RET_QK_DIM // 4
    inv = (np.float32(ROPE_BASE) ** (-np.arange(nf, dtype=np.float32) / np.float32(nf))).astype(np.float32)
    ang_r = ((pos // GRID_W).astype(np.float32)[:, None] * inv[None, :]).astype(np.float64)
    ang_c = ((pos % GRID_W).astype(np.float32)[:, None] * inv[None, :]).astype(np.float64)
    return (jnp.asarray(np.concatenate([np.cos(ang_r), np.cos(ang_c)], axis=1), F32),
            jnp.asarray(np.concatenate([np.sin(ang_r), np.sin(ang_c)], axis=1), F32))


def _pair_split_heads(w_qk):
    d, width = w_qk.shape
    quarter = RET_QK_DIM // 4
    w4 = w_qk.reshape(d, width // RET_QK_DIM, 4, quarter)
    return jnp.stack([w4[:, :, 0], w4[:, :, 2], w4[:, :, 1], w4[:, :, 3]], axis=2).reshape(d, width)


def _layer(x, c, ctx, c_ctx, w_mod, b_mod, norm1_w, norm2_w, w_in, log_decay, gn_w, w_ret_o,
           conv_w, w_conv_o, w_out, rg_w, rg_b, re_w, re_b, w13, w2, final_w):
    batch, seq, d = x.shape
    ctx_len = ctx.shape[1]
    n = batch * seq
    off_q = 0
    off_k = off_q + RET_QK_W
    off_v = off_k + RET_QK_W
    off_g = off_v + RET_V_W
    off_cb = off_g + RET_V_W

    pad = (-(batch + 1)) % 8
    c_all = jnp.concatenate([c, c_ctx[None, :], jnp.zeros((pad, d), F32)], axis=0)
    mod = _modulation(c_all, w_mod, b_mod)
    mx = mod[:batch].reshape(batch, 1, 6, d)
    shift1, scale1, gate1, shift2, scale2, gate2 = (mx[:, :, k, :] for k in range(6))
    mc = mod[batch].reshape(1, 1, 6, d)
    shift1c, scale1c = mc[:, :, 0, :], mc[:, :, 1, :]

    w_in_b = jnp.concatenate([_pair_split_heads(w_in[:, :off_v]), w_in[:, off_v:]], axis=1).astype(BF16)
    cos_t, sin_t = _rope_tables(seq)
    k_scale = RET_QK_DIM ** -0.5

    tm_proj = min(2048, seq)
    proj = _projection(x.reshape(n, d), norm1_w, shift1, scale1, cos_t, sin_t, w_in_b, seq,
                       modes=((True, 1.0), (True, k_scale)), tm=tm_proj)
    tm_ctx = batch * ctx_len
    no_rope = jnp.zeros((tm_ctx, LANES), F32)
    kv_ctx = _projection(ctx.reshape(tm_ctx, d), norm1_w, shift1c, scale1c, no_rope, no_rope,
                         w_in_b, tm_ctx, modes=((False, k_scale),), tm=tm_ctx,
                         col0=off_k, width=off_g - off_k)

    log_gamma = jnp.log1p(-jnp.exp(log_decay.astype(F32)))
    yg = _retention(proj, kv_ctx, log_gamma, gn_w, batch, seq, ctx_len, off_q, off_k, off_v, off_g)

    w_router = jnp.concatenate(
        [re_w, rg_w, jnp.zeros((d, LANES - N_EXPERTS - N_GROUPS), F32)], axis=1)
    b_router = jnp.concatenate(
        [re_b, rg_b, jnp.zeros((LANES - N_EXPERTS - N_GROUPS,), F32)]).reshape(1, LANES)
    w_router_hi = w_router.astype(BF16)
    w_router_lo = (w_router - w_router_hi.astype(F32)).astype(BF16)
    w_router = jnp.concatenate([w_router_hi, w_router_lo], axis=1)
    tm_mix = min(512, seq)
    x1, h2, route_i, route_w, counts = _mix(
        yg, proj, x.reshape(n, d), w_ret_o.astype(BF16), w_conv_o.astype(BF16), w_out.astype(BF16),
        conv_w, gate1, shift2, scale2, norm2_w, w_router, b_router, seq, off_cb, tm_mix)

    te = EXPERT_TILE
    cnt = counts[0, :N_EXPERTS].astype(jnp.int32)
    padded = ((cnt + te - 1) // te) * te
    ends = jnp.cumsum(padded)
    starts = ends - padded
    expert_ids = jnp.arange(N_EXPERTS, dtype=jnp.int32)
    seg_start = jnp.sum(jnp.where(route_i[:, 0:2, None] == expert_ids, starts, 0), axis=-1)
    pos = seg_start + route_i[:, 2:4]
    pos_flat = pos.reshape(-1).astype(jnp.int32)
    n_tiles = (2 * n) // te + N_EXPERTS
    tile_row = jnp.arange(n_tiles, dtype=jnp.int32) * te
    tile_expert = jnp.minimum(jnp.sum((tile_row[:, None] >= ends[None, :]).astype(jnp.int32), axis=1),
                              N_EXPERTS - 1).astype(jnp.int32)
    new_expert = jnp.concatenate([jnp.ones((1,), bool), tile_expert[1:] != tile_expert[:-1]])
    tile_flag = jnp.where(tile_row < ends[-1], jnp.where(new_expert, 2, 1), 0).astype(jnp.int32)

    valid_tiles = ends[-1] // te
    last_tile = jnp.where(padded > 0, ends // te - 1, -1)
    tail_tile = valid_tiles + jnp.arange(N_EXPERTS, dtype=jnp.int32)
    tail_tile = jnp.where(tail_tile < n_tiles, tail_tile, -1)
    zero_tiles = jnp.concatenate([last_tile, tail_tile]).astype(jnp.int32)

    xs = _scatter_rows(pos_flat, zero_tiles, h2, n_tiles * te, min(1024, seq))
    nonempty = padded > 0
    later = nonempty[None, :] & (expert_ids[None, :] > expert_ids[:, None])
    next_expert = jnp.min(jnp.where(later, expert_ids[None, :], N_EXPERTS), axis=1)
    next_expert = jnp.where(next_expert == N_EXPERTS, -1, next_expert)
    slot = (jnp.cumsum(nonempty.astype(jnp.int32)) - 1) % 2
    tile_hot = tile_expert[:, None] == expert_ids[None, :]
    tile_next = jnp.sum(jnp.where(tile_hot, next_expert, 0), axis=1).astype(jnp.int32)
    tile_slot = jnp.sum(jnp.where(tile_hot, slot, 0), axis=1).astype(jnp.int32)

    ys = _experts(tile_flag, tile_expert, tile_next, tile_slot, xs, w13, w2)
    out = _combine(pos_flat, ys, x1, route_w, gate2, final_w, seq, min(256, seq // 2))
    return out.reshape(batch, seq, d)


def kernel(x, c, ctx, c_ctx, w_mod, b_mod, norm1_w, norm2_w, w_in, ret_log_decay, ret_gn_w, w_ret_o,
           conv_w, w_conv_o, w_out, router_group_w, router_group_b, router_expert_w, router_expert_b,
           expert_w13, expert_w2, final_norm_w):
    assert w_mod.shape[0] == 1, "single-layer problem"
    return _layer(x, c, ctx, c_ctx, w_mod[0], b_mod[0], norm1_w[0], norm2_w[0], w_in[0],
                  ret_log_decay[0], ret_gn_w[0], w_ret_o[0], conv_w[0], w_conv_o[0], w_out[0],
                  router_group_w[0], router_group_b[0], router_expert_w[0], router_expert_b[0],
                  expert_w13[0], expert_w2[0], final_norm_w)
```

```python
import functools

import jax
import jax.numpy as jnp
import numpy as np
from jax import lax
from jax.experimental import pallas as pl
from jax.experimental.pallas import tpu as pltpu

GRID_W = 64
RET_HEADS = 4
RET_QK_DIM = 256
RET_V_DIM = 512
RET_QK_W = RET_HEADS * RET_QK_DIM
RET_V_W = RET_HEADS * RET_V_DIM
N_GROUPS = 4
EXPERTS_PER_GROUP = 8
N_EXPERTS = N_GROUPS * EXPERTS_PER_GROUP
EXPERT_HIDDEN = 512
ROPE_BASE = 10000.0
NORM_EPS = 1e-6
GN_EPS = 1e-5

RET_CHUNK = 256
LANES = 128
EXPERT_TILE = 256
VMEM_LIMIT = 56 * 1024 * 1024

BF16 = jnp.bfloat16
F32 = jnp.float32


def _params(sem, vmem=VMEM_LIMIT):
    return pltpu.CompilerParams(dimension_semantics=sem, vmem_limit_bytes=vmem)


def _mod_kernel(c_ref, w_ref, b_ref, o_ref):
    c = c_ref[...]
    s = c * jax.nn.sigmoid(c)
    o_ref[...] = jnp.dot(s.astype(BF16), w_ref[...].astype(BF16),
                         preferred_element_type=F32) + b_ref[...]


def _modulation(c_all, w_mod, b_mod):
    rows, d = c_all.shape
    n_out = w_mod.shape[1]
    tn = 1536
    return pl.pallas_call(
        _mod_kernel,
        out_shape=jax.ShapeDtypeStruct((rows, n_out), F32),
        grid=(n_out // tn,),
        in_specs=[pl.BlockSpec((rows, d), lambda j: (0, 0)),
                  pl.BlockSpec((d, tn), lambda j: (0, j)),
                  pl.BlockSpec((1, tn), lambda j: (0, j))],
        out_specs=pl.BlockSpec((rows, tn), lambda j: (0, j)),
        compiler_params=_params(("parallel",)),
        name="mod",
    )(c_all, w_mod, b_mod.reshape(1, n_out))


def _rope_store(acc, cos, sin, scale, o_ref):
    for hd in range(acc.shape[1] // RET_QK_DIM):
        lo = hd * RET_QK_DIM
        u1 = acc[:, lo:lo + LANES]
        u2 = acc[:, lo + LANES:lo + 2 * LANES]
        r1 = u1 * cos - u2 * sin
        r2 = u1 * sin + u2 * cos
        if scale != 1.0:
            r1 = r1 * scale
            r2 = r2 * scale
        o_ref[:, lo:lo + LANES] = r1.astype(o_ref.dtype)
        o_ref[:, lo + LANES:lo + 2 * LANES] = r2.astype(o_ref.dtype)


def _proj_kernel(x_ref, nw_ref, sh_ref, sc_ref, cos_ref, sin_ref, w_ref, o_ref, h_ref, *, modes):
    j = pl.program_id(1)

    @pl.when(j == 0)
    def _():
        x = x_ref[...]
        y = x * lax.rsqrt(jnp.mean(x * x, axis=-1, keepdims=True) + NORM_EPS) * nw_ref[...]
        h_ref[...] = (y * (1.0 + sc_ref[0]) + sh_ref[0]).astype(BF16)

    acc = jnp.dot(h_ref[...], w_ref[...], preferred_element_type=F32)
    n_special = len(modes)
    for t, (rope, scale) in enumerate(modes):
        @pl.when(j == t)
        def _(rope=rope, scale=scale):
            if rope:
                _rope_store(acc, cos_ref[...], sin_ref[...], scale, o_ref)
            else:
                o_ref[...] = (acc * scale).astype(o_ref.dtype)

    @pl.when(j >= n_special)
    def _():
        o_ref[...] = acc.astype(o_ref.dtype)


def _projection(x2d, norm_w, shift, scale, cos_t, sin_t, w_bf16, seq, modes, tm, col0=0, width=None):
    n, d = x2d.shape
    tn = 1024
    width = w_bf16.shape[1] if width is None else width
    jcol = col0 // tn
    per_b = seq // tm
    nb = shift.shape[0]
    bidx = (lambda i, j: (i // per_b, 0, 0)) if nb > 1 else (lambda i, j: (0, 0, 0))
    return pl.pallas_call(
        functools.partial(_proj_kernel, modes=modes),
        out_shape=jax.ShapeDtypeStruct((n, width), BF16),
        grid=(n // tm, width // tn),
        in_specs=[pl.BlockSpec((tm, d), lambda i, j: (i, 0)),
                  pl.BlockSpec((1, d), lambda i, j: (0, 0)),
                  pl.BlockSpec((1, 1, d), bidx),
                  pl.BlockSpec((1, 1, d), bidx),
                  pl.BlockSpec((tm, LANES), lambda i, j: (i % per_b, 0)),
                  pl.BlockSpec((tm, LANES), lambda i, j: (i % per_b, 0)),
                  pl.BlockSpec((d, tn), lambda i, j: (0, jcol + j))],
        out_specs=pl.BlockSpec((tm, tn), lambda i, j: (i, j)),
        scratch_shapes=[pltpu.VMEM((tm, d), BF16)],
        compiler_params=_params(("parallel", "arbitrary")),
        name="proj",
    )(x2d, norm_w.reshape(1, d), shift, scale, cos_t, sin_t, w_bf16)


def _dot_t0(a, b):
    return lax.dot_general(a, b, (((0,), (0,)), ((), ())), preferred_element_type=F32)


def _ret_kernel(lg_ref, q_ref, k_ref, v_ref, g_ref, kc_ref, vc_ref, gn_ref, o_ref,
                acc_ref, sf_ref, sb_ref, dm_ref, *, seq, ctx_len):
    h = pl.program_id(1)
    lgf = lg_ref[0, h]
    lgb = lg_ref[1, h]
    c = RET_CHUNK
    n_chunks = seq // c

    cpos = lax.broadcasted_iota(jnp.int32, (ctx_len, 1), 0).astype(F32)
    kc = kc_ref[...].astype(F32)
    vc = vc_ref[...]
    sf_ref[...] = _dot_t0((kc * jnp.exp((ctx_len - 1.0 - cpos) * lgf)).astype(BF16), vc)
    sb_ref[...] = _dot_t0((kc * jnp.exp(cpos * lgb)).astype(BF16), vc)

    ri = lax.broadcasted_iota(jnp.int32, (c, c), 0)
    ci = lax.broadcasted_iota(jnp.int32, (c, c), 1)
    dist = (ri - ci).astype(F32)
    dm_ref[...] = jnp.exp(jnp.abs(dist) * jnp.where(dist >= 0, lgf, lgb))

    idx = lax.broadcasted_iota(jnp.int32, (c, 1), 0).astype(F32)
    xi_f = jnp.exp((idx + 1.0) * lgf)
    zeta_f = jnp.exp((c - 1.0 - idx) * lgf)
    xi_b = jnp.exp((c - idx) * lgb)
    zeta_b = jnp.exp(idx * lgb)
    cd_f = jnp.exp(jnp.full((1, 1), float(c), F32) * lgf)
    cd_b = jnp.exp(jnp.full((1, 1), float(c), F32) * lgb)

    gn_w = gn_ref[...]

    def chunk_rows(ic):
        return pl.ds(pl.multiple_of(ic * c, c), c)

    def fwd_part(ic):
        rows = chunk_rows(ic)
        q = q_ref[rows, :]
        k = k_ref[rows, :]
        v = v_ref[rows, :]
        s = lax.dot_general(q, k, (((1,), (1,)), ((), ())), preferred_element_type=F32)
        intra = jnp.dot((s * dm_ref[...]).astype(BF16), v, preferred_element_type=F32)
        inter = jnp.dot(q, sf_ref[...].astype(BF16), preferred_element_type=F32)
        kz = (k.astype(F32) * zeta_f).astype(BF16)
        sf_ref[...] = cd_f * sf_ref[...] + _dot_t0(kz, v)
        return intra + xi_f * inter

    def bwd_part(ic):
        rows = chunk_rows(ic)
        q = q_ref[rows, :]
        k = k_ref[rows, :]
        v = v_ref[rows, :]
        inter = jnp.dot(q, sb_ref[...].astype(BF16), preferred_element_type=F32)
        kz = (k.astype(F32) * zeta_b).astype(BF16)
        sb_ref[...] = cd_b * sb_ref[...] + _dot_t0(kz, v)
        return xi_b * inter

    def finalize(ic, y):
        rows = chunk_rows(ic)
        mu = jnp.mean(y, axis=-1, keepdims=True)
        yc = y - mu
        var = jnp.mean(yc * yc, axis=-1, keepdims=True)
        yn = yc * lax.rsqrt(var + GN_EPS) * gn_w
        g = g_ref[rows, :].astype(F32)
        o_ref[rows, :] = (g * jax.nn.sigmoid(g) * yn).astype(o_ref.dtype)

    half = n_chunks // 2

    def first_half(t, carry):
        acc_ref[chunk_rows(t), :] = fwd_part(t)
        acc_ref[chunk_rows(n_chunks - 1 - t), :] = bwd_part(n_chunks - 1 - t)
        return carry

    def second_half(t, carry):
        finalize(t, acc_ref[chunk_rows(t), :] + fwd_part(t))
        u = n_chunks - 1 - t
        finalize(u, acc_ref[chunk_rows(u), :] + bwd_part(u))
        return carry

    lax.fori_loop(0, half, first_half, 0)
    lax.fori_loop(half, n_chunks, second_half, 0)


def _retention(proj, kv_ctx, log_gamma, gn_w, batch, seq, ctx_len, off_q, off_k, off_v, off_g):
    n = proj.shape[0]
    dk, dv = RET_QK_DIM, RET_V_DIM
    assert seq % (2 * RET_CHUNK) == 0, "the paired scan needs an even number of chunks"
    grid_spec = pltpu.PrefetchScalarGridSpec(
        num_scalar_prefetch=1,
        grid=(batch, RET_HEADS),
        in_specs=[pl.BlockSpec((seq, dk), lambda b, h, lg: (b, off_q // dk + h)),
                  pl.BlockSpec((seq, dk), lambda b, h, lg: (b, off_k // dk + h)),
                  pl.BlockSpec((seq, dv), lambda b, h, lg: (b, off_v // dv + h)),
                  pl.BlockSpec((seq, dv), lambda b, h, lg: (b, off_g // dv + h)),
                  pl.BlockSpec((ctx_len, dk), lambda b, h, lg: (b, h)),
                  pl.BlockSpec((ctx_len, dv), lambda b, h, lg: (b, RET_QK_W // dv + h)),
                  pl.BlockSpec((1, dv), lambda b, h, lg: (0, h))],
        out_specs=pl.BlockSpec((seq, dv), lambda b, h, lg: (b, h)),
        scratch_shapes=[pltpu.VMEM((seq, dv), F32),
                        pltpu.VMEM((dk, dv), F32),
                        pltpu.VMEM((dk, dv), F32),
                        pltpu.VMEM((RET_CHUNK, RET_CHUNK), F32)],
    )
    return pl.pallas_call(
        functools.partial(_ret_kernel, seq=seq, ctx_len=ctx_len),
        out_shape=jax.ShapeDtypeStruct((n, RET_V_W), BF16),
        grid_spec=grid_spec,
        compiler_params=_params(("parallel", "arbitrary")),
        name="ret",
    )(log_gamma, proj, proj, proj, proj, kv_ctx, kv_ctx, gn_w.reshape(1, RET_V_W))


def _mix_kernel(yg_ref, cb_ref, cc_ref, ch_ref, gr_ref, gc_ref, x_ref, wro_ref, wco_ref, wo_ref,
                cw_ref, g1_ref, sh2_ref, sc2_ref, n2_ref, wr_ref, br_ref,
                x1_ref, h2_ref, ri_ref, rw_ref, cnt_ref, carry_ref):
    i = pl.program_id(0)
    tm = x_ref.shape[0]

    @pl.when(i == 0)
    def _():
        carry_ref[...] = jnp.zeros_like(carry_ref)

    ret_branch = jnp.dot(yg_ref[...], wro_ref[...], preferred_element_type=F32)

    p = cc_ref[...].astype(F32) * ch_ref[...].astype(F32)
    tpos = lax.broadcasted_iota(jnp.int32, (tm, 1), 0) % GRID_W
    prev = jnp.where(tpos != 0, pltpu.roll(p, 1, axis=0), 0.0)
    nxt = jnp.where(tpos != GRID_W - 1, pltpu.roll(p, tm - 1, axis=0), 0.0)
    cw = cw_ref[...]
    u = cw[0:1, :] * prev + cw[1:2, :] * p + cw[2:3, :] * nxt
    conv_in = (cb_ref[...].astype(F32) * u).astype(BF16)
    conv_branch = jnp.dot(conv_in, wco_ref[...], preferred_element_type=F32)

    merged = (jax.nn.sigmoid(gr_ref[...].astype(F32)) * ret_branch
              + jax.nn.sigmoid(gc_ref[...].astype(F32)) * conv_branch)
    mixed = jnp.dot(merged.astype(BF16), wo_ref[...], preferred_element_type=F32)
    x1 = x_ref[...] + g1_ref[0] * mixed
    x1_ref[...] = x1

    y = x1 * lax.rsqrt(jnp.mean(x1 * x1, axis=-1, keepdims=True) + NORM_EPS) * n2_ref[...]
    h2 = y * (1.0 + sc2_ref[0]) + sh2_ref[0]
    h2_ref[...] = h2

    h_hi = h2.astype(BF16)
    h_lo = (h2 - h_hi.astype(F32)).astype(BF16)
    hw = jnp.dot(h_hi, wr_ref[...], preferred_element_type=F32)
    lw = jnp.dot(h_lo, wr_ref[:, :LANES], preferred_element_type=F32)
    logits = hw[:, :LANES] + (hw[:, LANES:] + lw) + br_ref[...]
    lane = lax.broadcasted_iota(jnp.int32, (tm, LANES), 1)
    neg = jnp.float32(-jnp.inf)
    big = jnp.int32(1 << 20)
    is_g = (lane >= N_EXPERTS) & (lane < N_EXPERTS + N_GROUPS)
    gl = jnp.where(is_g, logits, neg)
    gmax = jnp.max(gl, axis=1, keepdims=True)
    gidx = jnp.min(jnp.where(gl == gmax, lane, big), axis=1, keepdims=True) - N_EXPERTS
    gsum = jnp.sum(jnp.where(is_g, jnp.exp(logits - gmax), 0.0), axis=1, keepdims=True)
    g_w = 1.0 / gsum

    in_grp = (lane // EXPERTS_PER_GROUP) == gidx
    el = jnp.where(in_grp, logits, neg)
    emax = jnp.max(el, axis=1, keepdims=True)
    ex = jnp.where(in_grp, jnp.exp(logits - emax), 0.0)
    prob = ex / jnp.sum(ex, axis=1, keepdims=True)
    pm = jnp.where(in_grp, prob, -1.0)
    p1 = jnp.max(pm, axis=1, keepdims=True)
    i1 = jnp.min(jnp.where(pm == p1, lane, big), axis=1, keepdims=True)
    pm2 = jnp.where(lane == i1, -1.0, pm)
    p2 = jnp.max(pm2, axis=1, keepdims=True)
    i2 = jnp.min(jnp.where(pm2 == p2, lane, big), axis=1, keepdims=True)
    den = p1 + p2
    c1 = g_w * (p1 / den)
    c2 = g_w * (p2 / den)

    hot1 = lane == i1
    hot2 = lane == i2
    onehot = jnp.where(hot1, 1.0, jnp.where(hot2, 1.0, 0.0))
    tri = (lax.broadcasted_iota(jnp.int32, (tm, tm), 0)
           > lax.broadcasted_iota(jnp.int32, (tm, tm), 1))
    ranks = jnp.dot(jnp.where(tri, 1.0, 0.0).astype(BF16), onehot.astype(BF16),
                    preferred_element_type=F32) + carry_ref[...]
    r1 = jnp.sum(jnp.where(hot1, ranks, 0.0), axis=1, keepdims=True).astype(jnp.int32)
    r2 = jnp.sum(jnp.where(hot2, ranks, 0.0), axis=1, keepdims=True).astype(jnp.int32)
    carry_ref[...] = carry_ref[...] + jnp.sum(onehot, axis=0, keepdims=True)
    cnt_ref[...] = carry_ref[...]

    ri_ref[...] = jnp.where(lane == 0, i1, jnp.where(lane == 1, i2,
                            jnp.where(lane == 2, r1, jnp.where(lane == 3, r2, 0))))
    rw_ref[...] = jnp.where(lane == 0, c1, jnp.where(lane == 1, c2, 0.0))


def _mix(yg, proj, x2d, w_ret_o, w_conv_o, w_out, conv_w, gate1, shift2, scale2, norm2_w,
         w_router, b_router, seq, off_cb, tm):
    n, d = x2d.shape
    per_b = seq // tm
    cblk = off_cb // d
    row = lambda i: (i, 0)
    const2 = lambda i: (0, 0)
    bidx = lambda i: (i // per_b, 0, 0)
    col = lambda k: (lambda i: (i, cblk + k))
    return pl.pallas_call(
        _mix_kernel,
        out_shape=(jax.ShapeDtypeStruct((n, d), F32),
                   jax.ShapeDtypeStruct((n, d), F32),
                   jax.ShapeDtypeStruct((n, LANES), jnp.int32),
                   jax.ShapeDtypeStruct((n, LANES), F32),
                   jax.ShapeDtypeStruct((1, LANES), F32)),
        grid=(n // tm,),
        in_specs=[pl.BlockSpec((tm, RET_V_W), row),
                  pl.BlockSpec((tm, d), col(0)),
                  pl.BlockSpec((tm, d), col(1)),
                  pl.BlockSpec((tm, d), col(2)),
                  pl.BlockSpec((tm, d), col(3)),
                  pl.BlockSpec((tm, d), col(4)),
                  pl.BlockSpec((tm, d), row),
                  pl.BlockSpec((RET_V_W, d), const2, pipeline_mode=pl.Buffered(1)),
                  pl.BlockSpec((d, d), const2, pipeline_mode=pl.Buffered(1)),
                  pl.BlockSpec((d, d), const2, pipeline_mode=pl.Buffered(1)),
                  pl.BlockSpec((3, d), const2),
                  pl.BlockSpec((1, 1, d), bidx),
                  pl.BlockSpec((1, 1, d), bidx),
                  pl.BlockSpec((1, 1, d), bidx),
                  pl.BlockSpec((1, d), const2),
                  pl.BlockSpec((d, 2 * LANES), const2),
                  pl.BlockSpec((1, LANES), const2)],
        out_specs=(pl.BlockSpec((tm, d), row),
                   pl.BlockSpec((tm, d), row),
                   pl.BlockSpec((tm, LANES), row),
                   pl.BlockSpec((tm, LANES), row),
                   pl.BlockSpec((1, LANES), const2)),
        scratch_shapes=[pltpu.VMEM((1, LANES), F32)],
        compiler_params=_params(("arbitrary",)),
        name="mix",
    )(yg, proj, proj, proj, proj, proj, x2d, w_ret_o, w_conv_o, w_out, conv_w,
      gate1, shift2, scale2, norm2_w.reshape(1, d), w_router, b_router)


def _row_copy(src_ref, src_row, dst_ref, dst_row, sem):
    return pltpu.make_async_copy(src_ref.at[pl.ds(src_row, 1), :],
                                 dst_ref.at[pl.ds(dst_row, 1), :], sem)


def _scatter_kernel(pos_ref, zt_ref, h_ref, xs_ref, zero_ref, sem, zsem):
    i = pl.program_id(0)
    tm = h_ref.shape[0]
    te = zero_ref.shape[0]
    base = i * tm
    n_tok = pos_ref.shape[0] // 2

    @pl.when(i == 0)
    def _():
        zero_ref[...] = jnp.zeros_like(zero_ref)

        def tile_copy(z):
            row = pl.multiple_of(zt_ref[z] * te, te)
            return pltpu.make_async_copy(zero_ref, xs_ref.at[pl.ds(row, te), :], zsem)

        def zissue(z, carry):
            @pl.when(zt_ref[z] >= 0)
            def _():
                tile_copy(z).start()
            return carry

        def zdrain(z, carry):
            @pl.when(zt_ref[z] >= 0)
            def _():
                tile_copy(z).wait()
            return carry

        lax.fori_loop(0, zt_ref.shape[0], zissue, 0)
        lax.fori_loop(0, zt_ref.shape[0], zdrain, 0)

    for r in range(tm):
        _row_copy(h_ref, r, xs_ref, pos_ref[base + r], sem).start(priority=0)
        _row_copy(h_ref, r, xs_ref, pos_ref[n_tok + base + r], sem).start(priority=1)

    for _ in range(2):
        pltpu.make_async_copy(h_ref, xs_ref.at[pl.ds(0, tm), :], sem).wait()


def _scatter_rows(pos_flat, zero_tiles, h2, n_sorted, tm):
    n, d = h2.shape
    grid_spec = pltpu.PrefetchScalarGridSpec(
        num_scalar_prefetch=2,
        grid=(n // tm,),
        in_specs=[pl.BlockSpec((tm, d), lambda i, pos, zt: (i, 0))],
        out_specs=pl.BlockSpec(memory_space=pl.ANY),
        scratch_shapes=[pltpu.VMEM((EXPERT_TILE, d), F32),
                        pltpu.SemaphoreType.DMA,
                        pltpu.SemaphoreType.DMA],
    )
    return pl.pallas_call(
        _scatter_kernel,
        out_shape=jax.ShapeDtypeStruct((n_sorted, d), F32),
        grid_spec=grid_spec,
        compiler_params=_params(("arbitrary",)),
        name="scatter",
    )(pos_flat, zero_tiles, h2)


ROW_TILE_SLOTS = 3


def _expert_kernel(tf_ref, te_ref, tn_ref, ts_ref, xs_ref, w13_ref, w2_ref, ys_ref,
                   xbuf_ref, w13f_ref, w2f_ref, w13b_ref, w2b_ref, xsem, sem13, sem2):
    t = pl.program_id(0)
    nt = pl.num_programs(0)
    rows = xbuf_ref.shape[1]
    flag = tf_ref[t]

    def tile_copy(tile, slot):
        src = xs_ref.at[pl.ds(pl.multiple_of(tile * rows, rows), rows), :]
        return pltpu.make_async_copy(src, xbuf_ref.at[slot], xsem.at[slot])

    def weight_copies(e, slot):
        return (pltpu.make_async_copy(w13_ref.at[e], w13f_ref.at[slot], sem13.at[slot]),
                pltpu.make_async_copy(w2_ref.at[e], w2f_ref.at[slot], sem2.at[slot]))

    @pl.when(t == 0)
    def _():
        for cp in weight_copies(te_ref[0], ts_ref[0]):
            cp.start()
        tile_copy(0, 0).start()
        tile_copy(1, 1).start()

    slot = lax.rem(t, ROW_TILE_SLOTS)
    ahead_slot = lax.rem(t + 2, ROW_TILE_SLOTS)
    tile_copy(t, slot).wait()
    tile_copy(jnp.minimum(t + 2, nt - 1), ahead_slot).start()

    @pl.when(flag == 2)
    def _():
        slot = ts_ref[t]
        for cp in weight_copies(te_ref[t], slot):
            cp.wait()
        w13b_ref[...] = w13f_ref[slot].astype(BF16)
        w2b_ref[...] = w2f_ref[slot].astype(BF16)

        @pl.when(tn_ref[t] >= 0)
        def _():
            for cp in weight_copies(tn_ref[t], 1 - slot):
                cp.start()

    @pl.when(flag > 0)
    def _():
        x = xbuf_ref[slot].astype(BF16)
        hid = jnp.dot(x, w13b_ref[...], preferred_element_type=F32)
        a = hid[:, :EXPERT_HIDDEN]
        act = a * jax.nn.sigmoid(a) * hid[:, EXPERT_HIDDEN:]
        ys_ref[...] = jnp.dot(act.astype(BF16), w2b_ref[...], preferred_element_type=F32)

    @pl.when(flag == 0)
    def _():
        ys_ref[...] = jnp.zeros_like(ys_ref)

    @pl.when(t == nt - 1)
    def _():
        tile_copy(t, lax.rem(t + 1, ROW_TILE_SLOTS)).wait()
        tile_copy(t, ahead_slot).wait()


def _experts(tile_flag, tile_expert, tile_next, tile_slot, xs, w13, w2):
    n_sorted, d = xs.shape
    te = EXPERT_TILE
    n_tiles = n_sorted // te
    assert n_tiles >= ROW_TILE_SLOTS
    hid2 = w13.shape[2]
    grid_spec = pltpu.PrefetchScalarGridSpec(
        num_scalar_prefetch=4,
        grid=(n_tiles,),
        in_specs=[pl.BlockSpec(memory_space=pl.ANY),
                  pl.BlockSpec(memory_space=pl.ANY),
                  pl.BlockSpec(memory_space=pl.ANY)],
        out_specs=pl.BlockSpec((te, d), lambda t, *_: (t, 0)),
        scratch_shapes=[pltpu.VMEM((ROW_TILE_SLOTS, te, d), F32),
                        pltpu.VMEM((2, d, hid2), F32),
                        pltpu.VMEM((2, hid2 // 2, d), F32),
                        pltpu.VMEM((d, hid2), BF16),
                        pltpu.VMEM((hid2 // 2, d), BF16),
                        pltpu.SemaphoreType.DMA((ROW_TILE_SLOTS,)),
                        pltpu.SemaphoreType.DMA((2,)),
                        pltpu.SemaphoreType.DMA((2,))],
    )
    return pl.pallas_call(
        _expert_kernel,
        out_shape=jax.ShapeDtypeStruct((n_sorted, d), F32),
        grid_spec=grid_spec,
        compiler_params=_params(("arbitrary",)),
        name="experts",
    )(tile_flag, tile_expert, tile_next, tile_slot, xs, w13, w2)


def _combine_kernel(pos_ref, ys_ref, x1_ref, rw_ref, g2_ref, fw_ref, o_ref, ya_ref, yb_ref, sem):
    k = pl.program_id(0)
    nk = pl.num_programs(0)
    tm = ya_ref.shape[1]
    n_tok = pos_ref.shape[0] // 2

    def issue(tile, buf):
        base = tile * tm
        for r in range(tm):
            _row_copy(ys_ref, pos_ref[base + r], ya_ref.at[buf], r, sem.at[buf]).start(priority=0)
            _row_copy(ys_ref, pos_ref[n_tok + base + r], yb_ref.at[buf], r, sem.at[buf]).start(priority=1)

    def wait(buf):
        pltpu.make_async_copy(ys_ref.at[pl.ds(0, tm), :], ya_ref.at[buf], sem.at[buf]).wait()
        pltpu.make_async_copy(ys_ref.at[pl.ds(0, tm), :], yb_ref.at[buf], sem.at[buf]).wait()

    def finish(buf, half):
        rows = pl.ds(half * tm, tm)
        rw = rw_ref[rows, :]
        moe = rw[:, 0:1] * ya_ref[buf] + rw[:, 1:2] * yb_ref[buf]
        x2 = x1_ref[rows, :] + g2_ref[0] * moe
        o_ref[rows, :] = (x2 * lax.rsqrt(jnp.mean(x2 * x2, axis=-1, keepdims=True) + NORM_EPS)
                          * fw_ref[...])

    @pl.when(k == 0)
    def _():
        issue(0, 0)

    wait(0)
    issue(2 * k + 1, 1)
    finish(0, 0)
    wait(1)
    issue(jnp.minimum(2 * k + 2, 2 * nk - 1), 0)
    finish(1, 1)

    @pl.when(k == nk - 1)
    def _():
        wait(0)


def _combine(pos_flat, ys, x1, route_w, gate2, final_w, seq, tm):
    n, d = x1.shape
    blk = 2 * tm
    per_b = seq // blk
    grid_spec = pltpu.PrefetchScalarGridSpec(
        num_scalar_prefetch=1,
        grid=(n // blk,),
        in_specs=[pl.BlockSpec(memory_space=pl.ANY),
                  pl.BlockSpec((blk, d), lambda i, pos: (i, 0)),
                  pl.BlockSpec((blk, LANES), lambda i, pos: (i, 0)),
                  pl.BlockSpec((1, 1, d), lambda i, pos: (i // per_b, 0, 0)),
                  pl.BlockSpec((1, d), lambda i, pos: (0, 0))],
        out_specs=pl.BlockSpec((blk, d), lambda i, pos: (i, 0)),
        scratch_shapes=[pltpu.VMEM((2, tm, d), F32),
                        pltpu.VMEM((2, tm, d), F32),
                        pltpu.SemaphoreType.DMA((2,))],
    )
    return pl.pallas_call(
        _combine_kernel,
        out_shape=jax.ShapeDtypeStruct((n, d), F32),
        grid_spec=grid_spec,
        compiler_params=_params(("arbitrary",)),
        name="combine",
    )(pos_flat, ys, x1, route_w, gate2, final_w.reshape(1, d))


def _rope_tables(seq):
    pos = np.arange(seq)
    nf = RET_QK_DIM // 4
    inv = (np.float32(ROPE_BASE) ** (-np.arange(nf, dtype=np.float32) / np.float32(nf))).astype(np.float32)
    ang_r = ((pos // GRID_W).astype(np.float32)[:, None] * inv[None, :]).astype(np.float64)
    ang_c = ((pos % GRID_W).astype(np.float32)[:, None] * inv[None, :]).astype(np.float64)
    return (jnp.asarray(np.concatenate([np.cos(ang_r), np.cos(ang_c)], axis=1), F32),
            jnp.asarray(np.concatenate([np.sin(ang_r), np.sin(ang_c)], axis=1), F32))


def _pair_split_heads(w_qk):
    d, width = w_qk.shape
    quarter = RET_QK_DIM // 4
    w4 = w_qk.reshape(d, width // RET_QK_DIM, 4, quarter)
    return jnp.stack([w4[:, :, 0], w4[:, :, 2], w4[:, :, 1], w4[:, :, 3]], axis=2).reshape(d, width)


def _layer(x, c, ctx, c_ctx, w_mod, b_mod, norm1_w, norm2_w, w_in, log_decay, gn_w, w_ret_o,
           conv_w, w_conv_o, w_out, rg_w, rg_b, re_w, re_b, w13, w2, final_w):
    batch, seq, d = x.shape
    ctx_len = ctx.shape[1]
    n = batch * seq
    off_q = 0
    off_k = off_q + RET_QK_W
    off_v = off_k + RET_QK_W
    off_g = off_v + RET_V_W
    off_cb = off_g + RET_V_W

    pad = (-(batch + 1)) % 8
    c_all = jnp.concatenate([c, c_ctx[None, :], jnp.zeros((pad, d), F32)], axis=0)
    mod = _modulation(c_all, w_mod, b_mod)
    mx = mod[:batch].reshape(batch, 1, 6, d)
    shift1, scale1, gate1, shift2, scale2, gate2 = (mx[:, :, k, :] for k in range(6))
    mc = mod[batch].reshape(1, 1, 6, d)
    shift1c, scale1c = mc[:, :, 0, :], mc[:, :, 1, :]

    w_in_b = jnp.concatenate([_pair_split_heads(w_in[:, :off_v]), w_in[:, off_v:]], axis=1).astype(BF16)
    cos_t, sin_t = _rope_tables(seq)
    k_scale = RET_QK_DIM ** -0.5

    tm_proj = min(2048, seq)
    proj = _projection(x.reshape(n, d), norm1_w, shift1, scale1, cos_t, sin_t, w_in_b, seq,
                       modes=((True, 1.0), (True, k_scale)), tm=tm_proj)
    tm_ctx = batch * ctx_len
    no_rope = jnp.zeros((tm_ctx, LANES), F32)
    kv_ctx = _projection(ctx.reshape(tm_ctx, d), norm1_w, shift1c, scale1c, no_rope, no_rope,
                         w_in_b, tm_ctx, modes=((False, k_scale),), tm=tm_ctx,
                         col0=off_k, width=off_g - off_k)

    log_gamma = jnp.log1p(-jnp.exp(log_decay.astype(F32)))
    yg = _retention(proj, kv_ctx, log_gamma, gn_w, batch, seq, ctx_len, off_q, off_k, off_v, off_g)

    w_router = jnp.concatenate(
        [re_w, rg_w, jnp.zeros((d, LANES - N_EXPERTS - N_GROUPS), F32)], axis=1)
    b_router = jnp.concatenate(
        [re_b, rg_b, jnp.zeros((LANES - N_EXPERTS - N_GROUPS,), F32)]).reshape(1, LANES)
    w_router_hi = w_router.astype(BF16)
    w_router_lo = (w_router - w_router_hi.astype(F32)).astype(BF16)
    w_router = jnp.concatenate([w_router_hi, w_router_lo], axis=1)
    tm_mix = min(512, seq)
    x1, h2, route_i, route_w, counts = _mix(
        yg, proj, x.reshape(n, d), w_ret_o.astype(BF16), w_conv_o.astype(BF16), w_out.astype(BF16),
        conv_w, gate1, shift2, scale2, norm2_w, w_router, b_router, seq, off_cb, tm_mix)

    te = EXPERT_TILE
    cnt = counts[0, :N_EXPERTS].astype(jnp.int32)
    padded = ((cnt + te - 1) // te) * te
    ends = jnp.cumsum(padded)
    starts = ends - padded
    expert_ids = jnp.arange(N_EXPERTS, dtype=jnp.int32)
    route_t = route_i[:, 0:4].T
    seg_start = jnp.sum(jnp.where(route_t[0:2, :, None] == expert_ids, starts, 0), axis=-1)
    pos_flat = (seg_start + route_t[2:4]).reshape(-1).astype(jnp.int32)
    n_tiles = (2 * n) // te + N_EXPERTS
    tile_row = jnp.arange(n_tiles, dtype=jnp.int32) * te
    tile_expert = jnp.minimum(jnp.sum((tile_row[:, None] >= ends[None, :]).astype(jnp.int32), axis=1),
                              N_EXPERTS - 1).astype(jnp.int32)
    new_expert = jnp.concatenate([jnp.ones((1,), bool), tile_expert[1:] != tile_expert[:-1]])
    tile_flag = jnp.where(tile_row < ends[-1], jnp.where(new_expert, 2, 1), 0).astype(jnp.int32)

    valid_tiles = ends[-1] // te
    last_tile = jnp.where(padded > 0, ends // te - 1, -1)
    tail_tile = valid_tiles + jnp.arange(N_EXPERTS, dtype=jnp.int32)
    tail_tile = jnp.where(tail_tile < n_tiles, tail_tile, -1)
    zero_tiles = jnp.concatenate([last_tile, tail_tile]).astype(jnp.int32)

    xs = _scatter_rows(pos_flat, zero_tiles, h2, n_tiles * te, min(1024, seq))
    nonempty = padded > 0
    later = nonempty[None, :] & (expert_ids[None, :] > expert_ids[:, None])
    next_expert = jnp.min(jnp.where(later, expert_ids[None, :], N_EXPERTS), axis=1)
    next_expert = jnp.where(next_expert == N_EXPERTS, -1, next_expert)
    slot = (jnp.cumsum(nonempty.astype(jnp.int32)) - 1) % 2
    tile_hot = tile_expert[:, None] == expert_ids[None, :]
    tile_next = jnp.sum(jnp.where(tile_hot, next_expert, 0), axis=1).astype(jnp.int32)
    tile_slot = jnp.sum(jnp.where(tile_hot, slot, 0), axis=1).astype(jnp.int32)

    ys = _experts(tile_flag, tile_expert, tile_next, tile_slot, xs, w13, w2)
    out = _combine(pos_flat, ys, x1, route_w, gate2, final_w, seq, min(256, seq // 2))
    return out.reshape(batch, seq, d)


def kernel(x, c, ctx, c_ctx, w_mod, b_mod, norm1_w, norm2_w, w_in, ret_log_decay, ret_gn_w, w_ret_o,
           conv_w, w_conv_o, w_out, router_group_w, router_group_b, router_expert_w, router_expert_b,
           expert_w13, expert_w2, final_norm_w):
    assert w_mod.shape[0] == 1, "single-layer problem"
    return _layer(x, c, ctx, c_ctx, w_mod[0], b_mod[0], norm1_w[0], norm2_w[0], w_in[0],
                  ret_log_decay[0], ret_gn_w[0], w_ret_o[0], conv_w[0], w_conv_o[0], w_out[0],
                  router_group_w[0], router_group_b[0], router_expert_w[0], router_expert_b[0],
                  expert_w13[0], expert_w2[0], final_norm_w)
```

```python
import functools

import jax
import jax.numpy as jnp
import numpy as np
from jax import lax
from jax.experimental import pallas as pl
from jax.experimental.pallas import tpu as pltpu

GRID_W = 64
RET_HEADS = 4
RET_QK_DIM = 256
RET_V_DIM = 512
RET_QK_W = RET_HEADS * RET_QK_DIM
RET_V_W = RET_HEADS * RET_V_DIM
N_GROUPS = 4
EXPERTS_PER_GROUP = 8
N_EXPERTS = N_GROUPS * EXPERTS_PER_GROUP
EXPERT_HIDDEN = 512
ROPE_BASE = 10000.0
NORM_EPS = 1e-6
GN_EPS = 1e-5

RET_CHUNK = 256
LANES = 128
EXPERT_TILE = 256
VMEM_LIMIT = 56 * 1024 * 1024

BF16 = jnp.bfloat16
F32 = jnp.float32


def _params(sem, vmem=VMEM_LIMIT):
    return pltpu.CompilerParams(dimension_semantics=sem, vmem_limit_bytes=vmem)


def _mod_kernel(c_ref, w_ref, b_ref, o_ref):
    c = c_ref[...]
    s = c * jax.nn.sigmoid(c)
    o_ref[...] = jnp.dot(s.astype(BF16), w_ref[...].astype(BF16),
                         preferred_element_type=F32) + b_ref[...]


def _modulation(c_all, w_mod, b_mod):
    rows, d = c_all.shape
    n_out = w_mod.shape[1]
    tn = 1536
    return pl.pallas_call(
        _mod_kernel,
        out_shape=jax.ShapeDtypeStruct((rows, n_out), F32),
        grid=(n_out // tn,),
        in_specs=[pl.BlockSpec((rows, d), lambda j: (0, 0)),
                  pl.BlockSpec((d, tn), lambda j: (0, j)),
                  pl.BlockSpec((1, tn), lambda j: (0, j))],
        out_specs=pl.BlockSpec((rows, tn), lambda j: (0, j)),
        compiler_params=_params(("parallel",)),
        name="mod",
    )(c_all, w_mod, b_mod.reshape(1, n_out))


def _rope_store(acc, cos, sin, scale, o_ref):
    for hd in range(acc.shape[1] // RET_QK_DIM):
        lo = hd * RET_QK_DIM
        u1 = acc[:, lo:lo + LANES]
        u2 = acc[:, lo + LANES:lo + 2 * LANES]
        r1 = u1 * cos - u2 * sin
        r2 = u1 * sin + u2 * cos
        if scale != 1.0:
            r1 = r1 * scale
            r2 = r2 * scale
        o_ref[:, lo:lo + LANES] = r1.astype(o_ref.dtype)
        o_ref[:, lo + LANES:lo + 2 * LANES] = r2.astype(o_ref.dtype)


def _proj_kernel(x_ref, nw_ref, sh_ref, sc_ref, cos_ref, sin_ref, w_ref, o_ref, *, modes, tn):
    x = x_ref[...]
    y = x * lax.rsqrt(jnp.mean(x * x, axis=-1, keepdims=True) + NORM_EPS) * nw_ref[...]
    h = (y * (1.0 + sc_ref[0]) + sh_ref[0]).astype(BF16)
    for j in range(w_ref.shape[1] // tn):
        cols = slice(j * tn, (j + 1) * tn)
        acc = jnp.dot(h, w_ref[:, cols], preferred_element_type=F32)
        if j < len(modes):
            rope, scale = modes[j]
            if rope:
                _rope_store(acc, cos_ref[...], sin_ref[...], scale, o_ref.at[:, cols])
            else:
                o_ref[:, cols] = (acc * scale).astype(o_ref.dtype)
        else:
            o_ref[:, cols] = acc.astype(o_ref.dtype)


def _projection(x2d, norm_w, shift, scale, cos_t, sin_t, w_bf16, seq, modes, tm):
    n, d = x2d.shape
    width = w_bf16.shape[1]
    tn = 1024
    per_b = seq // tm
    nb = shift.shape[0]
    bidx = (lambda i: (i // per_b, 0, 0)) if nb > 1 else (lambda i: (0, 0, 0))
    return pl.pallas_call(
        functools.partial(_proj_kernel, modes=modes, tn=tn),
        out_shape=jax.ShapeDtypeStruct((n, width), BF16),
        grid=(n // tm,),
        in_specs=[pl.BlockSpec((tm, d), lambda i: (i, 0)),
                  pl.BlockSpec((1, d), lambda i: (0, 0)),
                  pl.BlockSpec((1, 1, d), bidx),
                  pl.BlockSpec((1, 1, d), bidx),
                  pl.BlockSpec((tm, LANES), lambda i: (i % per_b, 0)),
                  pl.BlockSpec((tm, LANES), lambda i: (i % per_b, 0)),
                  pl.BlockSpec((d, width), lambda i: (0, 0), pipeline_mode=pl.Buffered(1))],
        out_specs=pl.BlockSpec((tm, width), lambda i: (i, 0)),
        compiler_params=_params(("parallel",)),
        name="proj",
    )(x2d, norm_w.reshape(1, d), shift, scale, cos_t, sin_t, w_bf16)


def _dot_t0(a, b):
    return lax.dot_general(a, b, (((0,), (0,)), ((), ())), preferred_element_type=F32)


def _ret_kernel(lg_ref, q_ref, k_ref, v_ref, g_ref, kc_ref, vc_ref, gn_ref, o_ref,
                acc_ref, sf_ref, sb_ref, dm_ref, *, seq, ctx_len):
    h = pl.program_id(1)
    lgf = lg_ref[0, h]
    lgb = lg_ref[1, h]
    c = RET_CHUNK
    n_chunks = seq // c

    cpos = lax.broadcasted_iota(jnp.int32, (ctx_len, 1), 0).astype(F32)
    kc = kc_ref[...].astype(F32)
    vc = vc_ref[...]
    sf_ref[...] = _dot_t0((kc * jnp.exp((ctx_len - 1.0 - cpos) * lgf)).astype(BF16), vc)
    sb_ref[...] = _dot_t0((kc * jnp.exp(cpos * lgb)).astype(BF16), vc)

    ri = lax.broadcasted_iota(jnp.int32, (c, c), 0)
    ci = lax.broadcasted_iota(jnp.int32, (c, c), 1)
    dist = (ri - ci).astype(F32)
    dm_ref[...] = jnp.exp(jnp.abs(dist) * jnp.where(dist >= 0, lgf, lgb))

    idx = lax.broadcasted_iota(jnp.int32, (c, 1), 0).astype(F32)
    xi_f = jnp.exp((idx + 1.0) * lgf)
    zeta_f = jnp.exp((c - 1.0 - idx) * lgf)
    xi_b = jnp.exp((c - idx) * lgb)
    zeta_b = jnp.exp(idx * lgb)
    cd_f = jnp.exp(jnp.full((1, 1), float(c), F32) * lgf)
    cd_b = jnp.exp(jnp.full((1, 1), float(c), F32) * lgb)

    gn_w = gn_ref[...]

    def chunk_rows(ic):
        return pl.ds(pl.multiple_of(ic * c, c), c)

    def fwd_part(ic):
        rows = chunk_rows(ic)
        q = q_ref[rows, :]
        k = k_ref[rows, :]
        v = v_ref[rows, :]
        s = lax.dot_general(q, k, (((1,), (1,)), ((), ())), preferred_element_type=F32)
        intra = jnp.dot((s * dm_ref[...]).astype(BF16), v, preferred_element_type=F32)
        inter = jnp.dot(q, sf_ref[...].astype(BF16), preferred_element_type=F32)
        kz = (k.astype(F32) * zeta_f).astype(BF16)
        sf_ref[...] = cd_f * sf_ref[...] + _dot_t0(kz, v)
        return intra + xi_f * inter

    def bwd_part(ic):
        rows = chunk_rows(ic)
        q = q_ref[rows, :]
        k = k_ref[rows, :]
        v = v_ref[rows, :]
        inter = jnp.dot(q, sb_ref[...].astype(BF16), preferred_element_type=F32)
        kz = (k.astype(F32) * zeta_b).astype(BF16)
        sb_ref[...] = cd_b * sb_ref[...] + _dot_t0(kz, v)
        return xi_b * inter

    def finalize(ic, y):
        rows = chunk_rows(ic)
        mu = jnp.mean(y, axis=-1, keepdims=True)
        yc = y - mu
        var = jnp.mean(yc * yc, axis=-1, keepdims=True)
        yn = yc * lax.rsqrt(var + GN_EPS) * gn_w
        g = g_ref[rows, :].astype(F32)
        o_ref[rows, :] = (g * jax.nn.sigmoid(g) * yn).astype(o_ref.dtype)

    half = n_chunks // 2

    def first_half(t, carry):
        acc_ref[chunk_rows(t), :] = fwd_part(t)
        acc_ref[chunk_rows(n_chunks - 1 - t), :] = bwd_part(n_chunks - 1 - t)
        return carry

    def second_half(t, carry):
        finalize(t, acc_ref[chunk_rows(t), :] + fwd_part(t))
        u = n_chunks - 1 - t
        finalize(u, acc_ref[chunk_rows(u), :] + bwd_part(u))
        return carry

    lax.fori_loop(0, half, first_half, 0)
    lax.fori_loop(half, n_chunks, second_half, 0)


def _retention(proj, kv_ctx, log_gamma, gn_w, batch, seq, ctx_len, off_q, off_k, off_v, off_g):
    n = proj.shape[0]
    dk, dv = RET_QK_DIM, RET_V_DIM
    assert seq % (2 * RET_CHUNK) == 0, "the paired scan needs an even number of chunks"
    grid_spec = pltpu.PrefetchScalarGridSpec(
        num_scalar_prefetch=1,
        grid=(batch, RET_HEADS),
        in_specs=[pl.BlockSpec((seq, dk), lambda b, h, lg: (b, off_q // dk + h)),
                  pl.BlockSpec((seq, dk), lambda b, h, lg: (b, off_k // dk + h)),
                  pl.BlockSpec((seq, dv), lambda b, h, lg: (b, off_v // dv + h)),
                  pl.BlockSpec((seq, dv), lambda b, h, lg: (b, off_g // dv + h)),
                  pl.BlockSpec((ctx_len, dk), lambda b, h, lg: (b, h)),
                  pl.BlockSpec((ctx_len, dv), lambda b, h, lg: (b, RET_QK_W // dv + h)),
                  pl.BlockSpec((1, dv), lambda b, h, lg: (0, h))],
        out_specs=pl.BlockSpec((seq, dv), lambda b, h, lg: (b, h)),
        scratch_shapes=[pltpu.VMEM((seq, dv), F32),
                        pltpu.VMEM((dk, dv), F32),
                        pltpu.VMEM((dk, dv), F32),
                        pltpu.VMEM((RET_CHUNK, RET_CHUNK), F32)],
    )
    return pl.pallas_call(
        functools.partial(_ret_kernel, seq=seq, ctx_len=ctx_len),
        out_shape=jax.ShapeDtypeStruct((n, RET_V_W), BF16),
        grid_spec=grid_spec,
        compiler_params=_params(("parallel", "arbitrary")),
        name="ret",
    )(log_gamma, proj, proj, proj, proj, kv_ctx, kv_ctx, gn_w.reshape(1, RET_V_W))


def _mix_kernel(yg_ref, cb_ref, cc_ref, ch_ref, gr_ref, gc_ref, x_ref, wro_ref, wco_ref, wo_ref,
                cw_ref, g1_ref, sh2_ref, sc2_ref, n2_ref, wr_ref, br_ref,
                x1_ref, h2_ref, ri_ref, rw_ref, cnt_ref, carry_ref):
    i = pl.program_id(0)
    tm = x_ref.shape[0]

    @pl.when(i == 0)
    def _():
        carry_ref[...] = jnp.zeros_like(carry_ref)

    ret_branch = jnp.dot(yg_ref[...], wro_ref[...], preferred_element_type=F32)

    p = cc_ref[...].astype(F32) * ch_ref[...].astype(F32)
    tpos = lax.broadcasted_iota(jnp.int32, (tm, 1), 0) % GRID_W
    prev = jnp.where(tpos != 0, pltpu.roll(p, 1, axis=0), 0.0)
    nxt = jnp.where(tpos != GRID_W - 1, pltpu.roll(p, tm - 1, axis=0), 0.0)
    cw = cw_ref[...]
    u = cw[0:1, :] * prev + cw[1:2, :] * p + cw[2:3, :] * nxt
    conv_in = (cb_ref[...].astype(F32) * u).astype(BF16)
    conv_branch = jnp.dot(conv_in, wco_ref[...], preferred_element_type=F32)

    merged = (jax.nn.sigmoid(gr_ref[...].astype(F32)) * ret_branch
              + jax.nn.sigmoid(gc_ref[...].astype(F32)) * conv_branch)
    mixed = jnp.dot(merged.astype(BF16), wo_ref[...], preferred_element_type=F32)
    x1 = x_ref[...] + g1_ref[0] * mixed
    x1_ref[...] = x1

    y = x1 * lax.rsqrt(jnp.mean(x1 * x1, axis=-1, keepdims=True) + NORM_EPS) * n2_ref[...]
    h2 = y * (1.0 + sc2_ref[0]) + sh2_ref[0]
    h2_ref[...] = h2

    h_hi = h2.astype(BF16)
    h_lo = (h2 - h_hi.astype(F32)).astype(BF16)
    hw = jnp.dot(h_hi, wr_ref[...], preferred_element_type=F32)
    lw = jnp.dot(h_lo, wr_ref[:, :LANES], preferred_element_type=F32)
    logits = hw[:, :LANES] + (hw[:, LANES:] + lw) + br_ref[...]
    lane = lax.broadcasted_iota(jnp.int32, (tm, LANES), 1)
    neg = jnp.float32(-jnp.inf)
    big = jnp.int32(1 << 20)
    is_g = (lane >= N_EXPERTS) & (lane < N_EXPERTS + N_GROUPS)
    gl = jnp.where(is_g, logits, neg)
    gmax = jnp.max(gl, axis=1, keepdims=True)
    gidx = jnp.min(jnp.where(gl == gmax, lane, big), axis=1, keepdims=True) - N_EXPERTS
    gsum = jnp.sum(jnp.where(is_g, jnp.exp(logits - gmax), 0.0), axis=1, keepdims=True)
    g_w = 1.0 / gsum

    in_grp = (lane // EXPERTS_PER_GROUP) == gidx
    el = jnp.where(in_grp, logits, neg)
    emax = jnp.max(el, axis=1, keepdims=True)
    ex = jnp.where(in_grp, jnp.exp(logits - emax), 0.0)
    prob = ex / jnp.sum(ex, axis=1, keepdims=True)
    pm = jnp.where(in_grp, prob, -1.0)
    p1 = jnp.max(pm, axis=1, keepdims=True)
    i1 = jnp.min(jnp.where(pm == p1, lane, big), axis=1, keepdims=True)
    pm2 = jnp.where(lane == i1, -1.0, pm)
    p2 = jnp.max(pm2, axis=1, keepdims=True)
    i2 = jnp.min(jnp.where(pm2 == p2, lane, big), axis=1, keepdims=True)
    den = p1 + p2
    c1 = g_w * (p1 / den)
    c2 = g_w * (p2 / den)

    hot1 = lane == i1
    hot2 = lane == i2
    onehot = jnp.where(hot1, 1.0, jnp.where(hot2, 1.0, 0.0))
    tri = (lax.broadcasted_iota(jnp.int32, (tm, tm), 0)
           > lax.broadcasted_iota(jnp.int32, (tm, tm), 1))
    ranks = jnp.dot(jnp.where(tri, 1.0, 0.0).astype(BF16), onehot.astype(BF16),
                    preferred_element_type=F32) + carry_ref[...]
    r1 = jnp.sum(jnp.where(hot1, ranks, 0.0), axis=1, keepdims=True).astype(jnp.int32)
    r2 = jnp.sum(jnp.where(hot2, ranks, 0.0), axis=1, keepdims=True).astype(jnp.int32)
    carry_ref[...] = carry_ref[...] + jnp.sum(onehot, axis=0, keepdims=True)
    cnt_ref[...] = carry_ref[...]

    ri_ref[...] = jnp.where(lane == 0, i1, jnp.where(lane == 1, i2,
                            jnp.where(lane == 2, r1, jnp.where(lane == 3, r2, 0))))
    rw_ref[...] = jnp.where(lane == 0, c1, jnp.where(lane == 1, c2, 0.0))


def _mix(yg, proj, x2d, w_ret_o, w_conv_o, w_out, conv_w, gate1, shift2, scale2, norm2_w,
         w_router, b_router, seq, off_cb, tm):
    n, d = x2d.shape
    per_b = seq // tm
    cblk = off_cb // d
    row = lambda i: (i, 0)
    const2 = lambda i: (0, 0)
    bidx = lambda i: (i // per_b, 0, 0)
    col = lambda k: (lambda i: (i, cblk + k))
    return pl.pallas_call(
        _mix_kernel,
        out_shape=(jax.ShapeDtypeStruct((n, d), F32),
                   jax.ShapeDtypeStruct((n, d), F32),
                   jax.ShapeDtypeStruct((n, LANES), jnp.int32),
                   jax.ShapeDtypeStruct((n, LANES), F32),
                   jax.ShapeDtypeStruct((1, LANES), F32)),
        grid=(n // tm,),
        in_specs=[pl.BlockSpec((tm, RET_V_W), row),
                  pl.BlockSpec((tm, d), col(0)),
                  pl.BlockSpec((tm, d), col(1)),
                  pl.BlockSpec((tm, d), col(2)),
                  pl.BlockSpec((tm, d), col(3)),
                  pl.BlockSpec((tm, d), col(4)),
                  pl.BlockSpec((tm, d), row),
                  pl.BlockSpec((RET_V_W, d), const2, pipeline_mode=pl.Buffered(1)),
                  pl.BlockSpec((d, d), const2, pipeline_mode=pl.Buffered(1)),
                  pl.BlockSpec((d, d), const2, pipeline_mode=pl.Buffered(1)),
                  pl.BlockSpec((3, d), const2),
                  pl.BlockSpec((1, 1, d), bidx),
                  pl.BlockSpec((1, 1, d), bidx),
                  pl.BlockSpec((1, 1, d), bidx),
                  pl.BlockSpec((1, d), const2),
                  pl.BlockSpec((d, 2 * LANES), const2),
                  pl.BlockSpec((1, LANES), const2)],
        out_specs=(pl.BlockSpec((tm, d), row),
                   pl.BlockSpec((tm, d), row),
                   pl.BlockSpec((tm, LANES), row),
                   pl.BlockSpec((tm, LANES), row),
                   pl.BlockSpec((1, LANES), const2)),
        scratch_shapes=[pltpu.VMEM((1, LANES), F32)],
        compiler_params=_params(("arbitrary",)),
        name="mix",
    )(yg, proj, proj, proj, proj, proj, x2d, w_ret_o, w_conv_o, w_out, conv_w,
      gate1, shift2, scale2, norm2_w.reshape(1, d), w_router, b_router)


def _row_copy(src_ref, src_row, dst_ref, dst_row, sem):
    return pltpu.make_async_copy(src_ref.at[pl.ds(src_row, 1), :],
                                 dst_ref.at[pl.ds(dst_row, 1), :], sem)


def _scatter_kernel(pos_ref, zt_ref, h_ref, xs_ref, zero_ref, sem, zsem):
    i = pl.program_id(0)
    tm = h_ref.shape[0]
    te = zero_ref.shape[0]
    base = i * tm
    n_tok = pos_ref.shape[0] // 2

    @pl.when(i == 0)
    def _():
        zero_ref[...] = jnp.zeros_like(zero_ref)

        def tile_copy(z):
            row = pl.multiple_of(zt_ref[z] * te, te)
            return pltpu.make_async_copy(zero_ref, xs_ref.at[pl.ds(row, te), :], zsem)

        def zissue(z, carry):
            @pl.when(zt_ref[z] >= 0)
            def _():
                tile_copy(z).start()
            return carry

        def zdrain(z, carry):
            @pl.when(zt_ref[z] >= 0)
            def _():
                tile_copy(z).wait()
            return carry

        lax.fori_loop(0, zt_ref.shape[0], zissue, 0)
        lax.fori_loop(0, zt_ref.shape[0], zdrain, 0)

    for r in range(tm):
        _row_copy(h_ref, r, xs_ref, pos_ref[base + r], sem).start(priority=0)
        _row_copy(h_ref, r, xs_ref, pos_ref[n_tok + base + r], sem).start(priority=1)

    for _ in range(2):
        pltpu.make_async_copy(h_ref, xs_ref.at[pl.ds(0, tm), :], sem).wait()


def _scatter_rows(pos_flat, zero_tiles, h2, n_sorted, tm):
    n, d = h2.shape
    grid_spec = pltpu.PrefetchScalarGridSpec(
        num_scalar_prefetch=2,
        grid=(n // tm,),
        in_specs=[pl.BlockSpec((tm, d), lambda i, pos, zt: (i, 0))],
        out_specs=pl.BlockSpec(memory_space=pl.ANY),
        scratch_shapes=[pltpu.VMEM((EXPERT_TILE, d), F32),
                        pltpu.SemaphoreType.DMA,
                        pltpu.SemaphoreType.DMA],
    )
    return pl.pallas_call(
        _scatter_kernel,
        out_shape=jax.ShapeDtypeStruct((n_sorted, d), F32),
        grid_spec=grid_spec,
        compiler_params=_params(("arbitrary",)),
        name="scatter",
    )(pos_flat, zero_tiles, h2)


ROW_TILE_SLOTS = 3


def _expert_kernel(tf_ref, te_ref, tn_ref, ts_ref, xs_ref, w13_ref, w2_ref, ys_ref,
                   xbuf_ref, w13f_ref, w2f_ref, w13b_ref, w2b_ref, xsem, sem13, sem2):
    t = pl.program_id(0)
    nt = pl.num_programs(0)
    rows = xbuf_ref.shape[1]
    flag = tf_ref[t]

    def tile_copy(tile, slot):
        src = xs_ref.at[pl.ds(pl.multiple_of(tile * rows, rows), rows), :]
        return pltpu.make_async_copy(src, xbuf_ref.at[slot], xsem.at[slot])

    def weight_copies(e, slot):
        return (pltpu.make_async_copy(w13_ref.at[e], w13f_ref.at[slot], sem13.at[slot]),
                pltpu.make_async_copy(w2_ref.at[e], w2f_ref.at[slot], sem2.at[slot]))

    @pl.when(t == 0)
    def _():
        for cp in weight_copies(te_ref[0], ts_ref[0]):
            cp.start()
        tile_copy(0, 0).start()
        tile_copy(1, 1).start()

    slot = lax.rem(t, ROW_TILE_SLOTS)
    ahead_slot = lax.rem(t + 2, ROW_TILE_SLOTS)
    tile_copy(t, slot).wait()
    tile_copy(jnp.minimum(t + 2, nt - 1), ahead_slot).start()

    @pl.when(flag == 2)
    def _():
        slot = ts_ref[t]
        for cp in weight_copies(te_ref[t], slot):
            cp.wait()
        w13b_ref[...] = w13f_ref[slot].astype(BF16)
        w2b_ref[...] = w2f_ref[slot].astype(BF16)

        @pl.when(tn_ref[t] >= 0)
        def _():
            for cp in weight_copies(tn_ref[t], 1 - slot):
                cp.start()

    @pl.when(flag > 0)
    def _():
        x = xbuf_ref[slot].astype(BF16)
        hid = jnp.dot(x, w13b_ref[...], preferred_element_type=F32)
        a = hid[:, :EXPERT_HIDDEN]
        act = a * jax.nn.sigmoid(a) * hid[:, EXPERT_HIDDEN:]
        ys_ref[...] = jnp.dot(act.astype(BF16), w2b_ref[...], preferred_element_type=F32)

    @pl.when(flag == 0)
    def _():
        ys_ref[...] = jnp.zeros_like(ys_ref)

    @pl.when(t == nt - 1)
    def _():
        tile_copy(t, lax.rem(t + 1, ROW_TILE_SLOTS)).wait()
        tile_copy(t, ahead_slot).wait()


def _experts(tile_flag, tile_expert, tile_next, tile_slot, xs, w13, w2):
    n_sorted, d = xs.shape
    te = EXPERT_TILE
    n_tiles = n_sorted // te
    assert n_tiles >= ROW_TILE_SLOTS
    hid2 = w13.shape[2]
    grid_spec = pltpu.PrefetchScalarGridSpec(
        num_scalar_prefetch=4,
        grid=(n_tiles,),
        in_specs=[pl.BlockSpec(memory_space=pl.ANY),
                  pl.BlockSpec(memory_space=pl.ANY),
                  pl.BlockSpec(memory_space=pl.ANY)],
        out_specs=pl.BlockSpec((te, d), lambda t, *_: (t, 0)),
        scratch_shapes=[pltpu.VMEM((ROW_TILE_SLOTS, te, d), F32),
                        pltpu.VMEM((2, d, hid2), F32),
                        pltpu.VMEM((2, hid2 // 2, d), F32),
                        pltpu.VMEM((d, hid2), BF16),
                        pltpu.VMEM((hid2 // 2, d), BF16),
                        pltpu.SemaphoreType.DMA((ROW_TILE_SLOTS,)),
                        pltpu.SemaphoreType.DMA((2,)),
                        pltpu.SemaphoreType.DMA((2,))],
    )
    return pl.pallas_call(
        _expert_kernel,
        out_shape=jax.ShapeDtypeStruct((n_sorted, d), F32),
        grid_spec=grid_spec,
        compiler_params=_params(("arbitrary",)),
        name="experts",
    )(tile_flag, tile_expert, tile_next, tile_slot, xs, w13, w2)


GATHER_SLOTS = 3


def _combine_kernel(pos_ref, ys_ref, x1_ref, rw_ref, g2_ref, fw_ref, o_ref, ya_ref, yb_ref, sem):
    i = pl.program_id(0)
    n_steps = pl.num_programs(0)
    tm = ya_ref.shape[1]
    n_tok = pos_ref.shape[0] // 2

    def issue(tile, slot):
        base = tile * tm
        dst_a = ya_ref.at[slot]
        dst_b = yb_ref.at[slot]
        for r in range(tm):
            _row_copy(ys_ref, pos_ref[base + r], dst_a, r, sem.at[slot]).start(priority=0)
            _row_copy(ys_ref, pos_ref[n_tok + base + r], dst_b, r, sem.at[slot]).start(priority=1)

    def wait(slot):
        pltpu.make_async_copy(ys_ref.at[pl.ds(0, tm), :], ya_ref.at[slot], sem.at[slot]).wait()
        pltpu.make_async_copy(ys_ref.at[pl.ds(0, tm), :], yb_ref.at[slot], sem.at[slot]).wait()

    @pl.when(i == 0)
    def _():
        issue(0, 0)
        issue(1, 1)

    slot = lax.rem(i, GATHER_SLOTS)
    ahead_slot = lax.rem(i + 2, GATHER_SLOTS)
    wait(slot)
    issue(jnp.minimum(i + 2, n_steps - 1), ahead_slot)
    rw = rw_ref[...]
    moe = rw[:, 0:1] * ya_ref[slot] + rw[:, 1:2] * yb_ref[slot]
    x2 = x1_ref[...] + g2_ref[0] * moe
    o_ref[...] = x2 * lax.rsqrt(jnp.mean(x2 * x2, axis=-1, keepdims=True) + NORM_EPS) * fw_ref[...]

    @pl.when(i == n_steps - 1)
    def _():
        wait(lax.rem(i + 1, GATHER_SLOTS))
        wait(ahead_slot)


def _combine(pos_flat, ys, x1, route_w, gate2, final_w, seq, tm):
    n, d = x1.shape
    per_b = seq // tm
    assert n // tm >= GATHER_SLOTS
    grid_spec = pltpu.PrefetchScalarGridSpec(
        num_scalar_prefetch=1,
        grid=(n // tm,),
        in_specs=[pl.BlockSpec(memory_space=pl.ANY),
                  pl.BlockSpec((tm, d), lambda i, pos: (i, 0)),
                  pl.BlockSpec((tm, LANES), lambda i, pos: (i, 0)),
                  pl.BlockSpec((1, 1, d), lambda i, pos: (i // per_b, 0, 0)),
                  pl.BlockSpec((1, d), lambda i, pos: (0, 0))],
        out_specs=pl.BlockSpec((tm, d), lambda i, pos: (i, 0)),
        scratch_shapes=[pltpu.VMEM((GATHER_SLOTS, tm, d), F32),
                        pltpu.VMEM((GATHER_SLOTS, tm, d), F32),
                        pltpu.SemaphoreType.DMA((GATHER_SLOTS,))],
    )
    return pl.pallas_call(
        _combine_kernel,
        out_shape=jax.ShapeDtypeStruct((n, d), F32),
        grid_spec=grid_spec,
        compiler_params=_params(("arbitrary",)),
        name="combine",
    )(pos_flat, ys, x1, route_w, gate2, final_w.reshape(1, d))


def _rope_tables(seq):
    pos = np.arange(seq)
    nf = RET_QK_DIM // 4
    inv = (np.float32(ROPE_BASE) ** (-np.arange(nf, dtype=np.float32) / np.float32(nf))).astype(np.float32)
    ang_r = ((pos // GRID_W).astype(np.float32)[:, None] * inv[None, :]).astype(np.float64)
    ang_c = ((pos % GRID_W).astype(np.float32)[:, None] * inv[None, :]).astype(np.float64)
    return (jnp.asarray(np.concatenate([np.cos(ang_r), np.cos(ang_c)], axis=1), F32),
            jnp.asarray(np.concatenate([np.sin(ang_r), np.sin(ang_c)], axis=1), F32))


def _pair_split_heads(w_qk):
    d, width = w_qk.shape
    quarter = RET_QK_DIM // 4
    w4 = w_qk.reshape(d, width // RET_QK_DIM, 4, quarter)
    return jnp.stack([w4[:, :, 0], w4[:, :, 2], w4[:, :, 1], w4[:, :, 3]], axis=2).reshape(d, width)


def _layer(x, c, ctx, c_ctx, w_mod, b_mod, norm1_w, norm2_w, w_in, log_decay, gn_w, w_ret_o,
           conv_w, w_conv_o, w_out, rg_w, rg_b, re_w, re_b, w13, w2, final_w):
    batch, seq, d = x.shape
    ctx_len = ctx.shape[1]
    n = batch * seq
    off_q = 0
    off_k = off_q + RET_QK_W
    off_v = off_k + RET_QK_W
    off_g = off_v + RET_V_W
    off_cb = off_g + RET_V_W

    pad = (-(batch + 1)) % 8
    c_all = jnp.concatenate([c, c_ctx[None, :], jnp.zeros((pad, d), F32)], axis=0)
    mod = _modulation(c_all, w_mod, b_mod)
    mx = mod[:batch].reshape(batch, 1, 6, d)
    shift1, scale1, gate1, shift2, scale2, gate2 = (mx[:, :, k, :] for k in range(6))
    mc = mod[batch].reshape(1, 1, 6, d)
    shift1c, scale1c = mc[:, :, 0, :], mc[:, :, 1, :]

    w_in_b = jnp.concatenate([_pair_split_heads(w_in[:, :off_v]), w_in[:, off_v:]], axis=1).astype(BF16)
    cos_t, sin_t = _rope_tables(seq)
    k_scale = RET_QK_DIM ** -0.5

    tm_proj = min(512, seq)
    proj = _projection(x.reshape(n, d), norm1_w, shift1, scale1, cos_t, sin_t, w_in_b, seq,
                       modes=((True, 1.0), (True, k_scale)), tm=tm_proj)
    tm_ctx = batch * ctx_len
    no_rope = jnp.zeros((tm_ctx, LANES), F32)
    kv_ctx = _projection(ctx.reshape(tm_ctx, d), norm1_w, shift1c, scale1c, no_rope, no_rope,
                         w_in_b[:, off_k:off_g], tm_ctx, modes=((False, k_scale),), tm=min(512, tm_ctx))

    log_gamma = jnp.log1p(-jnp.exp(log_decay.astype(F32)))
    yg = _retention(proj, kv_ctx, log_gamma, gn_w, batch, seq, ctx_len, off_q, off_k, off_v, off_g)

    w_router = jnp.concatenate(
        [re_w, rg_w, jnp.zeros((d, LANES - N_EXPERTS - N_GROUPS), F32)], axis=1)
    b_router = jnp.concatenate(
        [re_b, rg_b, jnp.zeros((LANES - N_EXPERTS - N_GROUPS,), F32)]).reshape(1, LANES)
    w_router_hi = w_router.astype(BF16)
    w_router_lo = (w_router - w_router_hi.astype(F32)).astype(BF16)
    w_router = jnp.concatenate([w_router_hi, w_router_lo], axis=1)
    tm_mix = min(512, seq)
    x1, h2, route_i, route_w, counts = _mix(
        yg, proj, x.reshape(n, d), w_ret_o.astype(BF16), w_conv_o.astype(BF16), w_out.astype(BF16),
        conv_w, gate1, shift2, scale2, norm2_w, w_router, b_router, seq, off_cb, tm_mix)

    te = EXPERT_TILE
    cnt = counts[0, :N_EXPERTS].astype(jnp.int32)
    padded = ((cnt + te - 1) // te) * te
    ends = jnp.cumsum(padded)
    starts = ends - padded
    expert_ids = jnp.arange(N_EXPERTS, dtype=jnp.int32)
    route_t = route_i[:, 0:4].T
    seg_start = jnp.sum(jnp.where(route_t[0:2, :, None] == expert_ids, starts, 0), axis=-1)
    pos_flat = (seg_start + route_t[2:4]).reshape(-1).astype(jnp.int32)
    n_tiles = (2 * n) // te + N_EXPERTS
    tile_row = jnp.arange(n_tiles, dtype=jnp.int32) * te
    tile_expert = jnp.minimum(jnp.sum((tile_row[:, None] >= ends[None, :]).astype(jnp.int32), axis=1),
                              N_EXPERTS - 1).astype(jnp.int32)
    new_expert = jnp.concatenate([jnp.ones((1,), bool), tile_expert[1:] != tile_expert[:-1]])
    tile_flag = jnp.where(tile_row < ends[-1], jnp.where(new_expert, 2, 1), 0).astype(jnp.int32)

    valid_tiles = ends[-1] // te
    last_tile = jnp.where(padded > 0, ends // te - 1, -1)
    tail_tile = valid_tiles + jnp.arange(N_EXPERTS, dtype=jnp.int32)
    tail_tile = jnp.where(tail_tile < n_tiles, tail_tile, -1)
    zero_tiles = jnp.concatenate([last_tile, tail_tile]).astype(jnp.int32)

    xs = _scatter_rows(pos_flat, zero_tiles, h2, n_tiles * te, min(1024, seq))
    nonempty = padded > 0
    later = nonempty[None, :] & (expert_ids[None, :] > expert_ids[:, None])
    next_expert = jnp.min(jnp.where(later, expert_ids[None, :], N_EXPERTS), axis=1)
    next_expert = jnp.where(next_expert == N_EXPERTS, -1, next_expert)
    slot = (jnp.cumsum(nonempty.astype(jnp.int32)) - 1) % 2
    tile_hot = tile_expert[:, None] == expert_ids[None, :]
    tile_next = jnp.sum(jnp.where(tile_hot, next_expert, 0), axis=1).astype(jnp.int32)
    tile_slot = jnp.sum(jnp.where(tile_hot, slot, 0), axis=1).astype(jnp.int32)

    ys = _experts(tile_flag, tile_expert, tile_next, tile_slot, xs, w13, w2)
    out = _combine(pos_flat, ys, x1, route_w, gate2, final_w, seq, min(256, seq // 2))
    return out.reshape(batch, seq, d)


def kernel(x, c, ctx, c_ctx, w_mod, b_mod, norm1_w, norm2_w, w_in, ret_log_decay, ret_gn_w, w_ret_o,
           conv_w, w_conv_o, w_out, router_group_w, router_group_b, router_expert_w, router_expert_b,
           expert_w13, expert_w2, final_norm_w):
    assert w_mod.shape[0] == 1, "single-layer problem"
    return _layer(x, c, ctx, c_ctx, w_mod[0], b_mod[0], norm1_w[0], norm2_w[0], w_in[0],
                  ret_log_decay[0], ret_gn_w[0], w_ret_o[0], conv_w[0], w_conv_o[0], w_out[0],
                  router_group_w[0], router_group_b[0], router_expert_w[0], router_expert_b[0],
                  expert_w13[0], expert_w2[0], final_norm_w)
```

```python
import functools

import jax
import jax.numpy as jnp
import numpy as np
from jax import lax
from jax.experimental import pallas as pl
from jax.experimental.pallas import tpu as pltpu

GRID_W = 64
RET_HEADS = 4
RET_QK_DIM = 256
RET_V_DIM = 512
RET_QK_W = RET_HEADS * RET_QK_DIM
RET_V_W = RET_HEADS * RET_V_DIM
N_GROUPS = 4
EXPERTS_PER_GROUP = 8
N_EXPERTS = N_GROUPS * EXPERTS_PER_GROUP
EXPERT_HIDDEN = 512
ROPE_BASE = 10000.0
NORM_EPS = 1e-6
GN_EPS = 1e-5

RET_CHUNK = 256
LANES = 128
EXPERT_TILE = 256
VMEM_LIMIT = 56 * 1024 * 1024

BF16 = jnp.bfloat16
F32 = jnp.float32


def _params(sem, vmem=VMEM_LIMIT):
    return pltpu.CompilerParams(dimension_semantics=sem, vmem_limit_bytes=vmem)


def _mod_kernel(c_ref, w_ref, b_ref, o_ref):
    c = c_ref[...]
    s = c * jax.nn.sigmoid(c)
    o_ref[...] = jnp.dot(s.astype(BF16), w_ref[...].astype(BF16),
                         preferred_element_type=F32) + b_ref[...]


def _modulation(c_all, w_mod, b_mod):
    rows, d = c_all.shape
    n_out = w_mod.shape[1]
    tn = 1536
    return pl.pallas_call(
        _mod_kernel,
        out_shape=jax.ShapeDtypeStruct((rows, n_out), F32),
        grid=(n_out // tn,),
        in_specs=[pl.BlockSpec((rows, d), lambda j: (0, 0)),
                  pl.BlockSpec((d, tn), lambda j: (0, j)),
                  pl.BlockSpec((1, tn), lambda j: (0, j))],
        out_specs=pl.BlockSpec((rows, tn), lambda j: (0, j)),
        compiler_params=_params(("parallel",)),
        name="mod",
    )(c_all, w_mod, b_mod.reshape(1, n_out))


def _rope_store(acc, cos, sin, scale, o_ref):
    for hd in range(acc.shape[1] // RET_QK_DIM):
        lo = hd * RET_QK_DIM
        u1 = acc[:, lo:lo + LANES]
        u2 = acc[:, lo + LANES:lo + 2 * LANES]
        r1 = u1 * cos - u2 * sin
        r2 = u1 * sin + u2 * cos
        if scale != 1.0:
            r1 = r1 * scale
            r2 = r2 * scale
        o_ref[:, lo:lo + LANES] = r1.astype(o_ref.dtype)
        o_ref[:, lo + LANES:lo + 2 * LANES] = r2.astype(o_ref.dtype)


def _proj_kernel(x_ref, nw_ref, sh_ref, sc_ref, cos_ref, sin_ref, gn_ref, w_ref, o_ref, *, modes, tn):
    x = x_ref[...]
    y = x * lax.rsqrt(jnp.mean(x * x, axis=-1, keepdims=True) + NORM_EPS) * nw_ref[...]
    h = (y * (1.0 + sc_ref[0]) + sh_ref[0]).astype(BF16)
    assert len(modes) * tn == w_ref.shape[1]
    for j, (kind, arg) in enumerate(modes):
        cols = slice(j * tn, (j + 1) * tn)
        acc = jnp.dot(h, w_ref[:, cols], preferred_element_type=F32)
        if kind == "rope":
            _rope_store(acc, cos_ref[...], sin_ref[...], arg, o_ref.at[:, cols])
        elif kind == "scale":
            o_ref[:, cols] = (acc * arg).astype(o_ref.dtype)
        elif kind == "swish_gain":
            gain = gn_ref[:, arg * tn:(arg + 1) * tn]
            o_ref[:, cols] = (acc * jax.nn.sigmoid(acc) * gain).astype(o_ref.dtype)
        else:
            o_ref[:, cols] = acc.astype(o_ref.dtype)


def _projection(x2d, norm_w, shift, scale, cos_t, sin_t, gain, w_bf16, seq, modes, tm):
    n, d = x2d.shape
    width = w_bf16.shape[1]
    tn = 1024
    per_b = seq // tm
    nb = shift.shape[0]
    bidx = (lambda i: (i // per_b, 0, 0)) if nb > 1 else (lambda i: (0, 0, 0))
    return pl.pallas_call(
        functools.partial(_proj_kernel, modes=modes, tn=tn),
        out_shape=jax.ShapeDtypeStruct((n, width), BF16),
        grid=(n // tm,),
        in_specs=[pl.BlockSpec((tm, d), lambda i: (i, 0)),
                  pl.BlockSpec((1, d), lambda i: (0, 0)),
                  pl.BlockSpec((1, 1, d), bidx),
                  pl.BlockSpec((1, 1, d), bidx),
                  pl.BlockSpec((tm, LANES), lambda i: (i % per_b, 0)),
                  pl.BlockSpec((tm, LANES), lambda i: (i % per_b, 0)),
                  pl.BlockSpec(gain.shape, lambda i: (0, 0)),
                  pl.BlockSpec((d, width), lambda i: (0, 0), pipeline_mode=pl.Buffered(1))],
        out_specs=pl.BlockSpec((tm, width), lambda i: (i, 0)),
        compiler_params=_params(("parallel",)),
        name="proj",
    )(x2d, norm_w.reshape(1, d), shift, scale, cos_t, sin_t, gain, w_bf16)


def _dot_t0(a, b):
    return lax.dot_general(a, b, (((0,), (0,)), ((), ())), preferred_element_type=F32)


def _ret_kernel(lg_ref, q_ref, k_ref, v_ref, g_ref, kc_ref, vc_ref, o_ref,
                acc_ref, sf_ref, sb_ref, dm_ref, *, seq, ctx_len):
    h = pl.program_id(1)
    lgf = lg_ref[0, h]
    lgb = lg_ref[1, h]
    c = RET_CHUNK
    n_chunks = seq // c

    cpos = lax.broadcasted_iota(jnp.int32, (ctx_len, 1), 0).astype(F32)
    kc = kc_ref[...].astype(F32)
    vc = vc_ref[...]
    sf_ref[...] = _dot_t0((kc * jnp.exp((ctx_len - 1.0 - cpos) * lgf)).astype(BF16), vc)
    sb_ref[...] = _dot_t0((kc * jnp.exp(cpos * lgb)).astype(BF16), vc)

    ri = lax.broadcasted_iota(jnp.int32, (c, c), 0)
    ci = lax.broadcasted_iota(jnp.int32, (c, c), 1)
    dist = (ri - ci).astype(F32)
    dm_ref[...] = jnp.exp(jnp.abs(dist) * jnp.where(dist >= 0, lgf, lgb))

    idx = lax.broadcasted_iota(jnp.int32, (c, 1), 0).astype(F32)
    xi_f = jnp.exp((idx + 1.0) * lgf)
    zeta_f = jnp.exp((c - 1.0 - idx) * lgf)
    xi_b = jnp.exp((c - idx) * lgb)
    zeta_b = jnp.exp(idx * lgb)
    cd_f = jnp.exp(jnp.full((1, 1), float(c), F32) * lgf)
    cd_b = jnp.exp(jnp.full((1, 1), float(c), F32) * lgb)

    def chunk_rows(ic):
        return pl.ds(pl.multiple_of(ic * c, c), c)

    def fwd_part(ic):
        rows = chunk_rows(ic)
        q = q_ref[rows, :]
        k = k_ref[rows, :]
        v = v_ref[rows, :]
        s = lax.dot_general(q, k, (((1,), (1,)), ((), ())), preferred_element_type=F32)
        intra = jnp.dot((s * dm_ref[...]).astype(BF16), v, preferred_element_type=F32)
        inter = jnp.dot(q, sf_ref[...].astype(BF16), preferred_element_type=F32)
        kz = (k.astype(F32) * zeta_f).astype(BF16)
        sf_ref[...] = cd_f * sf_ref[...] + _dot_t0(kz, v)
        return intra + xi_f * inter

    def bwd_part(ic):
        rows = chunk_rows(ic)
        q = q_ref[rows, :]
        k = k_ref[rows, :]
        v = v_ref[rows, :]
        inter = jnp.dot(q, sb_ref[...].astype(BF16), preferred_element_type=F32)
        kz = (k.astype(F32) * zeta_b).astype(BF16)
        sb_ref[...] = cd_b * sb_ref[...] + _dot_t0(kz, v)
        return xi_b * inter

    def finalize(ic, y):
        rows = chunk_rows(ic)
        mu = jnp.mean(y, axis=-1, keepdims=True)
        yc = y - mu
        var = jnp.mean(yc * yc, axis=-1, keepdims=True)
        o_ref[rows, :] = (g_ref[rows, :].astype(F32) * (yc * lax.rsqrt(var + GN_EPS))).astype(o_ref.dtype)

    half = n_chunks // 2

    def first_half(t, carry):
        acc_ref[chunk_rows(t), :] = fwd_part(t)
        acc_ref[chunk_rows(n_chunks - 1 - t), :] = bwd_part(n_chunks - 1 - t)
        return carry

    def second_half(t, carry):
        finalize(t, acc_ref[chunk_rows(t), :] + fwd_part(t))
        u = n_chunks - 1 - t
        finalize(u, acc_ref[chunk_rows(u), :] + bwd_part(u))
        return carry

    lax.fori_loop(0, half, first_half, 0)
    lax.fori_loop(half, n_chunks, second_half, 0)


def _retention(proj, kv_ctx, log_gamma, batch, seq, ctx_len, off_q, off_k, off_v, off_g):
    n = proj.shape[0]
    dk, dv = RET_QK_DIM, RET_V_DIM
    assert seq % (2 * RET_CHUNK) == 0, "the paired scan needs an even number of chunks"
    grid_spec = pltpu.PrefetchScalarGridSpec(
        num_scalar_prefetch=1,
        grid=(batch, RET_HEADS),
        in_specs=[pl.BlockSpec((seq, dk), lambda b, h, lg: (b, off_q // dk + h)),
                  pl.BlockSpec((seq, dk), lambda b, h, lg: (b, off_k // dk + h)),
                  pl.BlockSpec((seq, dv), lambda b, h, lg: (b, off_v // dv + h)),
                  pl.BlockSpec((seq, dv), lambda b, h, lg: (b, off_g // dv + h)),
                  pl.BlockSpec((ctx_len, dk), lambda b, h, lg: (b, h)),
                  pl.BlockSpec((ctx_len, dv), lambda b, h, lg: (b, RET_QK_W // dv + h))],
        out_specs=pl.BlockSpec((seq, dv), lambda b, h, lg: (b, h)),
        scratch_shapes=[pltpu.VMEM((seq, dv), F32),
                        pltpu.VMEM((dk, dv), F32),
                        pltpu.VMEM((dk, dv), F32),
                        pltpu.VMEM((RET_CHUNK, RET_CHUNK), F32)],
    )
    return pl.pallas_call(
        functools.partial(_ret_kernel, seq=seq, ctx_len=ctx_len),
        out_shape=jax.ShapeDtypeStruct((n, RET_V_W), BF16),
        grid_spec=grid_spec,
        compiler_params=_params(("parallel", "arbitrary")),
        name="ret",
    )(log_gamma, proj, proj, proj, proj, kv_ctx, kv_ctx)


def _mix_kernel(yg_ref, cb_ref, cc_ref, ch_ref, gr_ref, gc_ref, x_ref, wro_ref, wco_ref, wo_ref,
                cw_ref, g1_ref, sh2_ref, sc2_ref, n2_ref, wr_ref, br_ref,
                x1_ref, h2_ref, ri_ref, rw_ref, cnt_ref, carry_ref):
    i = pl.program_id(0)
    tm = x_ref.shape[0]

    @pl.when(i == 0)
    def _():
        carry_ref[...] = jnp.zeros_like(carry_ref)

    ret_branch = jnp.dot(yg_ref[...], wro_ref[...], preferred_element_type=F32)

    p = cc_ref[...].astype(F32) * ch_ref[...].astype(F32)
    tpos = lax.broadcasted_iota(jnp.int32, (tm, 1), 0) % GRID_W
    prev = jnp.where(tpos != 0, pltpu.roll(p, 1, axis=0), 0.0)
    nxt = jnp.where(tpos != GRID_W - 1, pltpu.roll(p, tm - 1, axis=0), 0.0)
    cw = cw_ref[...]
    u = cw[0:1, :] * prev + cw[1:2, :] * p + cw[2:3, :] * nxt
    conv_in = (cb_ref[...].astype(F32) * u).astype(BF16)
    conv_branch = jnp.dot(conv_in, wco_ref[...], preferred_element_type=F32)

    merged = (jax.nn.sigmoid(gr_ref[...].astype(F32)) * ret_branch
              + jax.nn.sigmoid(gc_ref[...].astype(F32)) * conv_branch)
    mixed = jnp.dot(merged.astype(BF16), wo_ref[...], preferred_element_type=F32)
    x1 = x_ref[...] + g1_ref[0] * mixed
    x1_ref[...] = x1

    y = x1 * lax.rsqrt(jnp.mean(x1 * x1, axis=-1, keepdims=True) + NORM_EPS) * n2_ref[...]
    h2 = y * (1.0 + sc2_ref[0]) + sh2_ref[0]
    h2_ref[...] = h2

    h_hi = h2.astype(BF16)
    h_lo = (h2 - h_hi.astype(F32)).astype(BF16)
    hw = jnp.dot(h_hi, wr_ref[...], preferred_element_type=F32)
    lw = jnp.dot(h_lo, wr_ref[:, :LANES], preferred_element_type=F32)
    logits = hw[:, :LANES] + (hw[:, LANES:] + lw) + br_ref[...]
    lane = lax.broadcasted_iota(jnp.int32, (tm, LANES), 1)
    neg = jnp.float32(-jnp.inf)
    big = jnp.int32(1 << 20)
    is_g = (lane >= N_EXPERTS) & (lane < N_EXPERTS + N_GROUPS)
    gl = jnp.where(is_g, logits, neg)
    gmax = jnp.max(gl, axis=1, keepdims=True)
    gidx = jnp.min(jnp.where(gl == gmax, lane, big), axis=1, keepdims=True) - N_EXPERTS
    gsum = jnp.sum(jnp.where(is_g, jnp.exp(logits - gmax), 0.0), axis=1, keepdims=True)
    g_w = 1.0 / gsum

    in_grp = (lane // EXPERTS_PER_GROUP) == gidx
    el = jnp.where(in_grp, logits, neg)
    emax = jnp.max(el, axis=1, keepdims=True)
    ex = jnp.where(in_grp, jnp.exp(logits - emax), 0.0)
    prob = ex / jnp.sum(ex, axis=1, keepdims=True)
    pm = jnp.where(in_grp, prob, -1.0)
    p1 = jnp.max(pm, axis=1, keepdims=True)
    i1 = jnp.min(jnp.where(pm == p1, lane, big), axis=1, keepdims=True)
    pm2 = jnp.where(lane == i1, -1.0, pm)
    p2 = jnp.max(pm2, axis=1, keepdims=True)
    i2 = jnp.min(jnp.where(pm2 == p2, lane, big), axis=1, keepdims=True)
    den = p1 + p2
    c1 = g_w * (p1 / den)
    c2 = g_w * (p2 / den)

    hot1 = lane == i1
    hot2 = lane == i2
    onehot = jnp.where(hot1, 1.0, jnp.where(hot2, 1.0, 0.0))
    tri = (lax.broadcasted_iota(jnp.int32, (tm, tm), 0)
           > lax.broadcasted_iota(jnp.int32, (tm, tm), 1))
    ranks = jnp.dot(jnp.where(tri, 1.0, 0.0).astype(BF16), onehot.astype(BF16),
                    preferred_element_type=F32) + carry_ref[...]
    r1 = jnp.sum(jnp.where(hot1, ranks, 0.0), axis=1, keepdims=True).astype(jnp.int32)
    r2 = jnp.sum(jnp.where(hot2, ranks, 0.0), axis=1, keepdims=True).astype(jnp.int32)
    carry_ref[...] = carry_ref[...] + jnp.sum(onehot, axis=0, keepdims=True)
    cnt_ref[...] = carry_ref[...]

    ri_ref[...] = jnp.where(lane == 0, i1, jnp.where(lane == 1, i2,
                            jnp.where(lane == 2, r1, jnp.where(lane == 3, r2, 0))))
    rw_ref[...] = jnp.where(lane == 0, c1, jnp.where(lane == 1, c2, 0.0))


def _mix(yg, proj, x2d, w_ret_o, w_conv_o, w_out, conv_w, gate1, shift2, scale2, norm2_w,
         w_router, b_router, seq, off_cb, tm):
    n, d = x2d.shape
    per_b = seq // tm
    cblk = off_cb // d
    row = lambda i: (i, 0)
    const2 = lambda i: (0, 0)
    bidx = lambda i: (i // per_b, 0, 0)
    col = lambda k: (lambda i: (i, cblk + k))
    return pl.pallas_call(
        _mix_kernel,
        out_shape=(jax.ShapeDtypeStruct((n, d), F32),
                   jax.ShapeDtypeStruct((n, d), F32),
                   jax.ShapeDtypeStruct((n, LANES), jnp.int32),
                   jax.ShapeDtypeStruct((n, LANES), F32),
                   jax.ShapeDtypeStruct((1, LANES), F32)),
        grid=(n // tm,),
        in_specs=[pl.BlockSpec((tm, RET_V_W), row),
                  pl.BlockSpec((tm, d), col(0)),
                  pl.BlockSpec((tm, d), col(1)),
                  pl.BlockSpec((tm, d), col(2)),
                  pl.BlockSpec((tm, d), col(3)),
                  pl.BlockSpec((tm, d), col(4)),
                  pl.BlockSpec((tm, d), row),
                  pl.BlockSpec((RET_V_W, d), const2, pipeline_mode=pl.Buffered(1)),
                  pl.BlockSpec((d, d), const2, pipeline_mode=pl.Buffered(1)),
                  pl.BlockSpec((d, d), const2, pipeline_mode=pl.Buffered(1)),
                  pl.BlockSpec((3, d), const2),
                  pl.BlockSpec((1, 1, d), bidx),
                  pl.BlockSpec((1, 1, d), bidx),
                  pl.BlockSpec((1, 1, d), bidx),
                  pl.BlockSpec((1, d), const2),
                  pl.BlockSpec((d, 2 * LANES), const2),
                  pl.BlockSpec((1, LANES), const2)],
        out_specs=(pl.BlockSpec((tm, d), row),
                   pl.BlockSpec((tm, d), row),
                   pl.BlockSpec((tm, LANES), row),
                   pl.BlockSpec((tm, LANES), row),
                   pl.BlockSpec((1, LANES), const2)),
        scratch_shapes=[pltpu.VMEM((1, LANES), F32)],
        compiler_params=_params(("arbitrary",)),
        name="mix",
    )(yg, proj, proj, proj, proj, proj, x2d, w_ret_o, w_conv_o, w_out, conv_w,
      gate1, shift2, scale2, norm2_w.reshape(1, d), w_router, b_router)


def _row_copy(src_ref, src_row, dst_ref, dst_row, sem):
    return pltpu.make_async_copy(src_ref.at[pl.ds(src_row, 1), :],
                                 dst_ref.at[pl.ds(dst_row, 1), :], sem)


SCATTER_SLOTS = 3


def _scatter_kernel(pos_ref, zt_ref, h_ref, xs_ref, hbuf_ref, zero_ref, lsem, ssem, zsem):
    i = pl.program_id(0)
    n_steps = pl.num_programs(0)
    tm = hbuf_ref.shape[1]
    te = zero_ref.shape[0]
    base = i * tm
    n_tok = pos_ref.shape[0] // 2

    def load(tile, slot):
        src = h_ref.at[pl.ds(pl.multiple_of(tile * tm, tm), tm), :]
        return pltpu.make_async_copy(src, hbuf_ref.at[slot], lsem.at[slot])

    def wait_rows(slot):
        for _ in range(2):
            pltpu.make_async_copy(hbuf_ref.at[slot], xs_ref.at[pl.ds(0, tm), :], ssem.at[slot]).wait()

    @pl.when(i == 0)
    def _():
        load(0, 0).start()
        load(1, 1).start()
        zero_ref[...] = jnp.zeros_like(zero_ref)

        def tile_copy(z):
            row = pl.multiple_of(zt_ref[z] * te, te)
            return pltpu.make_async_copy(zero_ref, xs_ref.at[pl.ds(row, te), :], zsem)

        def zissue(z, carry):
            @pl.when(zt_ref[z] >= 0)
            def _():
                tile_copy(z).start()
            return carry

        def zdrain(z, carry):
            @pl.when(zt_ref[z] >= 0)
            def _():
                tile_copy(z).wait()
            return carry

        lax.fori_loop(0, zt_ref.shape[0], zissue, 0)
        lax.fori_loop(0, zt_ref.shape[0], zdrain, 0)

    slot = lax.rem(i, SCATTER_SLOTS)
    reload_slot = lax.rem(i + 2, SCATTER_SLOTS)
    load(i, slot).wait()
    src = hbuf_ref.at[slot]
    for r in range(tm):
        _row_copy(src, r, xs_ref, pos_ref[base + r], ssem.at[slot]).start(priority=0)
        _row_copy(src, r, xs_ref, pos_ref[n_tok + base + r], ssem.at[slot]).start(priority=1)

    @pl.when(i > 0)
    def _():
        wait_rows(reload_slot)

    load(jnp.minimum(i + 2, n_steps - 1), reload_slot).start()

    @pl.when(i == n_steps - 1)
    def _():
        wait_rows(slot)
        load(i, lax.rem(i + 1, SCATTER_SLOTS)).wait()
        load(i, reload_slot).wait()


def _scatter_rows(pos_flat, zero_tiles, h2, n_sorted, tm):
    n, d = h2.shape
    assert n // tm >= SCATTER_SLOTS
    grid_spec = pltpu.PrefetchScalarGridSpec(
        num_scalar_prefetch=2,
        grid=(n // tm,),
        in_specs=[pl.BlockSpec(memory_space=pl.ANY)],
        out_specs=pl.BlockSpec(memory_space=pl.ANY),
        scratch_shapes=[pltpu.VMEM((SCATTER_SLOTS, tm, d), F32),
                        pltpu.VMEM((EXPERT_TILE, d), F32),
                        pltpu.SemaphoreType.DMA((SCATTER_SLOTS,)),
                        pltpu.SemaphoreType.DMA((SCATTER_SLOTS,)),
                        pltpu.SemaphoreType.DMA],
    )
    return pl.pallas_call(
        _scatter_kernel,
        out_shape=jax.ShapeDtypeStruct((n_sorted, d), F32),
        grid_spec=grid_spec,
        compiler_params=_params(("arbitrary",)),
        name="scatter",
    )(pos_flat, zero_tiles, h2)


ROW_TILE_SLOTS = 3


def _expert_kernel(tf_ref, te_ref, tn_ref, ts_ref, xs_ref, w13_ref, w2_ref, ys_ref,
                   xbuf_ref, w13f_ref, w2f_ref, w13b_ref, w2b_ref, xsem, sem13, sem2):
    t = pl.program_id(0)
    nt = pl.num_programs(0)
    rows = xbuf_ref.shape[1]
    flag = tf_ref[t]

    def tile_copy(tile, slot):
        src = xs_ref.at[pl.ds(pl.multiple_of(tile * rows, rows), rows), :]
        return pltpu.make_async_copy(src, xbuf_ref.at[slot], xsem.at[slot])

    def weight_copies(e, slot):
        return (pltpu.make_async_copy(w13_ref.at[e], w13f_ref.at[slot], sem13.at[slot]),
                pltpu.make_async_copy(w2_ref.at[e], w2f_ref.at[slot], sem2.at[slot]))

    @pl.when(t == 0)
    def _():
        for cp in weight_copies(te_ref[0], ts_ref[0]):
            cp.start()
        tile_copy(0, 0).start()
        tile_copy(1, 1).start()

    slot = lax.rem(t, ROW_TILE_SLOTS)
    ahead_slot = lax.rem(t + 2, ROW_TILE_SLOTS)
    tile_copy(t, slot).wait()
    tile_copy(jnp.minimum(t + 2, nt - 1), ahead_slot).start()

    @pl.when(flag == 2)
    def _():
        slot = ts_ref[t]
        for cp in weight_copies(te_ref[t], slot):
            cp.wait()
        w13b_ref[...] = w13f_ref[slot].astype(BF16)
        w2b_ref[...] = w2f_ref[slot].astype(BF16)

        @pl.when(tn_ref[t] >= 0)
        def _():
            for cp in weight_copies(tn_ref[t], 1 - slot):
                cp.start()

    @pl.when(flag > 0)
    def _():
        x = xbuf_ref[slot].astype(BF16)
        hid = jnp.dot(x, w13b_ref[...], preferred_element_type=F32)
        a = hid[:, :EXPERT_HIDDEN]
        act = a * jax.nn.sigmoid(a) * hid[:, EXPERT_HIDDEN:]
        ys_ref[...] = jnp.dot(act.astype(BF16), w2b_ref[...], preferred_element_type=F32)

    @pl.when(flag == 0)
    def _():
        ys_ref[...] = jnp.zeros_like(ys_ref)

    @pl.when(t == nt - 1)
    def _():
        tile_copy(t, lax.rem(t + 1, ROW_TILE_SLOTS)).wait()
        tile_copy(t, ahead_slot).wait()


def _experts(tile_flag, tile_expert, tile_next, tile_slot, xs, w13, w2):
    n_sorted, d = xs.shape
    te = EXPERT_TILE
    n_tiles = n_sorted // te
    assert n_tiles >= ROW_TILE_SLOTS
    hid2 = w13.shape[2]
    grid_spec = pltpu.PrefetchScalarGridSpec(
        num_scalar_prefetch=4,
        grid=(n_tiles,),
        in_specs=[pl.BlockSpec(memory_space=pl.ANY),
                  pl.BlockSpec(memory_space=pl.ANY),
                  pl.BlockSpec(memory_space=pl.ANY)],
        out_specs=pl.BlockSpec((te, d), lambda t, *_: (t, 0)),
        scratch_shapes=[pltpu.VMEM((ROW_TILE_SLOTS, te, d), F32),
                        pltpu.VMEM((2, d, hid2), F32),
                        pltpu.VMEM((2, hid2 // 2, d), F32),
                        pltpu.VMEM((d, hid2), BF16),
                        pltpu.VMEM((hid2 // 2, d), BF16),
                        pltpu.SemaphoreType.DMA((ROW_TILE_SLOTS,)),
                        pltpu.SemaphoreType.DMA((2,)),
                        pltpu.SemaphoreType.DMA((2,))],
    )
    return pl.pallas_call(
        _expert_kernel,
        out_shape=jax.ShapeDtypeStruct((n_sorted, d), F32),
        grid_spec=grid_spec,
        compiler_params=_params(("arbitrary",)),
        name="experts",
    )(tile_flag, tile_expert, tile_next, tile_slot, xs, w13, w2)


GATHER_SLOTS = 3


def _combine_kernel(pos_ref, ys_ref, x1_ref, rw_ref, g2_ref, fw_ref, o_ref, ya_ref, yb_ref, sem):
    i = pl.program_id(0)
    n_steps = pl.num_programs(0)
    tm = ya_ref.shape[1]
    n_tok = pos_ref.shape[0] // 2

    def issue(tile, slot):
        base = tile * tm
        dst_a = ya_ref.at[slot]
        dst_b = yb_ref.at[slot]
        for r in range(tm):
            _row_copy(ys_ref, pos_ref[base + r], dst_a, r, sem.at[slot]).start(priority=0)
            _row_copy(ys_ref, pos_ref[n_tok + base + r], dst_b, r, sem.at[slot]).start(priority=1)

    def wait(slot):
        pltpu.make_async_copy(ys_ref.at[pl.ds(0, tm), :], ya_ref.at[slot], sem.at[slot]).wait()
        pltpu.make_async_copy(ys_ref.at[pl.ds(0, tm), :], yb_ref.at[slot], sem.at[slot]).wait()

    @pl.when(i == 0)
    def _():
        issue(0, 0)
        issue(1, 1)

    slot = lax.rem(i, GATHER_SLOTS)
    ahead_slot = lax.rem(i + 2, GATHER_SLOTS)
    wait(slot)
    issue(jnp.minimum(i + 2, n_steps - 1), ahead_slot)
    rw = rw_ref[...]
    moe = rw[:, 0:1] * ya_ref[slot] + rw[:, 1:2] * yb_ref[slot]
    x2 = x1_ref[...] + g2_ref[0] * moe
    o_ref[...] = x2 * lax.rsqrt(jnp.mean(x2 * x2, axis=-1, keepdims=True) + NORM_EPS) * fw_ref[...]

    @pl.when(i == n_steps - 1)
    def _():
        wait(lax.rem(i + 1, GATHER_SLOTS))
        wait(ahead_slot)


def _combine(pos_flat, ys, x1, route_w, gate2, final_w, seq, tm):
    n, d = x1.shape
    per_b = seq // tm
    assert n // tm >= GATHER_SLOTS
    grid_spec = pltpu.PrefetchScalarGridSpec(
        num_scalar_prefetch=1,
        grid=(n // tm,),
        in_specs=[pl.BlockSpec(memory_space=pl.ANY),
                  pl.BlockSpec((tm, d), lambda i, pos: (i, 0)),
                  pl.BlockSpec((tm, LANES), lambda i, pos: (i, 0)),
                  pl.BlockSpec((1, 1, d), lambda i, pos: (i // per_b, 0, 0)),
                  pl.BlockSpec((1, d), lambda i, pos: (0, 0))],
        out_specs=pl.BlockSpec((tm, d), lambda i, pos: (i, 0)),
        scratch_shapes=[pltpu.VMEM((GATHER_SLOTS, tm, d), F32),
                        pltpu.VMEM((GATHER_SLOTS, tm, d), F32),
                        pltpu.SemaphoreType.DMA((GATHER_SLOTS,))],
    )
    return pl.pallas_call(
        _combine_kernel,
        out_shape=jax.ShapeDtypeStruct((n, d), F32),
        grid_spec=grid_spec,
        compiler_params=_params(("arbitrary",)),
        name="combine",
    )(pos_flat, ys, x1, route_w, gate2, final_w.reshape(1, d))


def _rope_tables(seq):
    pos = np.arange(seq)
    nf = RET_QK_DIM // 4
    inv = (np.float32(ROPE_BASE) ** (-np.arange(nf, dtype=np.float32) / np.float32(nf))).astype(np.float32)
    ang_r = ((pos // GRID_W).astype(np.float32)[:, None] * inv[None, :]).astype(np.float64)
    ang_c = ((pos % GRID_W).astype(np.float32)[:, None] * inv[None, :]).astype(np.float64)
    return (jnp.asarray(np.concatenate([np.cos(ang_r), np.cos(ang_c)], axis=1), F32),
            jnp.asarray(np.concatenate([np.sin(ang_r), np.sin(ang_c)], axis=1), F32))


def _pair_split_heads(w_qk):
    d, width = w_qk.shape
    quarter = RET_QK_DIM // 4
    w4 = w_qk.reshape(d, width // RET_QK_DIM, 4, quarter)
    return jnp.stack([w4[:, :, 0], w4[:, :, 2], w4[:, :, 1], w4[:, :, 3]], axis=2).reshape(d, width)


def _layer(x, c, ctx, c_ctx, w_mod, b_mod, norm1_w, norm2_w, w_in, log_decay, gn_w, w_ret_o,
           conv_w, w_conv_o, w_out, rg_w, rg_b, re_w, re_b, w13, w2, final_w):
    batch, seq, d = x.shape
    ctx_len = ctx.shape[1]
    n = batch * seq
    off_q = 0
    off_k = off_q + RET_QK_W
    off_v = off_k + RET_QK_W
    off_g = off_v + RET_V_W
    off_cb = off_g + RET_V_W

    pad = (-(batch + 1)) % 8
    c_all = jnp.concatenate([c, c_ctx[None, :], jnp.zeros((pad, d), F32)], axis=0)
    mod = _modulation(c_all, w_mod, b_mod)
    mx = mod[:batch].reshape(batch, 1, 6, d)
    shift1, scale1, gate1, shift2, scale2, gate2 = (mx[:, :, k, :] for k in range(6))
    mc = mod[batch].reshape(1, 1, 6, d)
    shift1c, scale1c = mc[:, :, 0, :], mc[:, :, 1, :]

    w_in_b = jnp.concatenate([_pair_split_heads(w_in[:, :off_v]), w_in[:, off_v:]], axis=1).astype(BF16)
    cos_t, sin_t = _rope_tables(seq)
    k_scale = RET_QK_DIM ** -0.5

    tm_proj = min(512, seq)
    main_modes = (("rope", 1.0), ("rope", k_scale), ("plain", None), ("plain", None),
                  ("swish_gain", 0), ("swish_gain", 1), ("plain", None), ("plain", None), ("plain", None),
                  ("plain", None), ("plain", None))
    gain = gn_w.reshape(1, RET_V_W)
    proj = _projection(x.reshape(n, d), norm1_w, shift1, scale1, cos_t, sin_t, gain, w_in_b, seq,
                       modes=main_modes, tm=tm_proj)
    tm_ctx = batch * ctx_len
    no_rope = jnp.zeros((tm_ctx, LANES), F32)
    kv_ctx = _projection(ctx.reshape(tm_ctx, d), norm1_w, shift1c, scale1c, no_rope, no_rope, gain,
                         w_in_b[:, off_k:off_g], tm_ctx,
                         modes=(("scale", k_scale), ("plain", None), ("plain", None)), tm=min(512, tm_ctx))

    log_gamma = jnp.log1p(-jnp.exp(log_decay.astype(F32)))
    yg = _retention(proj, kv_ctx, log_gamma, batch, seq, ctx_len, off_q, off_k, off_v, off_g)

    w_router = jnp.concatenate(
        [re_w, rg_w, jnp.zeros((d, LANES - N_EXPERTS - N_GROUPS), F32)], axis=1)
    b_router = jnp.concatenate(
        [re_b, rg_b, jnp.zeros((LANES - N_EXPERTS - N_GROUPS,), F32)]).reshape(1, LANES)
    w_router_hi = w_router.astype(BF16)
    w_router_lo = (w_router - w_router_hi.astype(F32)).astype(BF16)
    w_router = jnp.concatenate([w_router_hi, w_router_lo], axis=1)
    tm_mix = min(512, seq)
    x1, h2, route_i, route_w, counts = _mix(
        yg, proj, x.reshape(n, d), w_ret_o.astype(BF16), w_conv_o.astype(BF16), w_out.astype(BF16),
        conv_w, gate1, shift2, scale2, norm2_w, w_router, b_router, seq, off_cb, tm_mix)

    te = EXPERT_TILE
    cnt = counts[0, :N_EXPERTS].astype(jnp.int32)
    padded = ((cnt + te - 1) // te) * te
    ends = jnp.cumsum(padded)
    starts = ends - padded
    expert_ids = jnp.arange(N_EXPERTS, dtype=jnp.int32)
    route_t = route_i[:, 0:4].T
    seg_start = jnp.sum(jnp.where(route_t[0:2, :, None] == expert_ids, starts, 0), axis=-1)
    pos_flat = (seg_start + route_t[2:4]).reshape(-1).astype(jnp.int32)
    n_tiles = (2 * n) // te + N_EXPERTS
    tile_row = jnp.arange(n_tiles, dtype=jnp.int32) * te
    tile_expert = jnp.minimum(jnp.sum((tile_row[:, None] >= ends[None, :]).astype(jnp.int32), axis=1),
                              N_EXPERTS - 1).astype(jnp.int32)
    new_expert = jnp.concatenate([jnp.ones((1,), bool), tile_expert[1:] != tile_expert[:-1]])
    tile_flag = jnp.where(tile_row < ends[-1], jnp.where(new_expert, 2, 1), 0).astype(jnp.int32)

    valid_tiles = ends[-1] // te
    last_tile = jnp.where(padded > 0, ends // te - 1, -1)
    tail_tile = valid_tiles + jnp.arange(N_EXPERTS, dtype=jnp.int32)
    tail_tile = jnp.where(tail_tile < n_tiles, tail_tile, -1)
    zero_tiles = jnp.concatenate([last_tile, tail_tile]).astype(jnp.int32)

    xs = _scatter_rows(pos_flat, zero_tiles, h2, n_tiles * te, min(512, seq // 2))
    nonempty = padded > 0
    later = nonempty[None, :] & (expert_ids[None, :] > expert_ids[:, None])
    next_expert = jnp.min(jnp.where(later, expert_ids[None, :], N_EXPERTS), axis=1)
    next_expert = jnp.where(next_expert == N_EXPERTS, -1, next_expert)
    slot = (jnp.cumsum(nonempty.astype(jnp.int32)) - 1) % 2
    tile_hot = tile_expert[:, None] == expert_ids[None, :]
    tile_next = jnp.sum(jnp.where(tile_hot, next_expert, 0), axis=1).astype(jnp.int32)
    tile_slot = jnp.sum(jnp.where(tile_hot, slot, 0), axis=1).astype(jnp.int32)

    ys = _experts(tile_flag, tile_expert, tile_next, tile_slot, xs, w13, w2)
    out = _combine(pos_flat, ys, x1, route_w, gate2, final_w, seq, min(256, seq // 2))
    return out.reshape(batch, seq, d)


def kernel(x, c, ctx, c_ctx, w_mod, b_mod, norm1_w, norm2_w, w_in, ret_log_decay, ret_gn_w, w_ret_o,
           conv_w, w_conv_o, w_out, router_group_w, router_group_b, router_expert_w, router_expert_b,
           expert_w13, expert_w2, final_norm_w):
    assert w_mod.shape[0] == 1, "single-layer problem"
    return _layer(x, c, ctx, c_ctx, w_mod[0], b_mod[0], norm1_w[0], norm2_w[0], w_in[0],
                  ret_log_decay[0], ret_gn_w[0], w_ret_o[0], conv_w[0], w_conv_o[0], w_out[0],
                  router_group_w[0], router_group_b[0], router_expert_w[0], router_expert_b[0],
                  expert_w13[0], expert_w2[0], final_norm_w)
```

```python
import functools

import jax
import jax.numpy as jnp
import numpy as np
from jax import lax
from jax.experimental import pallas as pl
from jax.experimental.pallas import tpu as pltpu

GRID_W = 64
RET_HEADS = 4
RET_QK_DIM = 256
RET_V_DIM = 512
RET_QK_W = RET_HEADS * RET_QK_DIM
RET_V_W = RET_HEADS * RET_V_DIM
N_GROUPS = 4
EXPERTS_PER_GROUP = 8
N_EXPERTS = N_GROUPS * EXPERTS_PER_GROUP
EXPERT_HIDDEN = 512
ROPE_BASE = 10000.0
NORM_EPS = 1e-6
GN_EPS = 1e-5

RET_CHUNK = 256
LANES = 128
EXPERT_TILE = 256
VMEM_LIMIT = 56 * 1024 * 1024

BF16 = jnp.bfloat16
F32 = jnp.float32


def _params(sem, vmem=VMEM_LIMIT):
    return pltpu.CompilerParams(dimension_semantics=sem, vmem_limit_bytes=vmem)


def _mod_kernel(c_ref, w_ref, b_ref, o_ref):
    c = c_ref[...]
    s = c * jax.nn.sigmoid(c)
    o_ref[...] = jnp.dot(s.astype(BF16), w_ref[...].astype(BF16),
                         preferred_element_type=F32) + b_ref[...]


def _modulation(c_all, w_mod, b_mod):
    rows, d = c_all.shape
    n_out = w_mod.shape[1]
    tn = 1536
    return pl.pallas_call(
        _mod_kernel,
        out_shape=jax.ShapeDtypeStruct((rows, n_out), F32),
        grid=(n_out // tn,),
        in_specs=[pl.BlockSpec((rows, d), lambda j: (0, 0)),
                  pl.BlockSpec((d, tn), lambda j: (0, j)),
                  pl.BlockSpec((1, tn), lambda j: (0, j))],
        out_specs=pl.BlockSpec((rows, tn), lambda j: (0, j)),
        compiler_params=_params(("parallel",)),
        name="mod",
    )(c_all, w_mod, b_mod.reshape(1, n_out))


def _rope_store(acc, cos, sin, scale, o_ref):
    for hd in range(acc.shape[1] // RET_QK_DIM):
        lo = hd * RET_QK_DIM
        u1 = acc[:, lo:lo + LANES]
        u2 = acc[:, lo + LANES:lo + 2 * LANES]
        r1 = u1 * cos - u2 * sin
        r2 = u1 * sin + u2 * cos
        if scale != 1.0:
            r1 = r1 * scale
            r2 = r2 * scale
        o_ref[:, lo:lo + LANES] = r1.astype(o_ref.dtype)
        o_ref[:, lo + LANES:lo + 2 * LANES] = r2.astype(o_ref.dtype)


def _proj_kernel(x_ref, nw_ref, sh_ref, sc_ref, cos_ref, sin_ref, gn_ref, w_ref, o_ref, *, modes, tn):
    x = x_ref[...]
    y = x * lax.rsqrt(jnp.mean(x * x, axis=-1, keepdims=True) + NORM_EPS) * nw_ref[...]
    h = (y * (1.0 + sc_ref[0]) + sh_ref[0]).astype(BF16)
    assert len(modes) * tn == w_ref.shape[1]
    for j, (kind, arg) in enumerate(modes):
        cols = slice(j * tn, (j + 1) * tn)
        acc = jnp.dot(h, w_ref[:, cols], preferred_element_type=F32)
        if kind == "rope":
            _rope_store(acc, cos_ref[...], sin_ref[...], arg, o_ref.at[:, cols])
        elif kind == "scale":
            o_ref[:, cols] = (acc * arg).astype(o_ref.dtype)
        elif kind == "swish_gain":
            gain = gn_ref[:, arg * tn:(arg + 1) * tn]
            o_ref[:, cols] = (acc * jax.nn.sigmoid(acc) * gain).astype(o_ref.dtype)
        else:
            o_ref[:, cols] = acc.astype(o_ref.dtype)


def _projection(x2d, norm_w, shift, scale, cos_t, sin_t, gain, w_bf16, seq, modes, tm):
    n, d = x2d.shape
    width = w_bf16.shape[1]
    tn = 1024
    per_b = seq // tm
    nb = shift.shape[0]
    bidx = (lambda i: (i // per_b, 0, 0)) if nb > 1 else (lambda i: (0, 0, 0))
    return pl.pallas_call(
        functools.partial(_proj_kernel, modes=modes, tn=tn),
        out_shape=jax.ShapeDtypeStruct((n, width), BF16),
        grid=(n // tm,),
        in_specs=[pl.BlockSpec((tm, d), lambda i: (i, 0)),
                  pl.BlockSpec((1, d), lambda i: (0, 0)),
                  pl.BlockSpec((1, 1, d), bidx),
                  pl.BlockSpec((1, 1, d), bidx),
                  pl.BlockSpec((tm, LANES), lambda i: (i % per_b, 0)),
                  pl.BlockSpec((tm, LANES), lambda i: (i % per_b, 0)),
                  pl.BlockSpec(gain.shape, lambda i: (0, 0)),
                  pl.BlockSpec((d, width), lambda i: (0, 0), pipeline_mode=pl.Buffered(1))],
        out_specs=pl.BlockSpec((tm, width), lambda i: (i, 0)),
        compiler_params=_params(("parallel",)),
        name="proj",
    )(x2d, norm_w.reshape(1, d), shift, scale, cos_t, sin_t, gain, w_bf16)


def _dot_t0(a, b):
    return lax.dot_general(a, b, (((0,), (0,)), ((), ())), preferred_element_type=F32)


def _ret_kernel(lg_ref, q_ref, k_ref, v_ref, g_ref, kc_ref, vc_ref, o_ref,
                acc_ref, sf_ref, sb_ref, dm_ref, *, seq, ctx_len):
    h = pl.program_id(1)
    lgf = lg_ref[0, h]
    lgb = lg_ref[1, h]
    c = RET_CHUNK
    n_chunks = seq // c

    cpos = lax.broadcasted_iota(jnp.int32, (ctx_len, 1), 0).astype(F32)
    kc = kc_ref[...].astype(F32)
    vc = vc_ref[...]
    sf_ref[...] = _dot_t0((kc * jnp.exp((ctx_len - 1.0 - cpos) * lgf)).astype(BF16), vc)
    sb_ref[...] = _dot_t0((kc * jnp.exp(cpos * lgb)).astype(BF16), vc)

    ri = lax.broadcasted_iota(jnp.int32, (c, c), 0)
    ci = lax.broadcasted_iota(jnp.int32, (c, c), 1)
    dist = (ri - ci).astype(F32)
    dm_ref[...] = jnp.exp(jnp.abs(dist) * jnp.where(dist >= 0, lgf, lgb))

    idx = lax.broadcasted_iota(jnp.int32, (c, 1), 0).astype(F32)
    xi_f = jnp.exp((idx + 1.0) * lgf)
    zeta_f = jnp.exp((c - 1.0 - idx) * lgf)
    xi_b = jnp.exp((c - idx) * lgb)
    zeta_b = jnp.exp(idx * lgb)
    cd_f = jnp.exp(jnp.full((1, 1), float(c), F32) * lgf)
    cd_b = jnp.exp(jnp.full((1, 1), float(c), F32) * lgb)

    def chunk_rows(ic):
        return pl.ds(pl.multiple_of(ic * c, c), c)

    def fwd_part(ic):
        rows = chunk_rows(ic)
        q = q_ref[rows, :]
        k = k_ref[rows, :]
        v = v_ref[rows, :]
        s = lax.dot_general(q, k, (((1,), (1,)), ((), ())), preferred_element_type=F32)
        intra = jnp.dot((s * dm_ref[...]).astype(BF16), v, preferred_element_type=F32)
        inter = jnp.dot(q, sf_ref[...].astype(BF16), preferred_element_type=F32)
        kz = (k.astype(F32) * zeta_f).astype(BF16)
        sf_ref[...] = cd_f * sf_ref[...] + _dot_t0(kz, v)
        return intra + xi_f * inter

    def bwd_part(ic):
        rows = chunk_rows(ic)
        q = q_ref[rows, :]
        k = k_ref[rows, :]
        v = v_ref[rows, :]
        inter = jnp.dot(q, sb_ref[...].astype(BF16), preferred_element_type=F32)
        kz = (k.astype(F32) * zeta_b).astype(BF16)
        sb_ref[...] = cd_b * sb_ref[...] + _dot_t0(kz, v)
        return xi_b * inter

    def finalize(ic, y):
        rows = chunk_rows(ic)
        mu = jnp.mean(y, axis=-1, keepdims=True)
        yc = y - mu
        var = jnp.mean(yc * yc, axis=-1, keepdims=True)
        o_ref[rows, :] = (g_ref[rows, :].astype(F32) * (yc * lax.rsqrt(var + GN_EPS))).astype(o_ref.dtype)

    half = n_chunks // 2

    def first_half(t, carry):
        acc_ref[chunk_rows(t), :] = fwd_part(t)
        acc_ref[chunk_rows(n_chunks - 1 - t), :] = bwd_part(n_chunks - 1 - t)
        return carry

    def second_half(t, carry):
        finalize(t, acc_ref[chunk_rows(t), :] + fwd_part(t))
        u = n_chunks - 1 - t
        finalize(u, acc_ref[chunk_rows(u), :] + bwd_part(u))
        return carry

    lax.fori_loop(0, half, first_half, 0)
    lax.fori_loop(half, n_chunks, second_half, 0)


def _retention(proj, kv_ctx, log_gamma, batch, seq, ctx_len, off_q, off_k, off_v, off_g):
    n = proj.shape[0]
    dk, dv = RET_QK_DIM, RET_V_DIM
    assert seq % (2 * RET_CHUNK) == 0, "the paired scan needs an even number of chunks"
    grid_spec = pltpu.PrefetchScalarGridSpec(
        num_scalar_prefetch=1,
        grid=(batch, RET_HEADS),
        in_specs=[pl.BlockSpec((seq, dk), lambda b, h, lg: (b, off_q // dk + h)),
                  pl.BlockSpec((seq, dk), lambda b, h, lg: (b, off_k // dk + h)),
                  pl.BlockSpec((seq, dv), lambda b, h, lg: (b, off_v // dv + h)),
                  pl.BlockSpec((seq, dv), lambda b, h, lg: (b, off_g // dv + h)),
                  pl.BlockSpec((ctx_len, dk), lambda b, h, lg: (b, h)),
                  pl.BlockSpec((ctx_len, dv), lambda b, h, lg: (b, RET_QK_W // dv + h))],
        out_specs=pl.BlockSpec((seq, dv), lambda b, h, lg: (b, h)),
        scratch_shapes=[pltpu.VMEM((seq, dv), F32),
                        pltpu.VMEM((dk, dv), F32),
                        pltpu.VMEM((dk, dv), F32),
                        pltpu.VMEM((RET_CHUNK, RET_CHUNK), F32)],
    )
    return pl.pallas_call(
        functools.partial(_ret_kernel, seq=seq, ctx_len=ctx_len),
        out_shape=jax.ShapeDtypeStruct((n, RET_V_W), BF16),
        grid_spec=grid_spec,
        compiler_params=_params(("parallel", "arbitrary")),
        name="ret",
    )(log_gamma, proj, proj, proj, proj, kv_ctx, kv_ctx)


def _mix_kernel(yg_ref, cb_ref, cc_ref, ch_ref, gr_ref, gc_ref, x_ref, wro_ref, wco_ref, wo_ref,
                cw_ref, g1_ref, sh2_ref, sc2_ref, n2_ref, wr_ref, br_ref,
                x1_ref, h2_ref, ri_ref, rw_ref, cnt_ref, carry_ref):
    i = pl.program_id(0)
    tm = x_ref.shape[0]

    @pl.when(i == 0)
    def _():
        carry_ref[...] = jnp.zeros_like(carry_ref)

    ret_branch = jnp.dot(yg_ref[...], wro_ref[...], preferred_element_type=F32)

    p = cc_ref[...].astype(F32) * ch_ref[...].astype(F32)
    tpos = lax.broadcasted_iota(jnp.int32, (tm, 1), 0) % GRID_W
    prev = jnp.where(tpos != 0, pltpu.roll(p, 1, axis=0), 0.0)
    nxt = jnp.where(tpos != GRID_W - 1, pltpu.roll(p, tm - 1, axis=0), 0.0)
    cw = cw_ref[...]
    u = cw[0:1, :] * prev + cw[1:2, :] * p + cw[2:3, :] * nxt
    conv_in = (cb_ref[...].astype(F32) * u).astype(BF16)
    conv_branch = jnp.dot(conv_in, wco_ref[...], preferred_element_type=F32)

    merged = (jax.nn.sigmoid(gr_ref[...].astype(F32)) * ret_branch
              + jax.nn.sigmoid(gc_ref[...].astype(F32)) * conv_branch)
    mixed = jnp.dot(merged.astype(BF16), wo_ref[...], preferred_element_type=F32)
    x1 = x_ref[...] + g1_ref[0] * mixed
    x1_ref[...] = x1

    y = x1 * lax.rsqrt(jnp.mean(x1 * x1, axis=-1, keepdims=True) + NORM_EPS) * n2_ref[...]
    h2 = y * (1.0 + sc2_ref[0]) + sh2_ref[0]
    h2_ref[...] = h2

    h_hi = h2.astype(BF16)
    h_lo = (h2 - h_hi.astype(F32)).astype(BF16)
    hw = jnp.dot(h_hi, wr_ref[...], preferred_element_type=F32)
    lw = jnp.dot(h_lo, wr_ref[:, :LANES], preferred_element_type=F32)
    logits = hw[:, :LANES] + (hw[:, LANES:] + lw) + br_ref[...]
    lane = lax.broadcasted_iota(jnp.int32, (tm, LANES), 1)
    neg = jnp.float32(-jnp.inf)
    big = jnp.int32(1 << 20)
    is_g = (lane >= N_EXPERTS) & (lane < N_EXPERTS + N_GROUPS)
    gl = jnp.where(is_g, logits, neg)
    gmax = jnp.max(gl, axis=1, keepdims=True)
    gidx = jnp.min(jnp.where(gl == gmax, lane, big), axis=1, keepdims=True) - N_EXPERTS
    gsum = jnp.sum(jnp.where(is_g, jnp.exp(logits - gmax), 0.0), axis=1, keepdims=True)
    g_w = 1.0 / gsum

    in_grp = (lane // EXPERTS_PER_GROUP) == gidx
    el = jnp.where(in_grp, logits, neg)
    emax = jnp.max(el, axis=1, keepdims=True)
    ex = jnp.where(in_grp, jnp.exp(logits - emax), 0.0)
    prob = ex / jnp.sum(ex, axis=1, keepdims=True)
    pm = jnp.where(in_grp, prob, -1.0)
    p1 = jnp.max(pm, axis=1, keepdims=True)
    i1 = jnp.min(jnp.where(pm == p1, lane, big), axis=1, keepdims=True)
    pm2 = jnp.where(lane == i1, -1.0, pm)
    p2 = jnp.max(pm2, axis=1, keepdims=True)
    i2 = jnp.min(jnp.where(pm2 == p2, lane, big), axis=1, keepdims=True)
    den = p1 + p2
    c1 = g_w * (p1 / den)
    c2 = g_w * (p2 / den)

    hot1 = lane == i1
    hot2 = lane == i2
    onehot = jnp.where(hot1, 1.0, jnp.where(hot2, 1.0, 0.0))
    tri = (lax.broadcasted_iota(jnp.int32, (tm, tm), 0)
           > lax.broadcasted_iota(jnp.int32, (tm, tm), 1))
    ranks = jnp.dot(jnp.where(tri, 1.0, 0.0).astype(BF16), onehot.astype(BF16),
                    preferred_element_type=F32) + carry_ref[...]
    r1 = jnp.sum(jnp.where(hot1, ranks, 0.0), axis=1, keepdims=True).astype(jnp.int32)
    r2 = jnp.sum(jnp.where(hot2, ranks, 0.0), axis=1, keepdims=True).astype(jnp.int32)
    carry_ref[...] = carry_ref[...] + jnp.sum(onehot, axis=0, keepdims=True)
    cnt_ref[...] = carry_ref[...]

    ri_ref[...] = jnp.where(lane == 0, i1, jnp.where(lane == 1, i2,
                            jnp.where(lane == 2, r1, jnp.where(lane == 3, r2, 0))))
    rw_ref[...] = jnp.where(lane == 0, c1, jnp.where(lane == 1, c2, 0.0))


def _mix(yg, proj, x2d, w_ret_o, w_conv_o, w_out, conv_w, gate1, shift2, scale2, norm2_w,
         w_router, b_router, seq, off_cb, tm):
    n, d = x2d.shape
    per_b = seq // tm
    cblk = off_cb // d
    row = lambda i: (i, 0)
    const2 = lambda i: (0, 0)
    bidx = lambda i: (i // per_b, 0, 0)
    col = lambda k: (lambda i: (i, cblk + k))
    return pl.pallas_call(
        _mix_kernel,
        out_shape=(jax.ShapeDtypeStruct((n, d), F32),
                   jax.ShapeDtypeStruct((n, d), F32),
                   jax.ShapeDtypeStruct((n, LANES), jnp.int32),
                   jax.ShapeDtypeStruct((n, LANES), F32),
                   jax.ShapeDtypeStruct((1, LANES), F32)),
        grid=(n // tm,),
        in_specs=[pl.BlockSpec((tm, RET_V_W), row),
                  pl.BlockSpec((tm, d), col(0)),
                  pl.BlockSpec((tm, d), col(1)),
                  pl.BlockSpec((tm, d), col(2)),
                  pl.BlockSpec((tm, d), col(3)),
                  pl.BlockSpec((tm, d), col(4)),
                  pl.BlockSpec((tm, d), row),
                  pl.BlockSpec((RET_V_W, d), const2, pipeline_mode=pl.Buffered(1)),
                  pl.BlockSpec((d, d), const2, pipeline_mode=pl.Buffered(1)),
                  pl.BlockSpec((d, d), const2, pipeline_mode=pl.Buffered(1)),
                  pl.BlockSpec((3, d), const2),
                  pl.BlockSpec((1, 1, d), bidx),
                  pl.BlockSpec((1, 1, d), bidx),
                  pl.BlockSpec((1, 1, d), bidx),
                  pl.BlockSpec((1, d), const2),
                  pl.BlockSpec((d, 2 * LANES), const2),
                  pl.BlockSpec((1, LANES), const2)],
        out_specs=(pl.BlockSpec((tm, d), row),
                   pl.BlockSpec((tm, d), row),
                   pl.BlockSpec((tm, LANES), row),
                   pl.BlockSpec((tm, LANES), row),
                   pl.BlockSpec((1, LANES), const2)),
        scratch_shapes=[pltpu.VMEM((1, LANES), F32)],
        compiler_params=_params(("arbitrary",)),
        name="mix",
    )(yg, proj, proj, proj, proj, proj, x2d, w_ret_o, w_conv_o, w_out, conv_w,
      gate1, shift2, scale2, norm2_w.reshape(1, d), w_router, b_router)


def _row_copy(src_ref, src_row, dst_ref, dst_row, sem):
    return pltpu.make_async_copy(src_ref.at[pl.ds(src_row, 1), :],
                                 dst_ref.at[pl.ds(dst_row, 1), :], sem)


SCATTER_SLOTS = 3


def _scatter_kernel(pos_ref, zt_ref, h_ref, xs_ref, hbuf_ref, zero_ref, lsem, ssem, zsem):
    i = pl.program_id(0)
    n_steps = pl.num_programs(0)
    tm = hbuf_ref.shape[1]
    te = zero_ref.shape[0]
    base = i * tm
    n_tok = pos_ref.shape[0] // 2

    def load(tile, slot):
        src = h_ref.at[pl.ds(pl.multiple_of(tile * tm, tm), tm), :]
        return pltpu.make_async_copy(src, hbuf_ref.at[slot], lsem.at[slot])

    def wait_rows(slot):
        for _ in range(2):
            pltpu.make_async_copy(hbuf_ref.at[slot], xs_ref.at[pl.ds(0, tm), :], ssem.at[slot]).wait()

    @pl.when(i == 0)
    def _():
        load(0, 0).start()
        load(1, 1).start()
        zero_ref[...] = jnp.zeros_like(zero_ref)

        def tile_copy(z):
            row = pl.multiple_of(zt_ref[z] * te, te)
            return pltpu.make_async_copy(zero_ref, xs_ref.at[pl.ds(row, te), :], zsem)

        def zissue(z, carry):
            @pl.when(zt_ref[z] >= 0)
            def _():
                tile_copy(z).start()
            return carry

        def zdrain(z, carry):
            @pl.when(zt_ref[z] >= 0)
            def _():
                tile_copy(z).wait()
            return carry

        lax.fori_loop(0, zt_ref.shape[0], zissue, 0)
        lax.fori_loop(0, zt_ref.shape[0], zdrain, 0)

    slot = lax.rem(i, SCATTER_SLOTS)
    reload_slot = lax.rem(i + 2, SCATTER_SLOTS)
    load(i, slot).wait()
    src = hbuf_ref.at[slot]
    for r in range(tm):
        _row_copy(src, r, xs_ref, pos_ref[base + r], ssem.at[slot]).start(priority=0)
        _row_copy(src, r, xs_ref, pos_ref[n_tok + base + r], ssem.at[slot]).start(priority=1)

    @pl.when(i > 0)
    def _():
        wait_rows(reload_slot)

    load(jnp.minimum(i + 2, n_steps - 1), reload_slot).start()

    @pl.when(i == n_steps - 1)
    def _():
        wait_rows(slot)
        load(i, lax.rem(i + 1, SCATTER_SLOTS)).wait()
        load(i, reload_slot).wait()


def _scatter_rows(pos_flat, zero_tiles, h2, n_sorted, tm):
    n, d = h2.shape
    assert n // tm >= SCATTER_SLOTS
    grid_spec = pltpu.PrefetchScalarGridSpec(
        num_scalar_prefetch=2,
        grid=(n // tm,),
        in_specs=[pl.BlockSpec(memory_space=pl.ANY)],
        out_specs=pl.BlockSpec(memory_space=pl.ANY),
        scratch_shapes=[pltpu.VMEM((SCATTER_SLOTS, tm, d), F32),
                        pltpu.VMEM((EXPERT_TILE, d), F32),
                        pltpu.SemaphoreType.DMA((SCATTER_SLOTS,)),
                        pltpu.SemaphoreType.DMA((SCATTER_SLOTS,)),
                        pltpu.SemaphoreType.DMA],
    )
    return pl.pallas_call(
        _scatter_kernel,
        out_shape=jax.ShapeDtypeStruct((n_sorted, d), F32),
        grid_spec=grid_spec,
        compiler_params=_params(("arbitrary",)),
        name="scatter",
    )(pos_flat, zero_tiles, h2)


ROW_TILE_SLOTS = 3


def _expert_kernel(tf_ref, te_ref, tn_ref, ts_ref, xs_ref, w13_ref, w2_ref, ys_ref,
                   xbuf_ref, w13f_ref, w2f_ref, w13b_ref, w2b_ref, xsem, sem13, sem2):
    t = pl.program_id(0)
    nt = pl.num_programs(0)
    rows = xbuf_ref.shape[1]
    flag = tf_ref[t]

    def tile_copy(tile, slot):
        src = xs_ref.at[pl.ds(pl.multiple_of(tile * rows, rows), rows), :]
        return pltpu.make_async_copy(src, xbuf_ref.at[slot], xsem.at[slot])

    def weight_copies(e, slot):
        return (pltpu.make_async_copy(w13_ref.at[e], w13f_ref.at[slot], sem13.at[slot]),
                pltpu.make_async_copy(w2_ref.at[e], w2f_ref.at[slot], sem2.at[slot]))

    @pl.when(t == 0)
    def _():
        for cp in weight_copies(te_ref[0], ts_ref[0]):
            cp.start()
        tile_copy(0, 0).start()
        tile_copy(1, 1).start()

    slot = lax.rem(t, ROW_TILE_SLOTS)
    ahead_slot = lax.rem(t + 2, ROW_TILE_SLOTS)
    tile_copy(t, slot).wait()
    tile_copy(jnp.minimum(t + 2, nt - 1), ahead_slot).start()

    @pl.when(flag == 2)
    def _():
        slot = ts_ref[t]
        for cp in weight_copies(te_ref[t], slot):
            cp.wait()
        w13b_ref[...] = w13f_ref[slot].astype(BF16)
        w2b_ref[...] = w2f_ref[slot].astype(BF16)

        @pl.when(tn_ref[t] >= 0)
        def _():
            for cp in weight_copies(tn_ref[t], 1 - slot):
                cp.start()

    @pl.when(flag > 0)
    def _():
        x = xbuf_ref[slot].astype(BF16)
        hid = jnp.dot(x, w13b_ref[...], preferred_element_type=F32)
        a = hid[:, :EXPERT_HIDDEN]
        act = a * jax.nn.sigmoid(a) * hid[:, EXPERT_HIDDEN:]
        ys_ref[...] = jnp.dot(act.astype(BF16), w2b_ref[...], preferred_element_type=F32)

    @pl.when(flag == 0)
    def _():
        ys_ref[...] = jnp.zeros_like(ys_ref)

    @pl.when(t == nt - 1)
    def _():
        tile_copy(t, lax.rem(t + 1, ROW_TILE_SLOTS)).wait()
        tile_copy(t, ahead_slot).wait()


def _experts(tile_flag, tile_expert, tile_next, tile_slot, xs, w13, w2):
    n_sorted, d = xs.shape
    te = EXPERT_TILE
    n_tiles = n_sorted // te
    assert n_tiles >= ROW_TILE_SLOTS
    hid2 = w13.shape[2]
    grid_spec = pltpu.PrefetchScalarGridSpec(
        num_scalar_prefetch=4,
        grid=(n_tiles,),
        in_specs=[pl.BlockSpec(memory_space=pl.ANY),
                  pl.BlockSpec(memory_space=pl.ANY),
                  pl.BlockSpec(memory_space=pl.ANY)],
        out_specs=pl.BlockSpec((te, d), lambda t, *_: (t, 0)),
        scratch_shapes=[pltpu.VMEM((ROW_TILE_SLOTS, te, d), F32),
                        pltpu.VMEM((2, d, hid2), F32),
                        pltpu.VMEM((2, hid2 // 2, d), F32),
                        pltpu.VMEM((d, hid2), BF16),
                        pltpu.VMEM((hid2 // 2, d), BF16),
                        pltpu.SemaphoreType.DMA((ROW_TILE_SLOTS,)),
                        pltpu.SemaphoreType.DMA((2,)),
                        pltpu.SemaphoreType.DMA((2,))],
    )
    return pl.pallas_call(
        _expert_kernel,
        out_shape=jax.ShapeDtypeStruct((n_sorted, d), F32),
        grid_spec=grid_spec,
        compiler_params=_params(("arbitrary",)),
        name="experts",
    )(tile_flag, tile_expert, tile_next, tile_slot, xs, w13, w2)


GATHER_SLOTS = 3


def _combine_kernel(pos_ref, ys_ref, x1_ref, rw_ref, g2_ref, fw_ref, o_ref, ya_ref, yb_ref, sem):
    i = pl.program_id(0)
    n_steps = pl.num_programs(0)
    tm = ya_ref.shape[1]
    n_tok = pos_ref.shape[0] // 2

    def issue(tile, slot):
        base = tile * tm
        dst_a = ya_ref.at[slot]
        dst_b = yb_ref.at[slot]
        for r in range(tm):
            _row_copy(ys_ref, pos_ref[base + r], dst_a, r, sem.at[slot]).start(priority=0)
            _row_copy(ys_ref, pos_ref[n_tok + base + r], dst_b, r, sem.at[slot]).start(priority=1)

    def wait(slot):
        pltpu.make_async_copy(ys_ref.at[pl.ds(0, tm), :], ya_ref.at[slot], sem.at[slot]).wait()
        pltpu.make_async_copy(ys_ref.at[pl.ds(0, tm), :], yb_ref.at[slot], sem.at[slot]).wait()

    @pl.when(i == 0)
    def _():
        issue(0, 0)
        issue(1, 1)

    slot = lax.rem(i, GATHER_SLOTS)
    ahead_slot = lax.rem(i + 2, GATHER_SLOTS)
    wait(slot)
    issue(jnp.minimum(i + 2, n_steps - 1), ahead_slot)
    rw = rw_ref[...]
    moe = rw[:, 0:1] * ya_ref[slot] + rw[:, 1:2] * yb_ref[slot]
    x2 = x1_ref[...] + g2_ref[0] * moe
    o_ref[...] = x2 * lax.rsqrt(jnp.mean(x2 * x2, axis=-1, keepdims=True) + NORM_EPS) * fw_ref[...]

    @pl.when(i == n_steps - 1)
    def _():
        wait(lax.rem(i + 1, GATHER_SLOTS))
        wait(ahead_slot)


def _combine(pos_flat, ys, x1, route_w, gate2, final_w, seq, tm):
    n, d = x1.shape
    per_b = seq // tm
    assert n // tm >= GATHER_SLOTS
    grid_spec = pltpu.PrefetchScalarGridSpec(
        num_scalar_prefetch=1,
        grid=(n // tm,),
        in_specs=[pl.BlockSpec(memory_space=pl.ANY),
                  pl.BlockSpec((tm, d), lambda i, pos: (i, 0)),
                  pl.BlockSpec((tm, LANES), lambda i, pos: (i, 0)),
                  pl.BlockSpec((1, 1, d), lambda i, pos: (i // per_b, 0, 0)),
                  pl.BlockSpec((1, d), lambda i, pos: (0, 0))],
        out_specs=pl.BlockSpec((tm, d), lambda i, pos: (i, 0)),
        scratch_shapes=[pltpu.VMEM((GATHER_SLOTS, tm, d), F32),
                        pltpu.VMEM((GATHER_SLOTS, tm, d), F32),
                        pltpu.SemaphoreType.DMA((GATHER_SLOTS,))],
    )
    return pl.pallas_call(
        _combine_kernel,
        out_shape=jax.ShapeDtypeStruct((n, d), F32),
        grid_spec=grid_spec,
        compiler_params=_params(("arbitrary",)),
        name="combine",
    )(pos_flat, ys, x1, route_w, gate2, final_w.reshape(1, d))


def _rope_tables(seq):
    pos = np.arange(seq)
    nf = RET_QK_DIM // 4
    inv = (np.float32(ROPE_BASE) ** (-np.arange(nf, dtype=np.float32) / np.float32(nf))).astype(np.float32)
    ang_r = ((pos // GRID_W).astype(np.float32)[:, None] * inv[None, :]).astype(np.float64)
    ang_c = ((pos % GRID_W).astype(np.float32)[:, None] * inv[None, :]).astype(np.float64)
    return (jnp.asarray(np.concatenate([np.cos(ang_r), np.cos(ang_c)], axis=1), F32),
            jnp.asarray(np.concatenate([np.sin(ang_r), np.sin(ang_c)], axis=1), F32))


def _weight_prep_kernel(w_ref, o_ref, octx_ref, *, n_rot, ctx_cols):
    half = LANES // 2
    first = lax.broadcasted_iota(jnp.int32, (w_ref.shape[0], LANES), 1) < half
    for c0 in range(0, n_rot, RET_QK_DIM):
        a = w_ref[:, c0:c0 + LANES]
        b = w_ref[:, c0 + LANES:c0 + 2 * LANES]
        o_ref[:, c0:c0 + LANES] = jnp.where(first, a, pltpu.roll(b, half, axis=1)).astype(o_ref.dtype)
        o_ref[:, c0 + LANES:c0 + 2 * LANES] = jnp.where(first, pltpu.roll(a, half, axis=1), b).astype(o_ref.dtype)
    o_ref[:, n_rot:] = w_ref[:, n_rot:].astype(o_ref.dtype)
    octx_ref[...] = o_ref[:, ctx_cols[0]:ctx_cols[1]]


def _weight_prep(w_in, n_rot, ctx_cols):
    d, width = w_in.shape
    tr = 128
    ctx_width = ctx_cols[1] - ctx_cols[0]
    return pl.pallas_call(
        functools.partial(_weight_prep_kernel, n_rot=n_rot, ctx_cols=ctx_cols),
        out_shape=(jax.ShapeDtypeStruct((d, width), BF16),
                   jax.ShapeDtypeStruct((d, ctx_width), BF16)),
        grid=(d // tr,),
        in_specs=[pl.BlockSpec((tr, width), lambda i: (i, 0))],
        out_specs=(pl.BlockSpec((tr, width), lambda i: (i, 0)),
                   pl.BlockSpec((tr, ctx_width), lambda i: (i, 0))),
        compiler_params=_params(("parallel",)),
        name="wprep",
    )(w_in)


def _layer(x, c, ctx, c_ctx, w_mod, b_mod, norm1_w, norm2_w, w_in, log_decay, gn_w, w_ret_o,
           conv_w, w_conv_o, w_out, rg_w, rg_b, re_w, re_b, w13, w2, final_w):
    batch, seq, d = x.shape
    ctx_len = ctx.shape[1]
    n = batch * seq
    off_q = 0
    off_k = off_q + RET_QK_W
    off_v = off_k + RET_QK_W
    off_g = off_v + RET_V_W
    off_cb = off_g + RET_V_W

    pad = (-(batch + 1)) % 8
    c_all = jnp.concatenate([c, c_ctx[None, :], jnp.zeros((pad, d), F32)], axis=0)
    mod = _modulation(c_all, w_mod, b_mod)
    mx = mod[:batch].reshape(batch, 1, 6, d)
    shift1, scale1, gate1, shift2, scale2, gate2 = (mx[:, :, k, :] for k in range(6))
    mc = mod[batch].reshape(1, 1, 6, d)
    shift1c, scale1c = mc[:, :, 0, :], mc[:, :, 1, :]

    w_in_b, w_ctx_b = _weight_prep(w_in, off_v, (off_k, off_g))
    cos_t, sin_t = _rope_tables(seq)
    k_scale = RET_QK_DIM ** -0.5

    tm_proj = min(512, seq)
    main_modes = (("rope", 1.0), ("rope", k_scale), ("plain", None), ("plain", None),
                  ("swish_gain", 0), ("swish_gain", 1), ("plain", None), ("plain", None), ("plain", None),
                  ("plain", None), ("plain", None))
    gain = gn_w.reshape(1, RET_V_W)
    proj = _projection(x.reshape(n, d), norm1_w, shift1, scale1, cos_t, sin_t, gain, w_in_b, seq,
                       modes=main_modes, tm=tm_proj)
    tm_ctx = batch * ctx_len
    no_rope = jnp.zeros((tm_ctx, LANES), F32)
    kv_ctx = _projection(ctx.reshape(tm_ctx, d), norm1_w, shift1c, scale1c, no_rope, no_rope, gain,
                         w_ctx_b, tm_ctx,
                         modes=(("scale", k_scale), ("plain", None), ("plain", None)), tm=min(512, tm_ctx))

    log_gamma = jnp.log1p(-jnp.exp(log_decay.astype(F32)))
    yg = _retention(proj, kv_ctx, log_gamma, batch, seq, ctx_len, off_q, off_k, off_v, off_g)

    w_router = jnp.concatenate(
        [re_w, rg_w, jnp.zeros((d, LANES - N_EXPERTS - N_GROUPS), F32)], axis=1)
    b_router = jnp.concatenate(
        [re_b, rg_b, jnp.zeros((LANES - N_EXPERTS - N_GROUPS,), F32)]).reshape(1, LANES)
    w_router_hi = w_router.astype(BF16)
    w_router_lo = (w_router - w_router_hi.astype(F32)).astype(BF16)
    w_router = jnp.concatenate([w_router_hi, w_router_lo], axis=1)
    tm_mix = min(512, seq)
    x1, h2, route_i, route_w, counts = _mix(
        yg, proj, x.reshape(n, d), w_ret_o.astype(BF16), w_conv_o.astype(BF16), w_out.astype(BF16),
        conv_w, gate1, shift2, scale2, norm2_w, w_router, b_router, seq, off_cb, tm_mix)

    te = EXPERT_TILE
    cnt = counts[0, :N_EXPERTS].astype(jnp.int32)
    padded = ((cnt + te - 1) // te) * te
    ends = jnp.cumsum(padded)
    starts = ends - padded
    expert_ids = jnp.arange(N_EXPERTS, dtype=jnp.int32)
    route_t = route_i[:, 0:4].T
    seg_start = jnp.sum(jnp.where(route_t[0:2, :, None] == expert_ids, starts, 0), axis=-1)
    pos_flat = (seg_start + route_t[2:4]).reshape(-1).astype(jnp.int32)
    n_tiles = (2 * n) // te + N_EXPERTS
    tile_row = jnp.arange(n_tiles, dtype=jnp.int32) * te
    tile_expert = jnp.minimum(jnp.sum((tile_row[:, None] >= ends[None, :]).astype(jnp.int32), axis=1),
                              N_EXPERTS - 1).astype(jnp.int32)
    new_expert = jnp.concatenate([jnp.ones((1,), bool), tile_expert[1:] != tile_expert[:-1]])
    tile_flag = jnp.where(tile_row < ends[-1], jnp.where(new_expert, 2, 1), 0).astype(jnp.int32)

    valid_tiles = ends[-1] // te
    last_tile = jnp.where(padded > 0, ends // te - 1, -1)
    tail_tile = valid_tiles + jnp.arange(N_EXPERTS, dtype=jnp.int32)
    tail_tile = jnp.where(tail_tile < n_tiles, tail_tile, -1)
    zero_tiles = jnp.concatenate([last_tile, tail_tile]).astype(jnp.int32)

    xs = _scatter_rows(pos_flat, zero_tiles, h2, n_tiles * te, min(512, seq // 2))
    nonempty = padded > 0
    later = nonempty[None, :] & (expert_ids[None, :] > expert_ids[:, None])
    next_expert = jnp.min(jnp.where(later, expert_ids[None, :], N_EXPERTS), axis=1)
    next_expert = jnp.where(next_expert == N_EXPERTS, -1, next_expert)
    slot = (jnp.cumsum(nonempty.astype(jnp.int32)) - 1) % 2
    tile_hot = tile_expert[:, None] == expert_ids[None, :]
    tile_next = jnp.sum(jnp.where(tile_hot, next_expert, 0), axis=1).astype(jnp.int32)
    tile_slot = jnp.sum(jnp.where(tile_hot, slot, 0), axis=1).astype(jnp.int32)

    ys = _experts(tile_flag, tile_expert, tile_next, tile_slot, xs, w13, w2)
    out = _combine(pos_flat, ys, x1, route_w, gate2, final_w, seq, min(256, seq // 2))
    return out.reshape(batch, seq, d)


def kernel(x, c, ctx, c_ctx, w_mod, b_mod, norm1_w, norm2_w, w_in, ret_log_decay, ret_gn_w, w_ret_o,
           conv_w, w_conv_o, w_out, router_group_w, router_group_b, router_expert_w, router_expert_b,
           expert_w13, expert_w2, final_norm_w):
    assert w_mod.shape[0] == 1, "single-layer problem"
    return _layer(x, c, ctx, c_ctx, w_mod[0], b_mod[0], norm1_w[0], norm2_w[0], w_in[0],
                  ret_log_decay[0], ret_gn_w[0], w_ret_o[0], conv_w[0], w_conv_o[0], w_out[0],
                  router_group_w[0], router_group_b[0], router_expert_w[0], router_expert_b[0],
                  expert_w13[0], expert_w2[0], final_norm_w)
```

```python
import functools

import jax
import jax.numpy as jnp
import numpy as np
from jax import lax
from jax.experimental import pallas as pl
from jax.experimental.pallas import tpu as pltpu

GRID_W = 64
RET_HEADS = 4
RET_QK_DIM = 256
RET_V_DIM = 512
RET_QK_W = RET_HEADS * RET_QK_DIM
RET_V_W = RET_HEADS * RET_V_DIM
N_GROUPS = 4
EXPERTS_PER_GROUP = 8
N_EXPERTS = N_GROUPS * EXPERTS_PER_GROUP
EXPERT_HIDDEN = 512
ROPE_BASE = 10000.0
NORM_EPS = 1e-6
GN_EPS = 1e-5

RET_CHUNK = 256
LANES = 128
EXPERT_TILE = 256
VMEM_LIMIT = 56 * 1024 * 1024

BF16 = jnp.bfloat16
F32 = jnp.float32


def _params(sem, vmem=VMEM_LIMIT):
    return pltpu.CompilerParams(dimension_semantics=sem, vmem_limit_bytes=vmem)


def _mod_kernel(c_ref, w_ref, b_ref, o_ref):
    c = c_ref[...]
    s = c * jax.nn.sigmoid(c)
    o_ref[...] = jnp.dot(s.astype(BF16), w_ref[...].astype(BF16),
                         preferred_element_type=F32) + b_ref[...]


def _modulation(c_all, w_mod, b_mod):
    rows, d = c_all.shape
    n_out = w_mod.shape[1]
    tn = 1536
    return pl.pallas_call(
        _mod_kernel,
        out_shape=jax.ShapeDtypeStruct((rows, n_out), F32),
        grid=(n_out // tn,),
        in_specs=[pl.BlockSpec((rows, d), lambda j: (0, 0)),
                  pl.BlockSpec((d, tn), lambda j: (0, j)),
                  pl.BlockSpec((1, tn), lambda j: (0, j))],
        out_specs=pl.BlockSpec((rows, tn), lambda j: (0, j)),
        compiler_params=_params(("parallel",)),
        name="mod",
    )(c_all, w_mod, b_mod.reshape(1, n_out))


def _rope_store(acc, cos, sin, scale, o_ref):
    for hd in range(acc.shape[1] // RET_QK_DIM):
        lo = hd * RET_QK_DIM
        u1 = acc[:, lo:lo + LANES]
        u2 = acc[:, lo + LANES:lo + 2 * LANES]
        r1 = u1 * cos - u2 * sin
        r2 = u1 * sin + u2 * cos
        if scale != 1.0:
            r1 = r1 * scale
            r2 = r2 * scale
        o_ref[:, lo:lo + LANES] = r1.astype(o_ref.dtype)
        o_ref[:, lo + LANES:lo + 2 * LANES] = r2.astype(o_ref.dtype)


def _proj_kernel(x_ref, nw_ref, sh_ref, sc_ref, cos_ref, sin_ref, gn_ref, w_ref, o_ref, *, modes, tn):
    x = x_ref[...]
    y = x * lax.rsqrt(jnp.mean(x * x, axis=-1, keepdims=True) + NORM_EPS) * nw_ref[...]
    h = (y * (1.0 + sc_ref[0]) + sh_ref[0]).astype(BF16)
    assert len(modes) * tn == w_ref.shape[1]
    for j, (kind, arg) in enumerate(modes):
        cols = slice(j * tn, (j + 1) * tn)
        acc = jnp.dot(h, w_ref[:, cols], preferred_element_type=F32)
        if kind == "rope":
            _rope_store(acc, cos_ref[...], sin_ref[...], arg, o_ref.at[:, cols])
        elif kind == "scale":
            o_ref[:, cols] = (acc * arg).astype(o_ref.dtype)
        elif kind == "swish_gain":
            gain = gn_ref[:, arg * tn:(arg + 1) * tn]
            o_ref[:, cols] = (acc * jax.nn.sigmoid(acc) * gain).astype(o_ref.dtype)
        else:
            o_ref[:, cols] = acc.astype(o_ref.dtype)


def _projection(x2d, norm_w, shift, scale, cos_t, sin_t, gain, w_bf16, seq, modes, tm):
    n, d = x2d.shape
    width = w_bf16.shape[1]
    tn = 1024
    per_b = seq // tm
    nb = shift.shape[0]
    bidx = (lambda i: (i // per_b, 0, 0)) if nb > 1 else (lambda i: (0, 0, 0))
    return pl.pallas_call(
        functools.partial(_proj_kernel, modes=modes, tn=tn),
        out_shape=jax.ShapeDtypeStruct((n, width), BF16),
        grid=(n // tm,),
        in_specs=[pl.BlockSpec((tm, d), lambda i: (i, 0)),
                  pl.BlockSpec((1, d), lambda i: (0, 0)),
                  pl.BlockSpec((1, 1, d), bidx),
                  pl.BlockSpec((1, 1, d), bidx),
                  pl.BlockSpec((tm, LANES), lambda i: (i % per_b, 0)),
                  pl.BlockSpec((tm, LANES), lambda i: (i % per_b, 0)),
                  pl.BlockSpec(gain.shape, lambda i: (0, 0)),
                  pl.BlockSpec((d, width), lambda i: (0, 0), pipeline_mode=pl.Buffered(1))],
        out_specs=pl.BlockSpec((tm, width), lambda i: (i, 0)),
        compiler_params=_params(("parallel",)),
        name="proj",
    )(x2d, norm_w.reshape(1, d), shift, scale, cos_t, sin_t, gain, w_bf16)


def _dot_t0(a, b):
    return lax.dot_general(a, b, (((0,), (0,)), ((), ())), preferred_element_type=F32)


def _ret_kernel(lg_ref, q_ref, k_ref, v_ref, g_ref, kc_ref, vc_ref, o_ref,
                acc_ref, sf_ref, sb_ref, dm_ref, *, seq, ctx_len):
    h = pl.program_id(1)
    lgf = lg_ref[0, h]
    lgb = lg_ref[1, h]
    c = RET_CHUNK
    n_chunks = seq // c

    cpos = lax.broadcasted_iota(jnp.int32, (ctx_len, 1), 0).astype(F32)
    kc = kc_ref[...].astype(F32)
    vc = vc_ref[...]
    sf_ref[...] = _dot_t0((kc * jnp.exp((ctx_len - 1.0 - cpos) * lgf)).astype(BF16), vc)
    sb_ref[...] = _dot_t0((kc * jnp.exp(cpos * lgb)).astype(BF16), vc)

    ri = lax.broadcasted_iota(jnp.int32, (c, c), 0)
    ci = lax.broadcasted_iota(jnp.int32, (c, c), 1)
    dist = (ri - ci).astype(F32)
    dm_ref[...] = jnp.exp(jnp.abs(dist) * jnp.where(dist >= 0, lgf, lgb))

    idx = lax.broadcasted_iota(jnp.int32, (c, 1), 0).astype(F32)
    xi_f = jnp.exp((idx + 1.0) * lgf)
    zeta_f = jnp.exp((c - 1.0 - idx) * lgf)
    xi_b = jnp.exp((c - idx) * lgb)
    zeta_b = jnp.exp(idx * lgb)
    cd_f = jnp.exp(jnp.full((1, 1), float(c), F32) * lgf)
    cd_b = jnp.exp(jnp.full((1, 1), float(c), F32) * lgb)

    def chunk_rows(ic):
        return pl.ds(pl.multiple_of(ic * c, c), c)

    def fwd_part(ic):
        rows = chunk_rows(ic)
        q = q_ref[rows, :]
        k = k_ref[rows, :]
        v = v_ref[rows, :]
        s = lax.dot_general(q, k, (((1,), (1,)), ((), ())), preferred_element_type=F32)
        intra = jnp.dot((s * dm_ref[...]).astype(BF16), v, preferred_element_type=F32)
        inter = jnp.dot(q, sf_ref[...].astype(BF16), preferred_element_type=F32)
        kz = (k.astype(F32) * zeta_f).astype(BF16)
        sf_ref[...] = cd_f * sf_ref[...] + _dot_t0(kz, v)
        return intra + xi_f * inter

    def bwd_part(ic):
        rows = chunk_rows(ic)
        q = q_ref[rows, :]
        k = k_ref[rows, :]
        v = v_ref[rows, :]
        inter = jnp.dot(q, sb_ref[...].astype(BF16), preferred_element_type=F32)
        kz = (k.astype(F32) * zeta_b).astype(BF16)
        sb_ref[...] = cd_b * sb_ref[...] + _dot_t0(kz, v)
        return xi_b * inter

    def finalize(ic, y):
        rows = chunk_rows(ic)
        mu = jnp.mean(y, axis=-1, keepdims=True)
        yc = y - mu
        var = jnp.mean(yc * yc, axis=-1, keepdims=True)
        o_ref[rows, :] = (g_ref[rows, :].astype(F32) * (yc * lax.rsqrt(var + GN_EPS))).astype(o_ref.dtype)

    half = n_chunks // 2

    def first_half(t, carry):
        acc_ref[chunk_rows(t), :] = fwd_part(t)
        acc_ref[chunk_rows(n_chunks - 1 - t), :] = bwd_part(n_chunks - 1 - t)
        return carry

    def second_half(t, carry):
        finalize(t, acc_ref[chunk_rows(t), :] + fwd_part(t))
        u = n_chunks - 1 - t
        finalize(u, acc_ref[chunk_rows(u), :] + bwd_part(u))
        return carry

    lax.fori_loop(0, half, first_half, 0)
    lax.fori_loop(half, n_chunks, second_half, 0)


def _retention(proj, kv_ctx, log_gamma, batch, seq, ctx_len, off_q, off_k, off_v, off_g):
    n = proj.shape[0]
    dk, dv = RET_QK_DIM, RET_V_DIM
    assert seq % (2 * RET_CHUNK) == 0, "the paired scan needs an even number of chunks"
    grid_spec = pltpu.PrefetchScalarGridSpec(
        num_scalar_prefetch=1,
        grid=(batch, RET_HEADS),
        in_specs=[pl.BlockSpec((seq, dk), lambda b, h, lg: (b, off_q // dk + h)),
                  pl.BlockSpec((seq, dk), lambda b, h, lg: (b, off_k // dk + h)),
                  pl.BlockSpec((seq, dv), lambda b, h, lg: (b, off_v // dv + h)),
                  pl.BlockSpec((seq, dv), lambda b, h, lg: (b, off_g // dv + h)),
                  pl.BlockSpec((ctx_len, dk), lambda b, h, lg: (b, h)),
                  pl.BlockSpec((ctx_len, dv), lambda b, h, lg: (b, RET_QK_W // dv + h))],
        out_specs=pl.BlockSpec((seq, dv), lambda b, h, lg: (b, h)),
        scratch_shapes=[pltpu.VMEM((seq, dv), F32),
                        pltpu.VMEM((dk, dv), F32),
                        pltpu.VMEM((dk, dv), F32),
                        pltpu.VMEM((RET_CHUNK, RET_CHUNK), F32)],
    )
    return pl.pallas_call(
        functools.partial(_ret_kernel, seq=seq, ctx_len=ctx_len),
        out_shape=jax.ShapeDtypeStruct((n, RET_V_W), BF16),
        grid_spec=grid_spec,
        compiler_params=_params(("parallel", "arbitrary")),
        name="ret",
    )(log_gamma, proj, proj, proj, proj, kv_ctx, kv_ctx)


def _mix_kernel(yg_ref, cb_ref, cc_ref, ch_ref, gr_ref, gc_ref, x_ref, wro_ref, wco_ref, wo_ref,
                cw_ref, g1_ref, sh2_ref, sc2_ref, n2_ref, wr_ref, br_ref,
                x1_ref, h2_ref, ri_ref, rw_ref, cnt_ref, carry_ref):
    i = pl.program_id(0)
    tm = x_ref.shape[0]

    @pl.when(i == 0)
    def _():
        carry_ref[...] = jnp.zeros_like(carry_ref)

    ret_branch = jnp.dot(yg_ref[...], wro_ref[...], preferred_element_type=F32)

    p = cc_ref[...].astype(F32) * ch_ref[...].astype(F32)
    tpos = lax.broadcasted_iota(jnp.int32, (tm, 1), 0) % GRID_W
    prev = jnp.where(tpos != 0, pltpu.roll(p, 1, axis=0), 0.0)
    nxt = jnp.where(tpos != GRID_W - 1, pltpu.roll(p, tm - 1, axis=0), 0.0)
    cw = cw_ref[...]
    u = cw[0:1, :] * prev + cw[1:2, :] * p + cw[2:3, :] * nxt
    conv_in = (cb_ref[...].astype(F32) * u).astype(BF16)
    conv_branch = jnp.dot(conv_in, wco_ref[...], preferred_element_type=F32)

    merged = (jax.nn.sigmoid(gr_ref[...].astype(F32)) * ret_branch
              + jax.nn.sigmoid(gc_ref[...].astype(F32)) * conv_branch)
    mixed = jnp.dot(merged.astype(BF16), wo_ref[...], preferred_element_type=F32)
    x1 = x_ref[...] + g1_ref[0] * mixed
    x1_ref[...] = x1

    y = x1 * lax.rsqrt(jnp.mean(x1 * x1, axis=-1, keepdims=True) + NORM_EPS) * n2_ref[...]
    h2 = y * (1.0 + sc2_ref[0]) + sh2_ref[0]
    h2_ref[...] = h2

    h_hi = h2.astype(BF16)
    h_lo = (h2 - h_hi.astype(F32)).astype(BF16)
    nt_dims = (((1,), (1,)), ((), ()))
    hw = lax.dot_general(wr_ref[...], h_hi, nt_dims, preferred_element_type=F32)
    lw = lax.dot_general(wr_ref[:LANES, :], h_lo, nt_dims, preferred_element_type=F32)
    logits_t = hw[:LANES] + (hw[LANES:] + lw) + br_ref[...]
    neg = jnp.float32(-jnp.inf)
    big = jnp.int32(1 << 20)

    grp = logits_t[N_EXPERTS:N_EXPERTS + N_GROUPS]
    grow = lax.broadcasted_iota(jnp.int32, grp.shape, 0)
    gmax = jnp.max(grp, axis=0, keepdims=True)
    gidx = jnp.min(jnp.where(grp == gmax, grow, big), axis=0, keepdims=True)
    g_w = 1.0 / jnp.sum(jnp.exp(grp - gmax), axis=0, keepdims=True)

    el_all = logits_t[:N_EXPERTS]
    erow = lax.broadcasted_iota(jnp.int32, el_all.shape, 0)
    in_grp = (erow // EXPERTS_PER_GROUP) == gidx
    el = jnp.where(in_grp, el_all, neg)
    emax = jnp.max(el, axis=0, keepdims=True)
    i1 = jnp.min(jnp.where(el == emax, erow, big), axis=0, keepdims=True)
    el2 = jnp.where(erow == i1, neg, el)
    l2 = jnp.max(el2, axis=0, keepdims=True)
    i2 = jnp.min(jnp.where(el2 == l2, erow, big), axis=0, keepdims=True)
    esum = jnp.sum(jnp.where(in_grp, jnp.exp(el_all - emax), 0.0), axis=0, keepdims=True)
    p1 = 1.0 / esum
    p2 = jnp.exp(l2 - emax) / esum
    den = p1 + p2
    c1 = g_w * (p1 / den)
    c2 = g_w * (p2 / den)

    hot1 = erow == i1
    hot2 = erow == i2
    onehot = jnp.where(hot1, 1.0, jnp.where(hot2, 1.0, 0.0))
    earlier = (lax.broadcasted_iota(jnp.int32, (tm, tm), 0)
               < lax.broadcasted_iota(jnp.int32, (tm, tm), 1))
    ranks = jnp.dot(onehot.astype(BF16), jnp.where(earlier, 1.0, 0.0).astype(BF16),
                    preferred_element_type=F32) + carry_ref[...]
    r1 = jnp.sum(jnp.where(hot1, ranks, 0.0), axis=0, keepdims=True).astype(jnp.int32)
    r2 = jnp.sum(jnp.where(hot2, ranks, 0.0), axis=0, keepdims=True).astype(jnp.int32)
    carry_ref[...] = carry_ref[...] + jnp.sum(onehot, axis=1, keepdims=True)
    cnt_ref[...] = carry_ref[...]

    row8 = lax.broadcasted_iota(jnp.int32, (8, tm), 0)
    ri_ref[...] = jnp.where(row8 == 0, i1, jnp.where(row8 == 1, i2,
                            jnp.where(row8 == 2, r1, jnp.where(row8 == 3, r2, 0))))
    rw_ref[...] = jnp.where(row8 == 0, c1, jnp.where(row8 == 1, c2, 0.0))


def _mix(yg, proj, x2d, w_ret_o, w_conv_o, w_out, conv_w, gate1, shift2, scale2, norm2_w,
         w_router, b_router, seq, off_cb, tm):
    n, d = x2d.shape
    per_b = seq // tm
    cblk = off_cb // d
    row = lambda i: (i, 0)
    const2 = lambda i: (0, 0)
    bidx = lambda i: (i // per_b, 0, 0)
    col = lambda k: (lambda i: (i, cblk + k))
    return pl.pallas_call(
        _mix_kernel,
        out_shape=(jax.ShapeDtypeStruct((n, d), F32),
                   jax.ShapeDtypeStruct((n, d), F32),
                   jax.ShapeDtypeStruct((8, n), jnp.int32),
                   jax.ShapeDtypeStruct((8, n), F32),
                   jax.ShapeDtypeStruct((N_EXPERTS, 1), F32)),
        grid=(n // tm,),
        in_specs=[pl.BlockSpec((tm, RET_V_W), row),
                  pl.BlockSpec((tm, d), col(0)),
                  pl.BlockSpec((tm, d), col(1)),
                  pl.BlockSpec((tm, d), col(2)),
                  pl.BlockSpec((tm, d), col(3)),
                  pl.BlockSpec((tm, d), col(4)),
                  pl.BlockSpec((tm, d), row),
                  pl.BlockSpec((RET_V_W, d), const2, pipeline_mode=pl.Buffered(1)),
                  pl.BlockSpec((d, d), const2, pipeline_mode=pl.Buffered(1)),
                  pl.BlockSpec((d, d), const2, pipeline_mode=pl.Buffered(1)),
                  pl.BlockSpec((3, d), const2),
                  pl.BlockSpec((1, 1, d), bidx),
                  pl.BlockSpec((1, 1, d), bidx),
                  pl.BlockSpec((1, 1, d), bidx),
                  pl.BlockSpec((1, d), const2),
                  pl.BlockSpec((2 * LANES, d), const2),
                  pl.BlockSpec((LANES, 1), const2)],
        out_specs=(pl.BlockSpec((tm, d), row),
                   pl.BlockSpec((tm, d), row),
                   pl.BlockSpec((8, tm), lambda i: (0, i)),
                   pl.BlockSpec((8, tm), lambda i: (0, i)),
                   pl.BlockSpec((N_EXPERTS, 1), const2)),
        scratch_shapes=[pltpu.VMEM((N_EXPERTS, 1), F32)],
        compiler_params=_params(("arbitrary",)),
        name="mix",
    )(yg, proj, proj, proj, proj, proj, x2d, w_ret_o, w_conv_o, w_out, conv_w,
      gate1, shift2, scale2, norm2_w.reshape(1, d), w_router, b_router)


def _row_copy(src_ref, src_row, dst_ref, dst_row, sem):
    return pltpu.make_async_copy(src_ref.at[pl.ds(src_row, 1), :],
                                 dst_ref.at[pl.ds(dst_row, 1), :], sem)


SCATTER_SLOTS = 3


def _scatter_kernel(pos_ref, zt_ref, h_ref, xs_ref, hbuf_ref, zero_ref, lsem, ssem, zsem):
    i = pl.program_id(0)
    n_steps = pl.num_programs(0)
    tm = hbuf_ref.shape[1]
    te = zero_ref.shape[0]
    base = i * tm
    n_tok = pos_ref.shape[0] // 2

    def load(tile, slot):
        src = h_ref.at[pl.ds(pl.multiple_of(tile * tm, tm), tm), :]
        return pltpu.make_async_copy(src, hbuf_ref.at[slot], lsem.at[slot])

    def wait_rows(slot):
        for _ in range(2):
            pltpu.make_async_copy(hbuf_ref.at[slot], xs_ref.at[pl.ds(0, tm), :], ssem.at[slot]).wait()

    @pl.when(i == 0)
    def _():
        load(0, 0).start()
        load(1, 1).start()
        zero_ref[...] = jnp.zeros_like(zero_ref)

        def tile_copy(z):
            row = pl.multiple_of(zt_ref[z] * te, te)
            return pltpu.make_async_copy(zero_ref, xs_ref.at[pl.ds(row, te), :], zsem)

        def zissue(z, carry):
            @pl.when(zt_ref[z] >= 0)
            def _():
                tile_copy(z).start()
            return carry

        def zdrain(z, carry):
            @pl.when(zt_ref[z] >= 0)
            def _():
                tile_copy(z).wait()
            return carry

        lax.fori_loop(0, zt_ref.shape[0], zissue, 0)
        lax.fori_loop(0, zt_ref.shape[0], zdrain, 0)

    slot = lax.rem(i, SCATTER_SLOTS)
    reload_slot = lax.rem(i + 2, SCATTER_SLOTS)
    load(i, slot).wait()
    src = hbuf_ref.at[slot]
    for r in range(tm):
        _row_copy(src, r, xs_ref, pos_ref[base + r], ssem.at[slot]).start(priority=0)
        _row_copy(src, r, xs_ref, pos_ref[n_tok + base + r], ssem.at[slot]).start(priority=1)

    @pl.when(i > 0)
    def _():
        wait_rows(reload_slot)

    load(jnp.minimum(i + 2, n_steps - 1), reload_slot).start()

    @pl.when(i == n_steps - 1)
    def _():
        wait_rows(slot)
        load(i, lax.rem(i + 1, SCATTER_SLOTS)).wait()
        load(i, reload_slot).wait()


def _scatter_rows(pos_flat, zero_tiles, h2, n_sorted, tm):
    n, d = h2.shape
    assert n // tm >= SCATTER_SLOTS
    grid_spec = pltpu.PrefetchScalarGridSpec(
        num_scalar_prefetch=2,
        grid=(n // tm,),
        in_specs=[pl.BlockSpec(memory_space=pl.ANY)],
        out_specs=pl.BlockSpec(memory_space=pl.ANY),
        scratch_shapes=[pltpu.VMEM((SCATTER_SLOTS, tm, d), F32),
                        pltpu.VMEM((EXPERT_TILE, d), F32),
                        pltpu.SemaphoreType.DMA((SCATTER_SLOTS,)),
                        pltpu.SemaphoreType.DMA((SCATTER_SLOTS,)),
                        pltpu.SemaphoreType.DMA],
    )
    return pl.pallas_call(
        _scatter_kernel,
        out_shape=jax.ShapeDtypeStruct((n_sorted, d), F32),
        grid_spec=grid_spec,
        compiler_params=_params(("arbitrary",)),
        name="scatter",
    )(pos_flat, zero_tiles, h2)


ROW_TILE_SLOTS = 3


def _expert_kernel(tf_ref, te_ref, tn_ref, ts_ref, xs_ref, w13_ref, w2_ref, ys_ref,
                   xbuf_ref, w13f_ref, w2f_ref, w13b_ref, w2b_ref, xsem, sem13, sem2):
    t = pl.program_id(0)
    nt = pl.num_programs(0)
    rows = xbuf_ref.shape[1]
    flag = tf_ref[t]

    def tile_copy(tile, slot):
        src = xs_ref.at[pl.ds(pl.multiple_of(tile * rows, rows), rows), :]
        return pltpu.make_async_copy(src, xbuf_ref.at[slot], xsem.at[slot])

    def weight_copies(e, slot):
        return (pltpu.make_async_copy(w13_ref.at[e], w13f_ref.at[slot], sem13.at[slot]),
                pltpu.make_async_copy(w2_ref.at[e], w2f_ref.at[slot], sem2.at[slot]))

    @pl.when(t == 0)
    def _():
        for cp in weight_copies(te_ref[0], ts_ref[0]):
            cp.start()
        tile_copy(0, 0).start()
        tile_copy(1, 1).start()

    slot = lax.rem(t, ROW_TILE_SLOTS)
    ahead_slot = lax.rem(t + 2, ROW_TILE_SLOTS)
    tile_copy(t, slot).wait()
    tile_copy(jnp.minimum(t + 2, nt - 1), ahead_slot).start()

    @pl.when(flag == 2)
    def _():
        slot = ts_ref[t]
        for cp in weight_copies(te_ref[t], slot):
            cp.wait()
        w13b_ref[...] = w13f_ref[slot].astype(BF16)
        w2b_ref[...] = w2f_ref[slot].astype(BF16)

        @pl.when(tn_ref[t] >= 0)
        def _():
            for cp in weight_copies(tn_ref[t], 1 - slot):
                cp.start()

    @pl.when(flag > 0)
    def _():
        x = xbuf_ref[slot].astype(BF16)
        hid = jnp.dot(x, w13b_ref[...], preferred_element_type=F32)
        a = hid[:, :EXPERT_HIDDEN]
        act = a * jax.nn.sigmoid(a) * hid[:, EXPERT_HIDDEN:]
        ys_ref[...] = jnp.dot(act.astype(BF16), w2b_ref[...], preferred_element_type=F32)

    @pl.when(flag == 0)
    def _():
        ys_ref[...] = jnp.zeros_like(ys_ref)

    @pl.when(t == nt - 1)
    def _():
        tile_copy(t, lax.rem(t + 1, ROW_TILE_SLOTS)).wait()
        tile_copy(t, ahead_slot).wait()


def _experts(tile_flag, tile_expert, tile_next, tile_slot, xs, w13, w2):
    n_sorted, d = xs.shape
    te = EXPERT_TILE
    n_tiles = n_sorted // te
    assert n_tiles >= ROW_TILE_SLOTS
    hid2 = w13.shape[2]
    grid_spec = pltpu.PrefetchScalarGridSpec(
        num_scalar_prefetch=4,
        grid=(n_tiles,),
        in_specs=[pl.BlockSpec(memory_space=pl.ANY),
                  pl.BlockSpec(memory_space=pl.ANY),
                  pl.BlockSpec(memory_space=pl.ANY)],
        out_specs=pl.BlockSpec((te, d), lambda t, *_: (t, 0)),
        scratch_shapes=[pltpu.VMEM((ROW_TILE_SLOTS, te, d), F32),
                        pltpu.VMEM((2, d, hid2), F32),
                        pltpu.VMEM((2, hid2 // 2, d), F32),
                        pltpu.VMEM((d, hid2), BF16),
                        pltpu.VMEM((hid2 // 2, d), BF16),
                        pltpu.SemaphoreType.DMA((ROW_TILE_SLOTS,)),
                        pltpu.SemaphoreType.DMA((2,)),
                        pltpu.SemaphoreType.DMA((2,))],
    )
    return pl.pallas_call(
        _expert_kernel,
        out_shape=jax.ShapeDtypeStruct((n_sorted, d), F32),
        grid_spec=grid_spec,
        compiler_params=_params(("arbitrary",)),
        name="experts",
    )(tile_flag, tile_expert, tile_next, tile_slot, xs, w13, w2)


GATHER_SLOTS = 3


def _combine_kernel(pos_ref, ys_ref, x1_ref, rw_ref, g2_ref, fw_ref, o_ref, ya_ref, yb_ref, sem):
    i = pl.program_id(0)
    n_steps = pl.num_programs(0)
    tm = ya_ref.shape[1]
    n_tok = pos_ref.shape[0] // 2

    def issue(tile, slot):
        base = tile * tm
        dst_a = ya_ref.at[slot]
        dst_b = yb_ref.at[slot]
        for r in range(tm):
            _row_copy(ys_ref, pos_ref[base + r], dst_a, r, sem.at[slot]).start(priority=0)
            _row_copy(ys_ref, pos_ref[n_tok + base + r], dst_b, r, sem.at[slot]).start(priority=1)

    def wait(slot):
        pltpu.make_async_copy(ys_ref.at[pl.ds(0, tm), :], ya_ref.at[slot], sem.at[slot]).wait()
        pltpu.make_async_copy(ys_ref.at[pl.ds(0, tm), :], yb_ref.at[slot], sem.at[slot]).wait()

    @pl.when(i == 0)
    def _():
        issue(0, 0)
        issue(1, 1)

    slot = lax.rem(i, GATHER_SLOTS)
    ahead_slot = lax.rem(i + 2, GATHER_SLOTS)
    wait(slot)
    issue(jnp.minimum(i + 2, n_steps - 1), ahead_slot)
    rw = rw_ref[...]
    moe = rw[:, 0:1] * ya_ref[slot] + rw[:, 1:2] * yb_ref[slot]
    x2 = x1_ref[...] + g2_ref[0] * moe
    o_ref[...] = x2 * lax.rsqrt(jnp.mean(x2 * x2, axis=-1, keepdims=True) + NORM_EPS) * fw_ref[...]

    @pl.when(i == n_steps - 1)
    def _():
        wait(lax.rem(i + 1, GATHER_SLOTS))
        wait(ahead_slot)


def _combine(pos_flat, ys, x1, route_w, gate2, final_w, seq, tm):
    n, d = x1.shape
    per_b = seq // tm
    assert n // tm >= GATHER_SLOTS
    grid_spec = pltpu.PrefetchScalarGridSpec(
        num_scalar_prefetch=1,
        grid=(n // tm,),
        in_specs=[pl.BlockSpec(memory_space=pl.ANY),
                  pl.BlockSpec((tm, d), lambda i, pos: (i, 0)),
                  pl.BlockSpec((tm, 2), lambda i, pos: (i, 0)),
                  pl.BlockSpec((1, 1, d), lambda i, pos: (i // per_b, 0, 0)),
                  pl.BlockSpec((1, d), lambda i, pos: (0, 0))],
        out_specs=pl.BlockSpec((tm, d), lambda i, pos: (i, 0)),
        scratch_shapes=[pltpu.VMEM((GATHER_SLOTS, tm, d), F32),
                        pltpu.VMEM((GATHER_SLOTS, tm, d), F32),
                        pltpu.SemaphoreType.DMA((GATHER_SLOTS,))],
    )
    return pl.pallas_call(
        _combine_kernel,
        out_shape=jax.ShapeDtypeStruct((n, d), F32),
        grid_spec=grid_spec,
        compiler_params=_params(("arbitrary",)),
        name="combine",
    )(pos_flat, ys, x1, route_w, gate2, final_w.reshape(1, d))


def _rope_tables(seq):
    pos = np.arange(seq)
    nf = RET_QK_DIM // 4
    inv = (np.float32(ROPE_BASE) ** (-np.arange(nf, dtype=np.float32) / np.float32(nf))).astype(np.float32)
    ang_r = ((pos // GRID_W).astype(np.float32)[:, None] * inv[None, :]).astype(np.float64)
    ang_c = ((pos % GRID_W).astype(np.float32)[:, None] * inv[None, :]).astype(np.float64)
    return (jnp.asarray(np.concatenate([np.cos(ang_r), np.cos(ang_c)], axis=1), F32),
            jnp.asarray(np.concatenate([np.sin(ang_r), np.sin(ang_c)], axis=1), F32))


def _weight_prep_kernel(w_ref, o_ref, octx_ref, *, n_rot, ctx_cols):
    half = LANES // 2
    first = lax.broadcasted_iota(jnp.int32, (w_ref.shape[0], LANES), 1) < half
    for c0 in range(0, n_rot, RET_QK_DIM):
        a = w_ref[:, c0:c0 + LANES]
        b = w_ref[:, c0 + LANES:c0 + 2 * LANES]
        o_ref[:, c0:c0 + LANES] = jnp.where(first, a, pltpu.roll(b, half, axis=1)).astype(o_ref.dtype)
        o_ref[:, c0 + LANES:c0 + 2 * LANES] = jnp.where(first, pltpu.roll(a, half, axis=1), b).astype(o_ref.dtype)
    o_ref[:, n_rot:] = w_ref[:, n_rot:].astype(o_ref.dtype)
    octx_ref[...] = o_ref[:, ctx_cols[0]:ctx_cols[1]]


def _weight_prep(w_in, n_rot, ctx_cols):
    d, width = w_in.shape
    tr = 128
    ctx_width = ctx_cols[1] - ctx_cols[0]
    return pl.pallas_call(
        functools.partial(_weight_prep_kernel, n_rot=n_rot, ctx_cols=ctx_cols),
        out_shape=(jax.ShapeDtypeStruct((d, width), BF16),
                   jax.ShapeDtypeStruct((d, ctx_width), BF16)),
        grid=(d // tr,),
        in_specs=[pl.BlockSpec((tr, width), lambda i: (i, 0))],
        out_specs=(pl.BlockSpec((tr, width), lambda i: (i, 0)),
                   pl.BlockSpec((tr, ctx_width), lambda i: (i, 0))),
        compiler_params=_params(("parallel",)),
        name="wprep",
    )(w_in)


def _layer(x, c, ctx, c_ctx, w_mod, b_mod, norm1_w, norm2_w, w_in, log_decay, gn_w, w_ret_o,
           conv_w, w_conv_o, w_out, rg_w, rg_b, re_w, re_b, w13, w2, final_w):
    batch, seq, d = x.shape
    ctx_len = ctx.shape[1]
    n = batch * seq
    off_q = 0
    off_k = off_q + RET_QK_W
    off_v = off_k + RET_QK_W
    off_g = off_v + RET_V_W
    off_cb = off_g + RET_V_W

    pad = (-(batch + 1)) % 8
    c_all = jnp.concatenate([c, c_ctx[None, :], jnp.zeros((pad, d), F32)], axis=0)
    mod = _modulation(c_all, w_mod, b_mod)
    mx = mod[:batch].reshape(batch, 1, 6, d)
    shift1, scale1, gate1, shift2, scale2, gate2 = (mx[:, :, k, :] for k in range(6))
    mc = mod[batch].reshape(1, 1, 6, d)
    shift1c, scale1c = mc[:, :, 0, :], mc[:, :, 1, :]

    w_in_b, w_ctx_b = _weight_prep(w_in, off_v, (off_k, off_g))
    cos_t, sin_t = _rope_tables(seq)
    k_scale = RET_QK_DIM ** -0.5

    tm_proj = min(512, seq)
    main_modes = (("rope", 1.0), ("rope", k_scale), ("plain", None), ("plain", None),
                  ("swish_gain", 0), ("swish_gain", 1), ("plain", None), ("plain", None), ("plain", None),
                  ("plain", None), ("plain", None))
    gain = gn_w.reshape(1, RET_V_W)
    proj = _projection(x.reshape(n, d), norm1_w, shift1, scale1, cos_t, sin_t, gain, w_in_b, seq,
                       modes=main_modes, tm=tm_proj)
    tm_ctx = batch * ctx_len
    no_rope = jnp.zeros((tm_ctx, LANES), F32)
    kv_ctx = _projection(ctx.reshape(tm_ctx, d), norm1_w, shift1c, scale1c, no_rope, no_rope, gain,
                         w_ctx_b, tm_ctx,
                         modes=(("scale", k_scale), ("plain", None), ("plain", None)), tm=min(512, tm_ctx))

    log_gamma = jnp.log1p(-jnp.exp(log_decay.astype(F32)))
    yg = _retention(proj, kv_ctx, log_gamma, batch, seq, ctx_len, off_q, off_k, off_v, off_g)

    w_router = jnp.concatenate(
        [re_w, rg_w, jnp.zeros((d, LANES - N_EXPERTS - N_GROUPS), F32)], axis=1)
    b_router = jnp.concatenate(
        [re_b, rg_b, jnp.zeros((LANES - N_EXPERTS - N_GROUPS,), F32)]).reshape(LANES, 1)
    w_router_hi = w_router.astype(BF16)
    w_router_lo = (w_router - w_router_hi.astype(F32)).astype(BF16)
    w_router = jnp.concatenate([w_router_hi, w_router_lo], axis=1).T
    tm_mix = min(512, seq)
    x1, h2, route_i, route_w, counts = _mix(
        yg, proj, x.reshape(n, d), w_ret_o.astype(BF16), w_conv_o.astype(BF16), w_out.astype(BF16),
        conv_w, gate1, shift2, scale2, norm2_w, w_router, b_router, seq, off_cb, tm_mix)

    te = EXPERT_TILE
    cnt = counts[:, 0].astype(jnp.int32)
    padded = ((cnt + te - 1) // te) * te
    ends = jnp.cumsum(padded)
    starts = ends - padded
    expert_ids = jnp.arange(N_EXPERTS, dtype=jnp.int32)
    route_t = route_i[0:4]
    seg_start = jnp.sum(jnp.where(route_t[0:2, :, None] == expert_ids, starts, 0), axis=-1)
    pos_flat = (seg_start + route_t[2:4]).reshape(-1).astype(jnp.int32)
    n_tiles = (2 * n) // te + N_EXPERTS
    tile_row = jnp.arange(n_tiles, dtype=jnp.int32) * te
    tile_expert = jnp.minimum(jnp.sum((tile_row[:, None] >= ends[None, :]).astype(jnp.int32), axis=1),
                              N_EXPERTS - 1).astype(jnp.int32)
    new_expert = jnp.concatenate([jnp.ones((1,), bool), tile_expert[1:] != tile_expert[:-1]])
    tile_flag = jnp.where(tile_row < ends[-1], jnp.where(new_expert, 2, 1), 0).astype(jnp.int32)

    valid_tiles = ends[-1] // te
    last_tile = jnp.where(padded > 0, ends // te - 1, -1)
    tail_tile = valid_tiles + jnp.arange(N_EXPERTS, dtype=jnp.int32)
    tail_tile = jnp.where(tail_tile < n_tiles, tail_tile, -1)
    zero_tiles = jnp.concatenate([last_tile, tail_tile]).astype(jnp.int32)

    xs = _scatter_rows(pos_flat, zero_tiles, h2, n_tiles * te, min(512, seq // 2))
    nonempty = padded > 0
    later = nonempty[None, :] & (expert_ids[None, :] > expert_ids[:, None])
    next_expert = jnp.min(jnp.where(later, expert_ids[None, :], N_EXPERTS), axis=1)
    next_expert = jnp.where(next_expert == N_EXPERTS, -1, next_expert)
    slot = (jnp.cumsum(nonempty.astype(jnp.int32)) - 1) % 2
    tile_hot = tile_expert[:, None] == expert_ids[None, :]
    tile_next = jnp.sum(jnp.where(tile_hot, next_expert, 0), axis=1).astype(jnp.int32)
    tile_slot = jnp.sum(jnp.where(tile_hot, slot, 0), axis=1).astype(jnp.int32)

    ys = _experts(tile_flag, tile_expert, tile_next, tile_slot, xs, w13, w2)
    out = _combine(pos_flat, ys, x1, route_w[0:2].T, gate2, final_w, seq, min(256, seq // 2))
    return out.reshape(batch, seq, d)


def kernel(x, c, ctx, c_ctx, w_mod, b_mod, norm1_w, norm2_w, w_in, ret_log_decay, ret_gn_w, w_ret_o,
           conv_w, w_conv_o, w_out, router_group_w, router_group_b, router_expert_w, router_expert_b,
           expert_w13, expert_w2, final_norm_w):
    assert w_mod.shape[0] == 1, "single-layer problem"
    return _layer(x, c, ctx, c_ctx, w_mod[0], b_mod[0], norm1_w[0], norm2_w[0], w_in[0],
                  ret_log_decay[0], ret_gn_w[0], w_ret_o[0], conv_w[0], w_conv_o[0], w_out[0],
                  router_group_w[0], router_group_b[0], router_expert_w[0], router_expert_b[0],
                  expert_w13[0], expert_w2[0], final_norm_w)
```

```python
import functools

import jax
import jax.numpy as jnp
import numpy as np
from jax import lax
from jax.experimental import pallas as pl
from jax.experimental.pallas import tpu as pltpu

GRID_W = 64
RET_HEADS = 4
RET_QK_DIM = 256
RET_V_DIM = 512
RET_QK_W = RET_HEADS * RET_QK_DIM
RET_V_W = RET_HEADS * RET_V_DIM
N_GROUPS = 4
EXPERTS_PER_GROUP = 8
N_EXPERTS = N_GROUPS * EXPERTS_PER_GROUP
EXPERT_HIDDEN = 512
ROPE_BASE = 10000.0
NORM_EPS = 1e-6
GN_EPS = 1e-5

RET_CHUNK = 256
LANES = 128
EXPERT_TILE = 256
VMEM_LIMIT = 56 * 1024 * 1024

BF16 = jnp.bfloat16
F32 = jnp.float32


def _params(sem, vmem=VMEM_LIMIT):
    return pltpu.CompilerParams(dimension_semantics=sem, vmem_limit_bytes=vmem)


def _mod_kernel(c_ref, w_ref, b_ref, o_ref):
    c = c_ref[...]
    s = c * jax.nn.sigmoid(c)
    o_ref[...] = jnp.dot(s.astype(BF16), w_ref[...].astype(BF16),
                         preferred_element_type=F32) + b_ref[...]


def _modulation(c_all, w_mod, b_mod):
    rows, d = c_all.shape
    n_out = w_mod.shape[1]
    tn = 3072
    return pl.pallas_call(
        _mod_kernel,
        out_shape=jax.ShapeDtypeStruct((rows, n_out), F32),
        grid=(n_out // tn,),
        in_specs=[pl.BlockSpec((rows, d), lambda j: (0, 0)),
                  pl.BlockSpec((d, tn), lambda j: (0, j)),
                  pl.BlockSpec((1, tn), lambda j: (0, j))],
        out_specs=pl.BlockSpec((rows, tn), lambda j: (0, j)),
        compiler_params=_params(("parallel",)),
        name="mod",
    )(c_all, w_mod, b_mod.reshape(1, n_out))


def _rope_store(acc, cos, sin, scale, o_ref):
    for hd in range(acc.shape[1] // RET_QK_DIM):
        lo = hd * RET_QK_DIM
        u1 = acc[:, lo:lo + LANES]
        u2 = acc[:, lo + LANES:lo + 2 * LANES]
        r1 = u1 * cos - u2 * sin
        r2 = u1 * sin + u2 * cos
        if scale != 1.0:
            r1 = r1 * scale
            r2 = r2 * scale
        o_ref[:, lo:lo + LANES] = r1.astype(o_ref.dtype)
        o_ref[:, lo + LANES:lo + 2 * LANES] = r2.astype(o_ref.dtype)


def _proj_kernel(x_ref, nw_ref, sh_ref, sc_ref, cos_ref, sin_ref, gn_ref, w_ref, o_ref, *, modes, tn):
    x = x_ref[...]
    y = x * lax.rsqrt(jnp.mean(x * x, axis=-1, keepdims=True) + NORM_EPS) * nw_ref[...]
    h = (y * (1.0 + sc_ref[0]) + sh_ref[0]).astype(BF16)
    assert len(modes) * tn == w_ref.shape[1]
    for j, (kind, arg) in enumerate(modes):
        cols = slice(j * tn, (j + 1) * tn)
        acc = jnp.dot(h, w_ref[:, cols], preferred_element_type=F32)
        if kind == "rope":
            _rope_store(acc, cos_ref[...], sin_ref[...], arg, o_ref.at[:, cols])
        elif kind == "scale":
            o_ref[:, cols] = (acc * arg).astype(o_ref.dtype)
        elif kind == "swish_gain":
            gain = gn_ref[:, arg * tn:(arg + 1) * tn]
            o_ref[:, cols] = (acc * jax.nn.sigmoid(acc) * gain).astype(o_ref.dtype)
        else:
            o_ref[:, cols] = acc.astype(o_ref.dtype)


def _projection(x2d, norm_w, shift, scale, cos_t, sin_t, gain, w_bf16, seq, modes, tm):
    n, d = x2d.shape
    width = w_bf16.shape[1]
    tn = 1024
    per_b = seq // tm
    nb = shift.shape[0]
    bidx = (lambda i: (i // per_b, 0, 0)) if nb > 1 else (lambda i: (0, 0, 0))
    return pl.pallas_call(
        functools.partial(_proj_kernel, modes=modes, tn=tn),
        out_shape=jax.ShapeDtypeStruct((n, width), BF16),
        grid=(n // tm,),
        in_specs=[pl.BlockSpec((tm, d), lambda i: (i, 0)),
                  pl.BlockSpec((1, d), lambda i: (0, 0)),
                  pl.BlockSpec((1, 1, d), bidx),
                  pl.BlockSpec((1, 1, d), bidx),
                  pl.BlockSpec((tm, LANES), lambda i: (i % per_b, 0)),
                  pl.BlockSpec((tm, LANES), lambda i: (i % per_b, 0)),
                  pl.BlockSpec(gain.shape, lambda i: (0, 0)),
                  pl.BlockSpec((d, width), lambda i: (0, 0), pipeline_mode=pl.Buffered(1))],
        out_specs=pl.BlockSpec((tm, width), lambda i: (i, 0)),
        compiler_params=_params(("parallel",)),
        name="proj",
    )(x2d, norm_w.reshape(1, d), shift, scale, cos_t, sin_t, gain, w_bf16)


def _dot_t0(a, b):
    return lax.dot_general(a, b, (((0,), (0,)), ((), ())), preferred_element_type=F32)


def _ret_kernel(lg_ref, q_ref, k_ref, v_ref, g_ref, kc_ref, vc_ref, o_ref,
                acc_ref, sf_ref, sb_ref, dm_ref, *, seq, ctx_len):
    h = pl.program_id(1)
    lgf = lg_ref[0, h]
    lgb = lg_ref[1, h]
    c = RET_CHUNK
    n_chunks = seq // c

    cpos = lax.broadcasted_iota(jnp.int32, (ctx_len, 1), 0).astype(F32)
    kc = kc_ref[...].astype(F32)
    vc = vc_ref[...]
    sf_ref[...] = _dot_t0((kc * jnp.exp((ctx_len - 1.0 - cpos) * lgf)).astype(BF16), vc)
    sb_ref[...] = _dot_t0((kc * jnp.exp(cpos * lgb)).astype(BF16), vc)

    ri = lax.broadcasted_iota(jnp.int32, (c, c), 0)
    ci = lax.broadcasted_iota(jnp.int32, (c, c), 1)
    dist = (ri - ci).astype(F32)
    dm_ref[...] = jnp.exp(jnp.abs(dist) * jnp.where(dist >= 0, lgf, lgb))

    idx = lax.broadcasted_iota(jnp.int32, (c, 1), 0).astype(F32)
    xi_f = jnp.exp((idx + 1.0) * lgf)
    zeta_f = jnp.exp((c - 1.0 - idx) * lgf)
    xi_b = jnp.exp((c - idx) * lgb)
    zeta_b = jnp.exp(idx * lgb)
    cd_f = jnp.exp(jnp.full((1, 1), float(c), F32) * lgf)
    cd_b = jnp.exp(jnp.full((1, 1), float(c), F32) * lgb)

    def chunk_rows(ic):
        return pl.ds(pl.multiple_of(ic * c, c), c)

    def fwd_part(ic):
        rows = chunk_rows(ic)
        q = q_ref[rows, :]
        k = k_ref[rows, :]
        v = v_ref[rows, :]
        s = lax.dot_general(q, k, (((1,), (1,)), ((), ())), preferred_element_type=F32)
        intra = jnp.dot((s * dm_ref[...]).astype(BF16), v, preferred_element_type=F32)
        inter = jnp.dot(q, sf_ref[...].astype(BF16), preferred_element_type=F32)
        kz = (k.astype(F32) * zeta_f).astype(BF16)
        sf_ref[...] = cd_f * sf_ref[...] + _dot_t0(kz, v)
        return intra + xi_f * inter

    def bwd_part(ic):
        rows = chunk_rows(ic)
        q = q_ref[rows, :]
        k = k_ref[rows, :]
        v = v_ref[rows, :]
        inter = jnp.dot(q, sb_ref[...].astype(BF16), preferred_element_type=F32)
        kz = (k.astype(F32) * zeta_b).astype(BF16)
        sb_ref[...] = cd_b * sb_ref[...] + _dot_t0(kz, v)
        return xi_b * inter

    def finalize(ic, y):
        rows = chunk_rows(ic)
        mu = jnp.mean(y, axis=-1, keepdims=True)
        yc = y - mu
        var = jnp.mean(yc * yc, axis=-1, keepdims=True)
        o_ref[rows, :] = (g_ref[rows, :].astype(F32) * (yc * lax.rsqrt(var + GN_EPS))).astype(o_ref.dtype)

    half = n_chunks // 2

    def first_half(t, carry):
        acc_ref[chunk_rows(t), :] = fwd_part(t)
        acc_ref[chunk_rows(n_chunks - 1 - t), :] = bwd_part(n_chunks - 1 - t)
        return carry

    def second_half(t, carry):
        finalize(t, acc_ref[chunk_rows(t), :] + fwd_part(t))
        u = n_chunks - 1 - t
        finalize(u, acc_ref[chunk_rows(u), :] + bwd_part(u))
        return carry

    lax.fori_loop(0, half, first_half, 0)
    lax.fori_loop(half, n_chunks, second_half, 0)


def _retention(proj, kv_ctx, log_gamma, batch, seq, ctx_len, off_q, off_k, off_v, off_g):
    n = proj.shape[0]
    dk, dv = RET_QK_DIM, RET_V_DIM
    assert seq % (2 * RET_CHUNK) == 0, "the paired scan needs an even number of chunks"
    grid_spec = pltpu.PrefetchScalarGridSpec(
        num_scalar_prefetch=1,
        grid=(batch, RET_HEADS),
        in_specs=[pl.BlockSpec((seq, dk), lambda b, h, lg: (b, off_q // dk + h)),
                  pl.BlockSpec((seq, dk), lambda b, h, lg: (b, off_k // dk + h)),
                  pl.BlockSpec((seq, dv), lambda b, h, lg: (b, off_v // dv + h)),
                  pl.BlockSpec((seq, dv), lambda b, h, lg: (b, off_g // dv + h)),
                  pl.BlockSpec((ctx_len, dk), lambda b, h, lg: (b, h)),
                  pl.BlockSpec((ctx_len, dv), lambda b, h, lg: (b, RET_QK_W // dv + h))],
        out_specs=pl.BlockSpec((seq, dv), lambda b, h, lg: (b, h)),
        scratch_shapes=[pltpu.VMEM((seq, dv), F32),
                        pltpu.VMEM((dk, dv), F32),
                        pltpu.VMEM((dk, dv), F32),
                        pltpu.VMEM((RET_CHUNK, RET_CHUNK), F32)],
    )
    return pl.pallas_call(
        functools.partial(_ret_kernel, seq=seq, ctx_len=ctx_len),
        out_shape=jax.ShapeDtypeStruct((n, RET_V_W), BF16),
        grid_spec=grid_spec,
        compiler_params=_params(("parallel", "arbitrary")),
        name="ret",
    )(log_gamma, proj, proj, proj, proj, kv_ctx, kv_ctx)


def _mix_kernel(yg_ref, cb_ref, cc_ref, ch_ref, gr_ref, gc_ref, x_ref, wro_ref, wco_ref, wo_ref,
                cw_ref, g1_ref, sh2_ref, sc2_ref, n2_ref, wr_ref, br_ref,
                x1_ref, h2_ref, ri_ref, rw_ref, cnt_ref, carry_ref):
    i = pl.program_id(0)
    tm = x_ref.shape[0]

    @pl.when(i == 0)
    def _():
        carry_ref[...] = jnp.zeros_like(carry_ref)

    ret_branch = jnp.dot(yg_ref[...], wro_ref[...], preferred_element_type=F32)

    p = cc_ref[...].astype(F32) * ch_ref[...].astype(F32)
    tpos = lax.broadcasted_iota(jnp.int32, (tm, 1), 0) % GRID_W
    prev = jnp.where(tpos != 0, pltpu.roll(p, 1, axis=0), 0.0)
    nxt = jnp.where(tpos != GRID_W - 1, pltpu.roll(p, tm - 1, axis=0), 0.0)
    cw = cw_ref[...]
    u = cw[0:1, :] * prev + cw[1:2, :] * p + cw[2:3, :] * nxt
    conv_in = (cb_ref[...].astype(F32) * u).astype(BF16)
    conv_branch = jnp.dot(conv_in, wco_ref[...], preferred_element_type=F32)

    merged = (jax.nn.sigmoid(gr_ref[...].astype(F32)) * ret_branch
              + jax.nn.sigmoid(gc_ref[...].astype(F32)) * conv_branch)
    mixed = jnp.dot(merged.astype(BF16), wo_ref[...], preferred_element_type=F32)
    x1 = x_ref[...] + g1_ref[0] * mixed
    x1_ref[...] = x1

    y = x1 * lax.rsqrt(jnp.mean(x1 * x1, axis=-1, keepdims=True) + NORM_EPS) * n2_ref[...]
    h2 = y * (1.0 + sc2_ref[0]) + sh2_ref[0]
    h2_ref[...] = h2

    h_hi = h2.astype(BF16)
    h_lo = (h2 - h_hi.astype(F32)).astype(BF16)
    nt_dims = (((1,), (1,)), ((), ()))
    hw = lax.dot_general(wr_ref[...], h_hi, nt_dims, preferred_element_type=F32)
    lw = lax.dot_general(wr_ref[:LANES, :], h_lo, nt_dims, preferred_element_type=F32)
    logits_t = hw[:LANES] + (hw[LANES:] + lw) + br_ref[...]
    neg = jnp.float32(-jnp.inf)
    big = jnp.int32(1 << 20)

    grp = logits_t[N_EXPERTS:N_EXPERTS + N_GROUPS]
    grow = lax.broadcasted_iota(jnp.int32, grp.shape, 0)
    gmax = jnp.max(grp, axis=0, keepdims=True)
    gidx = jnp.min(jnp.where(grp == gmax, grow, big), axis=0, keepdims=True)
    g_w = 1.0 / jnp.sum(jnp.exp(grp - gmax), axis=0, keepdims=True)

    el_all = logits_t[:N_EXPERTS]
    erow = lax.broadcasted_iota(jnp.int32, el_all.shape, 0)
    in_grp = (erow // EXPERTS_PER_GROUP) == gidx
    el = jnp.where(in_grp, el_all, neg)
    emax = jnp.max(el, axis=0, keepdims=True)
    i1 = jnp.min(jnp.where(el == emax, erow, big), axis=0, keepdims=True)
    el2 = jnp.where(erow == i1, neg, el)
    l2 = jnp.max(el2, axis=0, keepdims=True)
    i2 = jnp.min(jnp.where(el2 == l2, erow, big), axis=0, keepdims=True)
    esum = jnp.sum(jnp.where(in_grp, jnp.exp(el_all - emax), 0.0), axis=0, keepdims=True)
    p1 = 1.0 / esum
    p2 = jnp.exp(l2 - emax) / esum
    den = p1 + p2
    c1 = g_w * (p1 / den)
    c2 = g_w * (p2 / den)

    hot1 = erow == i1
    hot2 = erow == i2
    onehot = jnp.where(hot1, 1.0, jnp.where(hot2, 1.0, 0.0))
    earlier = (lax.broadcasted_iota(jnp.int32, (tm, tm), 0)
               < lax.broadcasted_iota(jnp.int32, (tm, tm), 1))
    ranks = jnp.dot(onehot.astype(BF16), jnp.where(earlier, 1.0, 0.0).astype(BF16),
                    preferred_element_type=F32) + carry_ref[...]
    r1 = jnp.sum(jnp.where(hot1, ranks, 0.0), axis=0, keepdims=True).astype(jnp.int32)
    r2 = jnp.sum(jnp.where(hot2, ranks, 0.0), axis=0, keepdims=True).astype(jnp.int32)
    carry_ref[...] = carry_ref[...] + jnp.sum(onehot, axis=1, keepdims=True)
    cnt_ref[...] = carry_ref[...]

    row8 = lax.broadcasted_iota(jnp.int32, (8, tm), 0)
    ri_ref[...] = jnp.where(row8 == 0, i1, jnp.where(row8 == 1, i2,
                            jnp.where(row8 == 2, r1, jnp.where(row8 == 3, r2, 0))))
    rw_ref[...] = jnp.where(row8 == 0, c1, jnp.where(row8 == 1, c2, 0.0))


def _mix(yg, proj, x2d, w_ret_o, w_conv_o, w_out, conv_w, gate1, shift2, scale2, norm2_w,
         w_router, b_router, seq, off_cb, tm):
    n, d = x2d.shape
    per_b = seq // tm
    cblk = off_cb // d
    row = lambda i: (i, 0)
    const2 = lambda i: (0, 0)
    bidx = lambda i: (i // per_b, 0, 0)
    col = lambda k: (lambda i: (i, cblk + k))
    return pl.pallas_call(
        _mix_kernel,
        out_shape=(jax.ShapeDtypeStruct((n, d), F32),
                   jax.ShapeDtypeStruct((n, d), F32),
                   jax.ShapeDtypeStruct((8, n), jnp.int32),
                   jax.ShapeDtypeStruct((8, n), F32),
                   jax.ShapeDtypeStruct((N_EXPERTS, 1), F32)),
        grid=(n // tm,),
        in_specs=[pl.BlockSpec((tm, RET_V_W), row),
                  pl.BlockSpec((tm, d), col(0)),
                  pl.BlockSpec((tm, d), col(1)),
                  pl.BlockSpec((tm, d), col(2)),
                  pl.BlockSpec((tm, d), col(3)),
                  pl.BlockSpec((tm, d), col(4)),
                  pl.BlockSpec((tm, d), row),
                  pl.BlockSpec((RET_V_W, d), const2, pipeline_mode=pl.Buffered(1)),
                  pl.BlockSpec((d, d), const2, pipeline_mode=pl.Buffered(1)),
                  pl.BlockSpec((d, d), const2, pipeline_mode=pl.Buffered(1)),
                  pl.BlockSpec((3, d), const2),
                  pl.BlockSpec((1, 1, d), bidx),
                  pl.BlockSpec((1, 1, d), bidx),
                  pl.BlockSpec((1, 1, d), bidx),
                  pl.BlockSpec((1, d), const2),
                  pl.BlockSpec((2 * LANES, d), const2),
                  pl.BlockSpec((LANES, 1), const2)],
        out_specs=(pl.BlockSpec((tm, d), row),
                   pl.BlockSpec((tm, d), row),
                   pl.BlockSpec((8, tm), lambda i: (0, i)),
                   pl.BlockSpec((8, tm), lambda i: (0, i)),
                   pl.BlockSpec((N_EXPERTS, 1), const2)),
        scratch_shapes=[pltpu.VMEM((N_EXPERTS, 1), F32)],
        compiler_params=_params(("arbitrary",)),
        name="mix",
    )(yg, proj, proj, proj, proj, proj, x2d, w_ret_o, w_conv_o, w_out, conv_w,
      gate1, shift2, scale2, norm2_w.reshape(1, d), w_router, b_router)


def _row_copy(src_ref, src_row, dst_ref, dst_row, sem):
    return pltpu.make_async_copy(src_ref.at[pl.ds(src_row, 1), :],
                                 dst_ref.at[pl.ds(dst_row, 1), :], sem)


SCATTER_SLOTS = 3


def _scatter_kernel(pos_ref, zt_ref, h_ref, xs_ref, hbuf_ref, zero_ref, lsem, ssem, zsem):
    i = pl.program_id(0)
    n_steps = pl.num_programs(0)
    tm = hbuf_ref.shape[1]
    te = zero_ref.shape[0]
    base = i * tm
    n_tok = pos_ref.shape[0] // 2

    def load(tile, slot):
        src = h_ref.at[pl.ds(pl.multiple_of(tile * tm, tm), tm), :]
        return pltpu.make_async_copy(src, hbuf_ref.at[slot], lsem.at[slot])

    def wait_rows(slot):
        for _ in range(2):
            pltpu.make_async_copy(hbuf_ref.at[slot], xs_ref.at[pl.ds(0, tm), :], ssem.at[slot]).wait()

    @pl.when(i == 0)
    def _():
        load(0, 0).start()
        load(1, 1).start()
        zero_ref[...] = jnp.zeros_like(zero_ref)

        def tile_copy(z):
            row = pl.multiple_of(zt_ref[z] * te, te)
            return pltpu.make_async_copy(zero_ref, xs_ref.at[pl.ds(row, te), :], zsem)

        def zissue(z, carry):
            @pl.when(zt_ref[z] >= 0)
            def _():
                tile_copy(z).start()
            return carry

        def zdrain(z, carry):
            @pl.when(zt_ref[z] >= 0)
            def _():
                tile_copy(z).wait()
            return carry

        lax.fori_loop(0, zt_ref.shape[0], zissue, 0)
        lax.fori_loop(0, zt_ref.shape[0], zdrain, 0)

    slot = lax.rem(i, SCATTER_SLOTS)
    reload_slot = lax.rem(i + 2, SCATTER_SLOTS)
    load(i, slot).wait()
    src = hbuf_ref.at[slot]
    pos_a = pos_ref.at[pl.ds(base, tm)]
    pos_b = pos_ref.at[pl.ds(n_tok + base, tm)]
    for r in range(tm):
        _row_copy(src, r, xs_ref, pos_a[r], ssem.at[slot]).start(priority=0)
        _row_copy(src, r, xs_ref, pos_b[r], ssem.at[slot]).start(priority=1)

    @pl.when(i > 0)
    def _():
        wait_rows(reload_slot)

    load(jnp.minimum(i + 2, n_steps - 1), reload_slot).start()

    @pl.when(i == n_steps - 1)
    def _():
        wait_rows(slot)
        load(i, lax.rem(i + 1, SCATTER_SLOTS)).wait()
        load(i, reload_slot).wait()


def _scatter_rows(pos_flat, zero_tiles, h2, n_sorted, tm):
    n, d = h2.shape
    assert n // tm >= SCATTER_SLOTS
    grid_spec = pltpu.PrefetchScalarGridSpec(
        num_scalar_prefetch=2,
        grid=(n // tm,),
        in_specs=[pl.BlockSpec(memory_space=pl.ANY)],
        out_specs=pl.BlockSpec(memory_space=pl.ANY),
        scratch_shapes=[pltpu.VMEM((SCATTER_SLOTS, tm, d), F32),
                        pltpu.VMEM((EXPERT_TILE, d), F32),
                        pltpu.SemaphoreType.DMA((SCATTER_SLOTS,)),
                        pltpu.SemaphoreType.DMA((SCATTER_SLOTS,)),
                        pltpu.SemaphoreType.DMA],
    )
    return pl.pallas_call(
        _scatter_kernel,
        out_shape=jax.ShapeDtypeStruct((n_sorted, d), F32),
        grid_spec=grid_spec,
        compiler_params=_params(("arbitrary",)),
        name="scatter",
    )(pos_flat, zero_tiles, h2)


ROW_TILE_SLOTS = 3


def _expert_kernel(tf_ref, te_ref, tn_ref, ts_ref, xs_ref, w13_ref, w2_ref, ys_ref,
                   xbuf_ref, w13f_ref, w2f_ref, w13b_ref, w2b_ref, xsem, sem13, sem2):
    t = pl.program_id(0)
    nt = pl.num_programs(0)
    rows = xbuf_ref.shape[1]
    flag = tf_ref[t]

    def tile_copy(tile, slot):
        src = xs_ref.at[pl.ds(pl.multiple_of(tile * rows, rows), rows), :]
        return pltpu.make_async_copy(src, xbuf_ref.at[slot], xsem.at[slot])

    def weight_copies(e, slot):
        return (pltpu.make_async_copy(w13_ref.at[e], w13f_ref.at[slot], sem13.at[slot]),
                pltpu.make_async_copy(w2_ref.at[e], w2f_ref.at[slot], sem2.at[slot]))

    @pl.when(t == 0)
    def _():
        for cp in weight_copies(te_ref[0], ts_ref[0]):
            cp.start()
        tile_copy(0, 0).start()

        @pl.when(tf_ref[1] > 0)
        def _():
            tile_copy(1, 1).start()

    slot = lax.rem(t, ROW_TILE_SLOTS)
    ahead = jnp.minimum(t + 2, nt - 1)

    @pl.when(flag > 0)
    def _():
        tile_copy(t, slot).wait()

    @pl.when((t + 2 < nt) & (tf_ref[ahead] > 0))
    def _():
        tile_copy(ahead, lax.rem(t + 2, ROW_TILE_SLOTS)).start()

    @pl.when(flag == 2)
    def _():
        slot = ts_ref[t]
        for cp in weight_copies(te_ref[t], slot):
            cp.wait()
        w13b_ref[...] = w13f_ref[slot].astype(BF16)
        w2b_ref[...] = w2f_ref[slot].astype(BF16)

        @pl.when(tn_ref[t] >= 0)
        def _():
            for cp in weight_copies(tn_ref[t], 1 - slot):
                cp.start()

    @pl.when(flag > 0)
    def _():
        x = xbuf_ref[slot].astype(BF16)
        hid = jnp.dot(x, w13b_ref[...], preferred_element_type=F32)
        a = hid[:, :EXPERT_HIDDEN]
        act = a * jax.nn.sigmoid(a) * hid[:, EXPERT_HIDDEN:]
        ys_ref[...] = jnp.dot(act.astype(BF16), w2b_ref[...], preferred_element_type=F32)

    @pl.when(flag == 0)
    def _():
        ys_ref[...] = jnp.zeros_like(ys_ref)


def _experts(tile_flag, tile_expert, tile_next, tile_slot, xs, w13, w2):
    n_sorted, d = xs.shape
    te = EXPERT_TILE
    n_tiles = n_sorted // te
    assert n_tiles >= ROW_TILE_SLOTS
    hid2 = w13.shape[2]
    grid_spec = pltpu.PrefetchScalarGridSpec(
        num_scalar_prefetch=4,
        grid=(n_tiles,),
        in_specs=[pl.BlockSpec(memory_space=pl.ANY),
                  pl.BlockSpec(memory_space=pl.ANY),
                  pl.BlockSpec(memory_space=pl.ANY)],
        out_specs=pl.BlockSpec((te, d), lambda t, *_: (t, 0)),
        scratch_shapes=[pltpu.VMEM((ROW_TILE_SLOTS, te, d), F32),
                        pltpu.VMEM((2, d, hid2), F32),
                        pltpu.VMEM((2, hid2 // 2, d), F32),
                        pltpu.VMEM((d, hid2), BF16),
                        pltpu.VMEM((hid2 // 2, d), BF16),
                        pltpu.SemaphoreType.DMA((ROW_TILE_SLOTS,)),
                        pltpu.SemaphoreType.DMA((2,)),
                        pltpu.SemaphoreType.DMA((2,))],
    )
    return pl.pallas_call(
        _expert_kernel,
        out_shape=jax.ShapeDtypeStruct((n_sorted, d), F32),
        grid_spec=grid_spec,
        compiler_params=_params(("arbitrary",)),
        name="experts",
    )(tile_flag, tile_expert, tile_next, tile_slot, xs, w13, w2)


GATHER_SLOTS = 3


def _combine_kernel(pos_ref, ys_ref, x1_ref, rw_ref, g2_ref, fw_ref, o_ref, ya_ref, yb_ref, sem):
    i = pl.program_id(0)
    n_steps = pl.num_programs(0)
    tm = ya_ref.shape[1]
    n_tok = pos_ref.shape[0] // 2

    def issue(tile, slot):
        base = tile * tm
        dst_a = ya_ref.at[slot]
        dst_b = yb_ref.at[slot]
        pos_a = pos_ref.at[pl.ds(base, tm)]
        pos_b = pos_ref.at[pl.ds(n_tok + base, tm)]
        for r in range(tm):
            _row_copy(ys_ref, pos_a[r], dst_a, r, sem.at[slot]).start(priority=0)
            _row_copy(ys_ref, pos_b[r], dst_b, r, sem.at[slot]).start(priority=1)

    def wait(slot):
        pltpu.make_async_copy(ys_ref.at[pl.ds(0, tm), :], ya_ref.at[slot], sem.at[slot]).wait()
        pltpu.make_async_copy(ys_ref.at[pl.ds(0, tm), :], yb_ref.at[slot], sem.at[slot]).wait()

    @pl.when(i == 0)
    def _():
        issue(0, 0)
        issue(1, 1)

    slot = lax.rem(i, GATHER_SLOTS)
    ahead_slot = lax.rem(i + 2, GATHER_SLOTS)
    wait(slot)
    issue(jnp.minimum(i + 2, n_steps - 1), ahead_slot)
    rw = rw_ref[...]
    moe = rw[:, 0:1] * ya_ref[slot] + rw[:, 1:2] * yb_ref[slot]
    x2 = x1_ref[...] + g2_ref[0] * moe
    o_ref[...] = x2 * lax.rsqrt(jnp.mean(x2 * x2, axis=-1, keepdims=True) + NORM_EPS) * fw_ref[...]

    @pl.when(i == n_steps - 1)
    def _():
        wait(lax.rem(i + 1, GATHER_SLOTS))
        wait(ahead_slot)


def _combine(pos_flat, ys, x1, route_w, gate2, final_w, seq, tm):
    n, d = x1.shape
    per_b = seq // tm
    assert n // tm >= GATHER_SLOTS
    grid_spec = pltpu.PrefetchScalarGridSpec(
        num_scalar_prefetch=1,
        grid=(n // tm,),
        in_specs=[pl.BlockSpec(memory_space=pl.ANY),
                  pl.BlockSpec((tm, d), lambda i, pos: (i, 0)),
                  pl.BlockSpec((tm, 2), lambda i, pos: (i, 0)),
                  pl.BlockSpec((1, 1, d), lambda i, pos: (i // per_b, 0, 0)),
                  pl.BlockSpec((1, d), lambda i, pos: (0, 0))],
        out_specs=pl.BlockSpec((tm, d), lambda i, pos: (i, 0)),
        scratch_shapes=[pltpu.VMEM((GATHER_SLOTS, tm, d), F32),
                        pltpu.VMEM((GATHER_SLOTS, tm, d), F32),
                        pltpu.SemaphoreType.DMA((GATHER_SLOTS,))],
    )
    return pl.pallas_call(
        _combine_kernel,
        out_shape=jax.ShapeDtypeStruct((n, d), F32),
        grid_spec=grid_spec,
        compiler_params=_params(("arbitrary",)),
        name="combine",
    )(pos_flat, ys, x1, route_w, gate2, final_w.reshape(1, d))


def _rope_tables(seq):
    pos = np.arange(seq)
    nf = RET_QK_DIM // 4
    inv = (np.float32(ROPE_BASE) ** (-np.arange(nf, dtype=np.float32) / np.float32(nf))).astype(np.float32)
    ang_r = ((pos // GRID_W).astype(np.float32)[:, None] * inv[None, :]).astype(np.float64)
    ang_c = ((pos % GRID_W).astype(np.float32)[:, None] * inv[None, :]).astype(np.float64)
    return (jnp.asarray(np.concatenate([np.cos(ang_r), np.cos(ang_c)], axis=1), F32),
            jnp.asarray(np.concatenate([np.sin(ang_r), np.sin(ang_c)], axis=1), F32))


def _weight_prep_kernel(w_ref, o_ref, octx_ref, *, n_rot, ctx_cols):
    half = LANES // 2
    first = lax.broadcasted_iota(jnp.int32, (w_ref.shape[0], LANES), 1) < half
    for c0 in range(0, n_rot, RET_QK_DIM):
        a = w_ref[:, c0:c0 + LANES]
        b = w_ref[:, c0 + LANES:c0 + 2 * LANES]
        o_ref[:, c0:c0 + LANES] = jnp.where(first, a, pltpu.roll(b, half, axis=1)).astype(o_ref.dtype)
        o_ref[:, c0 + LANES:c0 + 2 * LANES] = jnp.where(first, pltpu.roll(a, half, axis=1), b).astype(o_ref.dtype)
    o_ref[:, n_rot:] = w_ref[:, n_rot:].astype(o_ref.dtype)
    octx_ref[...] = o_ref[:, ctx_cols[0]:ctx_cols[1]]


def _weight_prep(w_in, n_rot, ctx_cols):
    d, width = w_in.shape
    tr = 128
    ctx_width = ctx_cols[1] - ctx_cols[0]
    return pl.pallas_call(
        functools.partial(_weight_prep_kernel, n_rot=n_rot, ctx_cols=ctx_cols),
        out_shape=(jax.ShapeDtypeStruct((d, width), BF16),
                   jax.ShapeDtypeStruct((d, ctx_width), BF16)),
        grid=(d // tr,),
        in_specs=[pl.BlockSpec((tr, width), lambda i: (i, 0))],
        out_specs=(pl.BlockSpec((tr, width), lambda i: (i, 0)),
                   pl.BlockSpec((tr, ctx_width), lambda i: (i, 0))),
        compiler_params=_params(("parallel",)),
        name="wprep",
    )(w_in)


def _layer(x, c, ctx, c_ctx, w_mod, b_mod, norm1_w, norm2_w, w_in, log_decay, gn_w, w_ret_o,
           conv_w, w_conv_o, w_out, rg_w, rg_b, re_w, re_b, w13, w2, final_w):
    batch, seq, d = x.shape
    ctx_len = ctx.shape[1]
    n = batch * seq
    off_q = 0
    off_k = off_q + RET_QK_W
    off_v = off_k + RET_QK_W
    off_g = off_v + RET_V_W
    off_cb = off_g + RET_V_W

    pad = (-(batch + 1)) % 8
    c_all = jnp.concatenate([c, c_ctx[None, :], jnp.zeros((pad, d), F32)], axis=0)
    mod = _modulation(c_all, w_mod, b_mod)
    mx = mod[:batch].reshape(batch, 1, 6, d)
    shift1, scale1, gate1, shift2, scale2, gate2 = (mx[:, :, k, :] for k in range(6))
    mc = mod[batch].reshape(1, 1, 6, d)
    shift1c, scale1c = mc[:, :, 0, :], mc[:, :, 1, :]

    w_in_b, w_ctx_b = _weight_prep(w_in, off_v, (off_k, off_g))
    cos_t, sin_t = _rope_tables(seq)
    k_scale = RET_QK_DIM ** -0.5

    tm_proj = min(512, seq)
    main_modes = (("rope", 1.0), ("rope", k_scale), ("plain", None), ("plain", None),
                  ("swish_gain", 0), ("swish_gain", 1), ("plain", None), ("plain", None), ("plain", None),
                  ("plain", None), ("plain", None))
    gain = gn_w.reshape(1, RET_V_W)
    proj = _projection(x.reshape(n, d), norm1_w, shift1, scale1, cos_t, sin_t, gain, w_in_b, seq,
                       modes=main_modes, tm=tm_proj)
    tm_ctx = batch * ctx_len
    no_rope = jnp.zeros((tm_ctx, LANES), F32)
    kv_ctx = _projection(ctx.reshape(tm_ctx, d), norm1_w, shift1c, scale1c, no_rope, no_rope, gain,
                         w_ctx_b, tm_ctx,
                         modes=(("scale", k_scale), ("plain", None), ("plain", None)), tm=tm_ctx)

    log_gamma = jnp.log1p(-jnp.exp(log_decay.astype(F32)))
    yg = _retention(proj, kv_ctx, log_gamma, batch, seq, ctx_len, off_q, off_k, off_v, off_g)

    w_router = jnp.concatenate(
        [re_w, rg_w, jnp.zeros((d, LANES - N_EXPERTS - N_GROUPS), F32)], axis=1)
    b_router = jnp.concatenate(
        [re_b, rg_b, jnp.zeros((LANES - N_EXPERTS - N_GROUPS,), F32)]).reshape(LANES, 1)
    w_router_hi = w_router.astype(BF16)
    w_router_lo = (w_router - w_router_hi.astype(F32)).astype(BF16)
    w_router = jnp.concatenate([w_router_hi, w_router_lo], axis=1).T
    tm_mix = min(512, seq)
    x1, h2, route_i, route_w, counts = _mix(
        yg, proj, x.reshape(n, d), w_ret_o.astype(BF16), w_conv_o.astype(BF16), w_out.astype(BF16),
        conv_w, gate1, shift2, scale2, norm2_w, w_router, b_router, seq, off_cb, tm_mix)

    te = EXPERT_TILE
    cnt = counts[:, 0].astype(jnp.int32)
    padded = ((cnt + te - 1) // te) * te
    ends = jnp.cumsum(padded)
    starts = ends - padded
    expert_ids = jnp.arange(N_EXPERTS, dtype=jnp.int32)
    route_t = route_i[0:4]
    seg_start = jnp.sum(jnp.where(route_t[0:2, :, None] == expert_ids, starts, 0), axis=-1)
    pos_flat = (seg_start + route_t[2:4]).reshape(-1).astype(jnp.int32)
    n_tiles = (2 * n) // te + N_EXPERTS
    tile_row = jnp.arange(n_tiles, dtype=jnp.int32) * te
    tile_expert = jnp.minimum(jnp.sum((tile_row[:, None] >= ends[None, :]).astype(jnp.int32), axis=1),
                              N_EXPERTS - 1).astype(jnp.int32)
    new_expert = jnp.concatenate([jnp.ones((1,), bool), tile_expert[1:] != tile_expert[:-1]])
    tile_flag = jnp.where(tile_row < ends[-1], jnp.where(new_expert, 2, 1), 0).astype(jnp.int32)

    valid_tiles = ends[-1] // te
    last_tile = jnp.where(padded > 0, ends // te - 1, -1)
    tail_tile = valid_tiles + jnp.arange(N_EXPERTS, dtype=jnp.int32)
    tail_tile = jnp.where(tail_tile < n_tiles, tail_tile, -1)
    zero_tiles = jnp.concatenate([last_tile, tail_tile]).astype(jnp.int32)

    xs = _scatter_rows(pos_flat, zero_tiles, h2, n_tiles * te, min(512, seq // 2))
    nonempty = padded > 0
    later = nonempty[None, :] & (expert_ids[None, :] > expert_ids[:, None])
    next_expert = jnp.min(jnp.where(later, expert_ids[None, :], N_EXPERTS), axis=1)
    next_expert = jnp.where(next_expert == N_EXPERTS, -1, next_expert)
    slot = (jnp.cumsum(nonempty.astype(jnp.int32)) - 1) % 2
    tile_hot = tile_expert[:, None] == expert_ids[None, :]
    tile_next = jnp.sum(jnp.where(tile_hot, next_expert, 0), axis=1).astype(jnp.int32)
    tile_slot = jnp.sum(jnp.where(tile_hot, slot, 0), axis=1).astype(jnp.int32)

    ys = _experts(tile_flag, tile_expert, tile_next, tile_slot, xs, w13, w2)
    out = _combine(pos_flat, ys, x1, route_w[0:2].T, gate2, final_w, seq, min(256, seq // 2))
    return out.reshape(batch, seq, d)


def kernel(x, c, ctx, c_ctx, w_mod, b_mod, norm1_w, norm2_w, w_in, ret_log_decay, ret_gn_w, w_ret_o,
           conv_w, w_conv_o, w_out, router_group_w, router_group_b, router_expert_w, router_expert_b,
           expert_w13, expert_w2, final_norm_w):
    assert w_mod.shape[0] == 1, "single-layer problem"
    return _layer(x, c, ctx, c_ctx, w_mod[0], b_mod[0], norm1_w[0], norm2_w[0], w_in[0],
                  ret_log_decay[0], ret_gn_w[0], w_ret_o[0], conv_w[0], w_conv_o[0], w_out[0],
                  router_group_w[0], router_group_b[0], router_expert_w[0], router_expert_b[0],
                  expert_w13[0], expert_w2[0], final_norm_w)
```

```python
import functools

import jax
import jax.numpy as jnp
import numpy as np
from jax import lax
from jax.experimental import pallas as pl
from jax.experimental.pallas import tpu as pltpu

GRID_W = 64
RET_HEADS = 4
RET_QK_DIM = 256
RET_V_DIM = 512
RET_QK_W = RET_HEADS * RET_QK_DIM
RET_V_W = RET_HEADS * RET_V_DIM
N_GROUPS = 4
EXPERTS_PER_GROUP = 8
N_EXPERTS = N_GROUPS * EXPERTS_PER_GROUP
EXPERT_HIDDEN = 512
ROPE_BASE = 10000.0
NORM_EPS = 1e-6
GN_EPS = 1e-5

RET_CHUNK = 256
LANES = 128
EXPERT_TILE = 256
VMEM_LIMIT = 56 * 1024 * 1024

BF16 = jnp.bfloat16
F32 = jnp.float32


def _params(sem, vmem=VMEM_LIMIT):
    return pltpu.CompilerParams(dimension_semantics=sem, vmem_limit_bytes=vmem)


def _mod_kernel(c_ref, w_ref, b_ref, o_ref):
    c = c_ref[...]
    s = c * jax.nn.sigmoid(c)
    o_ref[...] = jnp.dot(s.astype(BF16), w_ref[...].astype(BF16),
                         preferred_element_type=F32) + b_ref[...]


def _modulation(c_all, w_mod, b_mod):
    rows, d = c_all.shape
    n_out = w_mod.shape[1]
    tn = 3072
    return pl.pallas_call(
        _mod_kernel,
        out_shape=jax.ShapeDtypeStruct((rows, n_out), F32),
        grid=(n_out // tn,),
        in_specs=[pl.BlockSpec((rows, d), lambda j: (0, 0)),
                  pl.BlockSpec((d, tn), lambda j: (0, j)),
                  pl.BlockSpec((1, tn), lambda j: (0, j))],
        out_specs=pl.BlockSpec((rows, tn), lambda j: (0, j)),
        compiler_params=_params(("parallel",)),
        name="mod",
    )(c_all, w_mod, b_mod.reshape(1, n_out))


def _rope_store(acc, cos, sin, scale, o_ref):
    for hd in range(acc.shape[1] // RET_QK_DIM):
        lo = hd * RET_QK_DIM
        u1 = acc[:, lo:lo + LANES]
        u2 = acc[:, lo + LANES:lo + 2 * LANES]
        r1 = u1 * cos - u2 * sin
        r2 = u1 * sin + u2 * cos
        if scale != 1.0:
            r1 = r1 * scale
            r2 = r2 * scale
        o_ref[:, lo:lo + LANES] = r1.astype(o_ref.dtype)
        o_ref[:, lo + LANES:lo + 2 * LANES] = r2.astype(o_ref.dtype)


def _proj_kernel(x_ref, nw_ref, sh_ref, sc_ref, cos_ref, sin_ref, gn_ref, w_ref, o_ref, *, modes, tn):
    x = x_ref[...]
    y = x * lax.rsqrt(jnp.mean(x * x, axis=-1, keepdims=True) + NORM_EPS) * nw_ref[...]
    h = (y * (1.0 + sc_ref[0]) + sh_ref[0]).astype(BF16)
    assert len(modes) * tn == w_ref.shape[1]
    for j, (kind, arg) in enumerate(modes):
        cols = slice(j * tn, (j + 1) * tn)
        acc = jnp.dot(h, w_ref[:, cols], preferred_element_type=F32)
        if kind == "rope":
            _rope_store(acc, cos_ref[...], sin_ref[...], arg, o_ref.at[:, cols])
        elif kind == "scale":
            o_ref[:, cols] = (acc * arg).astype(o_ref.dtype)
        elif kind == "swish_gain":
            gain = gn_ref[:, arg * tn:(arg + 1) * tn]
            o_ref[:, cols] = (acc * jax.nn.sigmoid(acc) * gain).astype(o_ref.dtype)
        else:
            o_ref[:, cols] = acc.astype(o_ref.dtype)


def _projection(x2d, norm_w, shift, scale, cos_t, sin_t, gain, w_bf16, seq, modes, tm):
    n, d = x2d.shape
    width = w_bf16.shape[1]
    tn = 1024
    per_b = seq // tm
    nb = shift.shape[0]
    bidx = (lambda i: (i // per_b, 0, 0)) if nb > 1 else (lambda i: (0, 0, 0))
    return pl.pallas_call(
        functools.partial(_proj_kernel, modes=modes, tn=tn),
        out_shape=jax.ShapeDtypeStruct((n, width), BF16),
        grid=(n // tm,),
        in_specs=[pl.BlockSpec((tm, d), lambda i: (i, 0)),
                  pl.BlockSpec((1, d), lambda i: (0, 0)),
                  pl.BlockSpec((1, 1, d), bidx),
                  pl.BlockSpec((1, 1, d), bidx),
                  pl.BlockSpec((tm, LANES), lambda i: (i % per_b, 0)),
                  pl.BlockSpec((tm, LANES), lambda i: (i % per_b, 0)),
                  pl.BlockSpec(gain.shape, lambda i: (0, 0)),
                  pl.BlockSpec((d, width), lambda i: (0, 0), pipeline_mode=pl.Buffered(1))],
        out_specs=pl.BlockSpec((tm, width), lambda i: (i, 0)),
        compiler_params=_params(("parallel",)),
        name="proj",
    )(x2d, norm_w.reshape(1, d), shift, scale, cos_t, sin_t, gain, w_bf16)


def _dot_t0(a, b):
    return lax.dot_general(a, b, (((0,), (0,)), ((), ())), preferred_element_type=F32)


def _ret_kernel(lg_ref, q_ref, k_ref, v_ref, g_ref, kc_ref, vc_ref, o_ref,
                acc_ref, sf_ref, sb_ref, dm_ref, *, seq, ctx_len):
    h = pl.program_id(1)
    lgf = lg_ref[0, h]
    lgb = lg_ref[1, h]
    c = RET_CHUNK
    n_chunks = seq // c

    cpos = lax.broadcasted_iota(jnp.int32, (ctx_len, 1), 0).astype(F32)
    kc = kc_ref[...].astype(F32)
    vc = vc_ref[...]
    sf_ref[...] = _dot_t0((kc * jnp.exp((ctx_len - 1.0 - cpos) * lgf)).astype(BF16), vc)
    sb_ref[...] = _dot_t0((kc * jnp.exp(cpos * lgb)).astype(BF16), vc)

    ri = lax.broadcasted_iota(jnp.int32, (c, c), 0)
    ci = lax.broadcasted_iota(jnp.int32, (c, c), 1)
    dist = (ri - ci).astype(F32)
    dm_ref[...] = jnp.exp(jnp.abs(dist) * jnp.where(dist >= 0, lgf, lgb))

    idx = lax.broadcasted_iota(jnp.int32, (c, 1), 0).astype(F32)
    xi_f = jnp.exp((idx + 1.0) * lgf)
    zeta_f = jnp.exp((c - 1.0 - idx) * lgf)
    xi_b = jnp.exp((c - idx) * lgb)
    zeta_b = jnp.exp(idx * lgb)
    cd_f = jnp.exp(jnp.full((1, 1), float(c), F32) * lgf)
    cd_b = jnp.exp(jnp.full((1, 1), float(c), F32) * lgb)

    def chunk_rows(ic):
        return pl.ds(pl.multiple_of(ic * c, c), c)

    def fwd_part(ic):
        rows = chunk_rows(ic)
        q = q_ref[rows, :]
        k = k_ref[rows, :]
        v = v_ref[rows, :]
        s = lax.dot_general(q, k, (((1,), (1,)), ((), ())), preferred_element_type=F32)
        intra = jnp.dot((s * dm_ref[...]).astype(BF16), v, preferred_element_type=F32)
        inter = jnp.dot(q, sf_ref[...].astype(BF16), preferred_element_type=F32)
        kz = (k.astype(F32) * zeta_f).astype(BF16)
        sf_ref[...] = cd_f * sf_ref[...] + _dot_t0(kz, v)
        return intra + xi_f * inter

    def bwd_part(ic):
        rows = chunk_rows(ic)
        q = q_ref[rows, :]
        k = k_ref[rows, :]
        v = v_ref[rows, :]
        inter = jnp.dot(q, sb_ref[...].astype(BF16), preferred_element_type=F32)
        kz = (k.astype(F32) * zeta_b).astype(BF16)
        sb_ref[...] = cd_b * sb_ref[...] + _dot_t0(kz, v)
        return xi_b * inter

    def finalize(ic, y):
        rows = chunk_rows(ic)
        mu = jnp.mean(y, axis=-1, keepdims=True)
        yc = y - mu
        var = jnp.mean(yc * yc, axis=-1, keepdims=True)
        o_ref[rows, :] = (g_ref[rows, :].astype(F32) * (yc * lax.rsqrt(var + GN_EPS))).astype(o_ref.dtype)

    half = n_chunks // 2

    def first_half(t, carry):
        acc_ref[chunk_rows(t), :] = fwd_part(t)
        acc_ref[chunk_rows(n_chunks - 1 - t), :] = bwd_part(n_chunks - 1 - t)
        return carry

    def second_half(t, carry):
        finalize(t, acc_ref[chunk_rows(t), :] + fwd_part(t))
        u = n_chunks - 1 - t
        finalize(u, acc_ref[chunk_rows(u), :] + bwd_part(u))
        return carry

    lax.fori_loop(0, half, first_half, 0, unroll=True)
    lax.fori_loop(half, n_chunks, second_half, 0, unroll=min(4, half))


def _retention(proj, kv_ctx, log_gamma, batch, seq, ctx_len, off_q, off_k, off_v, off_g):
    n = proj.shape[0]
    dk, dv = RET_QK_DIM, RET_V_DIM
    assert seq % (2 * RET_CHUNK) == 0, "the paired scan needs an even number of chunks"
    grid_spec = pltpu.PrefetchScalarGridSpec(
        num_scalar_prefetch=1,
        grid=(batch, RET_HEADS),
        in_specs=[pl.BlockSpec((seq, dk), lambda b, h, lg: (b, off_q // dk + h)),
                  pl.BlockSpec((seq, dk), lambda b, h, lg: (b, off_k // dk + h)),
                  pl.BlockSpec((seq, dv), lambda b, h, lg: (b, off_v // dv + h)),
                  pl.BlockSpec((seq, dv), lambda b, h, lg: (b, off_g // dv + h)),
                  pl.BlockSpec((ctx_len, dk), lambda b, h, lg: (b, h)),
                  pl.BlockSpec((ctx_len, dv), lambda b, h, lg: (b, RET_QK_W // dv + h))],
        out_specs=pl.BlockSpec((seq, dv), lambda b, h, lg: (b, h)),
        scratch_shapes=[pltpu.VMEM((seq, dv), F32),
                        pltpu.VMEM((dk, dv), F32),
                        pltpu.VMEM((dk, dv), F32),
                        pltpu.VMEM((RET_CHUNK, RET_CHUNK), F32)],
    )
    return pl.pallas_call(
        functools.partial(_ret_kernel, seq=seq, ctx_len=ctx_len),
        out_shape=jax.ShapeDtypeStruct((n, RET_V_W), BF16),
        grid_spec=grid_spec,
        compiler_params=_params(("parallel", "arbitrary")),
        name="ret",
    )(log_gamma, proj, proj, proj, proj, kv_ctx, kv_ctx)


def _mix_kernel(yg_ref, cb_ref, cc_ref, ch_ref, gr_ref, gc_ref, x_ref, wro_ref, wco_ref, wo_ref,
                cw_ref, g1_ref, sh2_ref, sc2_ref, n2_ref, wr_ref, br_ref,
                x1_ref, h2_ref, ri_ref, rw_ref, cnt_ref, carry_ref):
    i = pl.program_id(0)
    tm = x_ref.shape[0]

    @pl.when(i == 0)
    def _():
        carry_ref[...] = jnp.zeros_like(carry_ref)

    ret_branch = jnp.dot(yg_ref[...], wro_ref[...], preferred_element_type=F32)

    p = cc_ref[...].astype(F32) * ch_ref[...].astype(F32)
    tpos = lax.broadcasted_iota(jnp.int32, (tm, 1), 0) % GRID_W
    prev = jnp.where(tpos != 0, pltpu.roll(p, 1, axis=0), 0.0)
    nxt = jnp.where(tpos != GRID_W - 1, pltpu.roll(p, tm - 1, axis=0), 0.0)
    cw = cw_ref[...]
    u = cw[0:1, :] * prev + cw[1:2, :] * p + cw[2:3, :] * nxt
    conv_in = (cb_ref[...].astype(F32) * u).astype(BF16)
    conv_branch = jnp.dot(conv_in, wco_ref[...], preferred_element_type=F32)

    merged = (jax.nn.sigmoid(gr_ref[...].astype(F32)) * ret_branch
              + jax.nn.sigmoid(gc_ref[...].astype(F32)) * conv_branch)
    mixed = jnp.dot(merged.astype(BF16), wo_ref[...], preferred_element_type=F32)
    x1 = x_ref[...] + g1_ref[0] * mixed
    x1_ref[...] = x1

    y = x1 * lax.rsqrt(jnp.mean(x1 * x1, axis=-1, keepdims=True) + NORM_EPS) * n2_ref[...]
    h2 = y * (1.0 + sc2_ref[0]) + sh2_ref[0]
    h2_ref[...] = h2

    h_hi = h2.astype(BF16)
    h_lo = (h2 - h_hi.astype(F32)).astype(BF16)
    nt_dims = (((1,), (1,)), ((), ()))
    hw = lax.dot_general(wr_ref[...], h_hi, nt_dims, preferred_element_type=F32)
    lw = lax.dot_general(wr_ref[:LANES, :], h_lo, nt_dims, preferred_element_type=F32)
    logits_t = hw[:LANES] + (hw[LANES:] + lw) + br_ref[...]
    neg = jnp.float32(-jnp.inf)
    big = jnp.int32(1 << 20)

    grp = logits_t[N_EXPERTS:N_EXPERTS + N_GROUPS]
    grow = lax.broadcasted_iota(jnp.int32, grp.shape, 0)
    gmax = jnp.max(grp, axis=0, keepdims=True)
    gidx = jnp.min(jnp.where(grp == gmax, grow, big), axis=0, keepdims=True)
    g_w = 1.0 / jnp.sum(jnp.exp(grp - gmax), axis=0, keepdims=True)

    el_all = logits_t[:N_EXPERTS]
    erow = lax.broadcasted_iota(jnp.int32, el_all.shape, 0)
    in_grp = (erow // EXPERTS_PER_GROUP) == gidx
    el = jnp.where(in_grp, el_all, neg)
    emax = jnp.max(el, axis=0, keepdims=True)
    i1 = jnp.min(jnp.where(el == emax, erow, big), axis=0, keepdims=True)
    el2 = jnp.where(erow == i1, neg, el)
    l2 = jnp.max(el2, axis=0, keepdims=True)
    i2 = jnp.min(jnp.where(el2 == l2, erow, big), axis=0, keepdims=True)
    esum = jnp.sum(jnp.where(in_grp, jnp.exp(el_all - emax), 0.0), axis=0, keepdims=True)
    p1 = 1.0 / esum
    p2 = jnp.exp(l2 - emax) / esum
    den = p1 + p2
    c1 = g_w * (p1 / den)
    c2 = g_w * (p2 / den)

    hot1 = erow == i1
    hot2 = erow == i2
    onehot = jnp.where(hot1, 1.0, jnp.where(hot2, 1.0, 0.0))
    earlier = (lax.broadcasted_iota(jnp.int32, (tm, tm), 0)
               < lax.broadcasted_iota(jnp.int32, (tm, tm), 1))
    ranks = jnp.dot(onehot.astype(BF16), jnp.where(earlier, 1.0, 0.0).astype(BF16),
                    preferred_element_type=F32) + carry_ref[...]
    r1 = jnp.sum(jnp.where(hot1, ranks, 0.0), axis=0, keepdims=True).astype(jnp.int32)
    r2 = jnp.sum(jnp.where(hot2, ranks, 0.0), axis=0, keepdims=True).astype(jnp.int32)
    carry_ref[...] = carry_ref[...] + jnp.sum(onehot, axis=1, keepdims=True)
    cnt_ref[...] = carry_ref[...]

    row8 = lax.broadcasted_iota(jnp.int32, (8, tm), 0)
    ri_ref[...] = jnp.where(row8 == 0, i1, jnp.where(row8 == 1, i2,
                            jnp.where(row8 == 2, r1, jnp.where(row8 == 3, r2, 0))))
    rw_ref[...] = jnp.where(row8 == 0, c1, jnp.where(row8 == 1, c2, 0.0))


def _mix(yg, proj, x2d, w_ret_o, w_conv_o, w_out, conv_w, gate1, shift2, scale2, norm2_w,
         w_router, b_router, seq, off_cb, tm):
    n, d = x2d.shape
    per_b = seq // tm
    cblk = off_cb // d
    row = lambda i: (i, 0)
    const2 = lambda i: (0, 0)
    bidx = lambda i: (i // per_b, 0, 0)
    col = lambda k: (lambda i: (i, cblk + k))
    return pl.pallas_call(
        _mix_kernel,
        out_shape=(jax.ShapeDtypeStruct((n, d), F32),
                   jax.ShapeDtypeStruct((n, d), F32),
                   jax.ShapeDtypeStruct((8, n), jnp.int32),
                   jax.ShapeDtypeStruct((8, n), F32),
                   jax.ShapeDtypeStruct((N_EXPERTS, 1), F32)),
        grid=(n // tm,),
        in_specs=[pl.BlockSpec((tm, RET_V_W), row),
                  pl.BlockSpec((tm, d), col(0)),
                  pl.BlockSpec((tm, d), col(1)),
                  pl.BlockSpec((tm, d), col(2)),
                  pl.BlockSpec((tm, d), col(3)),
                  pl.BlockSpec((tm, d), col(4)),
                  pl.BlockSpec((tm, d), row),
                  pl.BlockSpec((RET_V_W, d), const2, pipeline_mode=pl.Buffered(1)),
                  pl.BlockSpec((d, d), const2, pipeline_mode=pl.Buffered(1)),
                  pl.BlockSpec((d, d), const2, pipeline_mode=pl.Buffered(1)),
                  pl.BlockSpec((3, d), const2),
                  pl.BlockSpec((1, 1, d), bidx),
                  pl.BlockSpec((1, 1, d), bidx),
                  pl.BlockSpec((1, 1, d), bidx),
                  pl.BlockSpec((1, d), const2),
                  pl.BlockSpec((2 * LANES, d), const2),
                  pl.BlockSpec((LANES, 1), const2)],
        out_specs=(pl.BlockSpec((tm, d), row),
                   pl.BlockSpec((tm, d), row),
                   pl.BlockSpec((8, tm), lambda i: (0, i)),
                   pl.BlockSpec((8, tm), lambda i: (0, i)),
                   pl.BlockSpec((N_EXPERTS, 1), const2)),
        scratch_shapes=[pltpu.VMEM((N_EXPERTS, 1), F32)],
        compiler_params=_params(("arbitrary",)),
        name="mix",
    )(yg, proj, proj, proj, proj, proj, x2d, w_ret_o, w_conv_o, w_out, conv_w,
      gate1, shift2, scale2, norm2_w.reshape(1, d), w_router, b_router)


def _row_copy(src_ref, src_row, dst_ref, dst_row, sem):
    return pltpu.make_async_copy(src_ref.at[pl.ds(src_row, 1), :],
                                 dst_ref.at[pl.ds(dst_row, 1), :], sem)


SCATTER_SLOTS = 3


def _scatter_kernel(pos_ref, zt_ref, h_ref, xs_ref, hbuf_ref, zero_ref, lsem, ssem, zsem):
    i = pl.program_id(0)
    n_steps = pl.num_programs(0)
    tm = hbuf_ref.shape[1]
    te = zero_ref.shape[0]
    base = i * tm
    n_tok = pos_ref.shape[0] // 2

    def load(tile, slot):
        src = h_ref.at[pl.ds(pl.multiple_of(tile * tm, tm), tm), :]
        return pltpu.make_async_copy(src, hbuf_ref.at[slot], lsem.at[slot])

    def wait_rows(slot):
        for _ in range(2):
            pltpu.make_async_copy(hbuf_ref.at[slot], xs_ref.at[pl.ds(0, tm), :], ssem.at[slot]).wait()

    @pl.when(i == 0)
    def _():
        load(0, 0).start()
        load(1, 1).start()
        zero_ref[...] = jnp.zeros_like(zero_ref)

        def tile_copy(z):
            row = pl.multiple_of(zt_ref[z] * te, te)
            return pltpu.make_async_copy(zero_ref, xs_ref.at[pl.ds(row, te), :], zsem)

        def zissue(z, carry):
            @pl.when(zt_ref[z] >= 0)
            def _():
                tile_copy(z).start()
            return carry

        def zdrain(z, carry):
            @pl.when(zt_ref[z] >= 0)
            def _():
                tile_copy(z).wait()
            return carry

        lax.fori_loop(0, zt_ref.shape[0], zissue, 0)
        lax.fori_loop(0, zt_ref.shape[0], zdrain, 0)

    slot = lax.rem(i, SCATTER_SLOTS)
    reload_slot = lax.rem(i + 2, SCATTER_SLOTS)
    load(i, slot).wait()
    src = hbuf_ref.at[slot]
    pos_a = pos_ref.at[pl.ds(base, tm)]
    pos_b = pos_ref.at[pl.ds(n_tok + base, tm)]
    for r in range(tm):
        _row_copy(src, r, xs_ref, pos_a[r], ssem.at[slot]).start(priority=0)
        _row_copy(src, r, xs_ref, pos_b[r], ssem.at[slot]).start(priority=1)

    @pl.when(i > 0)
    def _():
        wait_rows(reload_slot)

    load(jnp.minimum(i + 2, n_steps - 1), reload_slot).start()

    @pl.when(i == n_steps - 1)
    def _():
        wait_rows(slot)
        load(i, lax.rem(i + 1, SCATTER_SLOTS)).wait()
        load(i, reload_slot).wait()


def _scatter_rows(pos_flat, zero_tiles, h2, n_sorted, tm):
    n, d = h2.shape
    assert n // tm >= SCATTER_SLOTS
    grid_spec = pltpu.PrefetchScalarGridSpec(
        num_scalar_prefetch=2,
        grid=(n // tm,),
        in_specs=[pl.BlockSpec(memory_space=pl.ANY)],
        out_specs=pl.BlockSpec(memory_space=pl.ANY),
        scratch_shapes=[pltpu.VMEM((SCATTER_SLOTS, tm, d), F32),
                        pltpu.VMEM((EXPERT_TILE, d), F32),
                        pltpu.SemaphoreType.DMA((SCATTER_SLOTS,)),
                        pltpu.SemaphoreType.DMA((SCATTER_SLOTS,)),
                        pltpu.SemaphoreType.DMA],
    )
    return pl.pallas_call(
        _scatter_kernel,
        out_shape=jax.ShapeDtypeStruct((n_sorted, d), F32),
        grid_spec=grid_spec,
        compiler_params=_params(("arbitrary",)),
        name="scatter",
    )(pos_flat, zero_tiles, h2)


ROW_TILE_SLOTS = 3


def _expert_kernel(tf_ref, te_ref, tn_ref, ts_ref, xs_ref, w13_ref, w2_ref, ys_ref,
                   xbuf_ref, w13f_ref, w2f_ref, w13b_ref, w2b_ref, xsem, sem13, sem2):
    t = pl.program_id(0)
    nt = pl.num_programs(0)
    rows = xbuf_ref.shape[1]
    flag = tf_ref[t]

    def tile_copy(tile, slot):
        src = xs_ref.at[pl.ds(pl.multiple_of(tile * rows, rows), rows), :]
        return pltpu.make_async_copy(src, xbuf_ref.at[slot], xsem.at[slot])

    def weight_copies(e, slot):
        return (pltpu.make_async_copy(w13_ref.at[e], w13f_ref.at[slot], sem13.at[slot]),
                pltpu.make_async_copy(w2_ref.at[e], w2f_ref.at[slot], sem2.at[slot]))

    @pl.when(t == 0)
    def _():
        for cp in weight_copies(te_ref[0], ts_ref[0]):
            cp.start()
        tile_copy(0, 0).start()

        @pl.when(tf_ref[1] > 0)
        def _():
            tile_copy(1, 1).start()

    slot = lax.rem(t, ROW_TILE_SLOTS)
    ahead = jnp.minimum(t + 2, nt - 1)

    @pl.when(flag > 0)
    def _():
        tile_copy(t, slot).wait()

    @pl.when((t + 2 < nt) & (tf_ref[ahead] > 0))
    def _():
        tile_copy(ahead, lax.rem(t + 2, ROW_TILE_SLOTS)).start()

    @pl.when(flag == 2)
    def _():
        slot = ts_ref[t]
        for cp in weight_copies(te_ref[t], slot):
            cp.wait()
        w13b_ref[...] = w13f_ref[slot].astype(BF16)
        w2b_ref[...] = w2f_ref[slot].astype(BF16)

        @pl.when(tn_ref[t] >= 0)
        def _():
            for cp in weight_copies(tn_ref[t], 1 - slot):
                cp.start()

    @pl.when(flag > 0)
    def _():
        x = xbuf_ref[slot].astype(BF16)
        hid = jnp.dot(x, w13b_ref[...], preferred_element_type=F32)
        a = hid[:, :EXPERT_HIDDEN]
        act = a * jax.nn.sigmoid(a) * hid[:, EXPERT_HIDDEN:]
        ys_ref[...] = jnp.dot(act.astype(BF16), w2b_ref[...], preferred_element_type=F32)

    @pl.when(flag == 0)
    def _():
        ys_ref[...] = jnp.zeros_like(ys_ref)


def _experts(tile_flag, tile_expert, tile_next, tile_slot, xs, w13, w2):
    n_sorted, d = xs.shape
    te = EXPERT_TILE
    n_tiles = n_sorted // te
    assert n_tiles >= ROW_TILE_SLOTS
    hid2 = w13.shape[2]
    grid_spec = pltpu.PrefetchScalarGridSpec(
        num_scalar_prefetch=4,
        grid=(n_tiles,),
        in_specs=[pl.BlockSpec(memory_space=pl.ANY),
                  pl.BlockSpec(memory_space=pl.ANY),
                  pl.BlockSpec(memory_space=pl.ANY)],
        out_specs=pl.BlockSpec((te, d), lambda t, *_: (t, 0)),
        scratch_shapes=[pltpu.VMEM((ROW_TILE_SLOTS, te, d), F32),
                        pltpu.VMEM((2, d, hid2), F32),
                        pltpu.VMEM((2, hid2 // 2, d), F32),
                        pltpu.VMEM((d, hid2), BF16),
                        pltpu.VMEM((hid2 // 2, d), BF16),
                        pltpu.SemaphoreType.DMA((ROW_TILE_SLOTS,)),
                        pltpu.SemaphoreType.DMA((2,)),
                        pltpu.SemaphoreType.DMA((2,))],
    )
    return pl.pallas_call(
        _expert_kernel,
        out_shape=jax.ShapeDtypeStruct((n_sorted, d), F32),
        grid_spec=grid_spec,
        compiler_params=_params(("arbitrary",)),
        name="experts",
    )(tile_flag, tile_expert, tile_next, tile_slot, xs, w13, w2)


GATHER_SLOTS = 3


def _combine_kernel(pos_ref, ys_ref, x1_ref, rw_ref, g2_ref, fw_ref, o_ref, ya_ref, yb_ref, sem):
    i = pl.program_id(0)
    n_steps = pl.num_programs(0)
    tm = ya_ref.shape[1]
    n_tok = pos_ref.shape[0] // 2

    def issue(tile, slot):
        base = tile * tm
        dst_a = ya_ref.at[slot]
        dst_b = yb_ref.at[slot]
        pos_a = pos_ref.at[pl.ds(base, tm)]
        pos_b = pos_ref.at[pl.ds(n_tok + base, tm)]
        for r in range(tm):
            _row_copy(ys_ref, pos_a[r], dst_a, r, sem.at[slot]).start(priority=0)
            _row_copy(ys_ref, pos_b[r], dst_b, r, sem.at[slot]).start(priority=1)

    def wait(slot):
        pltpu.make_async_copy(ys_ref.at[pl.ds(0, tm), :], ya_ref.at[slot], sem.at[slot]).wait()
        pltpu.make_async_copy(ys_ref.at[pl.ds(0, tm), :], yb_ref.at[slot], sem.at[slot]).wait()

    @pl.when(i == 0)
    def _():
        issue(0, 0)
        issue(1, 1)

    slot = lax.rem(i, GATHER_SLOTS)
    ahead_slot = lax.rem(i + 2, GATHER_SLOTS)
    wait(slot)
    issue(jnp.minimum(i + 2, n_steps - 1), ahead_slot)
    rw = rw_ref[...]
    moe = rw[:, 0:1] * ya_ref[slot] + rw[:, 1:2] * yb_ref[slot]
    x2 = x1_ref[...] + g2_ref[0] * moe
    o_ref[...] = x2 * lax.rsqrt(jnp.mean(x2 * x2, axis=-1, keepdims=True) + NORM_EPS) * fw_ref[...]

    @pl.when(i == n_steps - 1)
    def _():
        wait(lax.rem(i + 1, GATHER_SLOTS))
        wait(ahead_slot)


def _combine(pos_flat, ys, x1, route_w, gate2, final_w, seq, tm):
    n, d = x1.shape
    per_b = seq // tm
    assert n // tm >= GATHER_SLOTS
    grid_spec = pltpu.PrefetchScalarGridSpec(
        num_scalar_prefetch=1,
        grid=(n // tm,),
        in_specs=[pl.BlockSpec(memory_space=pl.ANY),
                  pl.BlockSpec((tm, d), lambda i, pos: (i, 0)),
                  pl.BlockSpec((tm, 2), lambda i, pos: (i, 0)),
                  pl.BlockSpec((1, 1, d), lambda i, pos: (i // per_b, 0, 0)),
                  pl.BlockSpec((1, d), lambda i, pos: (0, 0))],
        out_specs=pl.BlockSpec((tm, d), lambda i, pos: (i, 0)),
        scratch_shapes=[pltpu.VMEM((GATHER_SLOTS, tm, d), F32),
                        pltpu.VMEM((GATHER_SLOTS, tm, d), F32),
                        pltpu.SemaphoreType.DMA((GATHER_SLOTS,))],
    )
    return pl.pallas_call(
        _combine_kernel,
        out_shape=jax.ShapeDtypeStruct((n, d), F32),
        grid_spec=grid_spec,
        compiler_params=_params(("arbitrary",)),
        name="combine",
    )(pos_flat, ys, x1, route_w, gate2, final_w.reshape(1, d))


def _rope_tables(seq):
    pos = np.arange(seq)
    nf = RET_QK_DIM // 4
    inv = (np.float32(ROPE_BASE) ** (-np.arange(nf, dtype=np.float32) / np.float32(nf))).astype(np.float32)
    ang_r = ((pos // GRID_W).astype(np.float32)[:, None] * inv[None, :]).astype(np.float64)
    ang_c = ((pos % GRID_W).astype(np.float32)[:, None] * inv[None, :]).astype(np.float64)
    return (jnp.asarray(np.concatenate([np.cos(ang_r), np.cos(ang_c)], axis=1), F32),
            jnp.asarray(np.concatenate([np.sin(ang_r), np.sin(ang_c)], axis=1), F32))


def _weight_prep_kernel(w_ref, o_ref, octx_ref, *, n_rot, ctx_cols):
    half = LANES // 2
    first = lax.broadcasted_iota(jnp.int32, (w_ref.shape[0], LANES), 1) < half
    for c0 in range(0, n_rot, RET_QK_DIM):
        a = w_ref[:, c0:c0 + LANES]
        b = w_ref[:, c0 + LANES:c0 + 2 * LANES]
        o_ref[:, c0:c0 + LANES] = jnp.where(first, a, pltpu.roll(b, half, axis=1)).astype(o_ref.dtype)
        o_ref[:, c0 + LANES:c0 + 2 * LANES] = jnp.where(first, pltpu.roll(a, half, axis=1), b).astype(o_ref.dtype)
    o_ref[:, n_rot:] = w_ref[:, n_rot:].astype(o_ref.dtype)
    octx_ref[...] = o_ref[:, ctx_cols[0]:ctx_cols[1]]


def _weight_prep(w_in, n_rot, ctx_cols):
    d, width = w_in.shape
    tr = 128
    ctx_width = ctx_cols[1] - ctx_cols[0]
    return pl.pallas_call(
        functools.partial(_weight_prep_kernel, n_rot=n_rot, ctx_cols=ctx_cols),
        out_shape=(jax.ShapeDtypeStruct((d, width), BF16),
                   jax.ShapeDtypeStruct((d, ctx_width), BF16)),
        grid=(d // tr,),
        in_specs=[pl.BlockSpec((tr, width), lambda i: (i, 0))],
        out_specs=(pl.BlockSpec((tr, width), lambda i: (i, 0)),
                   pl.BlockSpec((tr, ctx_width), lambda i: (i, 0))),
        compiler_params=_params(("parallel",)),
        name="wprep",
    )(w_in)


def _layer(x, c, ctx, c_ctx, w_mod, b_mod, norm1_w, norm2_w, w_in, log_decay, gn_w, w_ret_o,
           conv_w, w_conv_o, w_out, rg_w, rg_b, re_w, re_b, w13, w2, final_w):
    batch, seq, d = x.shape
    ctx_len = ctx.shape[1]
    n = batch * seq
    off_q = 0
    off_k = off_q + RET_QK_W
    off_v = off_k + RET_QK_W
    off_g = off_v + RET_V_W
    off_cb = off_g + RET_V_W

    pad = (-(batch + 1)) % 8
    c_all = jnp.concatenate([c, c_ctx[None, :], jnp.zeros((pad, d), F32)], axis=0)
    mod = _modulation(c_all, w_mod, b_mod)
    mx = mod[:batch].reshape(batch, 1, 6, d)
    shift1, scale1, gate1, shift2, scale2, gate2 = (mx[:, :, k, :] for k in range(6))
    mc = mod[batch].reshape(1, 1, 6, d)
    shift1c, scale1c = mc[:, :, 0, :], mc[:, :, 1, :]

    w_in_b, w_ctx_b = _weight_prep(w_in, off_v, (off_k, off_g))
    cos_t, sin_t = _rope_tables(seq)
    k_scale = RET_QK_DIM ** -0.5

    tm_proj = min(512, seq)
    main_modes = (("rope", 1.0), ("rope", k_scale), ("plain", None), ("plain", None),
                  ("swish_gain", 0), ("swish_gain", 1), ("plain", None), ("plain", None), ("plain", None),
                  ("plain", None), ("plain", None))
    gain = gn_w.reshape(1, RET_V_W)
    proj = _projection(x.reshape(n, d), norm1_w, shift1, scale1, cos_t, sin_t, gain, w_in_b, seq,
                       modes=main_modes, tm=tm_proj)
    tm_ctx = batch * ctx_len
    no_rope = jnp.zeros((tm_ctx, LANES), F32)
    kv_ctx = _projection(ctx.reshape(tm_ctx, d), norm1_w, shift1c, scale1c, no_rope, no_rope, gain,
                         w_ctx_b, tm_ctx,
                         modes=(("scale", k_scale), ("plain", None), ("plain", None)), tm=tm_ctx)

    log_gamma = jnp.log1p(-jnp.exp(log_decay.astype(F32)))
    yg = _retention(proj, kv_ctx, log_gamma, batch, seq, ctx_len, off_q, off_k, off_v, off_g)

    w_router = jnp.concatenate(
        [re_w, rg_w, jnp.zeros((d, LANES - N_EXPERTS - N_GROUPS), F32)], axis=1)
    b_router = jnp.concatenate(
        [re_b, rg_b, jnp.zeros((LANES - N_EXPERTS - N_GROUPS,), F32)]).reshape(LANES, 1)
    w_router_hi = w_router.astype(BF16)
    w_router_lo = (w_router - w_router_hi.astype(F32)).astype(BF16)
    w_router = jnp.concatenate([w_router_hi, w_router_lo], axis=1).T
    tm_mix = min(512, seq)
    x1, h2, route_i, route_w, counts = _mix(
        yg, proj, x.reshape(n, d), w_ret_o.astype(BF16), w_conv_o.astype(BF16), w_out.astype(BF16),
        conv_w, gate1, shift2, scale2, norm2_w, w_router, b_router, seq, off_cb, tm_mix)

    te = EXPERT_TILE
    cnt = counts[:, 0].astype(jnp.int32)
    padded = ((cnt + te - 1) // te) * te
    ends = jnp.cumsum(padded)
    starts = ends - padded
    expert_ids = jnp.arange(N_EXPERTS, dtype=jnp.int32)
    route_t = route_i[0:4]
    seg_start = jnp.sum(jnp.where(route_t[0:2, :, None] == expert_ids, starts, 0), axis=-1)
    pos_flat = (seg_start + route_t[2:4]).reshape(-1).astype(jnp.int32)
    n_tiles = (2 * n) // te + N_EXPERTS
    tile_row = jnp.arange(n_tiles, dtype=jnp.int32) * te
    tile_expert = jnp.minimum(jnp.sum((tile_row[:, None] >= ends[None, :]).astype(jnp.int32), axis=1),
                              N_EXPERTS - 1).astype(jnp.int32)
    new_expert = jnp.concatenate([jnp.ones((1,), bool), tile_expert[1:] != tile_expert[:-1]])
    tile_flag = jnp.where(tile_row < ends[-1], jnp.where(new_expert, 2, 1), 0).astype(jnp.int32)

    valid_tiles = ends[-1] // te
    last_tile = jnp.where(padded > 0, ends // te - 1, -1)
    tail_tile = valid_tiles + jnp.arange(N_EXPERTS, dtype=jnp.int32)
    tail_tile = jnp.where(tail_tile < n_tiles, tail_tile, -1)
    zero_tiles = jnp.concatenate([last_tile, tail_tile]).astype(jnp.int32)

    xs = _scatter_rows(pos_flat, zero_tiles, h2, n_tiles * te, min(512, seq // 2))
    nonempty = padded > 0
    later = nonempty[None, :] & (expert_ids[None, :] > expert_ids[:, None])
    next_expert = jnp.min(jnp.where(later, expert_ids[None, :], N_EXPERTS), axis=1)
    next_expert = jnp.where(next_expert == N_EXPERTS, -1, next_expert)
    slot = (jnp.cumsum(nonempty.astype(jnp.int32)) - 1) % 2
    tile_hot = tile_expert[:, None] == expert_ids[None, :]
    tile_next = jnp.sum(jnp.where(tile_hot, next_expert, 0), axis=1).astype(jnp.int32)
    tile_slot = jnp.sum(jnp.where(tile_hot, slot, 0), axis=1).astype(jnp.int32)

    ys = _experts(tile_flag, tile_expert, tile_next, tile_slot, xs, w13, w2)
    out = _combine(pos_flat, ys, x1, route_w[0:2].T, gate2, final_w, seq, min(256, seq // 2))
    return out.reshape(batch, seq, d)


def kernel(x, c, ctx, c_ctx, w_mod, b_mod, norm1_w, norm2_w, w_in, ret_log_decay, ret_gn_w, w_ret_o,
           conv_w, w_conv_o, w_out, router_group_w, router_group_b, router_expert_w, router_expert_b,
           expert_w13, expert_w2, final_norm_w):
    assert w_mod.shape[0] == 1, "single-layer problem"
    return _layer(x, c, ctx, c_ctx, w_mod[0], b_mod[0], norm1_w[0], norm2_w[0], w_in[0],
                  ret_log_decay[0], ret_gn_w[0], w_ret_o[0], conv_w[0], w_conv_o[0], w_out[0],
                  router_group_w[0], router_group_b[0], router_expert_w[0], router_expert_b[0],
                  expert_w13[0], expert_w2[0], final_norm_w)
```

```python
import functools

import jax
import jax.numpy as jnp
import numpy as np
from jax import lax
from jax.experimental import pallas as pl
from jax.experimental.pallas import tpu as pltpu

GRID_W = 64
RET_HEADS = 4
RET_QK_DIM = 256
RET_V_DIM = 512
RET_QK_W = RET_HEADS * RET_QK_DIM
RET_V_W = RET_HEADS * RET_V_DIM
N_GROUPS = 4
EXPERTS_PER_GROUP = 8
N_EXPERTS = N_GROUPS * EXPERTS_PER_GROUP
EXPERT_HIDDEN = 512
ROPE_BASE = 10000.0
NORM_EPS = 1e-6
GN_EPS = 1e-5

RET_CHUNK = 256
LANES = 128
EXPERT_TILE = 256
TOKEN_TILE = 512
COMBINE_TILE = 256
COL_CHUNK = 1024
VMEM_LIMIT = 56 * 1024 * 1024

BF16 = jnp.bfloat16
F32 = jnp.float32


def _params(sem, vmem=VMEM_LIMIT):
    return pltpu.CompilerParams(dimension_semantics=sem, vmem_limit_bytes=vmem)


def _mod_kernel(c_ref, w_ref, b_ref, o_ref):
    c = c_ref[...]
    s = c * jax.nn.sigmoid(c)
    o_ref[...] = jnp.dot(s.astype(BF16), w_ref[...].astype(BF16),
                         preferred_element_type=F32) + b_ref[...]


def _modulation(c_all, w_mod, b_mod):
    rows, d = c_all.shape
    n_out = w_mod.shape[1]
    tn = 3072
    return pl.pallas_call(
        _mod_kernel,
        out_shape=jax.ShapeDtypeStruct((rows, n_out), F32),
        grid=(n_out // tn,),
        in_specs=[pl.BlockSpec((rows, d), lambda j: (0, 0)),
                  pl.BlockSpec((d, tn), lambda j: (0, j)),
                  pl.BlockSpec((1, tn), lambda j: (0, j))],
        out_specs=pl.BlockSpec((rows, tn), lambda j: (0, j)),
        compiler_params=_params(("parallel",)),
        name="mod",
    )(c_all, w_mod, b_mod.reshape(1, n_out))


def _rope_store(acc, cos, sin, scale, o_ref):
    for hd in range(acc.shape[1] // RET_QK_DIM):
        lo = hd * RET_QK_DIM
        u1 = acc[:, lo:lo + LANES]
        u2 = acc[:, lo + LANES:lo + 2 * LANES]
        r1 = u1 * cos - u2 * sin
        r2 = u1 * sin + u2 * cos
        if scale != 1.0:
            r1 = r1 * scale
            r2 = r2 * scale
        o_ref[:, lo:lo + LANES] = r1.astype(o_ref.dtype)
        o_ref[:, lo + LANES:lo + 2 * LANES] = r2.astype(o_ref.dtype)


def _proj_kernel(x_ref, nw_ref, sh_ref, sc_ref, cos_ref, sin_ref, gn_ref, w_ref, o_ref, *, modes, tn):
    x = x_ref[...]
    y = x * lax.rsqrt(jnp.mean(x * x, axis=-1, keepdims=True) + NORM_EPS) * nw_ref[...]
    h = (y * (1.0 + sc_ref[0]) + sh_ref[0]).astype(BF16)
    assert len(modes) * tn == w_ref.shape[1]
    for j, (kind, arg) in enumerate(modes):
        cols = slice(j * tn, (j + 1) * tn)
        acc = jnp.dot(h, w_ref[:, cols], preferred_element_type=F32)
        if kind == "rope":
            _rope_store(acc, cos_ref[...], sin_ref[...], arg, o_ref.at[:, cols])
        elif kind == "scale":
            o_ref[:, cols] = (acc * arg).astype(o_ref.dtype)
        elif kind == "swish_gain":
            gain = gn_ref[:, arg * tn:(arg + 1) * tn]
            o_ref[:, cols] = (acc * jax.nn.sigmoid(acc) * gain).astype(o_ref.dtype)
        else:
            o_ref[:, cols] = acc.astype(o_ref.dtype)


def _projection(x2d, norm_w, shift, scale, cos_t, sin_t, gain, w_bf16, seq, modes, tm):
    n, d = x2d.shape
    width = w_bf16.shape[1]
    tn = COL_CHUNK
    per_b = seq // tm
    nb = shift.shape[0]
    bidx = (lambda i: (i // per_b, 0, 0)) if nb > 1 else (lambda i: (0, 0, 0))
    return pl.pallas_call(
        functools.partial(_proj_kernel, modes=modes, tn=tn),
        out_shape=jax.ShapeDtypeStruct((n, width), BF16),
        grid=(n // tm,),
        in_specs=[pl.BlockSpec((tm, d), lambda i: (i, 0)),
                  pl.BlockSpec((1, d), lambda i: (0, 0)),
                  pl.BlockSpec((1, 1, d), bidx),
                  pl.BlockSpec((1, 1, d), bidx),
                  pl.BlockSpec((tm, LANES), lambda i: (i % per_b, 0)),
                  pl.BlockSpec((tm, LANES), lambda i: (i % per_b, 0)),
                  pl.BlockSpec(gain.shape, lambda i: (0, 0)),
                  pl.BlockSpec((d, width), lambda i: (0, 0), pipeline_mode=pl.Buffered(1))],
        out_specs=pl.BlockSpec((tm, width), lambda i: (i, 0)),
        compiler_params=_params(("parallel",)),
        name="proj",
    )(x2d, norm_w.reshape(1, d), shift, scale, cos_t, sin_t, gain, w_bf16)


def _dot_t0(a, b):
    return lax.dot_general(a, b, (((0,), (0,)), ((), ())), preferred_element_type=F32)


def _ret_kernel(lg_ref, q_ref, k_ref, v_ref, g_ref, kc_ref, vc_ref, o_ref,
                acc_ref, sf_ref, sb_ref, dm_ref, *, seq, ctx_len):
    h = pl.program_id(1)
    lgf = lg_ref[0, h]
    lgb = lg_ref[1, h]
    c = RET_CHUNK
    n_chunks = seq // c

    cpos = lax.broadcasted_iota(jnp.int32, (ctx_len, 1), 0).astype(F32)
    kc = kc_ref[...].astype(F32)
    vc = vc_ref[...]
    sf_ref[...] = _dot_t0((kc * jnp.exp((ctx_len - 1.0 - cpos) * lgf)).astype(BF16), vc)
    sb_ref[...] = _dot_t0((kc * jnp.exp(cpos * lgb)).astype(BF16), vc)

    ri = lax.broadcasted_iota(jnp.int32, (c, c), 0)
    ci = lax.broadcasted_iota(jnp.int32, (c, c), 1)
    dist = (ri - ci).astype(F32)
    dm_ref[...] = jnp.exp(jnp.abs(dist) * jnp.where(dist >= 0, lgf, lgb))

    idx = lax.broadcasted_iota(jnp.int32, (c, 1), 0).astype(F32)
    xi_f = jnp.exp((idx + 1.0) * lgf)
    zeta_f = jnp.exp((c - 1.0 - idx) * lgf)
    xi_b = jnp.exp((c - idx) * lgb)
    zeta_b = jnp.exp(idx * lgb)
    cd_f = jnp.exp(jnp.full((1, 1), float(c), F32) * lgf)
    cd_b = jnp.exp(jnp.full((1, 1), float(c), F32) * lgb)

    def chunk_rows(ic):
        return pl.ds(pl.multiple_of(ic * c, c), c)

    def fwd_part(ic):
        rows = chunk_rows(ic)
        q = q_ref[rows, :]
        k = k_ref[rows, :]
        v = v_ref[rows, :]
        s = lax.dot_general(q, k, (((1,), (1,)), ((), ())), preferred_element_type=F32)
        intra = jnp.dot((s * dm_ref[...]).astype(BF16), v, preferred_element_type=F32)
        inter = jnp.dot(q, sf_ref[...].astype(BF16), preferred_element_type=F32)
        kz = (k.astype(F32) * zeta_f).astype(BF16)
        sf_ref[...] = cd_f * sf_ref[...] + _dot_t0(kz, v)
        return intra + xi_f * inter

    def bwd_part(ic):
        rows = chunk_rows(ic)
        q = q_ref[rows, :]
        k = k_ref[rows, :]
        v = v_ref[rows, :]
        inter = jnp.dot(q, sb_ref[...].astype(BF16), preferred_element_type=F32)
        kz = (k.astype(F32) * zeta_b).astype(BF16)
        sb_ref[...] = cd_b * sb_ref[...] + _dot_t0(kz, v)
        return xi_b * inter

    def finalize(ic, y):
        rows = chunk_rows(ic)
        mu = jnp.mean(y, axis=-1, keepdims=True)
        yc = y - mu
        var = jnp.mean(yc * yc, axis=-1, keepdims=True)
        o_ref[rows, :] = (g_ref[rows, :].astype(F32) * (yc * lax.rsqrt(var + GN_EPS))).astype(o_ref.dtype)

    half = n_chunks // 2

    def first_half(t, carry):
        acc_ref[chunk_rows(t), :] = fwd_part(t)
        acc_ref[chunk_rows(n_chunks - 1 - t), :] = bwd_part(n_chunks - 1 - t)
        return carry

    def second_half(t, carry):
        finalize(t, acc_ref[chunk_rows(t), :] + fwd_part(t))
        u = n_chunks - 1 - t
        finalize(u, acc_ref[chunk_rows(u), :] + bwd_part(u))
        return carry

    lax.fori_loop(0, half, first_half, 0, unroll=True)
    lax.fori_loop(half, n_chunks, second_half, 0, unroll=min(4, half))


def _retention(proj, kv_ctx, log_gamma, batch, seq, ctx_len, off_q, off_k, off_v, off_g):
    n = proj.shape[0]
    dk, dv = RET_QK_DIM, RET_V_DIM
    assert seq % (2 * RET_CHUNK) == 0, "the paired scan needs an even number of chunks"
    grid_spec = pltpu.PrefetchScalarGridSpec(
        num_scalar_prefetch=1,
        grid=(batch, RET_HEADS),
        in_specs=[pl.BlockSpec((seq, dk), lambda b, h, lg: (b, off_q // dk + h)),
                  pl.BlockSpec((seq, dk), lambda b, h, lg: (b, off_k // dk + h)),
                  pl.BlockSpec((seq, dv), lambda b, h, lg: (b, off_v // dv + h)),
                  pl.BlockSpec((seq, dv), lambda b, h, lg: (b, off_g // dv + h)),
                  pl.BlockSpec((ctx_len, dk), lambda b, h, lg: (b, h)),
                  pl.BlockSpec((ctx_len, dv), lambda b, h, lg: (b, RET_QK_W // dv + h))],
        out_specs=pl.BlockSpec((seq, dv), lambda b, h, lg: (b, h)),
        scratch_shapes=[pltpu.VMEM((seq, dv), F32),
                        pltpu.VMEM((dk, dv), F32),
                        pltpu.VMEM((dk, dv), F32),
                        pltpu.VMEM((RET_CHUNK, RET_CHUNK), F32)],
    )
    return pl.pallas_call(
        functools.partial(_ret_kernel, seq=seq, ctx_len=ctx_len),
        out_shape=jax.ShapeDtypeStruct((n, RET_V_W), BF16),
        grid_spec=grid_spec,
        compiler_params=_params(("parallel", "arbitrary")),
        name="ret",
    )(log_gamma, proj, proj, proj, proj, kv_ctx, kv_ctx)


def _mix_kernel(yg_ref, cb_ref, cc_ref, ch_ref, gr_ref, gc_ref, x_ref, wro_ref, wco_ref, wo_ref,
                cw_ref, g1_ref, sh2_ref, sc2_ref, n2_ref, wr_ref, br_ref,
                x1_ref, h2_ref, ri_ref, rw_ref, cnt_ref, carry_ref):
    i = pl.program_id(0)
    tm = x_ref.shape[0]

    @pl.when(i == 0)
    def _():
        carry_ref[...] = jnp.zeros_like(carry_ref)

    ret_branch = jnp.dot(yg_ref[...], wro_ref[...], preferred_element_type=F32)

    p = cc_ref[...].astype(F32) * ch_ref[...].astype(F32)
    tpos = lax.broadcasted_iota(jnp.int32, (tm, 1), 0) % GRID_W
    prev = jnp.where(tpos != 0, pltpu.roll(p, 1, axis=0), 0.0)
    nxt = jnp.where(tpos != GRID_W - 1, pltpu.roll(p, tm - 1, axis=0), 0.0)
    cw = cw_ref[...]
    u = cw[0:1, :] * prev + cw[1:2, :] * p + cw[2:3, :] * nxt
    conv_in = (cb_ref[...].astype(F32) * u).astype(BF16)
    conv_branch = jnp.dot(conv_in, wco_ref[...], preferred_element_type=F32)

    merged = (jax.nn.sigmoid(gr_ref[...].astype(F32)) * ret_branch
              + jax.nn.sigmoid(gc_ref[...].astype(F32)) * conv_branch)
    mixed = jnp.dot(merged.astype(BF16), wo_ref[...], preferred_element_type=F32)
    x1 = x_ref[...] + g1_ref[0] * mixed
    x1_ref[...] = x1

    y = x1 * lax.rsqrt(jnp.mean(x1 * x1, axis=-1, keepdims=True) + NORM_EPS) * n2_ref[...]
    h2 = y * (1.0 + sc2_ref[0]) + sh2_ref[0]
    h2_ref[...] = h2

    h_hi = h2.astype(BF16)
    h_lo = (h2 - h_hi.astype(F32)).astype(BF16)
    nt_dims = (((1,), (1,)), ((), ()))
    hw = lax.dot_general(wr_ref[...], h_hi, nt_dims, preferred_element_type=F32)
    lw = lax.dot_general(wr_ref[:LANES, :], h_lo, nt_dims, preferred_element_type=F32)
    logits_t = hw[:LANES] + (hw[LANES:] + lw) + br_ref[...]
    neg = jnp.float32(-jnp.inf)
    big = jnp.int32(1 << 20)

    grp = logits_t[N_EXPERTS:N_EXPERTS + N_GROUPS]
    grow = lax.broadcasted_iota(jnp.int32, grp.shape, 0)
    gmax = jnp.max(grp, axis=0, keepdims=True)
    gidx = jnp.min(jnp.where(grp == gmax, grow, big), axis=0, keepdims=True)
    g_w = 1.0 / jnp.sum(jnp.exp(grp - gmax), axis=0, keepdims=True)

    el_all = logits_t[:N_EXPERTS]
    erow = lax.broadcasted_iota(jnp.int32, el_all.shape, 0)
    in_grp = (erow // EXPERTS_PER_GROUP) == gidx
    el = jnp.where(in_grp, el_all, neg)
    emax = jnp.max(el, axis=0, keepdims=True)
    i1 = jnp.min(jnp.where(el == emax, erow, big), axis=0, keepdims=True)
    el2 = jnp.where(erow == i1, neg, el)
    l2 = jnp.max(el2, axis=0, keepdims=True)
    i2 = jnp.min(jnp.where(el2 == l2, erow, big), axis=0, keepdims=True)
    esum = jnp.sum(jnp.where(in_grp, jnp.exp(el_all - emax), 0.0), axis=0, keepdims=True)
    p1 = 1.0 / esum
    p2 = jnp.exp(l2 - emax) / esum
    den = p1 + p2
    c1 = g_w * (p1 / den)
    c2 = g_w * (p2 / den)

    hot1 = erow == i1
    hot2 = erow == i2
    onehot = jnp.where(hot1, 1.0, jnp.where(hot2, 1.0, 0.0))
    earlier = (lax.broadcasted_iota(jnp.int32, (tm, tm), 0)
               < lax.broadcasted_iota(jnp.int32, (tm, tm), 1))
    ranks = jnp.dot(onehot.astype(BF16), jnp.where(earlier, 1.0, 0.0).astype(BF16),
                    preferred_element_type=F32) + carry_ref[...]
    r1 = jnp.sum(jnp.where(hot1, ranks, 0.0), axis=0, keepdims=True).astype(jnp.int32)
    r2 = jnp.sum(jnp.where(hot2, ranks, 0.0), axis=0, keepdims=True).astype(jnp.int32)
    carry_ref[...] = carry_ref[...] + jnp.sum(onehot, axis=1, keepdims=True)
    cnt_ref[...] = carry_ref[...]

    row8 = lax.broadcasted_iota(jnp.int32, (8, tm), 0)
    ri_ref[...] = jnp.where(row8 == 0, i1, jnp.where(row8 == 1, i2,
                            jnp.where(row8 == 2, r1, jnp.where(row8 == 3, r2, 0))))
    rw_ref[...] = jnp.where(row8 == 0, c1, jnp.where(row8 == 1, c2, 0.0))


def _mix(yg, proj, x2d, w_ret_o, w_conv_o, w_out, conv_w, gate1, shift2, scale2, norm2_w,
         w_router, b_router, seq, off_cb, tm):
    n, d = x2d.shape
    per_b = seq // tm
    cblk = off_cb // d
    row = lambda i: (i, 0)
    const2 = lambda i: (0, 0)
    bidx = lambda i: (i // per_b, 0, 0)
    col = lambda k: (lambda i: (i, cblk + k))
    return pl.pallas_call(
        _mix_kernel,
        out_shape=(jax.ShapeDtypeStruct((n, d), F32),
                   jax.ShapeDtypeStruct((n, d), F32),
                   jax.ShapeDtypeStruct((8, n), jnp.int32),
                   jax.ShapeDtypeStruct((8, n), F32),
                   jax.ShapeDtypeStruct((N_EXPERTS, 1), F32)),
        grid=(n // tm,),
        in_specs=[pl.BlockSpec((tm, RET_V_W), row),
                  pl.BlockSpec((tm, d), col(0)),
                  pl.BlockSpec((tm, d), col(1)),
                  pl.BlockSpec((tm, d), col(2)),
                  pl.BlockSpec((tm, d), col(3)),
                  pl.BlockSpec((tm, d), col(4)),
                  pl.BlockSpec((tm, d), row),
                  pl.BlockSpec((RET_V_W, d), const2, pipeline_mode=pl.Buffered(1)),
                  pl.BlockSpec((d, d), const2, pipeline_mode=pl.Buffered(1)),
                  pl.BlockSpec((d, d), const2, pipeline_mode=pl.Buffered(1)),
                  pl.BlockSpec((3, d), const2),
                  pl.BlockSpec((1, 1, d), bidx),
                  pl.BlockSpec((1, 1, d), bidx),
                  pl.BlockSpec((1, 1, d), bidx),
                  pl.BlockSpec((1, d), const2),
                  pl.BlockSpec((2 * LANES, d), const2),
                  pl.BlockSpec((LANES, 1), const2)],
        out_specs=(pl.BlockSpec((tm, d), row),
                   pl.BlockSpec((tm, d), row),
                   pl.BlockSpec((8, tm), lambda i: (0, i)),
                   pl.BlockSpec((8, tm), lambda i: (0, i)),
                   pl.BlockSpec((N_EXPERTS, 1), const2)),
        scratch_shapes=[pltpu.VMEM((N_EXPERTS, 1), F32)],
        compiler_params=_params(("arbitrary",)),
        name="mix",
    )(yg, proj, proj, proj, proj, proj, x2d, w_ret_o, w_conv_o, w_out, conv_w,
      gate1, shift2, scale2, norm2_w.reshape(1, d), w_router, b_router)


def _row_copy(src_ref, src_row, dst_ref, dst_row, sem):
    return pltpu.make_async_copy(src_ref.at[pl.ds(src_row, 1), :],
                                 dst_ref.at[pl.ds(dst_row, 1), :], sem)


SCATTER_SLOTS = 3


def _scatter_kernel(pos_ref, zt_ref, h_ref, xs_ref, hbuf_ref, zero_ref, lsem, ssem, zsem):
    i = pl.program_id(0)
    n_steps = pl.num_programs(0)
    tm = hbuf_ref.shape[1]
    te = zero_ref.shape[0]
    base = i * tm
    n_tok = pos_ref.shape[0] // 2

    def load(tile, slot):
        src = h_ref.at[pl.ds(pl.multiple_of(tile * tm, tm), tm), :]
        return pltpu.make_async_copy(src, hbuf_ref.at[slot], lsem.at[slot])

    def wait_rows(slot):
        for _ in range(2):
            pltpu.make_async_copy(hbuf_ref.at[slot], xs_ref.at[pl.ds(0, tm), :], ssem.at[slot]).wait()

    @pl.when(i == 0)
    def _():
        load(0, 0).start()
        load(1, 1).start()
        zero_ref[...] = jnp.zeros_like(zero_ref)

        def tile_copy(z):
            row = pl.multiple_of(zt_ref[z] * te, te)
            return pltpu.make_async_copy(zero_ref, xs_ref.at[pl.ds(row, te), :], zsem)

        def zissue(z, carry):
            @pl.when(zt_ref[z] >= 0)
            def _():
                tile_copy(z).start()
            return carry

        def zdrain(z, carry):
            @pl.when(zt_ref[z] >= 0)
            def _():
                tile_copy(z).wait()
            return carry

        lax.fori_loop(0, zt_ref.shape[0], zissue, 0)
        lax.fori_loop(0, zt_ref.shape[0], zdrain, 0)

    slot = lax.rem(i, SCATTER_SLOTS)
    reload_slot = lax.rem(i + 2, SCATTER_SLOTS)
    load(i, slot).wait()
    src = hbuf_ref.at[slot]
    pos_a = pos_ref.at[pl.ds(base, tm)]
    pos_b = pos_ref.at[pl.ds(n_tok + base, tm)]
    for r in range(tm):
        _row_copy(src, r, xs_ref, pos_a[r], ssem.at[slot]).start(priority=0)
        _row_copy(src, r, xs_ref, pos_b[r], ssem.at[slot]).start(priority=1)

    @pl.when(i > 0)
    def _():
        wait_rows(reload_slot)

    load(jnp.minimum(i + 2, n_steps - 1), reload_slot).start()

    @pl.when(i == n_steps - 1)
    def _():
        wait_rows(slot)
        load(i, lax.rem(i + 1, SCATTER_SLOTS)).wait()
        load(i, reload_slot).wait()


def _scatter_rows(pos_flat, zero_tiles, h2, n_sorted, tm):
    n, d = h2.shape
    assert n // tm >= SCATTER_SLOTS
    grid_spec = pltpu.PrefetchScalarGridSpec(
        num_scalar_prefetch=2,
        grid=(n // tm,),
        in_specs=[pl.BlockSpec(memory_space=pl.ANY)],
        out_specs=pl.BlockSpec(memory_space=pl.ANY),
        scratch_shapes=[pltpu.VMEM((SCATTER_SLOTS, tm, d), F32),
                        pltpu.VMEM((EXPERT_TILE, d), F32),
                        pltpu.SemaphoreType.DMA((SCATTER_SLOTS,)),
                        pltpu.SemaphoreType.DMA((SCATTER_SLOTS,)),
                        pltpu.SemaphoreType.DMA],
    )
    return pl.pallas_call(
        _scatter_kernel,
        out_shape=jax.ShapeDtypeStruct((n_sorted, d), F32),
        grid_spec=grid_spec,
        compiler_params=_params(("arbitrary",)),
        name="scatter",
    )(pos_flat, zero_tiles, h2)


ROW_TILE_SLOTS = 3


def _expert_kernel(tf_ref, te_ref, tn_ref, ts_ref, xs_ref, w13_ref, w2_ref, ys_ref,
                   xbuf_ref, w13f_ref, w2f_ref, w13b_ref, w2b_ref, xsem, sem13, sem2):
    t = pl.program_id(0)
    nt = pl.num_programs(0)
    rows = xbuf_ref.shape[1]
    flag = tf_ref[t]

    def tile_copy(tile, slot):
        src = xs_ref.at[pl.ds(pl.multiple_of(tile * rows, rows), rows), :]
        return pltpu.make_async_copy(src, xbuf_ref.at[slot], xsem.at[slot])

    def weight_copies(e, slot):
        return (pltpu.make_async_copy(w13_ref.at[e], w13f_ref.at[slot], sem13.at[slot]),
                pltpu.make_async_copy(w2_ref.at[e], w2f_ref.at[slot], sem2.at[slot]))

    @pl.when(t == 0)
    def _():
        for cp in weight_copies(te_ref[0], ts_ref[0]):
            cp.start()
        tile_copy(0, 0).start()

        @pl.when(tf_ref[1] > 0)
        def _():
            tile_copy(1, 1).start()

    slot = lax.rem(t, ROW_TILE_SLOTS)
    ahead = jnp.minimum(t + 2, nt - 1)

    @pl.when(flag > 0)
    def _():
        tile_copy(t, slot).wait()

    @pl.when((t + 2 < nt) & (tf_ref[ahead] > 0))
    def _():
        tile_copy(ahead, lax.rem(t + 2, ROW_TILE_SLOTS)).start()

    @pl.when(flag == 2)
    def _():
        slot = ts_ref[t]
        for cp in weight_copies(te_ref[t], slot):
            cp.wait()
        w13b_ref[...] = w13f_ref[slot].astype(BF16)
        w2b_ref[...] = w2f_ref[slot].astype(BF16)

        @pl.when(tn_ref[t] >= 0)
        def _():
            for cp in weight_copies(tn_ref[t], 1 - slot):
                cp.start()

    @pl.when(flag > 0)
    def _():
        x = xbuf_ref[slot].astype(BF16)
        hid = jnp.dot(x, w13b_ref[...], preferred_element_type=F32)
        a = hid[:, :EXPERT_HIDDEN]
        act = a * jax.nn.sigmoid(a) * hid[:, EXPERT_HIDDEN:]
        ys_ref[...] = jnp.dot(act.astype(BF16), w2b_ref[...], preferred_element_type=F32)

    @pl.when(flag == 0)
    def _():
        ys_ref[...] = jnp.zeros_like(ys_ref)


def _experts(tile_flag, tile_expert, tile_next, tile_slot, xs, w13, w2):
    n_sorted, d = xs.shape
    te = EXPERT_TILE
    n_tiles = n_sorted // te
    assert n_tiles >= ROW_TILE_SLOTS
    hid2 = w13.shape[2]
    grid_spec = pltpu.PrefetchScalarGridSpec(
        num_scalar_prefetch=4,
        grid=(n_tiles,),
        in_specs=[pl.BlockSpec(memory_space=pl.ANY),
                  pl.BlockSpec(memory_space=pl.ANY),
                  pl.BlockSpec(memory_space=pl.ANY)],
        out_specs=pl.BlockSpec((te, d), lambda t, *_: (t, 0)),
        scratch_shapes=[pltpu.VMEM((ROW_TILE_SLOTS, te, d), F32),
                        pltpu.VMEM((2, d, hid2), F32),
                        pltpu.VMEM((2, hid2 // 2, d), F32),
                        pltpu.VMEM((d, hid2), BF16),
                        pltpu.VMEM((hid2 // 2, d), BF16),
                        pltpu.SemaphoreType.DMA((ROW_TILE_SLOTS,)),
                        pltpu.SemaphoreType.DMA((2,)),
                        pltpu.SemaphoreType.DMA((2,))],
    )
    return pl.pallas_call(
        _expert_kernel,
        out_shape=jax.ShapeDtypeStruct((n_sorted, d), F32),
        grid_spec=grid_spec,
        compiler_params=_params(("arbitrary",)),
        name="experts",
    )(tile_flag, tile_expert, tile_next, tile_slot, xs, w13, w2)


GATHER_SLOTS = 3


def _combine_kernel(pos_ref, ys_ref, x1_ref, rw_ref, g2_ref, fw_ref, o_ref, ya_ref, yb_ref, sem):
    i = pl.program_id(0)
    n_steps = pl.num_programs(0)
    tm = ya_ref.shape[1]
    n_tok = pos_ref.shape[0] // 2

    def issue(tile, slot):
        base = tile * tm
        dst_a = ya_ref.at[slot]
        dst_b = yb_ref.at[slot]
        pos_a = pos_ref.at[pl.ds(base, tm)]
        pos_b = pos_ref.at[pl.ds(n_tok + base, tm)]
        for r in range(tm):
            _row_copy(ys_ref, pos_a[r], dst_a, r, sem.at[slot]).start(priority=0)
            _row_copy(ys_ref, pos_b[r], dst_b, r, sem.at[slot]).start(priority=1)

    def wait(slot):
        pltpu.make_async_copy(ys_ref.at[pl.ds(0, tm), :], ya_ref.at[slot], sem.at[slot]).wait()
        pltpu.make_async_copy(ys_ref.at[pl.ds(0, tm), :], yb_ref.at[slot], sem.at[slot]).wait()

    @pl.when(i == 0)
    def _():
        issue(0, 0)
        issue(1, 1)

    slot = lax.rem(i, GATHER_SLOTS)
    ahead_slot = lax.rem(i + 2, GATHER_SLOTS)
    wait(slot)
    issue(jnp.minimum(i + 2, n_steps - 1), ahead_slot)
    rw = rw_ref[...]
    moe = rw[:, 0:1] * ya_ref[slot] + rw[:, 1:2] * yb_ref[slot]
    x2 = x1_ref[...] + g2_ref[0] * moe
    o_ref[...] = x2 * lax.rsqrt(jnp.mean(x2 * x2, axis=-1, keepdims=True) + NORM_EPS) * fw_ref[...]

    @pl.when(i == n_steps - 1)
    def _():
        wait(lax.rem(i + 1, GATHER_SLOTS))
        wait(ahead_slot)


def _combine(pos_flat, ys, x1, route_w, gate2, final_w, seq, tm):
    n, d = x1.shape
    per_b = seq // tm
    assert n // tm >= GATHER_SLOTS
    grid_spec = pltpu.PrefetchScalarGridSpec(
        num_scalar_prefetch=1,
        grid=(n // tm,),
        in_specs=[pl.BlockSpec(memory_space=pl.ANY),
                  pl.BlockSpec((tm, d), lambda i, pos: (i, 0)),
                  pl.BlockSpec((tm, 2), lambda i, pos: (i, 0)),
                  pl.BlockSpec((1, 1, d), lambda i, pos: (i // per_b, 0, 0)),
                  pl.BlockSpec((1, d), lambda i, pos: (0, 0))],
        out_specs=pl.BlockSpec((tm, d), lambda i, pos: (i, 0)),
        scratch_shapes=[pltpu.VMEM((GATHER_SLOTS, tm, d), F32),
                        pltpu.VMEM((GATHER_SLOTS, tm, d), F32),
                        pltpu.SemaphoreType.DMA((GATHER_SLOTS,))],
    )
    return pl.pallas_call(
        _combine_kernel,
        out_shape=jax.ShapeDtypeStruct((n, d), F32),
        grid_spec=grid_spec,
        compiler_params=_params(("arbitrary",)),
        name="combine",
    )(pos_flat, ys, x1, route_w, gate2, final_w.reshape(1, d))


def _rope_tables(seq):
    pos = np.arange(seq)
    nf = RET_QK_DIM // 4
    inv = (np.float32(ROPE_BASE) ** (-np.arange(nf, dtype=np.float32) / np.float32(nf))).astype(np.float32)
    ang_r = ((pos // GRID_W).astype(np.float32)[:, None] * inv[None, :]).astype(np.float64)
    ang_c = ((pos % GRID_W).astype(np.float32)[:, None] * inv[None, :]).astype(np.float64)
    return (jnp.asarray(np.concatenate([np.cos(ang_r), np.cos(ang_c)], axis=1), F32),
            jnp.asarray(np.concatenate([np.sin(ang_r), np.sin(ang_c)], axis=1), F32))


def _weight_prep_kernel(w_ref, o_ref, octx_ref, *, n_rot, ctx_cols):
    half = LANES // 2
    first = lax.broadcasted_iota(jnp.int32, (w_ref.shape[0], LANES), 1) < half
    for c0 in range(0, n_rot, RET_QK_DIM):
        a = w_ref[:, c0:c0 + LANES]
        b = w_ref[:, c0 + LANES:c0 + 2 * LANES]
        o_ref[:, c0:c0 + LANES] = jnp.where(first, a, pltpu.roll(b, half, axis=1)).astype(o_ref.dtype)
        o_ref[:, c0 + LANES:c0 + 2 * LANES] = jnp.where(first, pltpu.roll(a, half, axis=1), b).astype(o_ref.dtype)
    o_ref[:, n_rot:] = w_ref[:, n_rot:].astype(o_ref.dtype)
    octx_ref[...] = o_ref[:, ctx_cols[0]:ctx_cols[1]]


def _weight_prep(w_in, n_rot, ctx_cols):
    d, width = w_in.shape
    tr = 128
    ctx_width = ctx_cols[1] - ctx_cols[0]
    return pl.pallas_call(
        functools.partial(_weight_prep_kernel, n_rot=n_rot, ctx_cols=ctx_cols),
        out_shape=(jax.ShapeDtypeStruct((d, width), BF16),
                   jax.ShapeDtypeStruct((d, ctx_width), BF16)),
        grid=(d // tr,),
        in_specs=[pl.BlockSpec((tr, width), lambda i: (i, 0))],
        out_specs=(pl.BlockSpec((tr, width), lambda i: (i, 0)),
                   pl.BlockSpec((tr, ctx_width), lambda i: (i, 0))),
        compiler_params=_params(("parallel",)),
        name="wprep",
    )(w_in)


def _layer(x, c, ctx, c_ctx, w_mod, b_mod, norm1_w, norm2_w, w_in, log_decay, gn_w, w_ret_o,
           conv_w, w_conv_o, w_out, rg_w, rg_b, re_w, re_b, w13, w2, final_w):
    batch, seq, d = x.shape
    ctx_len = ctx.shape[1]
    n = batch * seq
    off_q = 0
    off_k = off_q + RET_QK_W
    off_v = off_k + RET_QK_W
    off_g = off_v + RET_V_W
    off_cb = off_g + RET_V_W

    pad = (-(batch + 1)) % 8
    c_all = jnp.concatenate([c, c_ctx[None, :], jnp.zeros((pad, d), F32)], axis=0)
    mod = _modulation(c_all, w_mod, b_mod)
    mx = mod[:batch].reshape(batch, 1, 6, d)
    shift1, scale1, gate1, shift2, scale2, gate2 = (mx[:, :, k, :] for k in range(6))
    mc = mod[batch].reshape(1, 1, 6, d)
    shift1c, scale1c = mc[:, :, 0, :], mc[:, :, 1, :]

    w_in_b, w_ctx_b = _weight_prep(w_in, off_v, (off_k, off_g))
    cos_t, sin_t = _rope_tables(seq)
    k_scale = RET_QK_DIM ** -0.5

    tm_proj = min(TOKEN_TILE, seq)
    main_modes = (("rope", 1.0), ("rope", k_scale), ("plain", None), ("plain", None),
                  ("swish_gain", 0), ("swish_gain", 1), ("plain", None), ("plain", None), ("plain", None),
                  ("plain", None), ("plain", None))
    gain = gn_w.reshape(1, RET_V_W)
    proj = _projection(x.reshape(n, d), norm1_w, shift1, scale1, cos_t, sin_t, gain, w_in_b, seq,
                       modes=main_modes, tm=tm_proj)
    tm_ctx = batch * ctx_len
    no_rope = jnp.zeros((tm_ctx, LANES), F32)
    kv_ctx = _projection(ctx.reshape(tm_ctx, d), norm1_w, shift1c, scale1c, no_rope, no_rope, gain,
                         w_ctx_b, tm_ctx,
                         modes=(("scale", k_scale), ("plain", None), ("plain", None)), tm=tm_ctx)

    log_gamma = jnp.log1p(-jnp.exp(log_decay.astype(F32)))
    yg = _retention(proj, kv_ctx, log_gamma, batch, seq, ctx_len, off_q, off_k, off_v, off_g)

    w_router = jnp.concatenate(
        [re_w, rg_w, jnp.zeros((d, LANES - N_EXPERTS - N_GROUPS), F32)], axis=1)
    b_router = jnp.concatenate(
        [re_b, rg_b, jnp.zeros((LANES - N_EXPERTS - N_GROUPS,), F32)]).reshape(LANES, 1)
    w_router_hi = w_router.astype(BF16)
    w_router_lo = (w_router - w_router_hi.astype(F32)).astype(BF16)
    w_router = jnp.concatenate([w_router_hi, w_router_lo], axis=1).T
    tm_mix = min(TOKEN_TILE, seq)
    x1, h2, route_i, route_w, counts = _mix(
        yg, proj, x.reshape(n, d), w_ret_o.astype(BF16), w_conv_o.astype(BF16), w_out.astype(BF16),
        conv_w, gate1, shift2, scale2, norm2_w, w_router, b_router, seq, off_cb, tm_mix)

    te = EXPERT_TILE
    cnt = counts[:, 0].astype(jnp.int32)
    padded = ((cnt + te - 1) // te) * te
    ends = jnp.cumsum(padded)
    starts = ends - padded
    expert_ids = jnp.arange(N_EXPERTS, dtype=jnp.int32)
    route_t = route_i[0:4]
    seg_start = jnp.sum(jnp.where(route_t[0:2, :, None] == expert_ids, starts, 0), axis=-1)
    pos_flat = (seg_start + route_t[2:4]).reshape(-1).astype(jnp.int32)
    n_tiles = (2 * n) // te + N_EXPERTS
    tile_row = jnp.arange(n_tiles, dtype=jnp.int32) * te
    tile_expert = jnp.minimum(jnp.sum((tile_row[:, None] >= ends[None, :]).astype(jnp.int32), axis=1),
                              N_EXPERTS - 1).astype(jnp.int32)
    new_expert = jnp.concatenate([jnp.ones((1,), bool), tile_expert[1:] != tile_expert[:-1]])
    tile_flag = jnp.where(tile_row < ends[-1], jnp.where(new_expert, 2, 1), 0).astype(jnp.int32)

    valid_tiles = ends[-1] // te
    last_tile = jnp.where(padded > 0, ends // te - 1, -1)
    tail_tile = valid_tiles + jnp.arange(N_EXPERTS, dtype=jnp.int32)
    tail_tile = jnp.where(tail_tile < n_tiles, tail_tile, -1)
    zero_tiles = jnp.concatenate([last_tile, tail_tile]).astype(jnp.int32)

    xs = _scatter_rows(pos_flat, zero_tiles, h2, n_tiles * te, min(TOKEN_TILE, seq // 2))
    nonempty = padded > 0
    later = nonempty[None, :] & (expert_ids[None, :] > expert_ids[:, None])
    next_expert = jnp.min(jnp.where(later, expert_ids[None, :], N_EXPERTS), axis=1)
    next_expert = jnp.where(next_expert == N_EXPERTS, -1, next_expert)
    slot = (jnp.cumsum(nonempty.astype(jnp.int32)) - 1) % 2
    tile_hot = tile_expert[:, None] == expert_ids[None, :]
    tile_next = jnp.sum(jnp.where(tile_hot, next_expert, 0), axis=1).astype(jnp.int32)
    tile_slot = jnp.sum(jnp.where(tile_hot, slot, 0), axis=1).astype(jnp.int32)

    ys = _experts(tile_flag, tile_expert, tile_next, tile_slot, xs, w13, w2)
    out = _combine(pos_flat, ys, x1, route_w[0:2].T, gate2, final_w, seq, min(COMBINE_TILE, seq // 2))
    return out.reshape(batch, seq, d)


def kernel(x, c, ctx, c_ctx, w_mod, b_mod, norm1_w, norm2_w, w_in, ret_log_decay, ret_gn_w, w_ret_o,
           conv_w, w_conv_o, w_out, router_group_w, router_group_b, router_expert_w, router_expert_b,
           expert_w13, expert_w2, final_norm_w):
    assert w_mod.shape[0] == 1, "single-layer problem"
    return _layer(x, c, ctx, c_ctx, w_mod[0], b_mod[0], norm1_w[0], norm2_w[0], w_in[0],
                  ret_log_decay[0], ret_gn_w[0], w_ret_o[0], conv_w[0], w_conv_o[0], w_out[0],
                  router_group_w[0], router_group_b[0], router_expert_w[0], router_expert_b[0],
                  expert_w13[0], expert_w2[0], final_norm_w)
```

```python
import functools

import jax
import jax.numpy as jnp
import numpy as np
from jax import lax
from jax.experimental import pallas as pl
from jax.experimental.pallas import tpu as pltpu

GRID_W = 64
RET_HEADS = 4
RET_QK_DIM = 256
RET_V_DIM = 512
RET_QK_W = RET_HEADS * RET_QK_DIM
RET_V_W = RET_HEADS * RET_V_DIM
N_GROUPS = 4
EXPERTS_PER_GROUP = 8
N_EXPERTS = N_GROUPS * EXPERTS_PER_GROUP
EXPERT_HIDDEN = 512
ROPE_BASE = 10000.0
NORM_EPS = 1e-6
GN_EPS = 1e-5

RET_CHUNK = 256
LANES = 128
EXPERT_TILE = 256
TOKEN_TILE = 512
COMBINE_TILE = 256
COL_CHUNK = 1024
VMEM_LIMIT = 56 * 1024 * 1024

BF16 = jnp.bfloat16
F32 = jnp.float32


def _params(sem, vmem=VMEM_LIMIT):
    return pltpu.CompilerParams(dimension_semantics=sem, vmem_limit_bytes=vmem)


def _mod_kernel(c_ref, w_ref, b_ref, o_ref):
    c = c_ref[...]
    s = c * jax.nn.sigmoid(c)
    o_ref[...] = jnp.dot(s.astype(BF16), w_ref[...].astype(BF16),
                         preferred_element_type=F32) + b_ref[...]


def _modulation(c_all, w_mod, b_mod):
    rows, d = c_all.shape
    n_out = w_mod.shape[1]
    tn = 3072
    return pl.pallas_call(
        _mod_kernel,
        out_shape=jax.ShapeDtypeStruct((rows, n_out), F32),
        grid=(n_out // tn,),
        in_specs=[pl.BlockSpec((rows, d), lambda j: (0, 0)),
                  pl.BlockSpec((d, tn), lambda j: (0, j)),
                  pl.BlockSpec((1, tn), lambda j: (0, j))],
        out_specs=pl.BlockSpec((rows, tn), lambda j: (0, j)),
        compiler_params=_params(("parallel",)),
        name="mod",
    )(c_all, w_mod, b_mod.reshape(1, n_out))


def _rope_store(acc, cos, sin, scale, o_ref):
    for hd in range(acc.shape[1] // RET_QK_DIM):
        lo = hd * RET_QK_DIM
        u1 = acc[:, lo:lo + LANES]
        u2 = acc[:, lo + LANES:lo + 2 * LANES]
        r1 = u1 * cos - u2 * sin
        r2 = u1 * sin + u2 * cos
        if scale != 1.0:
            r1 = r1 * scale
            r2 = r2 * scale
        o_ref[:, lo:lo + LANES] = r1.astype(o_ref.dtype)
        o_ref[:, lo + LANES:lo + 2 * LANES] = r2.astype(o_ref.dtype)


def _proj_kernel(x_ref, nw_ref, sh_ref, sc_ref, cos_ref, sin_ref, gn_ref, w_ref, o_ref, *, modes, tn):
    x = x_ref[...]
    y = x * lax.rsqrt(jnp.mean(x * x, axis=-1, keepdims=True) + NORM_EPS) * nw_ref[...]
    h = (y * (1.0 + sc_ref[0]) + sh_ref[0]).astype(BF16)
    assert len(modes) * tn == w_ref.shape[1]
    for j, (kind, arg) in enumerate(modes):
        cols = slice(j * tn, (j + 1) * tn)
        acc = jnp.dot(h, w_ref[:, cols], preferred_element_type=F32)
        if kind == "rope":
            _rope_store(acc, cos_ref[...], sin_ref[...], arg, o_ref.at[:, cols])
        elif kind == "scale":
            o_ref[:, cols] = (acc * arg).astype(o_ref.dtype)
        elif kind == "swish_gain":
            gain = gn_ref[:, arg * tn:(arg + 1) * tn]
            o_ref[:, cols] = (acc * jax.nn.sigmoid(acc) * gain).astype(o_ref.dtype)
        else:
            o_ref[:, cols] = acc.astype(o_ref.dtype)


def _projection(x2d, norm_w, shift, scale, cos_t, sin_t, gain, w_bf16, seq, modes, tm):
    n, d = x2d.shape
    width = w_bf16.shape[1]
    tn = COL_CHUNK
    per_b = seq // tm
    nb = shift.shape[0]
    bidx = (lambda i: (i // per_b, 0, 0)) if nb > 1 else (lambda i: (0, 0, 0))
    return pl.pallas_call(
        functools.partial(_proj_kernel, modes=modes, tn=tn),
        out_shape=jax.ShapeDtypeStruct((n, width), BF16),
        grid=(n // tm,),
        in_specs=[pl.BlockSpec((tm, d), lambda i: (i, 0)),
                  pl.BlockSpec((1, d), lambda i: (0, 0)),
                  pl.BlockSpec((1, 1, d), bidx),
                  pl.BlockSpec((1, 1, d), bidx),
                  pl.BlockSpec((tm, LANES), lambda i: (i % per_b, 0)),
                  pl.BlockSpec((tm, LANES), lambda i: (i % per_b, 0)),
                  pl.BlockSpec(gain.shape, lambda i: (0, 0)),
                  pl.BlockSpec((d, width), lambda i: (0, 0), pipeline_mode=pl.Buffered(1))],
        out_specs=pl.BlockSpec((tm, width), lambda i: (i, 0)),
        compiler_params=_params(("parallel",)),
        name="proj",
    )(x2d, norm_w.reshape(1, d), shift, scale, cos_t, sin_t, gain, w_bf16)


def _dot_t0(a, b):
    return lax.dot_general(a, b, (((0,), (0,)), ((), ())), preferred_element_type=F32)


def _ret_kernel(lg_ref, q_ref, k_ref, v_ref, g_ref, kc_ref, vc_ref, o_ref,
                acc_ref, sf_ref, sb_ref, dm_ref, *, seq, ctx_len):
    h = pl.program_id(1)
    lgf = lg_ref[0, h]
    lgb = lg_ref[1, h]
    c = RET_CHUNK
    n_chunks = seq // c

    cpos = lax.broadcasted_iota(jnp.int32, (ctx_len, 1), 0).astype(F32)
    kc = kc_ref[...].astype(F32)
    vc = vc_ref[...]
    sf_ref[...] = _dot_t0((kc * jnp.exp((ctx_len - 1.0 - cpos) * lgf)).astype(BF16), vc)
    sb_ref[...] = _dot_t0((kc * jnp.exp(cpos * lgb)).astype(BF16), vc)

    ri = lax.broadcasted_iota(jnp.int32, (c, c), 0)
    ci = lax.broadcasted_iota(jnp.int32, (c, c), 1)
    dist = (ri - ci).astype(F32)
    dm_ref[...] = jnp.exp(jnp.abs(dist) * jnp.where(dist >= 0, lgf, lgb))

    idx = lax.broadcasted_iota(jnp.int32, (c, 1), 0).astype(F32)
    xi_f = jnp.exp((idx + 1.0) * lgf)
    zeta_f = jnp.exp((c - 1.0 - idx) * lgf)
    xi_b = jnp.exp((c - idx) * lgb)
    zeta_b = jnp.exp(idx * lgb)
    cd_f = jnp.exp(jnp.full((1, 1), float(c), F32) * lgf)
    cd_b = jnp.exp(jnp.full((1, 1), float(c), F32) * lgb)

    def chunk_rows(ic):
        return pl.ds(pl.multiple_of(ic * c, c), c)

    def fwd_part(ic):
        rows = chunk_rows(ic)
        q = q_ref[rows, :]
        k = k_ref[rows, :]
        v = v_ref[rows, :]
        s = lax.dot_general(q, k, (((1,), (1,)), ((), ())), preferred_element_type=F32)
        intra = jnp.dot((s * dm_ref[...]).astype(BF16), v, preferred_element_type=F32)
        inter = jnp.dot(q, sf_ref[...].astype(BF16), preferred_element_type=F32)
        kz = (k.astype(F32) * zeta_f).astype(BF16)
        sf_ref[...] = cd_f * sf_ref[...] + _dot_t0(kz, v)
        return intra + xi_f * inter

    def bwd_part(ic):
        rows = chunk_rows(ic)
        q = q_ref[rows, :]
        k = k_ref[rows, :]
        v = v_ref[rows, :]
        inter = jnp.dot(q, sb_ref[...].astype(BF16), preferred_element_type=F32)
        kz = (k.astype(F32) * zeta_b).astype(BF16)
        sb_ref[...] = cd_b * sb_ref[...] + _dot_t0(kz, v)
        return xi_b * inter

    def finalize(ic, y):
        rows = chunk_rows(ic)
        mu = jnp.mean(y, axis=-1, keepdims=True)
        yc = y - mu
        var = jnp.mean(yc * yc, axis=-1, keepdims=True)
        o_ref[rows, :] = (g_ref[rows, :].astype(F32) * (yc * lax.rsqrt(var + GN_EPS))).astype(o_ref.dtype)

    half = n_chunks // 2

    def first_half(t, carry):
        acc_ref[chunk_rows(t), :] = fwd_part(t)
        acc_ref[chunk_rows(n_chunks - 1 - t), :] = bwd_part(n_chunks - 1 - t)
        return carry

    def second_half(t, carry):
        finalize(t, acc_ref[chunk_rows(t), :] + fwd_part(t))
        u = n_chunks - 1 - t
        finalize(u, acc_ref[chunk_rows(u), :] + bwd_part(u))
        return carry

    lax.fori_loop(0, half, first_half, 0, unroll=True)
    lax.fori_loop(half, n_chunks, second_half, 0, unroll=min(4, half))


def _retention(proj, kv_ctx, log_gamma, batch, seq, ctx_len, off_q, off_k, off_v, off_g):
    n = proj.shape[0]
    dk, dv = RET_QK_DIM, RET_V_DIM
    assert seq % (2 * RET_CHUNK) == 0, "the paired scan needs an even number of chunks"
    grid_spec = pltpu.PrefetchScalarGridSpec(
        num_scalar_prefetch=1,
        grid=(batch, RET_HEADS),
        in_specs=[pl.BlockSpec((seq, dk), lambda b, h, lg: (b, off_q // dk + h)),
                  pl.BlockSpec((seq, dk), lambda b, h, lg: (b, off_k // dk + h)),
                  pl.BlockSpec((seq, dv), lambda b, h, lg: (b, off_v // dv + h)),
                  pl.BlockSpec((seq, dv), lambda b, h, lg: (b, off_g // dv + h)),
                  pl.BlockSpec((ctx_len, dk), lambda b, h, lg: (b, h)),
                  pl.BlockSpec((ctx_len, dv), lambda b, h, lg: (b, RET_QK_W // dv + h))],
        out_specs=pl.BlockSpec((seq, dv), lambda b, h, lg: (b, h)),
        scratch_shapes=[pltpu.VMEM((seq, dv), F32),
                        pltpu.VMEM((dk, dv), F32),
                        pltpu.VMEM((dk, dv), F32),
                        pltpu.VMEM((RET_CHUNK, RET_CHUNK), F32)],
    )
    return pl.pallas_call(
        functools.partial(_ret_kernel, seq=seq, ctx_len=ctx_len),
        out_shape=jax.ShapeDtypeStruct((n, RET_V_W), BF16),
        grid_spec=grid_spec,
        compiler_params=_params(("parallel", "arbitrary")),
        name="ret",
    )(log_gamma, proj, proj, proj, proj, kv_ctx, kv_ctx)


def _mix_kernel(yg_ref, cb_ref, cc_ref, ch_ref, gr_ref, gc_ref, x_ref, wro_ref, wco_ref, wo_ref,
                cw_ref, g1_ref, sh2_ref, sc2_ref, n2_ref, wr_ref, br_ref,
                x1_ref, h2_ref, ri_ref, rw_ref, cnt_ref, carry_ref):
    i = pl.program_id(0)
    tm = x_ref.shape[0]

    @pl.when(i == 0)
    def _():
        carry_ref[...] = jnp.zeros_like(carry_ref)

    ret_branch = jnp.dot(yg_ref[...], wro_ref[...], preferred_element_type=F32)

    p = cc_ref[...].astype(F32) * ch_ref[...].astype(F32)
    tpos = lax.broadcasted_iota(jnp.int32, (tm, 1), 0) % GRID_W
    prev = jnp.where(tpos != 0, pltpu.roll(p, 1, axis=0), 0.0)
    nxt = jnp.where(tpos != GRID_W - 1, pltpu.roll(p, tm - 1, axis=0), 0.0)
    cw = cw_ref[...]
    u = cw[0:1, :] * prev + cw[1:2, :] * p + cw[2:3, :] * nxt
    conv_in = (cb_ref[...].astype(F32) * u).astype(BF16)
    conv_branch = jnp.dot(conv_in, wco_ref[...], preferred_element_type=F32)

    merged = (jax.nn.sigmoid(gr_ref[...].astype(F32)) * ret_branch
              + jax.nn.sigmoid(gc_ref[...].astype(F32)) * conv_branch)
    mixed = jnp.dot(merged.astype(BF16), wo_ref[...], preferred_element_type=F32)
    x1 = x_ref[...] + g1_ref[0] * mixed
    x1_ref[...] = x1

    y = x1 * lax.rsqrt(jnp.mean(x1 * x1, axis=-1, keepdims=True) + NORM_EPS) * n2_ref[...]
    h2 = y * (1.0 + sc2_ref[0]) + sh2_ref[0]
    h2_ref[...] = h2

    h_hi = h2.astype(BF16)
    h_lo = (h2 - h_hi.astype(F32)).astype(BF16)
    nt_dims = (((1,), (1,)), ((), ()))
    hw = lax.dot_general(wr_ref[...], h_hi, nt_dims, preferred_element_type=F32)
    lw = lax.dot_general(wr_ref[:LANES, :], h_lo, nt_dims, preferred_element_type=F32)
    logits_t = hw[:LANES] + (hw[LANES:] + lw) + br_ref[...]
    neg = jnp.float32(-jnp.inf)
    big = jnp.int32(1 << 20)

    grp = logits_t[N_EXPERTS:N_EXPERTS + N_GROUPS]
    grow = lax.broadcasted_iota(jnp.int32, grp.shape, 0)
    gmax = jnp.max(grp, axis=0, keepdims=True)
    gidx = jnp.min(jnp.where(grp == gmax, grow, big), axis=0, keepdims=True)
    g_w = 1.0 / jnp.sum(jnp.exp(grp - gmax), axis=0, keepdims=True)

    el_all = logits_t[:N_EXPERTS]
    erow = lax.broadcasted_iota(jnp.int32, el_all.shape, 0)
    in_grp = (erow // EXPERTS_PER_GROUP) == gidx
    el = jnp.where(in_grp, el_all, neg)
    emax = jnp.max(el, axis=0, keepdims=True)
    i1 = jnp.min(jnp.where(el == emax, erow, big), axis=0, keepdims=True)
    el2 = jnp.where(erow == i1, neg, el)
    l2 = jnp.max(el2, axis=0, keepdims=True)
    i2 = jnp.min(jnp.where(el2 == l2, erow, big), axis=0, keepdims=True)
    esum = jnp.sum(jnp.where(in_grp, jnp.exp(el_all - emax), 0.0), axis=0, keepdims=True)
    p1 = 1.0 / esum
    p2 = jnp.exp(l2 - emax) / esum
    den = p1 + p2
    c1 = g_w * (p1 / den)
    c2 = g_w * (p2 / den)

    hot1 = erow == i1
    hot2 = erow == i2
    onehot = jnp.where(hot1, 1.0, jnp.where(hot2, 1.0, 0.0))
    earlier = (lax.broadcasted_iota(jnp.int32, (tm, tm), 0)
               < lax.broadcasted_iota(jnp.int32, (tm, tm), 1))
    ranks = jnp.dot(onehot.astype(BF16), jnp.where(earlier, 1.0, 0.0).astype(BF16),
                    preferred_element_type=F32) + carry_ref[...]
    r1 = jnp.sum(jnp.where(hot1, ranks, 0.0), axis=0, keepdims=True).astype(jnp.int32)
    r2 = jnp.sum(jnp.where(hot2, ranks, 0.0), axis=0, keepdims=True).astype(jnp.int32)
    carry_ref[...] = carry_ref[...] + jnp.sum(onehot, axis=1, keepdims=True)
    cnt_ref[...] = carry_ref[...]

    row8 = lax.broadcasted_iota(jnp.int32, (8, tm), 0)
    ri_ref[...] = jnp.where(row8 == 0, i1, jnp.where(row8 == 1, i2,
                            jnp.where(row8 == 2, r1, jnp.where(row8 == 3, r2, 0))))
    rw_ref[...] = jnp.where(row8 == 0, c1, jnp.where(row8 == 1, c2, 0.0))


def _mix(yg, proj, x2d, w_ret_o, w_conv_o, w_out, conv_w, gate1, shift2, scale2, norm2_w,
         w_router, b_router, seq, off_cb, tm):
    n, d = x2d.shape
    per_b = seq // tm
    cblk = off_cb // d
    row = lambda i: (i, 0)
    const2 = lambda i: (0, 0)
    bidx = lambda i: (i // per_b, 0, 0)
    col = lambda k: (lambda i: (i, cblk + k))
    return pl.pallas_call(
        _mix_kernel,
        out_shape=(jax.ShapeDtypeStruct((n, d), F32),
                   jax.ShapeDtypeStruct((n, d), F32),
                   jax.ShapeDtypeStruct((8, n), jnp.int32),
                   jax.ShapeDtypeStruct((8, n), F32),
                   jax.ShapeDtypeStruct((N_EXPERTS, 1), F32)),
        grid=(n // tm,),
        in_specs=[pl.BlockSpec((tm, RET_V_W), row),
                  pl.BlockSpec((tm, d), col(0)),
                  pl.BlockSpec((tm, d), col(1)),
                  pl.BlockSpec((tm, d), col(2)),
                  pl.BlockSpec((tm, d), col(3)),
                  pl.BlockSpec((tm, d), col(4)),
                  pl.BlockSpec((tm, d), row),
                  pl.BlockSpec((RET_V_W, d), const2, pipeline_mode=pl.Buffered(1)),
                  pl.BlockSpec((d, d), const2, pipeline_mode=pl.Buffered(1)),
                  pl.BlockSpec((d, d), const2, pipeline_mode=pl.Buffered(1)),
                  pl.BlockSpec((3, d), const2),
                  pl.BlockSpec((1, 1, d), bidx),
                  pl.BlockSpec((1, 1, d), bidx),
                  pl.BlockSpec((1, 1, d), bidx),
                  pl.BlockSpec((1, d), const2),
                  pl.BlockSpec((2 * LANES, d), const2),
                  pl.BlockSpec((LANES, 1), const2)],
        out_specs=(pl.BlockSpec((tm, d), row),
                   pl.BlockSpec((tm, d), row),
                   pl.BlockSpec((8, tm), lambda i: (0, i)),
                   pl.BlockSpec((8, tm), lambda i: (0, i)),
                   pl.BlockSpec((N_EXPERTS, 1), const2)),
        scratch_shapes=[pltpu.VMEM((N_EXPERTS, 1), F32)],
        compiler_params=_params(("arbitrary",)),
        name="mix",
    )(yg, proj, proj, proj, proj, proj, x2d, w_ret_o, w_conv_o, w_out, conv_w,
      gate1, shift2, scale2, norm2_w.reshape(1, d), w_router, b_router)


def _row_copy(src_ref, src_row, dst_ref, dst_row, sem):
    return pltpu.make_async_copy(src_ref.at[pl.ds(src_row, 1), :],
                                 dst_ref.at[pl.ds(dst_row, 1), :], sem)


def _row_token_kernel(pos_ref, zeros_ref, inv_ref, sem):
    n_tok = pos_ref.shape[0] // 2
    init = pltpu.make_async_copy(zeros_ref, inv_ref, sem)
    init.start()
    init.wait()

    def put(t, carry):
        inv_ref[pos_ref[t]] = t
        inv_ref[pos_ref[n_tok + t]] = t
        return carry

    lax.fori_loop(0, n_tok, put, 0, unroll=16)


def _row_tokens(pos_flat, n_sorted):
    return pl.pallas_call(
        _row_token_kernel,
        out_shape=jax.ShapeDtypeStruct((n_sorted,), jnp.int32),
        in_specs=[pl.BlockSpec(memory_space=pltpu.SMEM), pl.BlockSpec(memory_space=pl.ANY)],
        out_specs=pl.BlockSpec(memory_space=pltpu.SMEM),
        scratch_shapes=[pltpu.SemaphoreType.DMA],
        name="rowtok",
    )(pos_flat, jnp.zeros((n_sorted,), jnp.int32))


ROW_TILE_SLOTS = 3


def _expert_kernel(tf_ref, te_ref, tn_ref, ts_ref, inv_ref, h_ref, w13_ref, w2_ref, ys_ref,
                   xbuf_ref, w13f_ref, w2f_ref, w13b_ref, w2b_ref, xsem, sem13, sem2):
    t = pl.program_id(0)
    nt = pl.num_programs(0)
    rows = xbuf_ref.shape[1]
    flag = tf_ref[t]

    def gather_issue(tile, slot):
        inv_w = inv_ref.at[pl.ds(tile * rows, rows)]
        dst = xbuf_ref.at[slot]
        for r in range(rows):
            _row_copy(h_ref, inv_w[r], dst, r, xsem.at[slot]).start(priority=r % 2)

    def gather_wait(slot):
        pltpu.make_async_copy(h_ref.at[pl.ds(0, rows), :], xbuf_ref.at[slot], xsem.at[slot]).wait()

    def weight_copies(e, slot):
        return (pltpu.make_async_copy(w13_ref.at[e], w13f_ref.at[slot], sem13.at[slot]),
                pltpu.make_async_copy(w2_ref.at[e], w2f_ref.at[slot], sem2.at[slot]))

    @pl.when(t == 0)
    def _():
        for cp in weight_copies(te_ref[0], ts_ref[0]):
            cp.start()
        gather_issue(0, 0)
        gather_issue(1, 1)

    slot = lax.rem(t, ROW_TILE_SLOTS)
    ahead_slot = lax.rem(t + 2, ROW_TILE_SLOTS)

    @pl.when(flag > 0)
    def _():
        gather_wait(slot)

    @pl.when(flag == 2)
    def _():
        wslot = ts_ref[t]
        for cp in weight_copies(te_ref[t], wslot):
            cp.wait()
        w13b_ref[...] = w13f_ref[wslot].astype(BF16)
        w2b_ref[...] = w2f_ref[wslot].astype(BF16)

        @pl.when(tn_ref[t] >= 0)
        def _():
            for cp in weight_copies(tn_ref[t], 1 - wslot):
                cp.start()

    @pl.when(flag > 0)
    def _():
        x = xbuf_ref[slot].astype(BF16)
        hid = jnp.dot(x, w13b_ref[...], preferred_element_type=F32)
        a = hid[:, :EXPERT_HIDDEN]
        act = a * jax.nn.sigmoid(a) * hid[:, EXPERT_HIDDEN:]
        ys_ref[...] = jnp.dot(act.astype(BF16), w2b_ref[...], preferred_element_type=F32)
        gather_issue(jnp.minimum(t + 2, nt - 1), ahead_slot)

    @pl.when(flag == 0)
    def _():
        ys_ref[...] = jnp.zeros_like(ys_ref)

    last_with_rows = (flag > 0) & ((t == nt - 1) | (tf_ref[jnp.minimum(t + 1, nt - 1)] == 0))

    @pl.when(last_with_rows)
    def _():
        gather_wait(lax.rem(t + 1, ROW_TILE_SLOTS))
        gather_wait(ahead_slot)


def _experts(tile_flag, tile_expert, tile_next, tile_slot, row_token, h2, w13, w2):
    n, d = h2.shape
    te = EXPERT_TILE
    n_tiles = tile_flag.shape[0]
    assert (2 * n) // te >= ROW_TILE_SLOTS
    hid2 = w13.shape[2]
    grid_spec = pltpu.PrefetchScalarGridSpec(
        num_scalar_prefetch=5,
        grid=(n_tiles,),
        in_specs=[pl.BlockSpec(memory_space=pl.ANY),
                  pl.BlockSpec(memory_space=pl.ANY),
                  pl.BlockSpec(memory_space=pl.ANY)],
        out_specs=pl.BlockSpec((te, d), lambda t, *_: (t, 0)),
        scratch_shapes=[pltpu.VMEM((ROW_TILE_SLOTS, te, d), F32),
                        pltpu.VMEM((2, d, hid2), F32),
                        pltpu.VMEM((2, hid2 // 2, d), F32),
                        pltpu.VMEM((d, hid2), BF16),
                        pltpu.VMEM((hid2 // 2, d), BF16),
                        pltpu.SemaphoreType.DMA((ROW_TILE_SLOTS,)),
                        pltpu.SemaphoreType.DMA((2,)),
                        pltpu.SemaphoreType.DMA((2,))],
    )
    return pl.pallas_call(
        _expert_kernel,
        out_shape=jax.ShapeDtypeStruct((n_tiles * te, d), F32),
        grid_spec=grid_spec,
        compiler_params=_params(("arbitrary",)),
        name="experts",
    )(tile_flag, tile_expert, tile_next, tile_slot, row_token, h2, w13, w2)


GATHER_SLOTS = 3


def _combine_kernel(pos_ref, ys_ref, x1_ref, rw_ref, g2_ref, fw_ref, o_ref, ya_ref, yb_ref, sem):
    i = pl.program_id(0)
    n_steps = pl.num_programs(0)
    tm = ya_ref.shape[1]
    n_tok = pos_ref.shape[0] // 2

    def issue(tile, slot):
        base = tile * tm
        dst_a = ya_ref.at[slot]
        dst_b = yb_ref.at[slot]
        pos_a = pos_ref.at[pl.ds(base, tm)]
        pos_b = pos_ref.at[pl.ds(n_tok + base, tm)]
        for r in range(tm):
            _row_copy(ys_ref, pos_a[r], dst_a, r, sem.at[slot]).start(priority=0)
            _row_copy(ys_ref, pos_b[r], dst_b, r, sem.at[slot]).start(priority=1)

    def wait(slot):
        pltpu.make_async_copy(ys_ref.at[pl.ds(0, tm), :], ya_ref.at[slot], sem.at[slot]).wait()
        pltpu.make_async_copy(ys_ref.at[pl.ds(0, tm), :], yb_ref.at[slot], sem.at[slot]).wait()

    @pl.when(i == 0)
    def _():
        issue(0, 0)
        issue(1, 1)

    slot = lax.rem(i, GATHER_SLOTS)
    ahead_slot = lax.rem(i + 2, GATHER_SLOTS)
    wait(slot)
    rw = rw_ref[...]
    moe = rw[:, 0:1] * ya_ref[slot] + rw[:, 1:2] * yb_ref[slot]
    x2 = x1_ref[...] + g2_ref[0] * moe
    o_ref[...] = x2 * lax.rsqrt(jnp.mean(x2 * x2, axis=-1, keepdims=True) + NORM_EPS) * fw_ref[...]
    issue(jnp.minimum(i + 2, n_steps - 1), ahead_slot)

    @pl.when(i == n_steps - 1)
    def _():
        wait(lax.rem(i + 1, GATHER_SLOTS))
        wait(ahead_slot)


def _combine(pos_flat, ys, x1, route_w, gate2, final_w, seq, tm):
    n, d = x1.shape
    per_b = seq // tm
    assert n // tm >= GATHER_SLOTS
    grid_spec = pltpu.PrefetchScalarGridSpec(
        num_scalar_prefetch=1,
        grid=(n // tm,),
        in_specs=[pl.BlockSpec(memory_space=pl.ANY),
                  pl.BlockSpec((tm, d), lambda i, pos: (i, 0)),
                  pl.BlockSpec((tm, 2), lambda i, pos: (i, 0)),
                  pl.BlockSpec((1, 1, d), lambda i, pos: (i // per_b, 0, 0)),
                  pl.BlockSpec((1, d), lambda i, pos: (0, 0))],
        out_specs=pl.BlockSpec((tm, d), lambda i, pos: (i, 0)),
        scratch_shapes=[pltpu.VMEM((GATHER_SLOTS, tm, d), F32),
                        pltpu.VMEM((GATHER_SLOTS, tm, d), F32),
                        pltpu.SemaphoreType.DMA((GATHER_SLOTS,))],
    )
    return pl.pallas_call(
        _combine_kernel,
        out_shape=jax.ShapeDtypeStruct((n, d), F32),
        grid_spec=grid_spec,
        compiler_params=_params(("arbitrary",)),
        name="combine",
    )(pos_flat, ys, x1, route_w, gate2, final_w.reshape(1, d))


def _rope_tables(seq):
    pos = np.arange(seq)
    nf = RET_QK_DIM // 4
    inv = (np.float32(ROPE_BASE) ** (-np.arange(nf, dtype=np.float32) / np.float32(nf))).astype(np.float32)
    ang_r = ((pos // GRID_W).astype(np.float32)[:, None] * inv[None, :]).astype(np.float64)
    ang_c = ((pos % GRID_W).astype(np.float32)[:, None] * inv[None, :]).astype(np.float64)
    return (jnp.asarray(np.concatenate([np.cos(ang_r), np.cos(ang_c)], axis=1), F32),
            jnp.asarray(np.concatenate([np.sin(ang_r), np.sin(ang_c)], axis=1), F32))


def _weight_prep_kernel(w_ref, o_ref, octx_ref, *, n_rot, ctx_cols):
    half = LANES // 2
    first = lax.broadcasted_iota(jnp.int32, (w_ref.shape[0], LANES), 1) < half
    for c0 in range(0, n_rot, RET_QK_DIM):
        a = w_ref[:, c0:c0 + LANES]
        b = w_ref[:, c0 + LANES:c0 + 2 * LANES]
        o_ref[:, c0:c0 + LANES] = jnp.where(first, a, pltpu.roll(b, half, axis=1)).astype(o_ref.dtype)
        o_ref[:, c0 + LANES:c0 + 2 * LANES] = jnp.where(first, pltpu.roll(a, half, axis=1), b).astype(o_ref.dtype)
    o_ref[:, n_rot:] = w_ref[:, n_rot:].astype(o_ref.dtype)
    octx_ref[...] = o_ref[:, ctx_cols[0]:ctx_cols[1]]


def _weight_prep(w_in, n_rot, ctx_cols):
    d, width = w_in.shape
    tr = 128
    ctx_width = ctx_cols[1] - ctx_cols[0]
    return pl.pallas_call(
        functools.partial(_weight_prep_kernel, n_rot=n_rot, ctx_cols=ctx_cols),
        out_shape=(jax.ShapeDtypeStruct((d, width), BF16),
                   jax.ShapeDtypeStruct((d, ctx_width), BF16)),
        grid=(d // tr,),
        in_specs=[pl.BlockSpec((tr, width), lambda i: (i, 0))],
        out_specs=(pl.BlockSpec((tr, width), lambda i: (i, 0)),
                   pl.BlockSpec((tr, ctx_width), lambda i: (i, 0))),
        compiler_params=_params(("parallel",)),
        name="wprep",
    )(w_in)


def _layer(x, c, ctx, c_ctx, w_mod, b_mod, norm1_w, norm2_w, w_in, log_decay, gn_w, w_ret_o,
           conv_w, w_conv_o, w_out, rg_w, rg_b, re_w, re_b, w13, w2, final_w):
    batch, seq, d = x.shape
    ctx_len = ctx.shape[1]
    n = batch * seq
    off_q = 0
    off_k = off_q + RET_QK_W
    off_v = off_k + RET_QK_W
    off_g = off_v + RET_V_W
    off_cb = off_g + RET_V_W

    pad = (-(batch + 1)) % 8
    c_all = jnp.concatenate([c, c_ctx[None, :], jnp.zeros((pad, d), F32)], axis=0)
    mod = _modulation(c_all, w_mod, b_mod)
    mx = mod[:batch].reshape(batch, 1, 6, d)
    shift1, scale1, gate1, shift2, scale2, gate2 = (mx[:, :, k, :] for k in range(6))
    mc = mod[batch].reshape(1, 1, 6, d)
    shift1c, scale1c = mc[:, :, 0, :], mc[:, :, 1, :]

    w_in_b, w_ctx_b = _weight_prep(w_in, off_v, (off_k, off_g))
    cos_t, sin_t = _rope_tables(seq)
    k_scale = RET_QK_DIM ** -0.5

    tm_proj = min(TOKEN_TILE, seq)
    main_modes = (("rope", 1.0), ("rope", k_scale), ("plain", None), ("plain", None),
                  ("swish_gain", 0), ("swish_gain", 1), ("plain", None), ("plain", None), ("plain", None),
                  ("plain", None), ("plain", None))
    gain = gn_w.reshape(1, RET_V_W)
    proj = _projection(x.reshape(n, d), norm1_w, shift1, scale1, cos_t, sin_t, gain, w_in_b, seq,
                       modes=main_modes, tm=tm_proj)
    tm_ctx = batch * ctx_len
    no_rope = jnp.zeros((tm_ctx, LANES), F32)
    kv_ctx = _projection(ctx.reshape(tm_ctx, d), norm1_w, shift1c, scale1c, no_rope, no_rope, gain,
                         w_ctx_b, tm_ctx,
                         modes=(("scale", k_scale), ("plain", None), ("plain", None)), tm=tm_ctx)

    log_gamma = jnp.log1p(-jnp.exp(log_decay.astype(F32)))
    yg = _retention(proj, kv_ctx, log_gamma, batch, seq, ctx_len, off_q, off_k, off_v, off_g)

    w_router = jnp.concatenate(
        [re_w, rg_w, jnp.zeros((d, LANES - N_EXPERTS - N_GROUPS), F32)], axis=1)
    b_router = jnp.concatenate(
        [re_b, rg_b, jnp.zeros((LANES - N_EXPERTS - N_GROUPS,), F32)]).reshape(LANES, 1)
    w_router_hi = w_router.astype(BF16)
    w_router_lo = (w_router - w_router_hi.astype(F32)).astype(BF16)
    w_router = jnp.concatenate([w_router_hi, w_router_lo], axis=1).T
    tm_mix = min(TOKEN_TILE, seq)
    x1, h2, route_i, route_w, counts = _mix(
        yg, proj, x.reshape(n, d), w_ret_o.astype(BF16), w_conv_o.astype(BF16), w_out.astype(BF16),
        conv_w, gate1, shift2, scale2, norm2_w, w_router, b_router, seq, off_cb, tm_mix)

    te = EXPERT_TILE
    cnt = counts[:, 0].astype(jnp.int32)
    padded = ((cnt + te - 1) // te) * te
    ends = jnp.cumsum(padded)
    starts = ends - padded
    expert_ids = jnp.arange(N_EXPERTS, dtype=jnp.int32)
    route_t = route_i[0:4]
    seg_start = jnp.sum(jnp.where(route_t[0:2, :, None] == expert_ids, starts, 0), axis=-1)
    pos_flat = (seg_start + route_t[2:4]).reshape(-1).astype(jnp.int32)
    n_tiles = (2 * n) // te + N_EXPERTS
    tile_row = jnp.arange(n_tiles, dtype=jnp.int32) * te
    tile_expert = jnp.minimum(jnp.sum((tile_row[:, None] >= ends[None, :]).astype(jnp.int32), axis=1),
                              N_EXPERTS - 1).astype(jnp.int32)
    new_expert = jnp.concatenate([jnp.ones((1,), bool), tile_expert[1:] != tile_expert[:-1]])
    tile_flag = jnp.where(tile_row < ends[-1], jnp.where(new_expert, 2, 1), 0).astype(jnp.int32)

    nonempty = padded > 0
    later = nonempty[None, :] & (expert_ids[None, :] > expert_ids[:, None])
    next_expert = jnp.min(jnp.where(later, expert_ids[None, :], N_EXPERTS), axis=1)
    next_expert = jnp.where(next_expert == N_EXPERTS, -1, next_expert)
    slot = (jnp.cumsum(nonempty.astype(jnp.int32)) - 1) % 2
    tile_hot = tile_expert[:, None] == expert_ids[None, :]
    tile_next = jnp.sum(jnp.where(tile_hot, next_expert, 0), axis=1).astype(jnp.int32)
    tile_slot = jnp.sum(jnp.where(tile_hot, slot, 0), axis=1).astype(jnp.int32)

    row_token = _row_tokens(pos_flat, n_tiles * te)
    ys = _experts(tile_flag, tile_expert, tile_next, tile_slot, row_token, h2, w13, w2)
    out = _combine(pos_flat, ys, x1, route_w[0:2].T, gate2, final_w, seq, min(COMBINE_TILE, seq // 2))
    return out.reshape(batch, seq, d)


def kernel(x, c, ctx, c_ctx, w_mod, b_mod, norm1_w, norm2_w, w_in, ret_log_decay, ret_gn_w, w_ret_o,
           conv_w, w_conv_o, w_out, router_group_w, router_group_b, router_expert_w, router_expert_b,
           expert_w13, expert_w2, final_norm_w):
    assert w_mod.shape[0] == 1, "single-layer problem"
    return _layer(x, c, ctx, c_ctx, w_mod[0], b_mod[0], norm1_w[0], norm2_w[0], w_in[0],
                  ret_log_decay[0], ret_gn_w[0], w_ret_o[0], conv_w[0], w_conv_o[0], w_out[0],
                  router_group_w[0], router_group_b[0], router_expert_w[0], router_expert_b[0],
                  expert_w13[0], expert_w2[0], final_norm_w)
```

```python
import functools

import jax
import jax.numpy as jnp
import numpy as np
from jax import lax
from jax.experimental import pallas as pl
from jax.experimental.pallas import tpu as pltpu

GRID_W = 64
RET_HEADS = 4
RET_QK_DIM = 256
RET_V_DIM = 512
RET_QK_W = RET_HEADS * RET_QK_DIM
RET_V_W = RET_HEADS * RET_V_DIM
N_GROUPS = 4
EXPERTS_PER_GROUP = 8
N_EXPERTS = N_GROUPS * EXPERTS_PER_GROUP
EXPERT_HIDDEN = 512
ROPE_BASE = 10000.0
NORM_EPS = 1e-6
GN_EPS = 1e-5

RET_CHUNK = 256
LANES = 128
EXPERT_TILE = 256
TOKEN_TILE = 512
COMBINE_TILE = 256
COL_CHUNK = 1024
VMEM_LIMIT = 56 * 1024 * 1024

BF16 = jnp.bfloat16
F32 = jnp.float32


def _params(sem, vmem=VMEM_LIMIT):
    return pltpu.CompilerParams(dimension_semantics=sem, vmem_limit_bytes=vmem)


def _mod_kernel(c_ref, w_ref, b_ref, o_ref):
    c = c_ref[...]
    s = c * jax.nn.sigmoid(c)
    o_ref[...] = jnp.dot(s.astype(BF16), w_ref[...].astype(BF16),
                         preferred_element_type=F32) + b_ref[...]


def _modulation(c_all, w_mod, b_mod):
    rows, d = c_all.shape
    n_out = w_mod.shape[1]
    tn = 3072
    return pl.pallas_call(
        _mod_kernel,
        out_shape=jax.ShapeDtypeStruct((rows, n_out), F32),
        grid=(n_out // tn,),
        in_specs=[pl.BlockSpec((rows, d), lambda j: (0, 0)),
                  pl.BlockSpec((d, tn), lambda j: (0, j)),
                  pl.BlockSpec((1, tn), lambda j: (0, j))],
        out_specs=pl.BlockSpec((rows, tn), lambda j: (0, j)),
        compiler_params=_params(("parallel",)),
        name="mod",
    )(c_all, w_mod, b_mod.reshape(1, n_out))


def _rope_store(acc, cos, sin, scale, o_ref):
    for hd in range(acc.shape[1] // RET_QK_DIM):
        lo = hd * RET_QK_DIM
        u1 = acc[:, lo:lo + LANES]
        u2 = acc[:, lo + LANES:lo + 2 * LANES]
        r1 = u1 * cos - u2 * sin
        r2 = u1 * sin + u2 * cos
        if scale != 1.0:
            r1 = r1 * scale
            r2 = r2 * scale
        o_ref[:, lo:lo + LANES] = r1.astype(o_ref.dtype)
        o_ref[:, lo + LANES:lo + 2 * LANES] = r2.astype(o_ref.dtype)


def _proj_kernel(x_ref, nw_ref, sh_ref, sc_ref, cos_ref, sin_ref, gn_ref, w_ref, o_ref, *, modes, tn):
    x = x_ref[...]
    y = x * lax.rsqrt(jnp.mean(x * x, axis=-1, keepdims=True) + NORM_EPS) * nw_ref[...]
    h = (y * (1.0 + sc_ref[0]) + sh_ref[0]).astype(BF16)
    assert len(modes) * tn == w_ref.shape[1]
    for j, (kind, arg) in enumerate(modes):
        cols = slice(j * tn, (j + 1) * tn)
        acc = jnp.dot(h, w_ref[:, cols], preferred_element_type=F32)
        if kind == "rope":
            _rope_store(acc, cos_ref[...], sin_ref[...], arg, o_ref.at[:, cols])
        elif kind == "scale":
            o_ref[:, cols] = (acc * arg).astype(o_ref.dtype)
        elif kind == "swish_gain":
            gain = gn_ref[:, arg * tn:(arg + 1) * tn]
            o_ref[:, cols] = (acc * jax.nn.sigmoid(acc) * gain).astype(o_ref.dtype)
        else:
            o_ref[:, cols] = acc.astype(o_ref.dtype)


def _projection(x2d, norm_w, shift, scale, cos_t, sin_t, gain, w_bf16, seq, modes, tm):
    n, d = x2d.shape
    width = w_bf16.shape[1]
    tn = COL_CHUNK
    per_b = seq // tm
    nb = shift.shape[0]
    bidx = (lambda i: (i // per_b, 0, 0)) if nb > 1 else (lambda i: (0, 0, 0))
    return pl.pallas_call(
        functools.partial(_proj_kernel, modes=modes, tn=tn),
        out_shape=jax.ShapeDtypeStruct((n, width), BF16),
        grid=(n // tm,),
        in_specs=[pl.BlockSpec((tm, d), lambda i: (i, 0)),
                  pl.BlockSpec((1, d), lambda i: (0, 0)),
                  pl.BlockSpec((1, 1, d), bidx),
                  pl.BlockSpec((1, 1, d), bidx),
                  pl.BlockSpec((tm, LANES), lambda i: (i % per_b, 0)),
                  pl.BlockSpec((tm, LANES), lambda i: (i % per_b, 0)),
                  pl.BlockSpec(gain.shape, lambda i: (0, 0)),
                  pl.BlockSpec((d, width), lambda i: (0, 0), pipeline_mode=pl.Buffered(1))],
        out_specs=pl.BlockSpec((tm, width), lambda i: (i, 0)),
        compiler_params=_params(("parallel",)),
        name="proj",
    )(x2d, norm_w.reshape(1, d), shift, scale, cos_t, sin_t, gain, w_bf16)


def _dot_t0(a, b):
    return lax.dot_general(a, b, (((0,), (0,)), ((), ())), preferred_element_type=F32)


def _ret_kernel(lg_ref, q_ref, k_ref, v_ref, g_ref, kc_ref, vc_ref, o_ref,
                acc_ref, sf_ref, sb_ref, dm_ref, *, seq, ctx_len):
    h = pl.program_id(1)
    lgf = lg_ref[0, h]
    lgb = lg_ref[1, h]
    c = RET_CHUNK
    n_chunks = seq // c

    cpos = lax.broadcasted_iota(jnp.int32, (ctx_len, 1), 0).astype(F32)
    kc = kc_ref[...].astype(F32)
    vc = vc_ref[...]
    sf_ref[...] = _dot_t0((kc * jnp.exp((ctx_len - 1.0 - cpos) * lgf)).astype(BF16), vc)
    sb_ref[...] = _dot_t0((kc * jnp.exp(cpos * lgb)).astype(BF16), vc)

    ri = lax.broadcasted_iota(jnp.int32, (c, c), 0)
    ci = lax.broadcasted_iota(jnp.int32, (c, c), 1)
    dist = (ri - ci).astype(F32)
    dm_ref[...] = jnp.exp(jnp.abs(dist) * jnp.where(dist >= 0, lgf, lgb))

    idx = lax.broadcasted_iota(jnp.int32, (c, 1), 0).astype(F32)
    xi_f = jnp.exp((idx + 1.0) * lgf)
    zeta_f = jnp.exp((c - 1.0 - idx) * lgf)
    xi_b = jnp.exp((c - idx) * lgb)
    zeta_b = jnp.exp(idx * lgb)
    cd_f = jnp.exp(jnp.full((1, 1), float(c), F32) * lgf)
    cd_b = jnp.exp(jnp.full((1, 1), float(c), F32) * lgb)

    def chunk_rows(ic):
        return pl.ds(pl.multiple_of(ic * c, c), c)

    def fwd_part(ic):
        rows = chunk_rows(ic)
        q = q_ref[rows, :]
        k = k_ref[rows, :]
        v = v_ref[rows, :]
        s = lax.dot_general(q, k, (((1,), (1,)), ((), ())), preferred_element_type=F32)
        intra = jnp.dot((s * dm_ref[...]).astype(BF16), v, preferred_element_type=F32)
        inter = jnp.dot(q, sf_ref[...].astype(BF16), preferred_element_type=F32)
        kz = (k.astype(F32) * zeta_f).astype(BF16)
        sf_ref[...] = cd_f * sf_ref[...] + _dot_t0(kz, v)
        return intra + xi_f * inter

    def bwd_part(ic):
        rows = chunk_rows(ic)
        q = q_ref[rows, :]
        k = k_ref[rows, :]
        v = v_ref[rows, :]
        inter = jnp.dot(q, sb_ref[...].astype(BF16), preferred_element_type=F32)
        kz = (k.astype(F32) * zeta_b).astype(BF16)
        sb_ref[...] = cd_b * sb_ref[...] + _dot_t0(kz, v)
        return xi_b * inter

    def finalize(ic, y):
        rows = chunk_rows(ic)
        mu = jnp.mean(y, axis=-1, keepdims=True)
        yc = y - mu
        var = jnp.mean(yc * yc, axis=-1, keepdims=True)
        o_ref[rows, :] = (g_ref[rows, :].astype(F32) * (yc * lax.rsqrt(var + GN_EPS))).astype(o_ref.dtype)

    half = n_chunks // 2

    def first_half(t, carry):
        acc_ref[chunk_rows(t), :] = fwd_part(t)
        acc_ref[chunk_rows(n_chunks - 1 - t), :] = bwd_part(n_chunks - 1 - t)
        return carry

    def second_half(t, carry):
        finalize(t, acc_ref[chunk_rows(t), :] + fwd_part(t))
        u = n_chunks - 1 - t
        finalize(u, acc_ref[chunk_rows(u), :] + bwd_part(u))
        return carry

    lax.fori_loop(0, half, first_half, 0, unroll=True)
    lax.fori_loop(half, n_chunks, second_half, 0, unroll=min(4, half))


def _retention(proj, kv_ctx, log_gamma, batch, seq, ctx_len, off_q, off_k, off_v, off_g):
    n = proj.shape[0]
    dk, dv = RET_QK_DIM, RET_V_DIM
    assert seq % (2 * RET_CHUNK) == 0, "the paired scan needs an even number of chunks"
    grid_spec = pltpu.PrefetchScalarGridSpec(
        num_scalar_prefetch=1,
        grid=(batch, RET_HEADS),
        in_specs=[pl.BlockSpec((seq, dk), lambda b, h, lg: (b, off_q // dk + h)),
                  pl.BlockSpec((seq, dk), lambda b, h, lg: (b, off_k // dk + h)),
                  pl.BlockSpec((seq, dv), lambda b, h, lg: (b, off_v // dv + h)),
                  pl.BlockSpec((seq, dv), lambda b, h, lg: (b, off_g // dv + h)),
                  pl.BlockSpec((ctx_len, dk), lambda b, h, lg: (b, h)),
                  pl.BlockSpec((ctx_len, dv), lambda b, h, lg: (b, RET_QK_W // dv + h))],
        out_specs=pl.BlockSpec((seq, dv), lambda b, h, lg: (b, h)),
        scratch_shapes=[pltpu.VMEM((seq, dv), F32),
                        pltpu.VMEM((dk, dv), F32),
                        pltpu.VMEM((dk, dv), F32),
                        pltpu.VMEM((RET_CHUNK, RET_CHUNK), F32)],
    )
    return pl.pallas_call(
        functools.partial(_ret_kernel, seq=seq, ctx_len=ctx_len),
        out_shape=jax.ShapeDtypeStruct((n, RET_V_W), BF16),
        grid_spec=grid_spec,
        compiler_params=_params(("parallel", "arbitrary")),
        name="ret",
    )(log_gamma, proj, proj, proj, proj, kv_ctx, kv_ctx)


def _mix_kernel(yg_ref, cb_ref, cc_ref, ch_ref, gr_ref, gc_ref, x_ref, wro_ref, wco_ref, wo_ref,
                cw_ref, g1_ref, sh2_ref, sc2_ref, n2_ref, wr_ref, br_ref,
                x1_ref, h2_ref, ri_ref, rw_ref, cnt_ref, carry_ref):
    i = pl.program_id(0)
    tm = x_ref.shape[0]

    @pl.when(i == 0)
    def _():
        carry_ref[...] = jnp.zeros_like(carry_ref)

    ret_branch = jnp.dot(yg_ref[...], wro_ref[...], preferred_element_type=F32)

    p = cc_ref[...].astype(F32) * ch_ref[...].astype(F32)
    tpos = lax.broadcasted_iota(jnp.int32, (tm, 1), 0) % GRID_W
    prev = jnp.where(tpos != 0, pltpu.roll(p, 1, axis=0), 0.0)
    nxt = jnp.where(tpos != GRID_W - 1, pltpu.roll(p, tm - 1, axis=0), 0.0)
    cw = cw_ref[...]
    u = cw[0:1, :] * prev + cw[1:2, :] * p + cw[2:3, :] * nxt
    conv_in = (cb_ref[...].astype(F32) * u).astype(BF16)
    conv_branch = jnp.dot(conv_in, wco_ref[...], preferred_element_type=F32)

    merged = (jax.nn.sigmoid(gr_ref[...].astype(F32)) * ret_branch
              + jax.nn.sigmoid(gc_ref[...].astype(F32)) * conv_branch)
    mixed = jnp.dot(merged.astype(BF16), wo_ref[...], preferred_element_type=F32)
    x1 = x_ref[...] + g1_ref[0] * mixed
    x1_ref[...] = x1

    y = x1 * lax.rsqrt(jnp.mean(x1 * x1, axis=-1, keepdims=True) + NORM_EPS) * n2_ref[...]
    h2 = y * (1.0 + sc2_ref[0]) + sh2_ref[0]
    h2_ref[...] = h2

    h_hi = h2.astype(BF16)
    h_lo = (h2 - h_hi.astype(F32)).astype(BF16)
    nt_dims = (((1,), (1,)), ((), ()))
    hw = lax.dot_general(wr_ref[...], h_hi, nt_dims, preferred_element_type=F32)
    lw = lax.dot_general(wr_ref[:LANES, :], h_lo, nt_dims, preferred_element_type=F32)
    logits_t = hw[:LANES] + (hw[LANES:] + lw) + br_ref[...]
    neg = jnp.float32(-jnp.inf)
    big = jnp.int32(1 << 20)

    grp = logits_t[N_EXPERTS:N_EXPERTS + N_GROUPS]
    grow = lax.broadcasted_iota(jnp.int32, grp.shape, 0)
    gmax = jnp.max(grp, axis=0, keepdims=True)
    gidx = jnp.min(jnp.where(grp == gmax, grow, big), axis=0, keepdims=True)
    g_w = 1.0 / jnp.sum(jnp.exp(grp - gmax), axis=0, keepdims=True)

    el_all = logits_t[:N_EXPERTS]
    erow = lax.broadcasted_iota(jnp.int32, el_all.shape, 0)
    in_grp = (erow // EXPERTS_PER_GROUP) == gidx
    el = jnp.where(in_grp, el_all, neg)
    emax = jnp.max(el, axis=0, keepdims=True)
    i1 = jnp.min(jnp.where(el == emax, erow, big), axis=0, keepdims=True)
    el2 = jnp.where(erow == i1, neg, el)
    l2 = jnp.max(el2, axis=0, keepdims=True)
    i2 = jnp.min(jnp.where(el2 == l2, erow, big), axis=0, keepdims=True)
    esum = jnp.sum(jnp.where(in_grp, jnp.exp(el_all - emax), 0.0), axis=0, keepdims=True)
    p1 = 1.0 / esum
    p2 = jnp.exp(l2 - emax) / esum
    den = p1 + p2
    c1 = g_w * (p1 / den)
    c2 = g_w * (p2 / den)

    hot1 = erow == i1
    hot2 = erow == i2
    onehot = jnp.where(hot1, 1.0, jnp.where(hot2, 1.0, 0.0))
    earlier = (lax.broadcasted_iota(jnp.int32, (tm, tm), 0)
               < lax.broadcasted_iota(jnp.int32, (tm, tm), 1))
    ranks = jnp.dot(onehot.astype(BF16), jnp.where(earlier, 1.0, 0.0).astype(BF16),
                    preferred_element_type=F32) + carry_ref[...]
    r1 = jnp.sum(jnp.where(hot1, ranks, 0.0), axis=0, keepdims=True).astype(jnp.int32)
    r2 = jnp.sum(jnp.where(hot2, ranks, 0.0), axis=0, keepdims=True).astype(jnp.int32)
    carry_ref[...] = carry_ref[...] + jnp.sum(onehot, axis=1, keepdims=True)
    cnt_ref[...] = carry_ref[...]

    row8 = lax.broadcasted_iota(jnp.int32, (8, tm), 0)
    ri_ref[...] = jnp.where(row8 == 0, i1, jnp.where(row8 == 1, i2,
                            jnp.where(row8 == 2, r1, jnp.where(row8 == 3, r2, 0))))
    rw_ref[...] = jnp.where(row8 == 0, c1, jnp.where(row8 == 1, c2, 0.0))


def _mix(yg, proj, x2d, w_ret_o, w_conv_o, w_out, conv_w, gate1, shift2, scale2, norm2_w,
         w_router, b_router, seq, off_cb, tm):
    n, d = x2d.shape
    per_b = seq // tm
    cblk = off_cb // d
    row = lambda i: (i, 0)
    const2 = lambda i: (0, 0)
    bidx = lambda i: (i // per_b, 0, 0)
    col = lambda k: (lambda i: (i, cblk + k))
    return pl.pallas_call(
        _mix_kernel,
        out_shape=(jax.ShapeDtypeStruct((n, d), F32),
                   jax.ShapeDtypeStruct((n, d), F32),
                   jax.ShapeDtypeStruct((8, n), jnp.int32),
                   jax.ShapeDtypeStruct((8, n), F32),
                   jax.ShapeDtypeStruct((N_EXPERTS, 1), F32)),
        grid=(n // tm,),
        in_specs=[pl.BlockSpec((tm, RET_V_W), row),
                  pl.BlockSpec((tm, d), col(0)),
                  pl.BlockSpec((tm, d), col(1)),
                  pl.BlockSpec((tm, d), col(2)),
                  pl.BlockSpec((tm, d), col(3)),
                  pl.BlockSpec((tm, d), col(4)),
                  pl.BlockSpec((tm, d), row),
                  pl.BlockSpec((RET_V_W, d), const2, pipeline_mode=pl.Buffered(1)),
                  pl.BlockSpec((d, d), const2, pipeline_mode=pl.Buffered(1)),
                  pl.BlockSpec((d, d), const2, pipeline_mode=pl.Buffered(1)),
                  pl.BlockSpec((3, d), const2),
                  pl.BlockSpec((1, 1, d), bidx),
                  pl.BlockSpec((1, 1, d), bidx),
                  pl.BlockSpec((1, 1, d), bidx),
                  pl.BlockSpec((1, d), const2),
                  pl.BlockSpec((2 * LANES, d), const2),
                  pl.BlockSpec((LANES, 1), const2)],
        out_specs=(pl.BlockSpec((tm, d), row),
                   pl.BlockSpec((tm, d), row),
                   pl.BlockSpec((8, tm), lambda i: (0, i)),
                   pl.BlockSpec((8, tm), lambda i: (0, i)),
                   pl.BlockSpec((N_EXPERTS, 1), const2)),
        scratch_shapes=[pltpu.VMEM((N_EXPERTS, 1), F32)],
        compiler_params=_params(("arbitrary",)),
        name="mix",
    )(yg, proj, proj, proj, proj, proj, x2d, w_ret_o, w_conv_o, w_out, conv_w,
      gate1, shift2, scale2, norm2_w.reshape(1, d), w_router, b_router)


def _row_copy(src_ref, src_row, dst_ref, dst_row, sem):
    return pltpu.make_async_copy(src_ref.at[pl.ds(src_row, 1), :],
                                 dst_ref.at[pl.ds(dst_row, 1), :], sem)


SCATTER_SLOTS = 3


def _scatter_kernel(pos_ref, zt_ref, h_ref, xs_ref, hbuf_ref, zero_ref, lsem, ssem, zsem):
    i = pl.program_id(0)
    n_steps = pl.num_programs(0)
    tm = hbuf_ref.shape[1]
    te = zero_ref.shape[0]
    base = i * tm
    n_tok = pos_ref.shape[0] // 2

    def load(tile, slot):
        src = h_ref.at[pl.ds(pl.multiple_of(tile * tm, tm), tm), :]
        return pltpu.make_async_copy(src, hbuf_ref.at[slot], lsem.at[slot])

    def wait_rows(slot):
        for _ in range(2):
            pltpu.make_async_copy(hbuf_ref.at[slot], xs_ref.at[pl.ds(0, tm), :], ssem.at[slot]).wait()

    @pl.when(i == 0)
    def _():
        load(0, 0).start()
        load(1, 1).start()
        zero_ref[...] = jnp.zeros_like(zero_ref)

        def tile_copy(z):
            row = pl.multiple_of(zt_ref[z] * te, te)
            return pltpu.make_async_copy(zero_ref, xs_ref.at[pl.ds(row, te), :], zsem)

        def zissue(z, carry):
            @pl.when(zt_ref[z] >= 0)
            def _():
                tile_copy(z).start()
            return carry

        def zdrain(z, carry):
            @pl.when(zt_ref[z] >= 0)
            def _():
                tile_copy(z).wait()
            return carry

        lax.fori_loop(0, zt_ref.shape[0], zissue, 0)
        lax.fori_loop(0, zt_ref.shape[0], zdrain, 0)

    slot = lax.rem(i, SCATTER_SLOTS)
    reload_slot = lax.rem(i + 2, SCATTER_SLOTS)
    load(i, slot).wait()
    src = hbuf_ref.at[slot]
    pos_a = pos_ref.at[pl.ds(base, tm)]
    pos_b = pos_ref.at[pl.ds(n_tok + base, tm)]
    for r in range(tm):
        _row_copy(src, r, xs_ref, pos_a[r], ssem.at[slot]).start(priority=0)
        _row_copy(src, r, xs_ref, pos_b[r], ssem.at[slot]).start(priority=1)

    @pl.when(i > 0)
    def _():
        wait_rows(reload_slot)

    load(jnp.minimum(i + 2, n_steps - 1), reload_slot).start()

    @pl.when(i == n_steps - 1)
    def _():
        wait_rows(slot)
        load(i, lax.rem(i + 1, SCATTER_SLOTS)).wait()
        load(i, reload_slot).wait()


def _scatter_rows(pos_flat, zero_tiles, h2, n_sorted, tm):
    n, d = h2.shape
    assert n // tm >= SCATTER_SLOTS
    grid_spec = pltpu.PrefetchScalarGridSpec(
        num_scalar_prefetch=2,
        grid=(n // tm,),
        in_specs=[pl.BlockSpec(memory_space=pl.ANY)],
        out_specs=pl.BlockSpec(memory_space=pl.ANY),
        scratch_shapes=[pltpu.VMEM((SCATTER_SLOTS, tm, d), F32),
                        pltpu.VMEM((EXPERT_TILE, d), F32),
                        pltpu.SemaphoreType.DMA((SCATTER_SLOTS,)),
                        pltpu.SemaphoreType.DMA((SCATTER_SLOTS,)),
                        pltpu.SemaphoreType.DMA],
    )
    return pl.pallas_call(
        _scatter_kernel,
        out_shape=jax.ShapeDtypeStruct((n_sorted, d), F32),
        grid_spec=grid_spec,
        compiler_params=_params(("arbitrary",)),
        name="scatter",
    )(pos_flat, zero_tiles, h2)


ROW_TILE_SLOTS = 3


def _expert_kernel(tf_ref, te_ref, tn_ref, ts_ref, xs_ref, w13_ref, w2_ref, ys_ref,
                   xbuf_ref, w13f_ref, w2f_ref, w13b_ref, w2b_ref, xsem, sem13, sem2):
    t = pl.program_id(0)
    nt = pl.num_programs(0)
    rows = xbuf_ref.shape[1]
    flag = tf_ref[t]

    def tile_copy(tile, slot):
        src = xs_ref.at[pl.ds(pl.multiple_of(tile * rows, rows), rows), :]
        return pltpu.make_async_copy(src, xbuf_ref.at[slot], xsem.at[slot])

    def weight_copies(e, slot):
        return (pltpu.make_async_copy(w13_ref.at[e], w13f_ref.at[slot], sem13.at[slot]),
                pltpu.make_async_copy(w2_ref.at[e], w2f_ref.at[slot], sem2.at[slot]))

    @pl.when(t == 0)
    def _():
        for cp in weight_copies(te_ref[0], ts_ref[0]):
            cp.start()
        tile_copy(0, 0).start()

        @pl.when(tf_ref[1] > 0)
        def _():
            tile_copy(1, 1).start()

    slot = lax.rem(t, ROW_TILE_SLOTS)
    ahead = jnp.minimum(t + 2, nt - 1)

    @pl.when(flag > 0)
    def _():
        tile_copy(t, slot).wait()

    @pl.when((t + 2 < nt) & (tf_ref[ahead] > 0))
    def _():
        tile_copy(ahead, lax.rem(t + 2, ROW_TILE_SLOTS)).start()

    @pl.when(flag == 2)
    def _():
        slot = ts_ref[t]
        for cp in weight_copies(te_ref[t], slot):
            cp.wait()
        w13b_ref[...] = w13f_ref[slot].astype(BF16)
        w2b_ref[...] = w2f_ref[slot].astype(BF16)

        @pl.when(tn_ref[t] >= 0)
        def _():
            for cp in weight_copies(tn_ref[t], 1 - slot):
                cp.start()

    @pl.when(flag > 0)
    def _():
        x = xbuf_ref[slot].astype(BF16)
        hid = jnp.dot(x, w13b_ref[...], preferred_element_type=F32)
        a = hid[:, :EXPERT_HIDDEN]
        act = a * jax.nn.sigmoid(a) * hid[:, EXPERT_HIDDEN:]
        ys_ref[...] = jnp.dot(act.astype(BF16), w2b_ref[...], preferred_element_type=F32)

    @pl.when(flag == 0)
    def _():
        ys_ref[...] = jnp.zeros_like(ys_ref)


def _experts(tile_flag, tile_expert, tile_next, tile_slot, xs, w13, w2):
    n_sorted, d = xs.shape
    te = EXPERT_TILE
    n_tiles = n_sorted // te
    assert n_tiles >= ROW_TILE_SLOTS
    hid2 = w13.shape[2]
    grid_spec = pltpu.PrefetchScalarGridSpec(
        num_scalar_prefetch=4,
        grid=(n_tiles,),
        in_specs=[pl.BlockSpec(memory_space=pl.ANY),
                  pl.BlockSpec(memory_space=pl.ANY),
                  pl.BlockSpec(memory_space=pl.ANY)],
        out_specs=pl.BlockSpec((te, d), lambda t, *_: (t, 0)),
        scratch_shapes=[pltpu.VMEM((ROW_TILE_SLOTS, te, d), F32),
                        pltpu.VMEM((2, d, hid2), F32),
                        pltpu.VMEM((2, hid2 // 2, d), F32),
                        pltpu.VMEM((d, hid2), BF16),
                        pltpu.VMEM((hid2 // 2, d), BF16),
                        pltpu.SemaphoreType.DMA((ROW_TILE_SLOTS,)),
                        pltpu.SemaphoreType.DMA((2,)),
                        pltpu.SemaphoreType.DMA((2,))],
    )
    return pl.pallas_call(
        _expert_kernel,
        out_shape=jax.ShapeDtypeStruct((n_sorted, d), F32),
        grid_spec=grid_spec,
        compiler_params=_params(("arbitrary",)),
        name="experts",
    )(tile_flag, tile_expert, tile_next, tile_slot, xs, w13, w2)


GATHER_SLOTS = 3


def _combine_kernel(pos_ref, ys_ref, x1_ref, rw_ref, g2_ref, fw_ref, o_ref, ya_ref, yb_ref, sem):
    i = pl.program_id(0)
    n_steps = pl.num_programs(0)
    tm = ya_ref.shape[1]
    n_tok = pos_ref.shape[0] // 2

    def issue(tile, slot):
        base = tile * tm
        dst_a = ya_ref.at[slot]
        dst_b = yb_ref.at[slot]
        pos_a = pos_ref.at[pl.ds(base, tm)]
        pos_b = pos_ref.at[pl.ds(n_tok + base, tm)]
        for r in range(tm):
            _row_copy(ys_ref, pos_a[r], dst_a, r, sem.at[slot]).start(priority=0)
            _row_copy(ys_ref, pos_b[r], dst_b, r, sem.at[slot]).start(priority=1)

    def wait(slot):
        pltpu.make_async_copy(ys_ref.at[pl.ds(0, tm), :], ya_ref.at[slot], sem.at[slot]).wait()
        pltpu.make_async_copy(ys_ref.at[pl.ds(0, tm), :], yb_ref.at[slot], sem.at[slot]).wait()

    @pl.when(i == 0)
    def _():
        issue(0, 0)
        issue(1, 1)

    slot = lax.rem(i, GATHER_SLOTS)
    ahead_slot = lax.rem(i + 2, GATHER_SLOTS)
    wait(slot)
    rw = rw_ref[...]
    moe = rw[:, 0:1] * ya_ref[slot] + rw[:, 1:2] * yb_ref[slot]
    x2 = x1_ref[...] + g2_ref[0] * moe
    o_ref[...] = x2 * lax.rsqrt(jnp.mean(x2 * x2, axis=-1, keepdims=True) + NORM_EPS) * fw_ref[...]
    issue(jnp.minimum(i + 2, n_steps - 1), ahead_slot)

    @pl.when(i == n_steps - 1)
    def _():
        wait(lax.rem(i + 1, GATHER_SLOTS))
        wait(ahead_slot)


def _combine(pos_flat, ys, x1, route_w, gate2, final_w, seq, tm):
    n, d = x1.shape
    per_b = seq // tm
    assert n // tm >= GATHER_SLOTS
    grid_spec = pltpu.PrefetchScalarGridSpec(
        num_scalar_prefetch=1,
        grid=(n // tm,),
        in_specs=[pl.BlockSpec(memory_space=pl.ANY),
                  pl.BlockSpec((tm, d), lambda i, pos: (i, 0)),
                  pl.BlockSpec((tm, 2), lambda i, pos: (i, 0)),
                  pl.BlockSpec((1, 1, d), lambda i, pos: (i // per_b, 0, 0)),
                  pl.BlockSpec((1, d), lambda i, pos: (0, 0))],
        out_specs=pl.BlockSpec((tm, d), lambda i, pos: (i, 0)),
        scratch_shapes=[pltpu.VMEM((GATHER_SLOTS, tm, d), F32),
                        pltpu.VMEM((GATHER_SLOTS, tm, d), F32),
                        pltpu.SemaphoreType.DMA((GATHER_SLOTS,))],
    )
    return pl.pallas_call(
        _combine_kernel,
        out_shape=jax.ShapeDtypeStruct((n, d), F32),
        grid_spec=grid_spec,
        compiler_params=_params(("arbitrary",)),
        name="combine",
    )(pos_flat, ys, x1, route_w, gate2, final_w.reshape(1, d))


def _rope_tables(seq):
    pos = np.arange(seq)
    nf = RET_QK_DIM // 4
    inv = (np.float32(ROPE_BASE) ** (-np.arange(nf, dtype=np.float32) / np.float32(nf))).astype(np.float32)
    ang_r = ((pos // GRID_W).astype(np.float32)[:, None] * inv[None, :]).astype(np.float64)
    ang_c = ((pos % GRID_W).astype(np.float32)[:, None] * inv[None, :]).astype(np.float64)
    return (jnp.asarray(np.concatenate([np.cos(ang_r), np.cos(ang_c)], axis=1), F32),
            jnp.asarray(np.concatenate([np.sin(ang_r), np.sin(ang_c)], axis=1), F32))


def _weight_prep_kernel(w_ref, o_ref, octx_ref, *, n_rot, ctx_cols):
    half = LANES // 2
    first = lax.broadcasted_iota(jnp.int32, (w_ref.shape[0], LANES), 1) < half
    for c0 in range(0, n_rot, RET_QK_DIM):
        a = w_ref[:, c0:c0 + LANES]
        b = w_ref[:, c0 + LANES:c0 + 2 * LANES]
        o_ref[:, c0:c0 + LANES] = jnp.where(first, a, pltpu.roll(b, half, axis=1)).astype(o_ref.dtype)
        o_ref[:, c0 + LANES:c0 + 2 * LANES] = jnp.where(first, pltpu.roll(a, half, axis=1), b).astype(o_ref.dtype)
    o_ref[:, n_rot:] = w_ref[:, n_rot:].astype(o_ref.dtype)
    octx_ref[...] = o_ref[:, ctx_cols[0]:ctx_cols[1]]


def _weight_prep(w_in, n_rot, ctx_cols):
    d, width = w_in.shape
    tr = 128
    ctx_width = ctx_cols[1] - ctx_cols[0]
    return pl.pallas_call(
        functools.partial(_weight_prep_kernel, n_rot=n_rot, ctx_cols=ctx_cols),
        out_shape=(jax.ShapeDtypeStruct((d, width), BF16),
                   jax.ShapeDtypeStruct((d, ctx_width), BF16)),
        grid=(d // tr,),
        in_specs=[pl.BlockSpec((tr, width), lambda i: (i, 0))],
        out_specs=(pl.BlockSpec((tr, width), lambda i: (i, 0)),
                   pl.BlockSpec((tr, ctx_width), lambda i: (i, 0))),
        compiler_params=_params(("parallel",)),
        name="wprep",
    )(w_in)


def _layer(x, c, ctx, c_ctx, w_mod, b_mod, norm1_w, norm2_w, w_in, log_decay, gn_w, w_ret_o,
           conv_w, w_conv_o, w_out, rg_w, rg_b, re_w, re_b, w13, w2, final_w):
    batch, seq, d = x.shape
    ctx_len = ctx.shape[1]
    n = batch * seq
    off_q = 0
    off_k = off_q + RET_QK_W
    off_v = off_k + RET_QK_W
    off_g = off_v + RET_V_W
    off_cb = off_g + RET_V_W

    pad = (-(batch + 1)) % 8
    c_all = jnp.concatenate([c, c_ctx[None, :], jnp.zeros((pad, d), F32)], axis=0)
    mod = _modulation(c_all, w_mod, b_mod)
    mx = mod[:batch].reshape(batch, 1, 6, d)
    shift1, scale1, gate1, shift2, scale2, gate2 = (mx[:, :, k, :] for k in range(6))
    mc = mod[batch].reshape(1, 1, 6, d)
    shift1c, scale1c = mc[:, :, 0, :], mc[:, :, 1, :]

    w_in_b, w_ctx_b = _weight_prep(w_in, off_v, (off_k, off_g))
    cos_t, sin_t = _rope_tables(seq)
    k_scale = RET_QK_DIM ** -0.5

    tm_proj = min(TOKEN_TILE, seq)
    main_modes = (("rope", 1.0), ("rope", k_scale), ("plain", None), ("plain", None),
                  ("swish_gain", 0), ("swish_gain", 1), ("plain", None), ("plain", None), ("plain", None),
                  ("plain", None), ("plain", None))
    gain = gn_w.reshape(1, RET_V_W)
    proj = _projection(x.reshape(n, d), norm1_w, shift1, scale1, cos_t, sin_t, gain, w_in_b, seq,
                       modes=main_modes, tm=tm_proj)
    tm_ctx = batch * ctx_len
    no_rope = jnp.zeros((tm_ctx, LANES), F32)
    kv_ctx = _projection(ctx.reshape(tm_ctx, d), norm1_w, shift1c, scale1c, no_rope, no_rope, gain,
                         w_ctx_b, tm_ctx,
                         modes=(("scale", k_scale), ("plain", None), ("plain", None)), tm=tm_ctx)

    log_gamma = jnp.log1p(-jnp.exp(log_decay.astype(F32)))
    yg = _retention(proj, kv_ctx, log_gamma, batch, seq, ctx_len, off_q, off_k, off_v, off_g)

    w_router = jnp.concatenate(
        [re_w, rg_w, jnp.zeros((d, LANES - N_EXPERTS - N_GROUPS), F32)], axis=1)
    b_router = jnp.concatenate(
        [re_b, rg_b, jnp.zeros((LANES - N_EXPERTS - N_GROUPS,), F32)]).reshape(LANES, 1)
    w_router_hi = w_router.astype(BF16)
    w_router_lo = (w_router - w_router_hi.astype(F32)).astype(BF16)
    w_router = jnp.concatenate([w_router_hi, w_router_lo], axis=1).T
    tm_mix = min(TOKEN_TILE, seq)
    x1, h2, route_i, route_w, counts = _mix(
        yg, proj, x.reshape(n, d), w_ret_o.astype(BF16), w_conv_o.astype(BF16), w_out.astype(BF16),
        conv_w, gate1, shift2, scale2, norm2_w, w_router, b_router, seq, off_cb, tm_mix)

    te = EXPERT_TILE
    cnt = counts[:, 0].astype(jnp.int32)
    padded = ((cnt + te - 1) // te) * te
    ends = jnp.cumsum(padded)
    starts = ends - padded
    expert_ids = jnp.arange(N_EXPERTS, dtype=jnp.int32)
    route_t = route_i[0:4]
    seg_start = jnp.sum(jnp.where(route_t[0:2, :, None] == expert_ids, starts, 0), axis=-1)
    pos_flat = (seg_start + route_t[2:4]).reshape(-1).astype(jnp.int32)
    n_tiles = (2 * n) // te + N_EXPERTS
    tile_row = jnp.arange(n_tiles, dtype=jnp.int32) * te
    tile_expert = jnp.minimum(jnp.sum((tile_row[:, None] >= ends[None, :]).astype(jnp.int32), axis=1),
                              N_EXPERTS - 1).astype(jnp.int32)
    new_expert = jnp.concatenate([jnp.ones((1,), bool), tile_expert[1:] != tile_expert[:-1]])
    tile_flag = jnp.where(tile_row < ends[-1], jnp.where(new_expert, 2, 1), 0).astype(jnp.int32)

    valid_tiles = ends[-1] // te
    last_tile = jnp.where(padded > 0, ends // te - 1, -1)
    tail_tile = valid_tiles + jnp.arange(N_EXPERTS, dtype=jnp.int32)
    tail_tile = jnp.where(tail_tile < n_tiles, tail_tile, -1)
    zero_tiles = jnp.concatenate([last_tile, tail_tile]).astype(jnp.int32)

    xs = _scatter_rows(pos_flat, zero_tiles, h2, n_tiles * te, min(TOKEN_TILE, seq // 2))
    nonempty = padded > 0
    later = nonempty[None, :] & (expert_ids[None, :] > expert_ids[:, None])
    next_expert = jnp.min(jnp.where(later, expert_ids[None, :], N_EXPERTS), axis=1)
    next_expert = jnp.where(next_expert == N_EXPERTS, -1, next_expert)
    slot = (jnp.cumsum(nonempty.astype(jnp.int32)) - 1) % 2
    tile_hot = tile_expert[:, None] == expert_ids[None, :]
    tile_next = jnp.sum(jnp.where(tile_hot, next_expert, 0), axis=1).astype(jnp.int32)
    tile_slot = jnp.sum(jnp.where(tile_hot, slot, 0), axis=1).astype(jnp.int32)

    ys = _experts(tile_flag, tile_expert, tile_next, tile_slot, xs, w13, w2)
    out = _combine(pos_flat, ys, x1, route_w[0:2].T, gate2, final_w, seq, min(COMBINE_TILE, seq // 2))
    return out.reshape(batch, seq, d)


def kernel(x, c, ctx, c_ctx, w_mod, b_mod, norm1_w, norm2_w, w_in, ret_log_decay, ret_gn_w, w_ret_o,
           conv_w, w_conv_o, w_out, router_group_w, router_group_b, router_expert_w, router_expert_b,
           expert_w13, expert_w2, final_norm_w):
    assert w_mod.shape[0] == 1, "single-layer problem"
    return _layer(x, c, ctx, c_ctx, w_mod[0], b_mod[0], norm1_w[0], norm2_w[0], w_in[0],
                  ret_log_decay[0], ret_gn_w[0], w_ret_o[0], conv_w[0], w_conv_o[0], w_out[0],
                  router_group_w[0], router_group_b[0], router_expert_w[0], router_expert_b[0],
                  expert_w13[0], expert_w2[0], final_norm_w)
```
